```python
import math, functools
import jax, jax.numpy as jnp
from jax import lax
import numpy as np

D_MODEL = 1024
BATCH = 1
SEQ = 16384
DEPTH = 1
DEC_BATCH = 128
DEC_SEQ = 8
PAST_LEN = 8192
PAGE_SIZE = 128

N_HEADS = 16
HEAD_DIM = 64
KV_HEADS = 4
IDX_HEADS = 8
IDX_DIM = 64
IDX_TOPK_MAX = 256
Q_BLOCK = 128
ROPE_THETA = 10000.0
D_RNN = D_MODEL
RG_BLOCKS = 4
RG_BLOCK_W = D_RNN // RG_BLOCKS
CONV_W = 4
RG_C = 8.0
N_EXPERTS = 32
TOP_K = 4
D_FF = D_MODEL
SWIGLU_LIMIT = 7.0
SWIGLU_ALPHA = 1.702
MOE_BLOCK = 128
EPS = 1e-6

Q_W = N_HEADS * HEAD_DIM
KV_W = KV_HEADS * HEAD_DIM
IQ_W = IDX_HEADS * IDX_DIM
IN_WIDTHS = (Q_W, KV_W, KV_W, IQ_W, IDX_DIM, IDX_HEADS, D_RNN, D_RNN, D_MODEL, D_MODEL)
D_IN = sum(IN_WIDTHS)

kernel_name = "hybrid_dsa_rglru_moe_step"


def rms_norm(x, g):
    xf = x.astype(jnp.float32)
    y = xf * lax.rsqrt(jnp.mean(xf * xf, axis=-1, keepdims=True) + EPS)
    return (y * g.astype(jnp.float32)).astype(x.dtype)


def rope(x, pos):
    half = x.shape[-1] // 2
    inv_freq = ROPE_THETA ** (-jnp.arange(half, dtype=jnp.float32) / half)
    ang = pos.astype(jnp.float32)[:, None] * inv_freq[None, :]
    cos = jnp.cos(ang)[:, None, :]
    sin = jnp.sin(ang)[:, None, :]
    xf = x.astype(jnp.float32)
    x1, x2 = xf[..., :half], xf[..., half:]
    return jnp.concatenate([x1 * cos - x2 * sin, x2 * cos + x1 * sin], axis=-1).astype(x.dtype)


def index_scores(iq, iw, ik):
    s = jnp.einsum('bthd,bld->bthl', iq, ik, preferred_element_type=jnp.float32)
    return jnp.einsum('bthl,bth->btl', jax.nn.relu(s), iw.astype(jnp.float32))


def sparse_attend(q, k_sel, v_sel, valid):
    B, T = q.shape[:2]
    qg = q.reshape(B, T, KV_HEADS, N_HEADS // KV_HEADS, HEAD_DIM)
    logits = jnp.einsum('btkgd,btskd->btkgs', qg, k_sel, preferred_element_type=jnp.float32) * (HEAD_DIM ** -0.5)
    logits = jnp.where(valid[:, :, None, None, :], logits, -jnp.inf)
    p = jax.nn.softmax(logits, axis=-1)
    out = jnp.einsum('btkgs,btskd->btkgd', p.astype(v_sel.dtype), v_sel)
    return out.reshape(B, T, Q_W)


def gather_rows(rows, idx):
    return jax.vmap(lambda r, i: r[i])(rows, idx)


def prompt_sparse_attention(q, k, v, iq, iw, ik):
    B, S = q.shape[:2]
    topk = min(IDX_TOPK_MAX, S // 4)
    n_blocks = S // Q_BLOCK
    key_pos = jnp.arange(S, dtype=jnp.int32)

    def block(args):
        qb, iqb, iwb, start = args
        qpos = start + jnp.arange(Q_BLOCK, dtype=jnp.int32)
        sc = index_scores(iqb, iwb, ik)
        sc = jnp.where((key_pos[None, :] <= qpos[:, None])[None], sc, -jnp.inf)
        _, sel = lax.top_k(sc, topk)
        valid = sel <= qpos[None, :, None]
        return sparse_attend(qb, gather_rows(k, sel), gather_rows(v, sel), valid)

    to_blocks = lambda a: jnp.moveaxis(a.reshape(B, n_blocks, Q_BLOCK, *a.shape[2:]), 1, 0)
    starts = jnp.arange(n_blocks, dtype=jnp.int32) * Q_BLOCK
    out = lax.map(block, (to_blocks(q), to_blocks(iq), to_blocks(iw), starts))
    return jnp.moveaxis(out, 0, 1).reshape(B, S, Q_W)


def sample_sparse_attention(q, k, v, iq, iw, ik, cache_k, cache_v, cache_idx_k, page_table):
    DB, T = q.shape[:2]
    past = page_table.shape[1] * PAGE_SIZE
    L = past + T
    topk = min(IDX_TOPK_MAX, L // 4)
    ik_past = cache_idx_k[page_table].reshape(DB, past, IDX_DIM)
    ik_all = jnp.concatenate([ik_past, ik.astype(ik_past.dtype)], axis=1)
    qpos = past + jnp.arange(T, dtype=jnp.int32)
    sc = index_scores(iq, iw, ik_all)
    sc = jnp.where((jnp.arange(L, dtype=jnp.int32)[None, :] <= qpos[:, None])[None], sc, -jnp.inf)
    _, sel = lax.top_k(sc, topk)
    valid = sel <= qpos[None, :, None]
    in_past = sel < past
    sel_p = jnp.minimum(sel, past - 1)
    page = gather_rows(page_table, sel_p // PAGE_SIZE)
    phys = page * PAGE_SIZE + sel_p % PAGE_SIZE
    new_idx = jnp.clip(sel - past, 0, T - 1)
    k_pool = cache_k.reshape(-1, KV_HEADS, HEAD_DIM)
    v_pool = cache_v.reshape(-1, KV_HEADS, HEAD_DIM)
    m = in_past[..., None, None]
    k_sel = jnp.where(m, k_pool[phys], gather_rows(k, new_idx).astype(k_pool.dtype))
    v_sel = jnp.where(m, v_pool[phys], gather_rows(v, new_idx).astype(v_pool.dtype))
    return sparse_attend(q, k_sel, v_sel, valid)


def causal_conv(xr, conv_state, w, b):
    T = xr.shape[1]
    xp = jnp.concatenate([conv_state.astype(xr.dtype), xr], axis=1)
    out = b + sum(xp[:, j:j + T] * w[j] for j in range(CONV_W))
    return out, xp[:, -(CONV_W - 1):]


def block_diag(x, w, b):
    B, T, _ = x.shape
    y = jnp.einsum('btnc,ncd->btnd', x.reshape(B, T, RG_BLOCKS, RG_BLOCK_W), w)
    return y.reshape(B, T, D_RNN) + b


def rg_lru(x, h0, w_a, b_a, w_i, b_i, lam):
    xf = x.astype(jnp.float32)
    r = jax.nn.sigmoid(block_diag(x, w_a, b_a).astype(jnp.float32))
    i = jax.nn.sigmoid(block_diag(x, w_i, b_i).astype(jnp.float32))
    log_a = RG_C * r * jax.nn.log_sigmoid(lam.astype(jnp.float32))
    a = jnp.exp(log_a)
    u = jnp.sqrt(-jnp.expm1(2.0 * log_a)) * (i * xf)

    def step(h, au):
        h = au[0] * h + au[1]
        return h, h

    h_last, hs = lax.scan(step, h0.astype(jnp.float32), (jnp.moveaxis(a, 1, 0), jnp.moveaxis(u, 1, 0)))
    return jnp.moveaxis(hs, 0, 1).astype(x.dtype), h_last.astype(x.dtype)


def moe(h, w_router, b_router, w_gate_up, b_gate_up, w_down, b_down):
    shp = h.shape
    xt = h.reshape(-1, shp[-1])
    n = xt.shape[0]
    logits = (xt @ w_router + b_router).astype(jnp.float32)
    top_v, top_e = lax.top_k(logits, TOP_K)
    gates = jax.nn.softmax(top_v, axis=-1)
    n_assign = n * TOP_K
    flat_e = top_e.reshape(-1)
    order = jnp.argsort(flat_e, stable=True)
    sorted_e = flat_e[order]
    counts = jnp.bincount(flat_e, length=N_EXPERTS)
    padded = (counts + MOE_BLOCK - 1) // MOE_BLOCK * MOE_BLOCK
    pad_end = jnp.cumsum(padded)
    pad_start = pad_end - padded
    grp_start = jnp.cumsum(counts) - counts
    dest_sorted = pad_start[sorted_e] + jnp.arange(n_assign, dtype=jnp.int32) - grp_start[sorted_e]
    dest = jnp.zeros((n_assign,), jnp.int32).at[order].set(dest_sorted.astype(jnp.int32))
    n_blocks = -(-(n_assign + N_EXPERTS * (MOE_BLOCK - 1)) // MOE_BLOCK)
    n_slots = n_blocks * MOE_BLOCK
    slot_tok = jnp.full((n_slots,), n, jnp.int32).at[dest].set(jnp.arange(n_assign, dtype=jnp.int32) // TOP_K)
    blk_exp = jnp.minimum(jnp.searchsorted(pad_end, jnp.arange(n_blocks, dtype=jnp.int32) * MOE_BLOCK, side='right'), N_EXPERTS - 1)
    x_pad = jnp.concatenate([xt, jnp.zeros((1, xt.shape[1]), xt.dtype)], axis=0)
    xs = x_pad[slot_tok].reshape(n_blocks, MOE_BLOCK, -1)

    def expert_block(args):
        xb, e = args
        gu = xb @ w_gate_up[e] + b_gate_up[e]
        g = jnp.minimum(gu[:, :D_FF], SWIGLU_LIMIT)
        u = jnp.clip(gu[:, D_FF:], -SWIGLU_LIMIT, SWIGLU_LIMIT)
        act = (u + 1.0) * (g * jax.nn.sigmoid(SWIGLU_ALPHA * g))
        return act @ w_down[e] + b_down[e]

    ys = lax.map(expert_block, (xs, blk_exp)).reshape(n_slots, -1)
    y = jnp.einsum('tkd,tk->td', ys[dest].reshape(n, TOP_K, -1), gates.astype(ys.dtype))
    return y.reshape(shp)


def trunk_layer(x, pos, attend, conv_state, h0, norm1_g, w_in, q_norm_g, k_norm_g, conv_w, conv_b,
                rg_w_a, rg_b_a, rg_w_i, rg_b_i, rg_lambda, w_attn_out, w_rnn_out, w_out, norm2_g,
                w_router, b_router, w_gate_up, b_gate_up, w_down, b_down):
    B, T, _ = x.shape
    hn = rms_norm(x, norm1_g)
    z = hn @ w_in
    q, k, v, iq, ik, iw, xr, xg, ga, gr = jnp.split(z, np.cumsum(IN_WIDTHS)[:-1].tolist(), axis=-1)
    q = rope(rms_norm(q.reshape(B, T, N_HEADS, HEAD_DIM), q_norm_g), pos)
    k = rope(rms_norm(k.reshape(B, T, KV_HEADS, HEAD_DIM), k_norm_g), pos)
    v = v.reshape(B, T, KV_HEADS, HEAD_DIM)
    iq = rope(iq.reshape(B, T, IDX_HEADS, IDX_DIM), pos)
    ik = rope(ik[:, :, None, :], pos)[:, :, 0]
    iw = iw * (IDX_HEADS ** -0.5 * IDX_DIM ** -0.5)
    attn = attend(q, k, v, iq, iw, ik)
    conv_out, conv_new = causal_conv(xr, conv_state, conv_w, conv_b)
    rnn, h_new = rg_lru(conv_out, h0, rg_w_a, rg_b_a, rg_w_i, rg_b_i, rg_lambda)
    rnn = rnn * jax.nn.gelu(xg)
    merged = jax.nn.sigmoid(ga) * (attn @ w_attn_out) + jax.nn.sigmoid(gr) * (rnn @ w_rnn_out)
    x = x + merged @ w_out
    x = x + moe(rms_norm(x, norm2_g), w_router, b_router, w_gate_up, b_gate_up, w_down, b_down)
    return x, (k, v, ik, conv_new, h_new)


def setup_inputs(seed: int = 0) -> dict:
    key = jax.random.key(seed)
    ks = iter(jax.random.split(key, 40))
    nrm = lambda shape, scale: jax.random.normal(next(ks), shape, jnp.float32) * scale
    n_pages = PAST_LEN // PAGE_SIZE
    n_pool = (DEC_BATCH * n_pages * 5) // 4
    page_table = jax.random.permutation(next(ks), n_pool)[:DEC_BATCH * n_pages].reshape(DEC_BATCH, n_pages).astype(jnp.int32)
    lam_u = jax.random.uniform(next(ks), (DEPTH, D_RNN), jnp.float32, 0.9, 0.999)
    return {
        "x_prompt": nrm((BATCH, SEQ, D_MODEL), 1.0),
        "x_sample": nrm((DEC_BATCH, DEC_SEQ, D_MODEL), 1.0),
        "cache_k": nrm((DEPTH, n_pool, PAGE_SIZE, KV_HEADS, HEAD_DIM), 1.0),
        "cache_v": nrm((DEPTH, n_pool, PAGE_SIZE, KV_HEADS, HEAD_DIM), 1.0),
        "cache_idx_k": nrm((DEPTH, n_pool, PAGE_SIZE, IDX_DIM), 1.0),
        "state_conv": nrm((DEPTH, DEC_BATCH, CONV_W - 1, D_RNN), 1.0),
        "state_h": nrm((DEPTH, DEC_BATCH, D_RNN), 0.5),
        "page_table": page_table,
        "norm1_g": 1.0 + nrm((DEPTH, D_MODEL), 0.02),
        "w_in": nrm((DEPTH, D_MODEL, D_IN), D_MODEL ** -0.5),
        "q_norm_g": 1.0 + nrm((DEPTH, HEAD_DIM), 0.02),
        "k_norm_g": 1.0 + nrm((DEPTH, HEAD_DIM), 0.02),
        "conv_w": nrm((DEPTH, CONV_W, D_RNN), CONV_W ** -0.5),
        "conv_b": nrm((DEPTH, D_RNN), 0.01),
        "rg_w_a": nrm((DEPTH, RG_BLOCKS, RG_BLOCK_W, RG_BLOCK_W), RG_BLOCK_W ** -0.5),
        "rg_b_a": nrm((DEPTH, D_RNN), 0.01),
        "rg_w_i": nrm((DEPTH, RG_BLOCKS, RG_BLOCK_W, RG_BLOCK_W), RG_BLOCK_W ** -0.5),
        "rg_b_i": nrm((DEPTH, D_RNN), 0.01),
        "rg_lambda": jnp.log(lam_u) - jnp.log1p(-lam_u),
        "w_attn_out": nrm((DEPTH, Q_W, D_MODEL), Q_W ** -0.5),
        "w_rnn_out": nrm((DEPTH, D_RNN, D_MODEL), D_RNN ** -0.5),
        "w_out": nrm((DEPTH, D_MODEL, D_MODEL), D_MODEL ** -0.5),
        "norm2_g": 1.0 + nrm((DEPTH, D_MODEL), 0.02),
        "w_router": nrm((DEPTH, D_MODEL, N_EXPERTS), D_MODEL ** -0.5),
        "b_router": nrm((DEPTH, N_EXPERTS), 0.01),
        "w_gate_up": nrm((DEPTH, N_EXPERTS, D_MODEL, 2 * D_FF), D_MODEL ** -0.5),
        "b_gate_up": nrm((DEPTH, N_EXPERTS, 2 * D_FF), 0.01),
        "w_down": nrm((DEPTH, N_EXPERTS, D_FF, D_MODEL), D_FF ** -0.5),
        "b_down": nrm((DEPTH, N_EXPERTS, D_MODEL), 0.01),
    }


def reference(x_prompt, x_sample, cache_k, cache_v, cache_idx_k, state_conv, state_h, page_table,
              norm1_g, w_in, q_norm_g, k_norm_g, conv_w, conv_b, rg_w_a, rg_b_a, rg_w_i, rg_b_i, rg_lambda,
              w_attn_out, w_rnn_out, w_out, norm2_g, w_router, b_router, w_gate_up, b_gate_up, w_down, b_down):
    weights = (norm1_g, w_in, q_norm_g, k_norm_g, conv_w, conv_b, rg_w_a, rg_b_a, rg_w_i, rg_b_i, rg_lambda,
               w_attn_out, w_rnn_out, w_out, norm2_g, w_router, b_router, w_gate_up, b_gate_up, w_down, b_down)
    B, S = x_prompt.shape[:2]
    T = x_sample.shape[1]
    past = page_table.shape[1] * PAGE_SIZE
    pos_p = jnp.arange(S, dtype=jnp.int32)
    pos_s = past + jnp.arange(T, dtype=jnp.int32)
    conv0 = jnp.zeros((B, CONV_W - 1, D_RNN), x_prompt.dtype)
    h0 = jnp.zeros((B, D_RNN), jnp.float32)
    yp, ys = x_prompt, x_sample
    new_p, new_s = [], []
    for layer in range(DEPTH):
        wl = [w[layer] for w in weights]
        yp, st_p = trunk_layer(yp, pos_p, prompt_sparse_attention, conv0, h0, *wl)
        attend_s = functools.partial(sample_sparse_attention, cache_k=cache_k[layer], cache_v=cache_v[layer],
                                     cache_idx_k=cache_idx_k[layer], page_table=page_table)
        ys, st_s = trunk_layer(ys, pos_s, attend_s, state_conv[layer], state_h[layer], *wl)
        new_p.append(st_p)
        new_s.append(st_s)
    k_p, v_p, ik_p, conv_p, h_p = [jnp.stack(a) for a in zip(*new_p)]
    k_s, v_s, ik_s, conv_s, h_s = [jnp.stack(a) for a in zip(*new_s)]
    return (yp, ys, k_p, v_p, ik_p, conv_p, h_p, k_s, v_s, ik_s, conv_s, h_s)
```

```python
import functools
import math

import jax
import jax.numpy as jnp
import numpy as np
from jax import lax
from jax.experimental import pallas as pl
from jax.experimental.pallas import tpu as pltpu

D_MODEL = 1024
PAGE_SIZE = 128
N_HEADS = 16
HEAD_DIM = 64
KV_HEADS = 4
IDX_HEADS = 8
IDX_DIM = 64
IDX_TOPK_MAX = 256
Q_BLOCK = 128
ROPE_THETA = 10000.0
D_RNN = D_MODEL
RG_BLOCKS = 4
RG_BLOCK_W = D_RNN // RG_BLOCKS
CONV_W = 4
RG_C = 8.0
N_EXPERTS = 32
TOP_K = 4
D_FF = D_MODEL
SWIGLU_LIMIT = 7.0
SWIGLU_ALPHA = 1.702
MOE_BLOCK = 128
EPS = 1e-6

Q_W = N_HEADS * HEAD_DIM
KV_W = KV_HEADS * HEAD_DIM
IQ_W = IDX_HEADS * IDX_DIM
IN_WIDTHS = (Q_W, KV_W, KV_W, IQ_W, IDX_DIM, IDX_HEADS, D_RNN, D_RNN, D_MODEL, D_MODEL)
D_IN = sum(IN_WIDTHS)

LANE = 128


def _norm_proj_kernel(x_ref, g_ref, w_ref, o_ref):
    x = x_ref[...]
    y = x * lax.rsqrt(jnp.mean(x * x, axis=-1, keepdims=True) + EPS) * g_ref[...]
    o_ref[...] = jnp.dot(y.astype(jnp.bfloat16), w_ref[...], preferred_element_type=jnp.float32)


def _norm_proj(x2d, g, w_bf16, tm=512, tn=896):
    m, d = x2d.shape
    n = w_bf16.shape[1]
    return pl.pallas_call(
        _norm_proj_kernel,
        grid=(m // tm, n // tn),
        in_specs=[
            pl.BlockSpec((tm, d), lambda i, j: (i, 0)),
            pl.BlockSpec((1, d), lambda i, j: (0, 0)),
            pl.BlockSpec((d, tn), lambda i, j: (0, j)),
        ],
        out_specs=pl.BlockSpec((tm, tn), lambda i, j: (i, j)),
        out_shape=jax.ShapeDtypeStruct((m, n), jnp.float32),
        name="norm_proj",
    )(x2d, g.reshape(1, d), w_bf16)


def _rms_norm(x, g):
    xf = x.astype(jnp.float32)
    y = xf * lax.rsqrt(jnp.mean(xf * xf, axis=-1, keepdims=True) + EPS)
    return (y * g.astype(jnp.float32)).astype(x.dtype)


def _rope(x, pos):
    half = x.shape[-1] // 2
    inv_freq = ROPE_THETA ** (-jnp.arange(half, dtype=jnp.float32) / half)
    ang = pos.astype(jnp.float32)[:, None] * inv_freq[None, :]
    cos = jnp.cos(ang)[:, None, :]
    sin = jnp.sin(ang)[:, None, :]
    xf = x.astype(jnp.float32)
    x1, x2 = xf[..., :half], xf[..., half:]
    return jnp.concatenate([x1 * cos - x2 * sin, x2 * cos + x1 * sin], axis=-1).astype(x.dtype)


def _index_scores(iq, iw, ik):
    s = jnp.einsum('bthd,bld->bthl', iq, ik, preferred_element_type=jnp.float32)
    return jnp.einsum('bthl,bth->btl', jax.nn.relu(s), iw.astype(jnp.float32))


def _sparse_attend(q, k_sel, v_sel, valid):
    B, T = q.shape[:2]
    qg = q.reshape(B, T, KV_HEADS, N_HEADS // KV_HEADS, HEAD_DIM)
    logits = jnp.einsum('btkgd,btskd->btkgs', qg, k_sel, preferred_element_type=jnp.float32) * (HEAD_DIM ** -0.5)
    logits = jnp.where(valid[:, :, None, None, :], logits, -jnp.inf)
    p = jax.nn.softmax(logits, axis=-1)
    out = jnp.einsum('btkgs,btskd->btkgd', p.astype(v_sel.dtype), v_sel)
    return out.reshape(B, T, Q_W)


def _gather_rows(rows, idx):
    return jax.vmap(lambda r, i: r[i])(rows, idx)


def _prompt_sparse_attention(q, k, v, iq, iw, ik):
    B, S = q.shape[:2]
    topk = min(IDX_TOPK_MAX, S // 4)
    n_blocks = S // Q_BLOCK
    key_pos = jnp.arange(S, dtype=jnp.int32)

    def block(args):
        qb, iqb, iwb, start = args
        qpos = start + jnp.arange(Q_BLOCK, dtype=jnp.int32)
        sc = _index_scores(iqb, iwb, ik)
        sc = jnp.where((key_pos[None, :] <= qpos[:, None])[None], sc, -jnp.inf)
        _, sel = lax.top_k(sc, topk)
        valid = sel <= qpos[None, :, None]
        return _sparse_attend(qb, _gather_rows(k, sel), _gather_rows(v, sel), valid)

    to_blocks = lambda a: jnp.moveaxis(a.reshape(B, n_blocks, Q_BLOCK, *a.shape[2:]), 1, 0)
    starts = jnp.arange(n_blocks, dtype=jnp.int32) * Q_BLOCK
    out = lax.map(block, (to_blocks(q), to_blocks(iq), to_blocks(iw), starts))
    return jnp.moveaxis(out, 0, 1).reshape(B, S, Q_W)


def _sample_sparse_attention(q, k, v, iq, iw, ik, cache_k, cache_v, cache_idx_k, page_table):
    DB, T = q.shape[:2]
    past = page_table.shape[1] * PAGE_SIZE
    L = past + T
    topk = min(IDX_TOPK_MAX, L // 4)
    ik_past = cache_idx_k[page_table].reshape(DB, past, IDX_DIM)
    ik_all = jnp.concatenate([ik_past, ik.astype(ik_past.dtype)], axis=1)
    qpos = past + jnp.arange(T, dtype=jnp.int32)
    sc = _index_scores(iq, iw, ik_all)
    sc = jnp.where((jnp.arange(L, dtype=jnp.int32)[None, :] <= qpos[:, None])[None], sc, -jnp.inf)
    _, sel = lax.top_k(sc, topk)
    valid = sel <= qpos[None, :, None]
    in_past = sel < past
    sel_p = jnp.minimum(sel, past - 1)
    page = _gather_rows(page_table, sel_p // PAGE_SIZE)
    phys = page * PAGE_SIZE + sel_p % PAGE_SIZE
    new_idx = jnp.clip(sel - past, 0, T - 1)
    k_pool = cache_k.reshape(-1, KV_HEADS, HEAD_DIM)
    v_pool = cache_v.reshape(-1, KV_HEADS, HEAD_DIM)
    m = in_past[..., None, None]
    k_sel = jnp.where(m, k_pool[phys], _gather_rows(k, new_idx).astype(k_pool.dtype))
    v_sel = jnp.where(m, v_pool[phys], _gather_rows(v, new_idx).astype(v_pool.dtype))
    return _sparse_attend(q, k_sel, v_sel, valid)


def _causal_conv(xr, conv_state, w, b):
    T = xr.shape[1]
    xp = jnp.concatenate([conv_state.astype(xr.dtype), xr], axis=1)
    out = b + sum(xp[:, j:j + T] * w[j] for j in range(CONV_W))
    return out, xp[:, -(CONV_W - 1):]


def _block_diag(x, w, b):
    B, T, _ = x.shape
    y = jnp.einsum('btnc,ncd->btnd', x.reshape(B, T, RG_BLOCKS, RG_BLOCK_W), w)
    return y.reshape(B, T, D_RNN) + b


def _rg_lru(x, h0, w_a, b_a, w_i, b_i, lam):
    xf = x.astype(jnp.float32)
    r = jax.nn.sigmoid(_block_diag(x, w_a, b_a).astype(jnp.float32))
    i = jax.nn.sigmoid(_block_diag(x, w_i, b_i).astype(jnp.float32))
    log_a = RG_C * r * jax.nn.log_sigmoid(lam.astype(jnp.float32))
    a = jnp.exp(log_a)
    u = jnp.sqrt(-jnp.expm1(2.0 * log_a)) * (i * xf)

    def step(h, au):
        h = au[0] * h + au[1]
        return h, h

    h_last, hs = lax.scan(step, h0.astype(jnp.float32), (jnp.moveaxis(a, 1, 0), jnp.moveaxis(u, 1, 0)))
    return jnp.moveaxis(hs, 0, 1).astype(x.dtype), h_last.astype(x.dtype)


def _moe(h, w_router, b_router, w_gate_up, b_gate_up, w_down, b_down):
    shp = h.shape
    xt = h.reshape(-1, shp[-1])
    n = xt.shape[0]
    logits = (xt @ w_router + b_router).astype(jnp.float32)
    top_v, top_e = lax.top_k(logits, TOP_K)
    gates = jax.nn.softmax(top_v, axis=-1)
    n_assign = n * TOP_K
    flat_e = top_e.reshape(-1)
    order = jnp.argsort(flat_e, stable=True)
    sorted_e = flat_e[order]
    counts = jnp.bincount(flat_e, length=N_EXPERTS)
    padded = (counts + MOE_BLOCK - 1) // MOE_BLOCK * MOE_BLOCK
    pad_end = jnp.cumsum(padded)
    pad_start = pad_end - padded
    grp_start = jnp.cumsum(counts) - counts
    dest_sorted = pad_start[sorted_e] + jnp.arange(n_assign, dtype=jnp.int32) - grp_start[sorted_e]
    dest = jnp.zeros((n_assign,), jnp.int32).at[order].set(dest_sorted.astype(jnp.int32))
    n_blocks = -(-(n_assign + N_EXPERTS * (MOE_BLOCK - 1)) // MOE_BLOCK)
    n_slots = n_blocks * MOE_BLOCK
    slot_tok = jnp.full((n_slots,), n, jnp.int32).at[dest].set(jnp.arange(n_assign, dtype=jnp.int32) // TOP_K)
    blk_exp = jnp.minimum(jnp.searchsorted(pad_end, jnp.arange(n_blocks, dtype=jnp.int32) * MOE_BLOCK, side='right'), N_EXPERTS - 1)
    x_pad = jnp.concatenate([xt, jnp.zeros((1, xt.shape[1]), xt.dtype)], axis=0)
    xs = x_pad[slot_tok].reshape(n_blocks, MOE_BLOCK, -1)

    def expert_block(args):
        xb, e = args
        gu = xb @ w_gate_up[e] + b_gate_up[e]
        g = jnp.minimum(gu[:, :D_FF], SWIGLU_LIMIT)
        u = jnp.clip(gu[:, D_FF:], -SWIGLU_LIMIT, SWIGLU_LIMIT)
        act = (u + 1.0) * (g * jax.nn.sigmoid(SWIGLU_ALPHA * g))
        return act @ w_down[e] + b_down[e]

    ys = lax.map(expert_block, (xs, blk_exp)).reshape(n_slots, -1)
    y = jnp.einsum('tkd,tk->td', ys[dest].reshape(n, TOP_K, -1), gates.astype(ys.dtype))
    return y.reshape(shp)


def _trunk_layer(x, pos, attend, conv_state, h0, norm1_g, w_in, q_norm_g, k_norm_g, conv_w, conv_b,
                 rg_w_a, rg_b_a, rg_w_i, rg_b_i, rg_lambda, w_attn_out, w_rnn_out, w_out, norm2_g,
                 w_router, b_router, w_gate_up, b_gate_up, w_down, b_down):
    B, T, _ = x.shape
    n_pad = (-D_IN) % 896
    w_pad = jnp.pad(w_in, ((0, 0), (0, n_pad))).astype(jnp.bfloat16)
    z = _norm_proj(x.reshape(B * T, D_MODEL), norm1_g, w_pad)[:, :D_IN].reshape(B, T, D_IN)
    q, k, v, iq, ik, iw, xr, xg, ga, gr = jnp.split(z, np.cumsum(IN_WIDTHS)[:-1].tolist(), axis=-1)
    q = _rope(_rms_norm(q.reshape(B, T, N_HEADS, HEAD_DIM), q_norm_g), pos)
    k = _rope(_rms_norm(k.reshape(B, T, KV_HEADS, HEAD_DIM), k_norm_g), pos)
    v = v.reshape(B, T, KV_HEADS, HEAD_DIM)
    iq = _rope(iq.reshape(B, T, IDX_HEADS, IDX_DIM), pos)
    ik = _rope(ik[:, :, None, :], pos)[:, :, 0]
    iw = iw * (IDX_HEADS ** -0.5 * IDX_DIM ** -0.5)
    attn = attend(q, k, v, iq, iw, ik)
    conv_out, conv_new = _causal_conv(xr, conv_state, conv_w, conv_b)
    rnn, h_new = _rg_lru(conv_out, h0, rg_w_a, rg_b_a, rg_w_i, rg_b_i, rg_lambda)
    rnn = rnn * jax.nn.gelu(xg)
    merged = jax.nn.sigmoid(ga) * (attn @ w_attn_out) + jax.nn.sigmoid(gr) * (rnn @ w_rnn_out)
    x = x + merged @ w_out
    x = x + _moe(_rms_norm(x, norm2_g), w_router, b_router, w_gate_up, b_gate_up, w_down, b_down)
    return x, (k, v, ik, conv_new, h_new)


def kernel(x_prompt, x_sample, cache_k, cache_v, cache_idx_k, state_conv, state_h, page_table,
           norm1_g, w_in, q_norm_g, k_norm_g, conv_w, conv_b, rg_w_a, rg_b_a, rg_w_i, rg_b_i, rg_lambda,
           w_attn_out, w_rnn_out, w_out, norm2_g, w_router, b_router, w_gate_up, b_gate_up, w_down, b_down):
    weights = (norm1_g, w_in, q_norm_g, k_norm_g, conv_w, conv_b, rg_w_a, rg_b_a, rg_w_i, rg_b_i, rg_lambda,
               w_attn_out, w_rnn_out, w_out, norm2_g, w_router, b_router, w_gate_up, b_gate_up, w_down, b_down)
    B, S = x_prompt.shape[:2]
    T = x_sample.shape[1]
    past = page_table.shape[1] * PAGE_SIZE
    pos_p = jnp.arange(S, dtype=jnp.int32)
    pos_s = past + jnp.arange(T, dtype=jnp.int32)
    conv0 = jnp.zeros((B, CONV_W - 1, D_RNN), x_prompt.dtype)
    h0 = jnp.zeros((B, D_RNN), jnp.float32)
    wl = [w[0] for w in weights]
    yp, st_p = _trunk_layer(x_prompt, pos_p, _prompt_sparse_attention, conv0, h0, *wl)
    attend_s = functools.partial(_sample_sparse_attention, cache_k=cache_k[0], cache_v=cache_v[0],
                                 cache_idx_k=cache_idx_k[0], page_table=page_table)
    ys, st_s = _trunk_layer(x_sample, pos_s, attend_s, state_conv[0], state_h[0], *wl)
    k_p, v_p, ik_p, conv_p, h_p = [a[None] for a in st_p]
    k_s, v_s, ik_s, conv_s, h_s = [a[None] for a in st_s]
    return (yp, ys, k_p, v_p, ik_p, conv_p, h_p, k_s, v_s, ik_s, conv_s, h_s)
```

```python
import functools
import math

import jax
import jax.numpy as jnp
import numpy as np
from jax import lax
from jax.experimental import pallas as pl
from jax.experimental.pallas import tpu as pltpu

D_MODEL = 1024
PAGE_SIZE = 128
N_HEADS = 16
HEAD_DIM = 64
KV_HEADS = 4
IDX_HEADS = 8
IDX_DIM = 64
IDX_TOPK_MAX = 256
Q_BLOCK = 128
ROPE_THETA = 10000.0
D_RNN = D_MODEL
RG_BLOCKS = 4
RG_BLOCK_W = D_RNN // RG_BLOCKS
CONV_W = 4
RG_C = 8.0
N_EXPERTS = 32
TOP_K = 4
D_FF = D_MODEL
SWIGLU_LIMIT = 7.0
SWIGLU_ALPHA = 1.702
MOE_BLOCK = 128
EPS = 1e-6

Q_W = N_HEADS * HEAD_DIM
KV_W = KV_HEADS * HEAD_DIM
IQ_W = IDX_HEADS * IDX_DIM
IN_WIDTHS = (Q_W, KV_W, KV_W, IQ_W, IDX_DIM, IDX_HEADS, D_RNN, D_RNN, D_MODEL, D_MODEL)
D_IN = sum(IN_WIDTHS)

LANE = 128
VMEM_LIMIT_BYTES = 48 * 1024 * 1024


def _norm_proj_kernel(x_ref, g_ref, w_ref, o_ref):
    x = x_ref[...]
    y = x * lax.rsqrt(jnp.mean(x * x, axis=-1, keepdims=True) + EPS) * g_ref[...]
    o_ref[...] = jnp.dot(y.astype(jnp.bfloat16), w_ref[...], preferred_element_type=jnp.float32)


def _norm_proj(x2d, g, w_bf16, tm=512, tn=896):
    m, d = x2d.shape
    n = w_bf16.shape[1]
    return pl.pallas_call(
        _norm_proj_kernel,
        grid=(m // tm, n // tn),
        in_specs=[
            pl.BlockSpec((tm, d), lambda i, j: (i, 0)),
            pl.BlockSpec((1, d), lambda i, j: (0, 0)),
            pl.BlockSpec((d, tn), lambda i, j: (0, j)),
        ],
        out_specs=pl.BlockSpec((tm, tn), lambda i, j: (i, j)),
        out_shape=jax.ShapeDtypeStruct((m, n), jnp.float32),
        name="norm_proj",
    )(x2d, g.reshape(1, d), w_bf16)


INT_MIN = -(2 ** 31)
INT_MAX = 2 ** 31 - 1
MASK_BIAS = -1e30
NEG_INF_BITS_MASK = 0x7FFFFFFF


def _ordered_key(x):
    bits = pltpu.bitcast(x, jnp.int32)
    return bits ^ ((bits >> 31) & NEG_INF_BITS_MASK)


def _dsa_kernel(iq_ref, iw_ref, q_ref, ikT_ref, kT_ref, v_ref, o_ref,
                key_ref, wb_ref, cut_ref, m_ref, l_ref, acc_ref, *, qb, kb, topk, n_idx_bits, q_pos0, n_chunks_of):
    i = pl.program_id(0)
    nk = n_chunks_of(i)
    n_lane_tiles = kb // LANE
    row = lax.broadcasted_iota(jnp.int32, (qb, kb), 0)
    lane = lax.broadcasted_iota(jnp.int32, (qb, kb), 1)
    qpos = q_pos0 + i * qb + row

    iw = iw_ref[...]
    for h in range(IDX_HEADS):
        wb_ref[h] = jnp.broadcast_to(iw[:, h:h + 1], (qb, LANE))

    def score_chunk(c, carry):
        ikc = ikT_ref[c]
        acc = jnp.zeros((qb, kb), jnp.float32)
        for h in range(IDX_HEADS):
            s = jnp.dot(iq_ref[0, h], ikc, preferred_element_type=jnp.float32)
            acc = acc + jnp.tile(wb_ref[h], (1, n_lane_tiles)) * jnp.maximum(s, 0.0)
        acc = jnp.where(c * kb + lane <= qpos, acc, -jnp.inf)
        key_ref[c] = _ordered_key(acc)
        return carry

    lax.fori_loop(0, nk, score_chunk, 0)

    def count(pred_of_chunk):
        def body(c, cnt):
            p = pred_of_chunk(c, key_ref[c])
            for j in range(n_lane_tiles):
                cnt = cnt + jnp.where(p[:, j * LANE:(j + 1) * LANE], 1, 0)
            return cnt
        cnt = lax.fori_loop(0, nk, body, jnp.zeros((qb, LANE), jnp.int32))
        return jnp.sum(cnt, axis=-1, keepdims=True)

    def bit_step(b, carry):
        thr, n_ge = carry
        cand = thr ^ (jnp.int32(1) << (31 - b))
        cand_b = jnp.broadcast_to(cand, (qb, kb))
        cnt = count(lambda c, key: key >= cand_b)
        ok = cnt >= topk
        return jnp.where(ok, cand, thr), jnp.where(ok, cnt, n_ge)

    thr0 = jnp.full((qb, 1), INT_MIN, jnp.int32)
    n0 = jnp.full((qb, 1), 0, jnp.int32) + nk * kb
    thr, n_ge = lax.fori_loop(0, 32, bit_step, (thr0, n0))
    thr_b = jnp.broadcast_to(thr, (qb, kb))

    cut_ref[...] = jnp.full((qb, LANE), INT_MAX, jnp.int32)

    @pl.when(jnp.max(n_ge) > topk)
    def _():
        n_eq = count(lambda c, key: key == thr_b)
        need = topk - (n_ge - n_eq)

        def idx_step(b, lo):
            step = jnp.int32(1) << (n_idx_bits - 1 - b)
            mid_b = jnp.broadcast_to(lo + step - 1, (qb, kb))
            f = count(lambda c, key: (key == thr_b) & (c * kb + lane <= mid_b))
            return jnp.where(f < need, lo + step, lo)

        lo = lax.fori_loop(0, n_idx_bits, idx_step, jnp.zeros((qb, 1), jnp.int32))
        cut = jnp.where(n_ge > topk, lo, INT_MAX)
        cut_ref[...] = jnp.broadcast_to(cut, (qb, LANE))

    m_ref[...] = jnp.full(m_ref.shape, MASK_BIAS, jnp.float32)
    l_ref[...] = jnp.zeros(l_ref.shape, jnp.float32)
    acc_ref[...] = jnp.zeros(acc_ref.shape, jnp.float32)
    cut_b = jnp.tile(cut_ref[...], (1, n_lane_tiles))
    rep = N_HEADS // KV_HEADS

    def attend_chunk(c, carry):
        key = key_ref[c]
        kpos = c * kb + lane
        sel = (key > thr_b) | ((key == thr_b) & (kpos <= cut_b))
        sel = sel & (kpos <= qpos)
        bias = jnp.where(sel, 0.0, MASK_BIAS)
        kc = kT_ref[c]
        for g in range(KV_HEADS):
            s = jnp.dot(q_ref[0, g], kc[g * HEAD_DIM:(g + 1) * HEAD_DIM, :], preferred_element_type=jnp.float32)
            s = (s.reshape(rep, qb, kb) + bias[None]).reshape(rep * qb, kb)
            m_prev = m_ref[g]
            m_new = jnp.maximum(m_prev, jnp.max(s, axis=-1, keepdims=True))
            alpha = jnp.exp(m_prev - m_new)
            p = jnp.exp(s - jnp.tile(m_new, (1, n_lane_tiles)))
            l_ref[g] = alpha * l_ref[g] + jnp.sum(p, axis=-1, keepdims=True)
            pv = jnp.dot(p.astype(jnp.bfloat16), v_ref[c, g], preferred_element_type=jnp.float32)
            acc_ref[g] = acc_ref[g] * alpha[:, :HEAD_DIM] + pv
            m_ref[g] = m_new
        return carry

    lax.fori_loop(0, nk, attend_chunk, 0)
    for g in range(KV_HEADS):
        o_ref[0, g] = acc_ref[g] / l_ref[g][:, :HEAD_DIM]


def _prompt_dsa_attention(q, k, v, iq, iw, ik, *, qb=128, kb=512):
    s_len = q.shape[0]
    topk = min(IDX_TOPK_MAX, s_len // 4)
    nqb, nkc = s_len // qb, s_len // kb
    rep = N_HEADS // KV_HEADS
    bf = jnp.bfloat16
    scale = HEAD_DIM ** -0.5
    q_t = (q * scale).astype(bf).reshape(nqb, qb, KV_HEADS, rep, HEAD_DIM).transpose(0, 2, 3, 1, 4)
    q_t = q_t.reshape(nqb, KV_HEADS, rep * qb, HEAD_DIM)
    iq_t = iq.astype(bf).reshape(nqb, qb, IDX_HEADS, IDX_DIM).transpose(0, 2, 1, 3)
    ikT = ik.astype(bf).reshape(nkc, kb, IDX_DIM).transpose(0, 2, 1)
    kT = k.astype(bf).reshape(nkc, kb, KV_HEADS * HEAD_DIM).transpose(0, 2, 1)
    v_t = v.astype(bf).reshape(nkc, kb, KV_HEADS, HEAD_DIM).transpose(0, 2, 1, 3)
    n_idx_bits = max(1, (s_len - 1).bit_length())
    body = functools.partial(
        _dsa_kernel, qb=qb, kb=kb, topk=topk, n_idx_bits=n_idx_bits, q_pos0=0,
        n_chunks_of=lambda i: ((i + 1) * qb + kb - 1) // kb)
    whole = lambda shape: pl.BlockSpec(shape, lambda i: (0,) * len(shape), pipeline_mode=pl.Buffered(1))
    out = pl.pallas_call(
        body,
        grid=(nqb,),
        in_specs=[
            pl.BlockSpec((1, IDX_HEADS, qb, IDX_DIM), lambda i: (i, 0, 0, 0)),
            pl.BlockSpec((qb, IDX_HEADS), lambda i: (i, 0)),
            pl.BlockSpec((1, KV_HEADS, rep * qb, HEAD_DIM), lambda i: (i, 0, 0, 0)),
            whole((nkc, IDX_DIM, kb)),
            whole((nkc, KV_HEADS * HEAD_DIM, kb)),
            whole((nkc, KV_HEADS, kb, HEAD_DIM)),
        ],
        out_specs=pl.BlockSpec((1, KV_HEADS, rep * qb, HEAD_DIM), lambda i: (i, 0, 0, 0)),
        out_shape=jax.ShapeDtypeStruct((nqb, KV_HEADS, rep * qb, HEAD_DIM), jnp.float32),
        scratch_shapes=[
            pltpu.VMEM((nkc, qb, kb), jnp.int32),
            pltpu.VMEM((IDX_HEADS, qb, LANE), jnp.float32),
            pltpu.VMEM((qb, LANE), jnp.int32),
            pltpu.VMEM((KV_HEADS, rep * qb, LANE), jnp.float32),
            pltpu.VMEM((KV_HEADS, rep * qb, LANE), jnp.float32),
            pltpu.VMEM((KV_HEADS, rep * qb, HEAD_DIM), jnp.float32),
        ],
        compiler_params=pltpu.CompilerParams(dimension_semantics=("arbitrary",), vmem_limit_bytes=VMEM_LIMIT_BYTES),
        name="prompt_dsa_attention",
    )(iq_t, iw, q_t, ikT, kT, v_t)
    out = out.reshape(nqb, KV_HEADS, rep, qb, HEAD_DIM).transpose(0, 3, 1, 2, 4)
    return out.reshape(s_len, N_HEADS * HEAD_DIM)


SUBLANE = 8
GELU_C = math.sqrt(2.0 / math.pi)


def _rglru_kernel(x_ref, xg_ref, prev_ref, h0_ref, cw_ref, cb_ref, wa_ref, ba_ref, wi_ref, bi_ref, lam_ref,
                  o_ref, hl_ref, conv_ref, a_ref, u_ref, xprev_ref, h_ref, *, rows, per_group_state):
    i = pl.program_id(0)
    n_groups = rows // SUBLANE
    d_rnn = x_ref.shape[1]
    r8 = lax.broadcasted_iota(jnp.int32, (SUBLANE, d_rnn), 0)
    grp = lambda g: pl.ds(pl.multiple_of(g * SUBLANE, SUBLANE), SUBLANE)

    if not per_group_state:
        @pl.when(i == 0)
        def _():
            xprev_ref[...] = jnp.zeros_like(xprev_ref)
            h_ref[...] = jnp.zeros_like(h_ref)

    cw = cw_ref[...]
    cb = cb_ref[...]

    def conv_group(g, prev):
        x8 = x_ref[grp(g), :]
        if per_group_state:
            prev = prev_ref[grp(g), :]
        out = cb + x8 * cw[CONV_W - 1:CONV_W, :]
        for d in range(1, CONV_W):
            shifted = jnp.where(r8 < d, pltpu.roll(prev, d, axis=0), pltpu.roll(x8, d, axis=0))
            out = out + shifted * cw[CONV_W - 1 - d:CONV_W - d, :]
        conv_ref[grp(g), :] = out
        return x8

    zeros8 = jnp.zeros((SUBLANE, d_rnn), jnp.float32)
    xlast = lax.fori_loop(0, n_groups, conv_group, zeros8 if per_group_state else xprev_ref[...])
    if not per_group_state:
        xprev_ref[...] = xlast

    xc = conv_ref[...]
    xb = xc.astype(jnp.bfloat16)
    r_parts, i_parts = [], []
    for n in range(RG_BLOCKS):
        xn = xb[:, n * RG_BLOCK_W:(n + 1) * RG_BLOCK_W]
        r_parts.append(jnp.dot(xn, wa_ref[n], preferred_element_type=jnp.float32))
        i_parts.append(jnp.dot(xn, wi_ref[n], preferred_element_type=jnp.float32))
    r = jax.nn.sigmoid(jnp.concatenate(r_parts, axis=-1) + ba_ref[...])
    ig = jax.nn.sigmoid(jnp.concatenate(i_parts, axis=-1) + bi_ref[...])
    lam = lam_ref[...]
    log_sig_lam = -(jnp.maximum(-lam, 0.0) + jnp.log1p(jnp.exp(-jnp.abs(lam))))
    log_a = RG_C * r * log_sig_lam
    a = jnp.exp(log_a)
    a_ref[...] = a
    u_ref[...] = jnp.sqrt(-jnp.tanh(log_a) * (a * a + 1.0)) * (ig * xc)

    def scan_group(g, hprev):
        a = a_ref[grp(g), :]
        u = u_ref[grp(g), :]
        if per_group_state:
            hprev = jnp.broadcast_to(h0_ref[pl.ds(g, 1), :], (SUBLANE, d_rnn))
        for d in (1, 2, 4):
            u = jnp.where(r8 >= d, a * pltpu.roll(u, d, axis=0) + u, u)
            a = jnp.where(r8 >= d, a * pltpu.roll(a, d, axis=0), a)
        h = a * hprev + u
        u_ref[grp(g), :] = h
        hlast = jnp.broadcast_to(h[SUBLANE - 1:SUBLANE, :], (SUBLANE, d_rnn))
        if per_group_state:
            hl_ref[pl.ds(g, 1), :] = h[SUBLANE - 1:SUBLANE, :]
        return hlast

    hlast = lax.fori_loop(0, n_groups, scan_group, zeros8 if per_group_state else h_ref[...])
    if not per_group_state:
        h_ref[...] = hlast
        hl_ref[...] = hlast

    xg = xg_ref[...]
    gelu = 0.5 * xg * (1.0 + jnp.tanh(GELU_C * (xg + 0.044715 * (xg * xg * xg))))
    o_ref[...] = u_ref[...] * gelu


def _rglru(x, xg, prev, h0, conv_w, conv_b, w_a, b_a, w_i, b_i, lam, *, per_group_state, tile_rows):
    n_rows, d = x.shape
    row = lambda a: a.reshape(1, d)
    const = lambda shape: pl.BlockSpec(shape, lambda i: (0,) * len(shape))
    n_hl = n_rows // SUBLANE if per_group_state else SUBLANE
    body = functools.partial(_rglru_kernel, rows=tile_rows, per_group_state=per_group_state)
    tile = pl.BlockSpec((tile_rows, d), lambda i: (i, 0))
    if per_group_state:
        assert tile_rows == n_rows
        prev_spec, h0_spec = const(prev.shape), const(h0.shape)
    else:
        prev_spec, h0_spec = const(prev.shape), const(h0.shape)
    return pl.pallas_call(
        body,
        grid=(n_rows // tile_rows,),
        in_specs=[tile, tile, prev_spec, h0_spec, const((CONV_W, d)), const((1, d)),
                  const(w_a.shape), const((1, d)), const(w_i.shape), const((1, d)), const((1, d))],
        out_specs=[tile, const((n_hl, d))],
        out_shape=[jax.ShapeDtypeStruct((n_rows, d), jnp.float32), jax.ShapeDtypeStruct((n_hl, d), jnp.float32)],
        scratch_shapes=[pltpu.VMEM((tile_rows, d), jnp.float32), pltpu.VMEM((tile_rows, d), jnp.float32),
                        pltpu.VMEM((tile_rows, d), jnp.float32), pltpu.VMEM((SUBLANE, d), jnp.float32),
                        pltpu.VMEM((SUBLANE, d), jnp.float32)],
        compiler_params=pltpu.CompilerParams(dimension_semantics=("arbitrary",), vmem_limit_bytes=VMEM_LIMIT_BYTES),
        name="rglru_sample" if per_group_state else "rglru_prompt",
    )(x, xg, prev, h0, conv_w, row(conv_b), w_a.astype(jnp.bfloat16), row(b_a), w_i.astype(jnp.bfloat16), row(b_i), row(lam))


def _rglru_prompt(x, xg, conv_w, conv_b, w_a, b_a, w_i, b_i, lam, tile_rows=512):
    dummy = jnp.zeros((SUBLANE, x.shape[1]), jnp.float32)
    rnn, hl = _rglru(x, xg, dummy, dummy, conv_w, conv_b, w_a, b_a, w_i, b_i, lam,
                     per_group_state=False, tile_rows=tile_rows)
    return rnn, hl[0]


def _rglru_sample(x, xg, state_conv, state_h, conv_w, conv_b, w_a, b_a, w_i, b_i, lam):
    b, t, d = x.shape
    assert t == SUBLANE
    prev = jnp.concatenate([jnp.zeros((b, SUBLANE - (CONV_W - 1), d), jnp.float32), state_conv], axis=1)
    rnn, hl = _rglru(x.reshape(b * t, d), xg.reshape(b * t, d), prev.reshape(b * t, d), state_h,
                     conv_w, conv_b, w_a, b_a, w_i, b_i, lam, per_group_state=True, tile_rows=b * t)
    return rnn.reshape(b, t, d), hl


def _rms_norm(x, g):
    xf = x.astype(jnp.float32)
    y = xf * lax.rsqrt(jnp.mean(xf * xf, axis=-1, keepdims=True) + EPS)
    return (y * g.astype(jnp.float32)).astype(x.dtype)


def _rope(x, pos):
    half = x.shape[-1] // 2
    inv_freq = ROPE_THETA ** (-jnp.arange(half, dtype=jnp.float32) / half)
    ang = pos.astype(jnp.float32)[:, None] * inv_freq[None, :]
    cos = jnp.cos(ang)[:, None, :]
    sin = jnp.sin(ang)[:, None, :]
    xf = x.astype(jnp.float32)
    x1, x2 = xf[..., :half], xf[..., half:]
    return jnp.concatenate([x1 * cos - x2 * sin, x2 * cos + x1 * sin], axis=-1).astype(x.dtype)


def _index_scores(iq, iw, ik):
    s = jnp.einsum('bthd,bld->bthl', iq, ik, preferred_element_type=jnp.float32)
    return jnp.einsum('bthl,bth->btl', jax.nn.relu(s), iw.astype(jnp.float32))


def _sparse_attend(q, k_sel, v_sel, valid):
    B, T = q.shape[:2]
    qg = q.reshape(B, T, KV_HEADS, N_HEADS // KV_HEADS, HEAD_DIM)
    logits = jnp.einsum('btkgd,btskd->btkgs', qg, k_sel, preferred_element_type=jnp.float32) * (HEAD_DIM ** -0.5)
    logits = jnp.where(valid[:, :, None, None, :], logits, -jnp.inf)
    p = jax.nn.softmax(logits, axis=-1)
    out = jnp.einsum('btkgs,btskd->btkgd', p.astype(v_sel.dtype), v_sel)
    return out.reshape(B, T, Q_W)


def _gather_rows(rows, idx):
    return jax.vmap(lambda r, i: r[i])(rows, idx)


def _prompt_sparse_attention(q, k, v, iq, iw, ik):
    B, S = q.shape[:2]
    topk = min(IDX_TOPK_MAX, S // 4)
    n_blocks = S // Q_BLOCK
    key_pos = jnp.arange(S, dtype=jnp.int32)

    def block(args):
        qb, iqb, iwb, start = args
        qpos = start + jnp.arange(Q_BLOCK, dtype=jnp.int32)
        sc = _index_scores(iqb, iwb, ik)
        sc = jnp.where((key_pos[None, :] <= qpos[:, None])[None], sc, -jnp.inf)
        _, sel = lax.top_k(sc, topk)
        valid = sel <= qpos[None, :, None]
        return _sparse_attend(qb, _gather_rows(k, sel), _gather_rows(v, sel), valid)

    to_blocks = lambda a: jnp.moveaxis(a.reshape(B, n_blocks, Q_BLOCK, *a.shape[2:]), 1, 0)
    starts = jnp.arange(n_blocks, dtype=jnp.int32) * Q_BLOCK
    out = lax.map(block, (to_blocks(q), to_blocks(iq), to_blocks(iw), starts))
    return jnp.moveaxis(out, 0, 1).reshape(B, S, Q_W)


def _sample_sparse_attention(q, k, v, iq, iw, ik, cache_k, cache_v, cache_idx_k, page_table):
    DB, T = q.shape[:2]
    past = page_table.shape[1] * PAGE_SIZE
    L = past + T
    topk = min(IDX_TOPK_MAX, L // 4)
    ik_past = cache_idx_k[page_table].reshape(DB, past, IDX_DIM)
    ik_all = jnp.concatenate([ik_past, ik.astype(ik_past.dtype)], axis=1)
    qpos = past + jnp.arange(T, dtype=jnp.int32)
    sc = _index_scores(iq, iw, ik_all)
    sc = jnp.where((jnp.arange(L, dtype=jnp.int32)[None, :] <= qpos[:, None])[None], sc, -jnp.inf)
    _, sel = lax.top_k(sc, topk)
    valid = sel <= qpos[None, :, None]
    in_past = sel < past
    sel_p = jnp.minimum(sel, past - 1)
    page = _gather_rows(page_table, sel_p // PAGE_SIZE)
    phys = page * PAGE_SIZE + sel_p % PAGE_SIZE
    new_idx = jnp.clip(sel - past, 0, T - 1)
    k_pool = cache_k.reshape(-1, KV_HEADS, HEAD_DIM)
    v_pool = cache_v.reshape(-1, KV_HEADS, HEAD_DIM)
    m = in_past[..., None, None]
    k_sel = jnp.where(m, k_pool[phys], _gather_rows(k, new_idx).astype(k_pool.dtype))
    v_sel = jnp.where(m, v_pool[phys], _gather_rows(v, new_idx).astype(v_pool.dtype))
    return _sparse_attend(q, k_sel, v_sel, valid)


def _causal_conv(xr, conv_state, w, b):
    T = xr.shape[1]
    xp = jnp.concatenate([conv_state.astype(xr.dtype), xr], axis=1)
    out = b + sum(xp[:, j:j + T] * w[j] for j in range(CONV_W))
    return out, xp[:, -(CONV_W - 1):]


def _block_diag(x, w, b):
    B, T, _ = x.shape
    y = jnp.einsum('btnc,ncd->btnd', x.reshape(B, T, RG_BLOCKS, RG_BLOCK_W), w)
    return y.reshape(B, T, D_RNN) + b


def _rg_lru(x, h0, w_a, b_a, w_i, b_i, lam):
    xf = x.astype(jnp.float32)
    r = jax.nn.sigmoid(_block_diag(x, w_a, b_a).astype(jnp.float32))
    i = jax.nn.sigmoid(_block_diag(x, w_i, b_i).astype(jnp.float32))
    log_a = RG_C * r * jax.nn.log_sigmoid(lam.astype(jnp.float32))
    a = jnp.exp(log_a)
    u = jnp.sqrt(-jnp.expm1(2.0 * log_a)) * (i * xf)

    def step(h, au):
        h = au[0] * h + au[1]
        return h, h

    h_last, hs = lax.scan(step, h0.astype(jnp.float32), (jnp.moveaxis(a, 1, 0), jnp.moveaxis(u, 1, 0)))
    return jnp.moveaxis(hs, 0, 1).astype(x.dtype), h_last.astype(x.dtype)


def _moe(h, w_router, b_router, w_gate_up, b_gate_up, w_down, b_down):
    shp = h.shape
    xt = h.reshape(-1, shp[-1])
    n = xt.shape[0]
    logits = (xt @ w_router + b_router).astype(jnp.float32)
    top_v, top_e = lax.top_k(logits, TOP_K)
    gates = jax.nn.softmax(top_v, axis=-1)
    n_assign = n * TOP_K
    flat_e = top_e.reshape(-1)
    order = jnp.argsort(flat_e, stable=True)
    sorted_e = flat_e[order]
    counts = jnp.bincount(flat_e, length=N_EXPERTS)
    padded = (counts + MOE_BLOCK - 1) // MOE_BLOCK * MOE_BLOCK
    pad_end = jnp.cumsum(padded)
    pad_start = pad_end - padded
    grp_start = jnp.cumsum(counts) - counts
    dest_sorted = pad_start[sorted_e] + jnp.arange(n_assign, dtype=jnp.int32) - grp_start[sorted_e]
    dest = jnp.zeros((n_assign,), jnp.int32).at[order].set(dest_sorted.astype(jnp.int32))
    n_blocks = -(-(n_assign + N_EXPERTS * (MOE_BLOCK - 1)) // MOE_BLOCK)
    n_slots = n_blocks * MOE_BLOCK
    slot_tok = jnp.full((n_slots,), n, jnp.int32).at[dest].set(jnp.arange(n_assign, dtype=jnp.int32) // TOP_K)
    blk_exp = jnp.minimum(jnp.searchsorted(pad_end, jnp.arange(n_blocks, dtype=jnp.int32) * MOE_BLOCK, side='right'), N_EXPERTS - 1)
    x_pad = jnp.concatenate([xt, jnp.zeros((1, xt.shape[1]), xt.dtype)], axis=0)
    xs = x_pad[slot_tok].reshape(n_blocks, MOE_BLOCK, -1)

    def expert_block(args):
        xb, e = args
        gu = xb @ w_gate_up[e] + b_gate_up[e]
        g = jnp.minimum(gu[:, :D_FF], SWIGLU_LIMIT)
        u = jnp.clip(gu[:, D_FF:], -SWIGLU_LIMIT, SWIGLU_LIMIT)
        act = (u + 1.0) * (g * jax.nn.sigmoid(SWIGLU_ALPHA * g))
        return act @ w_down[e] + b_down[e]

    ys = lax.map(expert_block, (xs, blk_exp)).reshape(n_slots, -1)
    y = jnp.einsum('tkd,tk->td', ys[dest].reshape(n, TOP_K, -1), gates.astype(ys.dtype))
    return y.reshape(shp)


def _trunk_layer(x, pos, attend, conv_state, h0, norm1_g, w_in, q_norm_g, k_norm_g, conv_w, conv_b,
                 rg_w_a, rg_b_a, rg_w_i, rg_b_i, rg_lambda, w_attn_out, w_rnn_out, w_out, norm2_g,
                 w_router, b_router, w_gate_up, b_gate_up, w_down, b_down):
    B, T, _ = x.shape
    n_pad = (-D_IN) % 896
    w_pad = jnp.pad(w_in, ((0, 0), (0, n_pad))).astype(jnp.bfloat16)
    z = _norm_proj(x.reshape(B * T, D_MODEL), norm1_g, w_pad)[:, :D_IN].reshape(B, T, D_IN)
    q, k, v, iq, ik, iw, xr, xg, ga, gr = jnp.split(z, np.cumsum(IN_WIDTHS)[:-1].tolist(), axis=-1)
    q = _rope(_rms_norm(q.reshape(B, T, N_HEADS, HEAD_DIM), q_norm_g), pos)
    k = _rope(_rms_norm(k.reshape(B, T, KV_HEADS, HEAD_DIM), k_norm_g), pos)
    v = v.reshape(B, T, KV_HEADS, HEAD_DIM)
    iq = _rope(iq.reshape(B, T, IDX_HEADS, IDX_DIM), pos)
    ik = _rope(ik[:, :, None, :], pos)[:, :, 0]
    iw = iw * (IDX_HEADS ** -0.5 * IDX_DIM ** -0.5)
    rg = (conv_w, conv_b, rg_w_a, rg_b_a, rg_w_i, rg_b_i, rg_lambda)
    if attend is None:
        assert B == 1
        attn = _prompt_dsa_attention(q[0], k[0], v[0], iq[0], iw[0], ik[0])[None]
        rnn, h_new = _rglru_prompt(xr[0], xg[0], *rg)
        rnn, h_new = rnn[None], h_new[None]
        conv_new = xr[:, -(CONV_W - 1):]
    else:
        attn = attend(q, k, v, iq, iw, ik)
        rnn, h_new = _rglru_sample(xr, xg, conv_state, h0, *rg)
        conv_new = jnp.concatenate([conv_state, xr], axis=1)[:, -(CONV_W - 1):]
    merged = jax.nn.sigmoid(ga) * (attn @ w_attn_out) + jax.nn.sigmoid(gr) * (rnn @ w_rnn_out)
    x = x + merged @ w_out
    x = x + _moe(_rms_norm(x, norm2_g), w_router, b_router, w_gate_up, b_gate_up, w_down, b_down)
    return x, (k, v, ik, conv_new, h_new)


def kernel(x_prompt, x_sample, cache_k, cache_v, cache_idx_k, state_conv, state_h, page_table,
           norm1_g, w_in, q_norm_g, k_norm_g, conv_w, conv_b, rg_w_a, rg_b_a, rg_w_i, rg_b_i, rg_lambda,
           w_attn_out, w_rnn_out, w_out, norm2_g, w_router, b_router, w_gate_up, b_gate_up, w_down, b_down):
    weights = (norm1_g, w_in, q_norm_g, k_norm_g, conv_w, conv_b, rg_w_a, rg_b_a, rg_w_i, rg_b_i, rg_lambda,
               w_attn_out, w_rnn_out, w_out, norm2_g, w_router, b_router, w_gate_up, b_gate_up, w_down, b_down)
    B, S = x_prompt.shape[:2]
    T = x_sample.shape[1]
    past = page_table.shape[1] * PAGE_SIZE
    pos_p = jnp.arange(S, dtype=jnp.int32)
    pos_s = past + jnp.arange(T, dtype=jnp.int32)
    conv0 = jnp.zeros((B, CONV_W - 1, D_RNN), x_prompt.dtype)
    h0 = jnp.zeros((B, D_RNN), jnp.float32)
    wl = [w[0] for w in weights]
    yp, st_p = _trunk_layer(x_prompt, pos_p, None, conv0, h0, *wl)
    attend_s = functools.partial(_sample_sparse_attention, cache_k=cache_k[0], cache_v=cache_v[0],
                                 cache_idx_k=cache_idx_k[0], page_table=page_table)
    ys, st_s = _trunk_layer(x_sample, pos_s, attend_s, state_conv[0], state_h[0], *wl)
    k_p, v_p, ik_p, conv_p, h_p = [a[None] for a in st_p]
    k_s, v_s, ik_s, conv_s, h_s = [a[None] for a in st_s]
    return (yp, ys, k_p, v_p, ik_p, conv_p, h_p, k_s, v_s, ik_s, conv_s, h_s)
```

```python
import functools
import math

import jax
import jax.numpy as jnp
import numpy as np
from jax import lax
from jax.experimental import pallas as pl
from jax.experimental.pallas import tpu as pltpu

D_MODEL = 1024
PAGE_SIZE = 128
N_HEADS = 16
HEAD_DIM = 64
KV_HEADS = 4
IDX_HEADS = 8
IDX_DIM = 64
IDX_TOPK_MAX = 256
Q_BLOCK = 128
ROPE_THETA = 10000.0
D_RNN = D_MODEL
RG_BLOCKS = 4
RG_BLOCK_W = D_RNN // RG_BLOCKS
CONV_W = 4
RG_C = 8.0
N_EXPERTS = 32
TOP_K = 4
D_FF = D_MODEL
SWIGLU_LIMIT = 7.0
SWIGLU_ALPHA = 1.702
MOE_BLOCK = 128
EPS = 1e-6

Q_W = N_HEADS * HEAD_DIM
KV_W = KV_HEADS * HEAD_DIM
IQ_W = IDX_HEADS * IDX_DIM
IN_WIDTHS = (Q_W, KV_W, KV_W, IQ_W, IDX_DIM, IDX_HEADS, D_RNN, D_RNN, D_MODEL, D_MODEL)
D_IN = sum(IN_WIDTHS)

LANE = 128
SUBLANE = 8
VMEM_LIMIT_BYTES = 48 * 1024 * 1024


def _norm_proj_kernel(x_ref, g_ref, w_ref, o_ref):
    x = x_ref[...]
    y = x * lax.rsqrt(jnp.mean(x * x, axis=-1, keepdims=True) + EPS) * g_ref[...]
    o_ref[...] = jnp.dot(y.astype(jnp.bfloat16), w_ref[...], preferred_element_type=jnp.float32)


def _norm_proj(x2d, g, w_bf16, tm=512, tn=896):
    m, d = x2d.shape
    n = w_bf16.shape[1]
    return pl.pallas_call(
        _norm_proj_kernel,
        grid=(m // tm, n // tn),
        in_specs=[
            pl.BlockSpec((tm, d), lambda i, j: (i, 0)),
            pl.BlockSpec((1, d), lambda i, j: (0, 0)),
            pl.BlockSpec((d, tn), lambda i, j: (0, j)),
        ],
        out_specs=pl.BlockSpec((tm, tn), lambda i, j: (i, j)),
        out_shape=jax.ShapeDtypeStruct((m, n), jnp.float32),
        name="norm_proj",
    )(x2d, g.reshape(1, d), w_bf16)


INT_MIN = -(2 ** 31)
INT_MAX = 2 ** 31 - 1
MASK_BIAS = -1e30
NEG_INF_BITS_MASK = 0x7FFFFFFF


def _ordered_key(x):
    bits = pltpu.bitcast(x, jnp.int32)
    return bits ^ ((bits >> 31) & NEG_INF_BITS_MASK)


def _topk_threshold(key_ref, cut_ref, nk, *, rows, kb, topk, n_idx_bits):
    n_lane_tiles = kb // LANE
    lane = lax.broadcasted_iota(jnp.int32, (rows, kb), 1)

    def count(pred_of_chunk):
        def body(c, cnt):
            p = pred_of_chunk(c, key_ref[c])
            for j in range(n_lane_tiles):
                cnt = cnt + jnp.where(p[:, j * LANE:(j + 1) * LANE], 1, 0)
            return cnt
        cnt = lax.fori_loop(0, nk, body, jnp.zeros((rows, LANE), jnp.int32))
        return jnp.sum(cnt, axis=-1, keepdims=True)

    def bit_step(b, carry):
        thr, n_ge = carry
        cand = thr ^ (jnp.int32(1) << (31 - b))
        cand_b = jnp.broadcast_to(cand, (rows, kb))
        cnt = count(lambda c, key: key >= cand_b)
        ok = cnt >= topk
        return jnp.where(ok, cand, thr), jnp.where(ok, cnt, n_ge)

    thr0 = jnp.full((rows, 1), INT_MIN, jnp.int32)
    n0 = jnp.full((rows, 1), 0, jnp.int32) + nk * kb
    thr, n_ge = lax.fori_loop(0, 32, bit_step, (thr0, n0))
    thr_b = jnp.broadcast_to(thr, (rows, kb))

    cut_ref[...] = jnp.full((rows, LANE), INT_MAX, jnp.int32)

    @pl.when(jnp.max(n_ge) > topk)
    def _():
        n_eq = count(lambda c, key: key == thr_b)
        need = topk - (n_ge - n_eq)

        def idx_step(b, lo):
            step = jnp.int32(1) << (n_idx_bits - 1 - b)
            mid_b = jnp.broadcast_to(lo + step - 1, (rows, kb))
            f = count(lambda c, key: (key == thr_b) & (c * kb + lane <= mid_b))
            return jnp.where(f < need, lo + step, lo)

        lo = lax.fori_loop(0, n_idx_bits, idx_step, jnp.zeros((rows, 1), jnp.int32))
        cut = jnp.where(n_ge > topk, lo, INT_MAX)
        cut_ref[...] = jnp.broadcast_to(cut, (rows, LANE))

    return thr


def _dsa_kernel(iq_ref, iw_ref, q_ref, ikT_ref, kT_ref, v_ref, o_ref,
                key_ref, wb_ref, cut_ref, m_ref, l_ref, acc_ref, *, qb, kb, topk, n_idx_bits, q_pos0, n_chunks_of):
    i = pl.program_id(0)
    nk = n_chunks_of(i)
    n_lane_tiles = kb // LANE
    row = lax.broadcasted_iota(jnp.int32, (qb, kb), 0)
    lane = lax.broadcasted_iota(jnp.int32, (qb, kb), 1)
    qpos = q_pos0 + i * qb + row

    iw = iw_ref[...]
    for h in range(IDX_HEADS):
        wb_ref[h] = jnp.broadcast_to(iw[:, h:h + 1], (qb, LANE))

    def score_chunk(c, carry):
        ikc = ikT_ref[c]
        acc = jnp.zeros((qb, kb), jnp.float32)
        for h in range(IDX_HEADS):
            s = jnp.dot(iq_ref[0, h], ikc, preferred_element_type=jnp.float32)
            acc = acc + jnp.tile(wb_ref[h], (1, n_lane_tiles)) * jnp.maximum(s, 0.0)
        acc = jnp.where(c * kb + lane <= qpos, acc, -jnp.inf)
        key_ref[c] = _ordered_key(acc)
        return carry

    lax.fori_loop(0, nk, score_chunk, 0)

    thr = _topk_threshold(key_ref, cut_ref, nk, rows=qb, kb=kb, topk=topk, n_idx_bits=n_idx_bits)
    thr_b = jnp.broadcast_to(thr, (qb, kb))

    m_ref[...] = jnp.full(m_ref.shape, MASK_BIAS, jnp.float32)
    l_ref[...] = jnp.zeros(l_ref.shape, jnp.float32)
    acc_ref[...] = jnp.zeros(acc_ref.shape, jnp.float32)
    cut_b = jnp.tile(cut_ref[...], (1, n_lane_tiles))
    rep = N_HEADS // KV_HEADS

    def attend_chunk(c, carry):
        key = key_ref[c]
        kpos = c * kb + lane
        sel = (key > thr_b) | ((key == thr_b) & (kpos <= cut_b))
        sel = sel & (kpos <= qpos)
        bias = jnp.where(sel, 0.0, MASK_BIAS)
        kc = kT_ref[c]
        for g in range(KV_HEADS):
            s = jnp.dot(q_ref[0, g], kc[g * HEAD_DIM:(g + 1) * HEAD_DIM, :], preferred_element_type=jnp.float32)
            s = (s.reshape(rep, qb, kb) + bias[None]).reshape(rep * qb, kb)
            m_prev = m_ref[g]
            m_new = jnp.maximum(m_prev, jnp.max(s, axis=-1, keepdims=True))
            alpha = jnp.exp(m_prev - m_new)
            p = jnp.exp(s - jnp.tile(m_new, (1, n_lane_tiles)))
            l_ref[g] = alpha * l_ref[g] + jnp.sum(p, axis=-1, keepdims=True)
            pv = jnp.dot(p.astype(jnp.bfloat16), v_ref[c, g], preferred_element_type=jnp.float32)
            acc_ref[g] = acc_ref[g] * alpha[:, :HEAD_DIM] + pv
            m_ref[g] = m_new
        return carry

    lax.fori_loop(0, nk, attend_chunk, 0)
    for g in range(KV_HEADS):
        o_ref[0, g] = acc_ref[g] / l_ref[g][:, :HEAD_DIM]


def _prompt_dsa_attention(q, k, v, iq, iw, ik, *, qb=128, kb=512):
    s_len = q.shape[0]
    topk = min(IDX_TOPK_MAX, s_len // 4)
    nqb, nkc = s_len // qb, s_len // kb
    rep = N_HEADS // KV_HEADS
    bf = jnp.bfloat16
    scale = HEAD_DIM ** -0.5
    q_t = (q * scale).astype(bf).reshape(nqb, qb, KV_HEADS, rep, HEAD_DIM).transpose(0, 2, 3, 1, 4)
    q_t = q_t.reshape(nqb, KV_HEADS, rep * qb, HEAD_DIM)
    iq_t = iq.astype(bf).reshape(nqb, qb, IDX_HEADS, IDX_DIM).transpose(0, 2, 1, 3)
    ikT = ik.astype(bf).reshape(nkc, kb, IDX_DIM).transpose(0, 2, 1)
    kT = k.astype(bf).reshape(nkc, kb, KV_HEADS * HEAD_DIM).transpose(0, 2, 1)
    v_t = v.astype(bf).reshape(nkc, kb, KV_HEADS, HEAD_DIM).transpose(0, 2, 1, 3)
    n_idx_bits = max(1, (s_len - 1).bit_length())
    body = functools.partial(
        _dsa_kernel, qb=qb, kb=kb, topk=topk, n_idx_bits=n_idx_bits, q_pos0=0,
        n_chunks_of=lambda i: ((i + 1) * qb + kb - 1) // kb)
    whole = lambda shape: pl.BlockSpec(shape, lambda i: (0,) * len(shape), pipeline_mode=pl.Buffered(1))
    out = pl.pallas_call(
        body,
        grid=(nqb,),
        in_specs=[
            pl.BlockSpec((1, IDX_HEADS, qb, IDX_DIM), lambda i: (i, 0, 0, 0)),
            pl.BlockSpec((qb, IDX_HEADS), lambda i: (i, 0)),
            pl.BlockSpec((1, KV_HEADS, rep * qb, HEAD_DIM), lambda i: (i, 0, 0, 0)),
            whole((nkc, IDX_DIM, kb)),
            whole((nkc, KV_HEADS * HEAD_DIM, kb)),
            whole((nkc, KV_HEADS, kb, HEAD_DIM)),
        ],
        out_specs=pl.BlockSpec((1, KV_HEADS, rep * qb, HEAD_DIM), lambda i: (i, 0, 0, 0)),
        out_shape=jax.ShapeDtypeStruct((nqb, KV_HEADS, rep * qb, HEAD_DIM), jnp.float32),
        scratch_shapes=[
            pltpu.VMEM((nkc, qb, kb), jnp.int32),
            pltpu.VMEM((IDX_HEADS, qb, LANE), jnp.float32),
            pltpu.VMEM((qb, LANE), jnp.int32),
            pltpu.VMEM((KV_HEADS, rep * qb, LANE), jnp.float32),
            pltpu.VMEM((KV_HEADS, rep * qb, LANE), jnp.float32),
            pltpu.VMEM((KV_HEADS, rep * qb, HEAD_DIM), jnp.float32),
        ],
        compiler_params=pltpu.CompilerParams(dimension_semantics=("arbitrary",), vmem_limit_bytes=VMEM_LIMIT_BYTES),
        name="prompt_dsa_attention",
    )(iq_t, iw, q_t, ikT, kT, v_t)
    out = out.reshape(nqb, KV_HEADS, rep, qb, HEAD_DIM).transpose(0, 3, 1, 2, 4)
    return out.reshape(s_len, N_HEADS * HEAD_DIM)


def _sample_index_kernel(pt_ref, iq_ref, iw_ref, iknew_ref, *rest, pp, n_pages, topk, n_idx_bits):
    page_refs, o_ref, (key_ref, wb_ref, cut_ref) = rest[:pp], rest[pp], rest[pp + 1:]
    pg = pl.program_id(1)
    t_new = o_ref.shape[2]
    row = lax.broadcasted_iota(jnp.int32, (t_new, PAGE_SIZE), 0)
    lane = lax.broadcasted_iota(jnp.int32, (t_new, PAGE_SIZE), 1)

    @pl.when(pg == 0)
    def _():
        wb_ref[...] = jnp.broadcast_to(iw_ref[0], wb_ref.shape)

    iq = iq_ref[0]

    def scores(ik):
        s = lax.dot_general(iq, ik.astype(jnp.bfloat16), (((1,), (1,)), ((), ())),
                            preferred_element_type=jnp.float32)
        s = wb_ref[...] * jnp.maximum(s, 0.0)
        acc = jnp.zeros((t_new, PAGE_SIZE), jnp.float32)
        for h in range(IDX_HEADS):
            acc = acc + s[h * t_new:(h + 1) * t_new, :]
        return acc

    for j in range(pp):
        key_ref[pg * pp + j] = _ordered_key(scores(page_refs[j][0]))

    @pl.when(pg == pl.num_programs(1) - 1)
    def _():
        s_new = jnp.where(lane <= row, scores(iknew_ref[0]), -jnp.inf)
        key_ref[n_pages] = _ordered_key(s_new)
        thr = _topk_threshold(key_ref, cut_ref, n_pages + 1, rows=t_new, kb=PAGE_SIZE, topk=topk, n_idx_bits=n_idx_bits)
        thr_b = jnp.broadcast_to(thr, (t_new, PAGE_SIZE))
        cut_b = cut_ref[...]

        def emit(c, carry):
            key = key_ref[c]
            kpos = c * PAGE_SIZE + lane
            sel = (key > thr_b) | ((key == thr_b) & (kpos <= cut_b))
            sel = sel & (kpos <= n_pages * PAGE_SIZE + row)
            o_ref[0, c] = jnp.where(sel, 0.0, MASK_BIAS)
            return carry

        lax.fori_loop(0, n_pages + 1, emit, 0)


def _sample_attend_kernel(pt_ref, q_ref, bias_ref, knew_ref, vnew_ref, *rest, pp, n_pages):
    k_refs, v_refs, o_ref, (m_ref, l_ref, acc_ref) = rest[:pp], rest[pp:2 * pp], rest[2 * pp], rest[2 * pp + 1:]
    pg = pl.program_id(1)
    n_rows = q_ref.shape[1]
    t_new = bias_ref.shape[2]

    @pl.when(pg == 0)
    def _():
        m_ref[...] = jnp.full(m_ref.shape, MASK_BIAS, jnp.float32)
        l_ref[...] = jnp.zeros(l_ref.shape, jnp.float32)
        acc_ref[...] = jnp.zeros(acc_ref.shape, jnp.float32)

    q = q_ref[0]

    def step(c, kp, vp):
        s = lax.dot_general(q, kp.astype(jnp.bfloat16), (((1,), (1,)), ((), ())), preferred_element_type=jnp.float32)
        s = s + jnp.tile(bias_ref[0, c], (n_rows // t_new, 1))
        m_prev = m_ref[...]
        m_new = jnp.maximum(m_prev, jnp.max(s, axis=-1, keepdims=True))
        alpha = jnp.exp(m_prev - m_new)
        p = jnp.exp(s - m_new)
        l_ref[...] = alpha * l_ref[...] + jnp.sum(p, axis=-1, keepdims=True)
        pv = jnp.dot(p.astype(jnp.bfloat16), vp.astype(jnp.bfloat16), preferred_element_type=jnp.float32)
        acc_ref[...] = acc_ref[...] * jnp.tile(alpha, (1, KV_W // LANE)) + pv
        m_ref[...] = m_new

    for j in range(pp):
        step(pg * pp + j, k_refs[j][0], v_refs[j][0])

    @pl.when(pg == pl.num_programs(1) - 1)
    def _():
        step(n_pages, knew_ref[0], vnew_ref[0])
        acc = acc_ref[...]
        row_head = lax.broadcasted_iota(jnp.int32, acc.shape, 0) // (n_rows // KV_HEADS)
        col_head = lax.broadcasted_iota(jnp.int32, acc.shape, 1) // HEAD_DIM
        own = jnp.where(row_head == col_head, acc, 0.0)
        folded = own[:, :LANE] + own[:, LANE:]
        folded = folded + pltpu.roll(folded, HEAD_DIM, axis=1)
        o_ref[0] = folded / l_ref[...]


def _sample_dsa_attention(q, k, v, iq, iw, ik, cache_k, cache_v, cache_idx_k, page_table, *, pp=8):
    b, t = q.shape[:2]
    n_pages = page_table.shape[1]
    n_pool = cache_k.shape[0]
    past = n_pages * PAGE_SIZE
    topk = min(IDX_TOPK_MAX, (past + t) // 4)
    n_idx_bits = max(1, (past + PAGE_SIZE - 1).bit_length())
    assert n_pages % pp == 0 and t == SUBLANE and KV_W == 2 * LANE
    npg = n_pages // pp
    bf = jnp.bfloat16
    rep = N_HEADS // KV_HEADS
    n_rows = N_HEADS * t

    iq_r = iq.astype(bf).transpose(0, 2, 1, 3).reshape(b, IDX_HEADS * t, IDX_DIM)
    iw_r = iw.transpose(0, 2, 1).reshape(b, IDX_HEADS * t, 1)
    pad_rows = lambda a: jnp.pad(a, ((0, 0), (0, PAGE_SIZE - t), (0, 0)))
    ik_new = pad_rows(ik)
    k_new = pad_rows(k.reshape(b, t, KV_W))
    v_new = pad_rows(v.reshape(b, t, KV_W))
    qs = (q * HEAD_DIM ** -0.5).astype(bf).reshape(b, t, KV_HEADS, rep, HEAD_DIM).transpose(0, 2, 3, 1, 4)
    q_bd = (qs[:, :, :, :, None, :] * jnp.eye(KV_HEADS, dtype=bf)[None, :, None, None, :, None])
    q_bd = q_bd.reshape(b, n_rows, KV_W)

    page_map = lambda j: (lambda bi, pg, pt: (pt[bi, pg * pp + j], 0, 0))
    per_seq = lambda shape: pl.BlockSpec((1,) + shape, lambda bi, pg, pt: (bi,) + (0,) * len(shape))

    bias = pl.pallas_call(
        functools.partial(_sample_index_kernel, pp=pp, n_pages=n_pages, topk=topk, n_idx_bits=n_idx_bits),
        grid_spec=pltpu.PrefetchScalarGridSpec(
            num_scalar_prefetch=1, grid=(b, npg),
            in_specs=[per_seq((IDX_HEADS * t, IDX_DIM)), per_seq((IDX_HEADS * t, 1)), per_seq((PAGE_SIZE, IDX_DIM))]
            + [pl.BlockSpec((1, PAGE_SIZE, IDX_DIM), page_map(j)) for j in range(pp)],
            out_specs=per_seq((n_pages + 1, t, PAGE_SIZE)),
            scratch_shapes=[pltpu.VMEM((n_pages + 1, t, PAGE_SIZE), jnp.int32),
                            pltpu.VMEM((IDX_HEADS * t, PAGE_SIZE), jnp.float32),
                            pltpu.VMEM((t, LANE), jnp.int32)]),
        out_shape=jax.ShapeDtypeStruct((b, n_pages + 1, t, PAGE_SIZE), jnp.float32),
        compiler_params=pltpu.CompilerParams(dimension_semantics=("arbitrary", "arbitrary")),
        name="sample_index",
    )(page_table, iq_r, iw_r, ik_new, *([cache_idx_k] * pp))

    ck = cache_k.reshape(n_pool, PAGE_SIZE, KV_W)
    cv = cache_v.reshape(n_pool, PAGE_SIZE, KV_W)
    out = pl.pallas_call(
        functools.partial(_sample_attend_kernel, pp=pp, n_pages=n_pages),
        grid_spec=pltpu.PrefetchScalarGridSpec(
            num_scalar_prefetch=1, grid=(b, npg),
            in_specs=[per_seq((n_rows, KV_W)), per_seq((n_pages + 1, t, PAGE_SIZE)),
                      per_seq((PAGE_SIZE, KV_W)), per_seq((PAGE_SIZE, KV_W))]
            + [pl.BlockSpec((1, PAGE_SIZE, KV_W), page_map(j)) for j in range(pp)] * 2,
            out_specs=per_seq((n_rows, LANE)),
            scratch_shapes=[pltpu.VMEM((n_rows, LANE), jnp.float32), pltpu.VMEM((n_rows, LANE), jnp.float32),
                            pltpu.VMEM((n_rows, KV_W), jnp.float32)]),
        out_shape=jax.ShapeDtypeStruct((b, n_rows, LANE), jnp.float32),
        compiler_params=pltpu.CompilerParams(dimension_semantics=("arbitrary", "arbitrary")),
        name="sample_attend",
    )(page_table, q_bd, bias, k_new, v_new, *([ck] * pp), *([cv] * pp))
    out = out[:, :, :HEAD_DIM].reshape(b, N_HEADS, t, HEAD_DIM).transpose(0, 2, 1, 3)
    return out.reshape(b, t, N_HEADS * HEAD_DIM)


GELU_C = math.sqrt(2.0 / math.pi)


def _rglru_kernel(x_ref, xg_ref, prev_ref, h0_ref, cw_ref, cb_ref, wa_ref, ba_ref, wi_ref, bi_ref, lam_ref,
                  o_ref, hl_ref, conv_ref, a_ref, u_ref, xprev_ref, h_ref, *, rows, per_group_state):
    i = pl.program_id(0)
    n_groups = rows // SUBLANE
    d_rnn = x_ref.shape[1]
    r8 = lax.broadcasted_iota(jnp.int32, (SUBLANE, d_rnn), 0)
    grp = lambda g: pl.ds(pl.multiple_of(g * SUBLANE, SUBLANE), SUBLANE)

    if not per_group_state:
        @pl.when(i == 0)
        def _():
            xprev_ref[...] = jnp.zeros_like(xprev_ref)
            h_ref[...] = jnp.zeros_like(h_ref)

    cw = cw_ref[...]
    cb = cb_ref[...]

    def conv_group(g, prev):
        x8 = x_ref[grp(g), :]
        if per_group_state:
            prev = prev_ref[grp(g), :]
        out = cb + x8 * cw[CONV_W - 1:CONV_W, :]
        for d in range(1, CONV_W):
            shifted = jnp.where(r8 < d, pltpu.roll(prev, d, axis=0), pltpu.roll(x8, d, axis=0))
            out = out + shifted * cw[CONV_W - 1 - d:CONV_W - d, :]
        conv_ref[grp(g), :] = out
        return x8

    zeros8 = jnp.zeros((SUBLANE, d_rnn), jnp.float32)
    xlast = lax.fori_loop(0, n_groups, conv_group, zeros8 if per_group_state else xprev_ref[...])
    if not per_group_state:
        xprev_ref[...] = xlast

    xc = conv_ref[...]
    xb = xc.astype(jnp.bfloat16)
    r_parts, i_parts = [], []
    for n in range(RG_BLOCKS):
        xn = xb[:, n * RG_BLOCK_W:(n + 1) * RG_BLOCK_W]
        r_parts.append(jnp.dot(xn, wa_ref[n], preferred_element_type=jnp.float32))
        i_parts.append(jnp.dot(xn, wi_ref[n], preferred_element_type=jnp.float32))
    r = jax.nn.sigmoid(jnp.concatenate(r_parts, axis=-1) + ba_ref[...])
    ig = jax.nn.sigmoid(jnp.concatenate(i_parts, axis=-1) + bi_ref[...])
    lam = lam_ref[...]
    log_sig_lam = -(jnp.maximum(-lam, 0.0) + jnp.log1p(jnp.exp(-jnp.abs(lam))))
    log_a = RG_C * r * log_sig_lam
    a = jnp.exp(log_a)
    a_ref[...] = a
    u_ref[...] = jnp.sqrt(-jnp.tanh(log_a) * (a * a + 1.0)) * (ig * xc)

    def scan_group(g, hprev):
        a = a_ref[grp(g), :]
        u = u_ref[grp(g), :]
        if per_group_state:
            hprev = jnp.broadcast_to(h0_ref[pl.ds(g, 1), :], (SUBLANE, d_rnn))
        for d in (1, 2, 4):
            u = jnp.where(r8 >= d, a * pltpu.roll(u, d, axis=0) + u, u)
            a = jnp.where(r8 >= d, a * pltpu.roll(a, d, axis=0), a)
        h = a * hprev + u
        u_ref[grp(g), :] = h
        hlast = jnp.broadcast_to(h[SUBLANE - 1:SUBLANE, :], (SUBLANE, d_rnn))
        if per_group_state:
            hl_ref[pl.ds(g, 1), :] = h[SUBLANE - 1:SUBLANE, :]
        return hlast

    hlast = lax.fori_loop(0, n_groups, scan_group, zeros8 if per_group_state else h_ref[...])
    if not per_group_state:
        h_ref[...] = hlast
        hl_ref[...] = hlast

    xg = xg_ref[...]
    gelu = 0.5 * xg * (1.0 + jnp.tanh(GELU_C * (xg + 0.044715 * (xg * xg * xg))))
    o_ref[...] = u_ref[...] * gelu


def _rglru(x, xg, prev, h0, conv_w, conv_b, w_a, b_a, w_i, b_i, lam, *, per_group_state, tile_rows):
    n_rows, d = x.shape
    row = lambda a: a.reshape(1, d)
    const = lambda shape: pl.BlockSpec(shape, lambda i: (0,) * len(shape))
    n_hl = n_rows // SUBLANE if per_group_state else SUBLANE
    body = functools.partial(_rglru_kernel, rows=tile_rows, per_group_state=per_group_state)
    tile = pl.BlockSpec((tile_rows, d), lambda i: (i, 0))
    if per_group_state:
        assert tile_rows == n_rows
        prev_spec, h0_spec = const(prev.shape), const(h0.shape)
    else:
        prev_spec, h0_spec = const(prev.shape), const(h0.shape)
    return pl.pallas_call(
        body,
        grid=(n_rows // tile_rows,),
        in_specs=[tile, tile, prev_spec, h0_spec, const((CONV_W, d)), const((1, d)),
                  const(w_a.shape), const((1, d)), const(w_i.shape), const((1, d)), const((1, d))],
        out_specs=[tile, const((n_hl, d))],
        out_shape=[jax.ShapeDtypeStruct((n_rows, d), jnp.float32), jax.ShapeDtypeStruct((n_hl, d), jnp.float32)],
        scratch_shapes=[pltpu.VMEM((tile_rows, d), jnp.float32), pltpu.VMEM((tile_rows, d), jnp.float32),
                        pltpu.VMEM((tile_rows, d), jnp.float32), pltpu.VMEM((SUBLANE, d), jnp.float32),
                        pltpu.VMEM((SUBLANE, d), jnp.float32)],
        compiler_params=pltpu.CompilerParams(dimension_semantics=("arbitrary",), vmem_limit_bytes=VMEM_LIMIT_BYTES),
        name="rglru_sample" if per_group_state else "rglru_prompt",
    )(x, xg, prev, h0, conv_w, row(conv_b), w_a.astype(jnp.bfloat16), row(b_a), w_i.astype(jnp.bfloat16), row(b_i), row(lam))


def _rglru_prompt(x, xg, conv_w, conv_b, w_a, b_a, w_i, b_i, lam, tile_rows=512):
    dummy = jnp.zeros((SUBLANE, x.shape[1]), jnp.float32)
    rnn, hl = _rglru(x, xg, dummy, dummy, conv_w, conv_b, w_a, b_a, w_i, b_i, lam,
                     per_group_state=False, tile_rows=tile_rows)
    return rnn, hl[0]


def _rglru_sample(x, xg, state_conv, state_h, conv_w, conv_b, w_a, b_a, w_i, b_i, lam):
    b, t, d = x.shape
    assert t == SUBLANE
    prev = jnp.concatenate([jnp.zeros((b, SUBLANE - (CONV_W - 1), d), jnp.float32), state_conv], axis=1)
    rnn, hl = _rglru(x.reshape(b * t, d), xg.reshape(b * t, d), prev.reshape(b * t, d), state_h,
                     conv_w, conv_b, w_a, b_a, w_i, b_i, lam, per_group_state=True, tile_rows=b * t)
    return rnn.reshape(b, t, d), hl


def _rms_norm(x, g):
    xf = x.astype(jnp.float32)
    y = xf * lax.rsqrt(jnp.mean(xf * xf, axis=-1, keepdims=True) + EPS)
    return (y * g.astype(jnp.float32)).astype(x.dtype)


def _rope(x, pos):
    half = x.shape[-1] // 2
    inv_freq = ROPE_THETA ** (-jnp.arange(half, dtype=jnp.float32) / half)
    ang = pos.astype(jnp.float32)[:, None] * inv_freq[None, :]
    cos = jnp.cos(ang)[:, None, :]
    sin = jnp.sin(ang)[:, None, :]
    xf = x.astype(jnp.float32)
    x1, x2 = xf[..., :half], xf[..., half:]
    return jnp.concatenate([x1 * cos - x2 * sin, x2 * cos + x1 * sin], axis=-1).astype(x.dtype)


def _index_scores(iq, iw, ik):
    s = jnp.einsum('bthd,bld->bthl', iq, ik, preferred_element_type=jnp.float32)
    return jnp.einsum('bthl,bth->btl', jax.nn.relu(s), iw.astype(jnp.float32))


def _sparse_attend(q, k_sel, v_sel, valid):
    B, T = q.shape[:2]
    qg = q.reshape(B, T, KV_HEADS, N_HEADS // KV_HEADS, HEAD_DIM)
    logits = jnp.einsum('btkgd,btskd->btkgs', qg, k_sel, preferred_element_type=jnp.float32) * (HEAD_DIM ** -0.5)
    logits = jnp.where(valid[:, :, None, None, :], logits, -jnp.inf)
    p = jax.nn.softmax(logits, axis=-1)
    out = jnp.einsum('btkgs,btskd->btkgd', p.astype(v_sel.dtype), v_sel)
    return out.reshape(B, T, Q_W)


def _gather_rows(rows, idx):
    return jax.vmap(lambda r, i: r[i])(rows, idx)


def _prompt_sparse_attention(q, k, v, iq, iw, ik):
    B, S = q.shape[:2]
    topk = min(IDX_TOPK_MAX, S // 4)
    n_blocks = S // Q_BLOCK
    key_pos = jnp.arange(S, dtype=jnp.int32)

    def block(args):
        qb, iqb, iwb, start = args
        qpos = start + jnp.arange(Q_BLOCK, dtype=jnp.int32)
        sc = _index_scores(iqb, iwb, ik)
        sc = jnp.where((key_pos[None, :] <= qpos[:, None])[None], sc, -jnp.inf)
        _, sel = lax.top_k(sc, topk)
        valid = sel <= qpos[None, :, None]
        return _sparse_attend(qb, _gather_rows(k, sel), _gather_rows(v, sel), valid)

    to_blocks = lambda a: jnp.moveaxis(a.reshape(B, n_blocks, Q_BLOCK, *a.shape[2:]), 1, 0)
    starts = jnp.arange(n_blocks, dtype=jnp.int32) * Q_BLOCK
    out = lax.map(block, (to_blocks(q), to_blocks(iq), to_blocks(iw), starts))
    return jnp.moveaxis(out, 0, 1).reshape(B, S, Q_W)


def _sample_sparse_attention(q, k, v, iq, iw, ik, cache_k, cache_v, cache_idx_k, page_table):
    DB, T = q.shape[:2]
    past = page_table.shape[1] * PAGE_SIZE
    L = past + T
    topk = min(IDX_TOPK_MAX, L // 4)
    ik_past = cache_idx_k[page_table].reshape(DB, past, IDX_DIM)
    ik_all = jnp.concatenate([ik_past, ik.astype(ik_past.dtype)], axis=1)
    qpos = past + jnp.arange(T, dtype=jnp.int32)
    sc = _index_scores(iq, iw, ik_all)
    sc = jnp.where((jnp.arange(L, dtype=jnp.int32)[None, :] <= qpos[:, None])[None], sc, -jnp.inf)
    _, sel = lax.top_k(sc, topk)
    valid = sel <= qpos[None, :, None]
    in_past = sel < past
    sel_p = jnp.minimum(sel, past - 1)
    page = _gather_rows(page_table, sel_p // PAGE_SIZE)
    phys = page * PAGE_SIZE + sel_p % PAGE_SIZE
    new_idx = jnp.clip(sel - past, 0, T - 1)
    k_pool = cache_k.reshape(-1, KV_HEADS, HEAD_DIM)
    v_pool = cache_v.reshape(-1, KV_HEADS, HEAD_DIM)
    m = in_past[..., None, None]
    k_sel = jnp.where(m, k_pool[phys], _gather_rows(k, new_idx).astype(k_pool.dtype))
    v_sel = jnp.where(m, v_pool[phys], _gather_rows(v, new_idx).astype(v_pool.dtype))
    return _sparse_attend(q, k_sel, v_sel, valid)


def _causal_conv(xr, conv_state, w, b):
    T = xr.shape[1]
    xp = jnp.concatenate([conv_state.astype(xr.dtype), xr], axis=1)
    out = b + sum(xp[:, j:j + T] * w[j] for j in range(CONV_W))
    return out, xp[:, -(CONV_W - 1):]


def _block_diag(x, w, b):
    B, T, _ = x.shape
    y = jnp.einsum('btnc,ncd->btnd', x.reshape(B, T, RG_BLOCKS, RG_BLOCK_W), w)
    return y.reshape(B, T, D_RNN) + b


def _rg_lru(x, h0, w_a, b_a, w_i, b_i, lam):
    xf = x.astype(jnp.float32)
    r = jax.nn.sigmoid(_block_diag(x, w_a, b_a).astype(jnp.float32))
    i = jax.nn.sigmoid(_block_diag(x, w_i, b_i).astype(jnp.float32))
    log_a = RG_C * r * jax.nn.log_sigmoid(lam.astype(jnp.float32))
    a = jnp.exp(log_a)
    u = jnp.sqrt(-jnp.expm1(2.0 * log_a)) * (i * xf)

    def step(h, au):
        h = au[0] * h + au[1]
        return h, h

    h_last, hs = lax.scan(step, h0.astype(jnp.float32), (jnp.moveaxis(a, 1, 0), jnp.moveaxis(u, 1, 0)))
    return jnp.moveaxis(hs, 0, 1).astype(x.dtype), h_last.astype(x.dtype)


def _moe(h, w_router, b_router, w_gate_up, b_gate_up, w_down, b_down):
    shp = h.shape
    xt = h.reshape(-1, shp[-1])
    n = xt.shape[0]
    logits = (xt @ w_router + b_router).astype(jnp.float32)
    top_v, top_e = lax.top_k(logits, TOP_K)
    gates = jax.nn.softmax(top_v, axis=-1)
    n_assign = n * TOP_K
    flat_e = top_e.reshape(-1)
    order = jnp.argsort(flat_e, stable=True)
    sorted_e = flat_e[order]
    counts = jnp.bincount(flat_e, length=N_EXPERTS)
    padded = (counts + MOE_BLOCK - 1) // MOE_BLOCK * MOE_BLOCK
    pad_end = jnp.cumsum(padded)
    pad_start = pad_end - padded
    grp_start = jnp.cumsum(counts) - counts
    dest_sorted = pad_start[sorted_e] + jnp.arange(n_assign, dtype=jnp.int32) - grp_start[sorted_e]
    dest = jnp.zeros((n_assign,), jnp.int32).at[order].set(dest_sorted.astype(jnp.int32))
    n_blocks = -(-(n_assign + N_EXPERTS * (MOE_BLOCK - 1)) // MOE_BLOCK)
    n_slots = n_blocks * MOE_BLOCK
    slot_tok = jnp.full((n_slots,), n, jnp.int32).at[dest].set(jnp.arange(n_assign, dtype=jnp.int32) // TOP_K)
    blk_exp = jnp.minimum(jnp.searchsorted(pad_end, jnp.arange(n_blocks, dtype=jnp.int32) * MOE_BLOCK, side='right'), N_EXPERTS - 1)
    x_pad = jnp.concatenate([xt, jnp.zeros((1, xt.shape[1]), xt.dtype)], axis=0)
    xs = x_pad[slot_tok].reshape(n_blocks, MOE_BLOCK, -1)

    def expert_block(args):
        xb, e = args
        gu = xb @ w_gate_up[e] + b_gate_up[e]
        g = jnp.minimum(gu[:, :D_FF], SWIGLU_LIMIT)
        u = jnp.clip(gu[:, D_FF:], -SWIGLU_LIMIT, SWIGLU_LIMIT)
        act = (u + 1.0) * (g * jax.nn.sigmoid(SWIGLU_ALPHA * g))
        return act @ w_down[e] + b_down[e]

    ys = lax.map(expert_block, (xs, blk_exp)).reshape(n_slots, -1)
    y = jnp.einsum('tkd,tk->td', ys[dest].reshape(n, TOP_K, -1), gates.astype(ys.dtype))
    return y.reshape(shp)


def _trunk_layer(x, pos, attend, conv_state, h0, norm1_g, w_in, q_norm_g, k_norm_g, conv_w, conv_b,
                 rg_w_a, rg_b_a, rg_w_i, rg_b_i, rg_lambda, w_attn_out, w_rnn_out, w_out, norm2_g,
                 w_router, b_router, w_gate_up, b_gate_up, w_down, b_down):
    B, T, _ = x.shape
    n_pad = (-D_IN) % 896
    w_pad = jnp.pad(w_in, ((0, 0), (0, n_pad))).astype(jnp.bfloat16)
    z = _norm_proj(x.reshape(B * T, D_MODEL), norm1_g, w_pad)[:, :D_IN].reshape(B, T, D_IN)
    q, k, v, iq, ik, iw, xr, xg, ga, gr = jnp.split(z, np.cumsum(IN_WIDTHS)[:-1].tolist(), axis=-1)
    q = _rope(_rms_norm(q.reshape(B, T, N_HEADS, HEAD_DIM), q_norm_g), pos)
    k = _rope(_rms_norm(k.reshape(B, T, KV_HEADS, HEAD_DIM), k_norm_g), pos)
    v = v.reshape(B, T, KV_HEADS, HEAD_DIM)
    iq = _rope(iq.reshape(B, T, IDX_HEADS, IDX_DIM), pos)
    ik = _rope(ik[:, :, None, :], pos)[:, :, 0]
    iw = iw * (IDX_HEADS ** -0.5 * IDX_DIM ** -0.5)
    rg = (conv_w, conv_b, rg_w_a, rg_b_a, rg_w_i, rg_b_i, rg_lambda)
    if attend is None:
        assert B == 1
        attn = _prompt_dsa_attention(q[0], k[0], v[0], iq[0], iw[0], ik[0])[None]
        rnn, h_new = _rglru_prompt(xr[0], xg[0], *rg)
        rnn, h_new = rnn[None], h_new[None]
        conv_new = xr[:, -(CONV_W - 1):]
    else:
        attn = attend(q, k, v, iq, iw, ik)
        rnn, h_new = _rglru_sample(xr, xg, conv_state, h0, *rg)
        conv_new = jnp.concatenate([conv_state, xr], axis=1)[:, -(CONV_W - 1):]
    merged = jax.nn.sigmoid(ga) * (attn @ w_attn_out) + jax.nn.sigmoid(gr) * (rnn @ w_rnn_out)
    x = x + merged @ w_out
    x = x + _moe(_rms_norm(x, norm2_g), w_router, b_router, w_gate_up, b_gate_up, w_down, b_down)
    return x, (k, v, ik, conv_new, h_new)


def kernel(x_prompt, x_sample, cache_k, cache_v, cache_idx_k, state_conv, state_h, page_table,
           norm1_g, w_in, q_norm_g, k_norm_g, conv_w, conv_b, rg_w_a, rg_b_a, rg_w_i, rg_b_i, rg_lambda,
           w_attn_out, w_rnn_out, w_out, norm2_g, w_router, b_router, w_gate_up, b_gate_up, w_down, b_down):
    weights = (norm1_g, w_in, q_norm_g, k_norm_g, conv_w, conv_b, rg_w_a, rg_b_a, rg_w_i, rg_b_i, rg_lambda,
               w_attn_out, w_rnn_out, w_out, norm2_g, w_router, b_router, w_gate_up, b_gate_up, w_down, b_down)
    B, S = x_prompt.shape[:2]
    T = x_sample.shape[1]
    past = page_table.shape[1] * PAGE_SIZE
    pos_p = jnp.arange(S, dtype=jnp.int32)
    pos_s = past + jnp.arange(T, dtype=jnp.int32)
    conv0 = jnp.zeros((B, CONV_W - 1, D_RNN), x_prompt.dtype)
    h0 = jnp.zeros((B, D_RNN), jnp.float32)
    wl = [w[0] for w in weights]
    yp, st_p = _trunk_layer(x_prompt, pos_p, None, conv0, h0, *wl)
    attend_s = functools.partial(_sample_dsa_attention, cache_k=cache_k[0], cache_v=cache_v[0],
                                 cache_idx_k=cache_idx_k[0], page_table=page_table)
    ys, st_s = _trunk_layer(x_sample, pos_s, attend_s, state_conv[0], state_h[0], *wl)
    k_p, v_p, ik_p, conv_p, h_p = [a[None] for a in st_p]
    k_s, v_s, ik_s, conv_s, h_s = [a[None] for a in st_s]
    return (yp, ys, k_p, v_p, ik_p, conv_p, h_p, k_s, v_s, ik_s, conv_s, h_s)
```

```python
import functools
import math

import jax
import jax.numpy as jnp
import numpy as np
from jax import lax
from jax.experimental import pallas as pl
from jax.experimental.pallas import tpu as pltpu

D_MODEL = 1024
PAGE_SIZE = 128
N_HEADS = 16
HEAD_DIM = 64
KV_HEADS = 4
IDX_HEADS = 8
IDX_DIM = 64
IDX_TOPK_MAX = 256
Q_BLOCK = 128
ROPE_THETA = 10000.0
D_RNN = D_MODEL
RG_BLOCKS = 4
RG_BLOCK_W = D_RNN // RG_BLOCKS
CONV_W = 4
RG_C = 8.0
N_EXPERTS = 32
TOP_K = 4
D_FF = D_MODEL
SWIGLU_LIMIT = 7.0
SWIGLU_ALPHA = 1.702
MOE_BLOCK = 128
EPS = 1e-6

Q_W = N_HEADS * HEAD_DIM
KV_W = KV_HEADS * HEAD_DIM
IQ_W = IDX_HEADS * IDX_DIM
IN_WIDTHS = (Q_W, KV_W, KV_W, IQ_W, IDX_DIM, IDX_HEADS, D_RNN, D_RNN, D_MODEL, D_MODEL)
D_IN = sum(IN_WIDTHS)

LANE = 128
SUBLANE = 8
VMEM_LIMIT_BYTES = 48 * 1024 * 1024


def _norm_proj_kernel(x_ref, g_ref, w_ref, o_ref):
    x = x_ref[...]
    y = x * lax.rsqrt(jnp.mean(x * x, axis=-1, keepdims=True) + EPS) * g_ref[...]
    o_ref[...] = jnp.dot(y.astype(jnp.bfloat16), w_ref[...], preferred_element_type=jnp.float32)


def _norm_proj(x2d, g, w_bf16, tm=512, tn=896):
    m, d = x2d.shape
    n = w_bf16.shape[1]
    return pl.pallas_call(
        _norm_proj_kernel,
        grid=(m // tm, n // tn),
        in_specs=[
            pl.BlockSpec((tm, d), lambda i, j: (i, 0)),
            pl.BlockSpec((1, d), lambda i, j: (0, 0)),
            pl.BlockSpec((d, tn), lambda i, j: (0, j)),
        ],
        out_specs=pl.BlockSpec((tm, tn), lambda i, j: (i, j)),
        out_shape=jax.ShapeDtypeStruct((m, n), jnp.float32),
        name="norm_proj",
    )(x2d, g.reshape(1, d), w_bf16)


INT_MIN = -(2 ** 31)
INT_MAX = 2 ** 31 - 1
MASK_BIAS = -1e30
NEG_INF_BITS_MASK = 0x7FFFFFFF


def _ordered_key(x):
    bits = pltpu.bitcast(x, jnp.int32)
    return bits ^ ((bits >> 31) & NEG_INF_BITS_MASK)


def _topk_threshold(key_ref, cut_ref, nk, *, rows, kb, topk, n_idx_bits):
    n_lane_tiles = kb // LANE
    lane = lax.broadcasted_iota(jnp.int32, (rows, kb), 1)

    def count(pred_of_chunk):
        def body(c, cnt):
            p = pred_of_chunk(c, key_ref[c])
            for j in range(n_lane_tiles):
                cnt = cnt + jnp.where(p[:, j * LANE:(j + 1) * LANE], 1, 0)
            return cnt
        cnt = lax.fori_loop(0, nk, body, jnp.zeros((rows, LANE), jnp.int32))
        return jnp.sum(cnt, axis=-1, keepdims=True)

    def bit_step(b, carry):
        thr, n_ge = carry
        cand = thr ^ (jnp.int32(1) << (31 - b))
        cand_b = jnp.broadcast_to(cand, (rows, kb))
        cnt = count(lambda c, key: key >= cand_b)
        ok = cnt >= topk
        return jnp.where(ok, cand, thr), jnp.where(ok, cnt, n_ge)

    thr0 = jnp.full((rows, 1), INT_MIN, jnp.int32)
    n0 = jnp.full((rows, 1), 0, jnp.int32) + nk * kb
    thr, n_ge = lax.fori_loop(0, 32, bit_step, (thr0, n0))
    thr_b = jnp.broadcast_to(thr, (rows, kb))

    cut_ref[...] = jnp.full((rows, LANE), INT_MAX, jnp.int32)

    @pl.when(jnp.max(n_ge) > topk)
    def _():
        n_eq = count(lambda c, key: key == thr_b)
        need = topk - (n_ge - n_eq)

        def idx_step(b, lo):
            step = jnp.int32(1) << (n_idx_bits - 1 - b)
            mid_b = jnp.broadcast_to(lo + step - 1, (rows, kb))
            f = count(lambda c, key: (key == thr_b) & (c * kb + lane <= mid_b))
            return jnp.where(f < need, lo + step, lo)

        lo = lax.fori_loop(0, n_idx_bits, idx_step, jnp.zeros((rows, 1), jnp.int32))
        cut = jnp.where(n_ge > topk, lo, INT_MAX)
        cut_ref[...] = jnp.broadcast_to(cut, (rows, LANE))

    return thr


def _dsa_kernel(iq_ref, iw_ref, q_ref, ikT_ref, kT_ref, v_ref, o_ref,
                key_ref, wb_ref, cut_ref, m_ref, l_ref, acc_ref, *, qb, kb, topk, n_idx_bits, q_pos0, n_chunks_of):
    i = pl.program_id(0)
    nk = n_chunks_of(i)
    n_lane_tiles = kb // LANE
    row = lax.broadcasted_iota(jnp.int32, (qb, kb), 0)
    lane = lax.broadcasted_iota(jnp.int32, (qb, kb), 1)
    qpos = q_pos0 + i * qb + row

    iw = iw_ref[...]
    for h in range(IDX_HEADS):
        wb_ref[h] = jnp.broadcast_to(iw[:, h:h + 1], (qb, LANE))

    def score_chunk(c, carry):
        ikc = ikT_ref[c]
        acc = jnp.zeros((qb, kb), jnp.float32)
        for h in range(IDX_HEADS):
            s = jnp.dot(iq_ref[0, h], ikc, preferred_element_type=jnp.float32)
            acc = acc + jnp.tile(wb_ref[h], (1, n_lane_tiles)) * jnp.maximum(s, 0.0)
        acc = jnp.where(c * kb + lane <= qpos, acc, -jnp.inf)
        key_ref[c] = _ordered_key(acc)
        return carry

    lax.fori_loop(0, nk, score_chunk, 0)

    thr = _topk_threshold(key_ref, cut_ref, nk, rows=qb, kb=kb, topk=topk, n_idx_bits=n_idx_bits)
    thr_b = jnp.broadcast_to(thr, (qb, kb))

    m_ref[...] = jnp.full(m_ref.shape, MASK_BIAS, jnp.float32)
    l_ref[...] = jnp.zeros(l_ref.shape, jnp.float32)
    acc_ref[...] = jnp.zeros(acc_ref.shape, jnp.float32)
    cut_b = jnp.tile(cut_ref[...], (1, n_lane_tiles))
    rep = N_HEADS // KV_HEADS

    def attend_chunk(c, carry):
        key = key_ref[c]
        kpos = c * kb + lane
        sel = (key > thr_b) | ((key == thr_b) & (kpos <= cut_b))
        sel = sel & (kpos <= qpos)
        bias = jnp.where(sel, 0.0, MASK_BIAS)
        kc = kT_ref[c]
        for g in range(KV_HEADS):
            s = jnp.dot(q_ref[0, g], kc[g * HEAD_DIM:(g + 1) * HEAD_DIM, :], preferred_element_type=jnp.float32)
            s = (s.reshape(rep, qb, kb) + bias[None]).reshape(rep * qb, kb)
            m_prev = m_ref[g]
            m_new = jnp.maximum(m_prev, jnp.max(s, axis=-1, keepdims=True))
            alpha = jnp.exp(m_prev - m_new)
            p = jnp.exp(s - jnp.tile(m_new, (1, n_lane_tiles)))
            l_ref[g] = alpha * l_ref[g] + jnp.sum(p, axis=-1, keepdims=True)
            pv = jnp.dot(p.astype(jnp.bfloat16), v_ref[c, g], preferred_element_type=jnp.float32)
            acc_ref[g] = acc_ref[g] * alpha[:, :HEAD_DIM] + pv
            m_ref[g] = m_new
        return carry

    lax.fori_loop(0, nk, attend_chunk, 0)
    for g in range(KV_HEADS):
        o_ref[0, g] = acc_ref[g] / l_ref[g][:, :HEAD_DIM]


def _prompt_dsa_attention(q, k, v, iq, iw, ik, *, qb=128, kb=512):
    s_len = q.shape[0]
    topk = min(IDX_TOPK_MAX, s_len // 4)
    nqb, nkc = s_len // qb, s_len // kb
    rep = N_HEADS // KV_HEADS
    bf = jnp.bfloat16
    scale = HEAD_DIM ** -0.5
    q_t = (q * scale).astype(bf).reshape(nqb, qb, KV_HEADS, rep, HEAD_DIM).transpose(0, 2, 3, 1, 4)
    q_t = q_t.reshape(nqb, KV_HEADS, rep * qb, HEAD_DIM)
    iq_t = iq.astype(bf).reshape(nqb, qb, IDX_HEADS, IDX_DIM).transpose(0, 2, 1, 3)
    ikT = ik.astype(bf).reshape(nkc, kb, IDX_DIM).transpose(0, 2, 1)
    kT = k.astype(bf).reshape(nkc, kb, KV_HEADS * HEAD_DIM).transpose(0, 2, 1)
    v_t = v.astype(bf).reshape(nkc, kb, KV_HEADS, HEAD_DIM).transpose(0, 2, 1, 3)
    n_idx_bits = max(1, (s_len - 1).bit_length())
    body = functools.partial(
        _dsa_kernel, qb=qb, kb=kb, topk=topk, n_idx_bits=n_idx_bits, q_pos0=0,
        n_chunks_of=lambda i: ((i + 1) * qb + kb - 1) // kb)
    whole = lambda shape: pl.BlockSpec(shape, lambda i: (0,) * len(shape), pipeline_mode=pl.Buffered(1))
    out = pl.pallas_call(
        body,
        grid=(nqb,),
        in_specs=[
            pl.BlockSpec((1, IDX_HEADS, qb, IDX_DIM), lambda i: (i, 0, 0, 0)),
            pl.BlockSpec((qb, IDX_HEADS), lambda i: (i, 0)),
            pl.BlockSpec((1, KV_HEADS, rep * qb, HEAD_DIM), lambda i: (i, 0, 0, 0)),
            whole((nkc, IDX_DIM, kb)),
            whole((nkc, KV_HEADS * HEAD_DIM, kb)),
            whole((nkc, KV_HEADS, kb, HEAD_DIM)),
        ],
        out_specs=pl.BlockSpec((1, KV_HEADS, rep * qb, HEAD_DIM), lambda i: (i, 0, 0, 0)),
        out_shape=jax.ShapeDtypeStruct((nqb, KV_HEADS, rep * qb, HEAD_DIM), jnp.float32),
        scratch_shapes=[
            pltpu.VMEM((nkc, qb, kb), jnp.int32),
            pltpu.VMEM((IDX_HEADS, qb, LANE), jnp.float32),
            pltpu.VMEM((qb, LANE), jnp.int32),
            pltpu.VMEM((KV_HEADS, rep * qb, LANE), jnp.float32),
            pltpu.VMEM((KV_HEADS, rep * qb, LANE), jnp.float32),
            pltpu.VMEM((KV_HEADS, rep * qb, HEAD_DIM), jnp.float32),
        ],
        compiler_params=pltpu.CompilerParams(dimension_semantics=("arbitrary",), vmem_limit_bytes=VMEM_LIMIT_BYTES),
        name="prompt_dsa_attention",
    )(iq_t, iw, q_t, ikT, kT, v_t)
    out = out.reshape(nqb, KV_HEADS, rep, qb, HEAD_DIM).transpose(0, 3, 1, 2, 4)
    return out.reshape(s_len, N_HEADS * HEAD_DIM)


def _sample_index_kernel(pt_ref, iq_ref, iw_ref, iknew_ref, *rest, pp, n_pages, topk, n_idx_bits):
    page_refs, o_ref, (key_ref, wide_ref, wb_ref, cut_ref) = rest[:pp], rest[pp], rest[pp + 1:]
    pg = pl.program_id(1)
    t_new = o_ref.shape[2]
    row = lax.broadcasted_iota(jnp.int32, (t_new, PAGE_SIZE), 0)
    lane = lax.broadcasted_iota(jnp.int32, (t_new, PAGE_SIZE), 1)

    @pl.when(pg == 0)
    def _():
        wb_ref[...] = jnp.broadcast_to(iw_ref[0], wb_ref.shape)

    iq = iq_ref[0]

    def scores(ik):
        s = lax.dot_general(iq, ik.astype(jnp.bfloat16), (((1,), (1,)), ((), ())),
                            preferred_element_type=jnp.float32)
        s = wb_ref[...] * jnp.maximum(s, 0.0)
        acc = jnp.zeros((t_new, PAGE_SIZE), jnp.float32)
        for h in range(IDX_HEADS):
            acc = acc + s[h * t_new:(h + 1) * t_new, :]
        return acc

    for j in range(pp):
        key_ref[pg * pp + j] = _ordered_key(scores(page_refs[j][0]))

    @pl.when(pg == pl.num_programs(1) - 1)
    def _():
        s_new = jnp.where(lane <= row, scores(iknew_ref[0]), -jnp.inf)
        key_ref[n_pages] = _ordered_key(s_new)
        n_keys = (n_pages + 1) * PAGE_SIZE
        for c in range(n_pages + 1):
            wide_ref[0, :, c * PAGE_SIZE:(c + 1) * PAGE_SIZE] = key_ref[c]
        thr = _topk_threshold(wide_ref, cut_ref, 1, rows=t_new, kb=n_keys, topk=topk, n_idx_bits=n_idx_bits)
        key = wide_ref[0]
        kpos = lax.broadcasted_iota(jnp.int32, (t_new, n_keys), 1)
        qpos = n_pages * PAGE_SIZE + lax.broadcasted_iota(jnp.int32, (t_new, n_keys), 0)
        thr_b = jnp.broadcast_to(thr, (t_new, n_keys))
        cut_b = jnp.tile(cut_ref[...], (1, n_pages + 1))
        sel = (key > thr_b) | ((key == thr_b) & (kpos <= cut_b))
        bias = jnp.where(sel & (kpos <= qpos), 0.0, MASK_BIAS)
        for c in range(n_pages + 1):
            o_ref[0, c] = bias[:, c * PAGE_SIZE:(c + 1) * PAGE_SIZE]


def _sample_attend_kernel(pt_ref, q_ref, bias_ref, knew_ref, vnew_ref, *rest, pp, n_pages):
    k_refs, v_refs, o_ref, (m_ref, l_ref, acc_ref) = rest[:pp], rest[pp:2 * pp], rest[2 * pp], rest[2 * pp + 1:]
    pg = pl.program_id(1)
    n_rows = q_ref.shape[1]
    t_new = bias_ref.shape[2]

    @pl.when(pg == 0)
    def _():
        m_ref[...] = jnp.full(m_ref.shape, MASK_BIAS, jnp.float32)
        l_ref[...] = jnp.zeros(l_ref.shape, jnp.float32)
        acc_ref[...] = jnp.zeros(acc_ref.shape, jnp.float32)

    q = q_ref[0]

    def step(c, kp, vp):
        s = lax.dot_general(q, kp.astype(jnp.bfloat16), (((1,), (1,)), ((), ())), preferred_element_type=jnp.float32)
        s = s + jnp.tile(bias_ref[0, c], (n_rows // t_new, 1))
        m_prev = m_ref[...]
        m_new = jnp.maximum(m_prev, jnp.max(s, axis=-1, keepdims=True))
        alpha = jnp.exp(m_prev - m_new)
        p = jnp.exp(s - m_new)
        l_ref[...] = alpha * l_ref[...] + jnp.sum(p, axis=-1, keepdims=True)
        pv = jnp.dot(p.astype(jnp.bfloat16), vp.astype(jnp.bfloat16), preferred_element_type=jnp.float32)
        acc_ref[...] = acc_ref[...] * jnp.tile(alpha, (1, KV_W // LANE)) + pv
        m_ref[...] = m_new

    for j in range(pp):
        step(pg * pp + j, k_refs[j][0], v_refs[j][0])

    @pl.when(pg == pl.num_programs(1) - 1)
    def _():
        step(n_pages, knew_ref[0], vnew_ref[0])
        acc = acc_ref[...]
        row_head = lax.broadcasted_iota(jnp.int32, acc.shape, 0) // (n_rows // KV_HEADS)
        col_head = lax.broadcasted_iota(jnp.int32, acc.shape, 1) // HEAD_DIM
        own = jnp.where(row_head == col_head, acc, 0.0)
        folded = own[:, :LANE] + own[:, LANE:]
        folded = folded + pltpu.roll(folded, HEAD_DIM, axis=1)
        o_ref[0] = folded / l_ref[...]


def _sample_dsa_attention(q, k, v, iq, iw, ik, cache_k, cache_v, cache_idx_k, page_table, *, pp=8):
    b, t = q.shape[:2]
    n_pages = page_table.shape[1]
    n_pool = cache_k.shape[0]
    past = n_pages * PAGE_SIZE
    topk = min(IDX_TOPK_MAX, (past + t) // 4)
    n_idx_bits = max(1, (past + PAGE_SIZE - 1).bit_length())
    assert n_pages % pp == 0 and t == SUBLANE and KV_W == 2 * LANE
    npg = n_pages // pp
    bf = jnp.bfloat16
    rep = N_HEADS // KV_HEADS
    n_rows = N_HEADS * t

    iq_r = iq.astype(bf).transpose(0, 2, 1, 3).reshape(b, IDX_HEADS * t, IDX_DIM)
    iw_r = iw.transpose(0, 2, 1).reshape(b, IDX_HEADS * t, 1)
    pad_rows = lambda a: jnp.pad(a, ((0, 0), (0, PAGE_SIZE - t), (0, 0)))
    ik_new = pad_rows(ik)
    k_new = pad_rows(k.reshape(b, t, KV_W))
    v_new = pad_rows(v.reshape(b, t, KV_W))
    qs = (q * HEAD_DIM ** -0.5).astype(bf).reshape(b, t, KV_HEADS, rep, HEAD_DIM).transpose(0, 2, 3, 1, 4)
    q_bd = (qs[:, :, :, :, None, :] * jnp.eye(KV_HEADS, dtype=bf)[None, :, None, None, :, None])
    q_bd = q_bd.reshape(b, n_rows, KV_W)

    page_map = lambda j: (lambda bi, pg, pt: (pt[bi, pg * pp + j], 0, 0))
    per_seq = lambda shape: pl.BlockSpec((1,) + shape, lambda bi, pg, pt: (bi,) + (0,) * len(shape))

    bias = pl.pallas_call(
        functools.partial(_sample_index_kernel, pp=pp, n_pages=n_pages, topk=topk, n_idx_bits=n_idx_bits),
        grid_spec=pltpu.PrefetchScalarGridSpec(
            num_scalar_prefetch=1, grid=(b, npg),
            in_specs=[per_seq((IDX_HEADS * t, IDX_DIM)), per_seq((IDX_HEADS * t, 1)), per_seq((PAGE_SIZE, IDX_DIM))]
            + [pl.BlockSpec((1, PAGE_SIZE, IDX_DIM), page_map(j)) for j in range(pp)],
            out_specs=per_seq((n_pages + 1, t, PAGE_SIZE)),
            scratch_shapes=[pltpu.VMEM((n_pages + 1, t, PAGE_SIZE), jnp.int32),
                            pltpu.VMEM((1, t, (n_pages + 1) * PAGE_SIZE), jnp.int32),
                            pltpu.VMEM((IDX_HEADS * t, PAGE_SIZE), jnp.float32),
                            pltpu.VMEM((t, LANE), jnp.int32)]),
        out_shape=jax.ShapeDtypeStruct((b, n_pages + 1, t, PAGE_SIZE), jnp.float32),
        compiler_params=pltpu.CompilerParams(dimension_semantics=("arbitrary", "arbitrary")),
        name="sample_index",
    )(page_table, iq_r, iw_r, ik_new, *([cache_idx_k] * pp))

    ck = cache_k.reshape(n_pool, PAGE_SIZE, KV_W)
    cv = cache_v.reshape(n_pool, PAGE_SIZE, KV_W)
    out = pl.pallas_call(
        functools.partial(_sample_attend_kernel, pp=pp, n_pages=n_pages),
        grid_spec=pltpu.PrefetchScalarGridSpec(
            num_scalar_prefetch=1, grid=(b, npg),
            in_specs=[per_seq((n_rows, KV_W)), per_seq((n_pages + 1, t, PAGE_SIZE)),
                      per_seq((PAGE_SIZE, KV_W)), per_seq((PAGE_SIZE, KV_W))]
            + [pl.BlockSpec((1, PAGE_SIZE, KV_W), page_map(j)) for j in range(pp)] * 2,
            out_specs=per_seq((n_rows, LANE)),
            scratch_shapes=[pltpu.VMEM((n_rows, LANE), jnp.float32), pltpu.VMEM((n_rows, LANE), jnp.float32),
                            pltpu.VMEM((n_rows, KV_W), jnp.float32)]),
        out_shape=jax.ShapeDtypeStruct((b, n_rows, LANE), jnp.float32),
        compiler_params=pltpu.CompilerParams(dimension_semantics=("arbitrary", "arbitrary")),
        name="sample_attend",
    )(page_table, q_bd, bias, k_new, v_new, *([ck] * pp), *([cv] * pp))
    out = out[:, :, :HEAD_DIM].reshape(b, N_HEADS, t, HEAD_DIM).transpose(0, 2, 1, 3)
    return out.reshape(b, t, N_HEADS * HEAD_DIM)


GELU_C = math.sqrt(2.0 / math.pi)


def _rglru_kernel(x_ref, xg_ref, prev_ref, h0_ref, cw_ref, cb_ref, wa_ref, ba_ref, wi_ref, bi_ref, lam_ref,
                  o_ref, hl_ref, conv_ref, a_ref, u_ref, xprev_ref, h_ref, *, rows, per_group_state):
    i = pl.program_id(0)
    n_groups = rows // SUBLANE
    d_rnn = x_ref.shape[1]
    r8 = lax.broadcasted_iota(jnp.int32, (SUBLANE, d_rnn), 0)
    grp = lambda g: pl.ds(pl.multiple_of(g * SUBLANE, SUBLANE), SUBLANE)

    if not per_group_state:
        @pl.when(i == 0)
        def _():
            xprev_ref[...] = jnp.zeros_like(xprev_ref)
            h_ref[...] = jnp.zeros_like(h_ref)

    cw = cw_ref[...]
    cb = cb_ref[...]

    def conv_group(g, prev):
        x8 = x_ref[grp(g), :]
        if per_group_state:
            prev = prev_ref[grp(g), :]
        out = cb + x8 * cw[CONV_W - 1:CONV_W, :]
        for d in range(1, CONV_W):
            shifted = jnp.where(r8 < d, pltpu.roll(prev, d, axis=0), pltpu.roll(x8, d, axis=0))
            out = out + shifted * cw[CONV_W - 1 - d:CONV_W - d, :]
        conv_ref[grp(g), :] = out
        return x8

    zeros8 = jnp.zeros((SUBLANE, d_rnn), jnp.float32)
    xlast = lax.fori_loop(0, n_groups, conv_group, zeros8 if per_group_state else xprev_ref[...])
    if not per_group_state:
        xprev_ref[...] = xlast

    xc = conv_ref[...]
    xb = xc.astype(jnp.bfloat16)
    r_parts, i_parts = [], []
    for n in range(RG_BLOCKS):
        xn = xb[:, n * RG_BLOCK_W:(n + 1) * RG_BLOCK_W]
        r_parts.append(jnp.dot(xn, wa_ref[n], preferred_element_type=jnp.float32))
        i_parts.append(jnp.dot(xn, wi_ref[n], preferred_element_type=jnp.float32))
    r = jax.nn.sigmoid(jnp.concatenate(r_parts, axis=-1) + ba_ref[...])
    ig = jax.nn.sigmoid(jnp.concatenate(i_parts, axis=-1) + bi_ref[...])
    lam = lam_ref[...]
    log_sig_lam = -(jnp.maximum(-lam, 0.0) + jnp.log1p(jnp.exp(-jnp.abs(lam))))
    log_a = RG_C * r * log_sig_lam
    a = jnp.exp(log_a)
    a_ref[...] = a
    u_ref[...] = jnp.sqrt(-jnp.tanh(log_a) * (a * a + 1.0)) * (ig * xc)

    def scan_group(g, hprev):
        a = a_ref[grp(g), :]
        u = u_ref[grp(g), :]
        if per_group_state:
            hprev = jnp.broadcast_to(h0_ref[pl.ds(g, 1), :], (SUBLANE, d_rnn))
        for d in (1, 2, 4):
            u = jnp.where(r8 >= d, a * pltpu.roll(u, d, axis=0) + u, u)
            a = jnp.where(r8 >= d, a * pltpu.roll(a, d, axis=0), a)
        h = a * hprev + u
        u_ref[grp(g), :] = h
        hlast = jnp.broadcast_to(h[SUBLANE - 1:SUBLANE, :], (SUBLANE, d_rnn))
        if per_group_state:
            hl_ref[pl.ds(g, 1), :] = h[SUBLANE - 1:SUBLANE, :]
        return hlast

    hlast = lax.fori_loop(0, n_groups, scan_group, zeros8 if per_group_state else h_ref[...])
    if not per_group_state:
        h_ref[...] = hlast
        hl_ref[...] = hlast

    xg = xg_ref[...]
    gelu = 0.5 * xg * (1.0 + jnp.tanh(GELU_C * (xg + 0.044715 * (xg * xg * xg))))
    o_ref[...] = u_ref[...] * gelu


def _rglru(x, xg, prev, h0, conv_w, conv_b, w_a, b_a, w_i, b_i, lam, *, per_group_state, tile_rows):
    n_rows, d = x.shape
    row = lambda a: a.reshape(1, d)
    const = lambda shape: pl.BlockSpec(shape, lambda i: (0,) * len(shape))
    n_hl = n_rows // SUBLANE if per_group_state else SUBLANE
    body = functools.partial(_rglru_kernel, rows=tile_rows, per_group_state=per_group_state)
    tile = pl.BlockSpec((tile_rows, d), lambda i: (i, 0))
    if per_group_state:
        assert tile_rows == n_rows
        prev_spec, h0_spec = const(prev.shape), const(h0.shape)
    else:
        prev_spec, h0_spec = const(prev.shape), const(h0.shape)
    return pl.pallas_call(
        body,
        grid=(n_rows // tile_rows,),
        in_specs=[tile, tile, prev_spec, h0_spec, const((CONV_W, d)), const((1, d)),
                  const(w_a.shape), const((1, d)), const(w_i.shape), const((1, d)), const((1, d))],
        out_specs=[tile, const((n_hl, d))],
        out_shape=[jax.ShapeDtypeStruct((n_rows, d), jnp.float32), jax.ShapeDtypeStruct((n_hl, d), jnp.float32)],
        scratch_shapes=[pltpu.VMEM((tile_rows, d), jnp.float32), pltpu.VMEM((tile_rows, d), jnp.float32),
                        pltpu.VMEM((tile_rows, d), jnp.float32), pltpu.VMEM((SUBLANE, d), jnp.float32),
                        pltpu.VMEM((SUBLANE, d), jnp.float32)],
        compiler_params=pltpu.CompilerParams(dimension_semantics=("arbitrary",), vmem_limit_bytes=VMEM_LIMIT_BYTES),
        name="rglru_sample" if per_group_state else "rglru_prompt",
    )(x, xg, prev, h0, conv_w, row(conv_b), w_a.astype(jnp.bfloat16), row(b_a), w_i.astype(jnp.bfloat16), row(b_i), row(lam))


def _rglru_prompt(x, xg, conv_w, conv_b, w_a, b_a, w_i, b_i, lam, tile_rows=512):
    dummy = jnp.zeros((SUBLANE, x.shape[1]), jnp.float32)
    rnn, hl = _rglru(x, xg, dummy, dummy, conv_w, conv_b, w_a, b_a, w_i, b_i, lam,
                     per_group_state=False, tile_rows=tile_rows)
    return rnn, hl[0]


def _rglru_sample(x, xg, state_conv, state_h, conv_w, conv_b, w_a, b_a, w_i, b_i, lam):
    b, t, d = x.shape
    assert t == SUBLANE
    prev = jnp.concatenate([jnp.zeros((b, SUBLANE - (CONV_W - 1), d), jnp.float32), state_conv], axis=1)
    rnn, hl = _rglru(x.reshape(b * t, d), xg.reshape(b * t, d), prev.reshape(b * t, d), state_h,
                     conv_w, conv_b, w_a, b_a, w_i, b_i, lam, per_group_state=True, tile_rows=b * t)
    return rnn.reshape(b, t, d), hl


def _rms_norm(x, g):
    xf = x.astype(jnp.float32)
    y = xf * lax.rsqrt(jnp.mean(xf * xf, axis=-1, keepdims=True) + EPS)
    return (y * g.astype(jnp.float32)).astype(x.dtype)


def _rope(x, pos):
    half = x.shape[-1] // 2
    inv_freq = ROPE_THETA ** (-jnp.arange(half, dtype=jnp.float32) / half)
    ang = pos.astype(jnp.float32)[:, None] * inv_freq[None, :]
    cos = jnp.cos(ang)[:, None, :]
    sin = jnp.sin(ang)[:, None, :]
    xf = x.astype(jnp.float32)
    x1, x2 = xf[..., :half], xf[..., half:]
    return jnp.concatenate([x1 * cos - x2 * sin, x2 * cos + x1 * sin], axis=-1).astype(x.dtype)


def _index_scores(iq, iw, ik):
    s = jnp.einsum('bthd,bld->bthl', iq, ik, preferred_element_type=jnp.float32)
    return jnp.einsum('bthl,bth->btl', jax.nn.relu(s), iw.astype(jnp.float32))


def _sparse_attend(q, k_sel, v_sel, valid):
    B, T = q.shape[:2]
    qg = q.reshape(B, T, KV_HEADS, N_HEADS // KV_HEADS, HEAD_DIM)
    logits = jnp.einsum('btkgd,btskd->btkgs', qg, k_sel, preferred_element_type=jnp.float32) * (HEAD_DIM ** -0.5)
    logits = jnp.where(valid[:, :, None, None, :], logits, -jnp.inf)
    p = jax.nn.softmax(logits, axis=-1)
    out = jnp.einsum('btkgs,btskd->btkgd', p.astype(v_sel.dtype), v_sel)
    return out.reshape(B, T, Q_W)


def _gather_rows(rows, idx):
    return jax.vmap(lambda r, i: r[i])(rows, idx)


def _prompt_sparse_attention(q, k, v, iq, iw, ik):
    B, S = q.shape[:2]
    topk = min(IDX_TOPK_MAX, S // 4)
    n_blocks = S // Q_BLOCK
    key_pos = jnp.arange(S, dtype=jnp.int32)

    def block(args):
        qb, iqb, iwb, start = args
        qpos = start + jnp.arange(Q_BLOCK, dtype=jnp.int32)
        sc = _index_scores(iqb, iwb, ik)
        sc = jnp.where((key_pos[None, :] <= qpos[:, None])[None], sc, -jnp.inf)
        _, sel = lax.top_k(sc, topk)
        valid = sel <= qpos[None, :, None]
        return _sparse_attend(qb, _gather_rows(k, sel), _gather_rows(v, sel), valid)

    to_blocks = lambda a: jnp.moveaxis(a.reshape(B, n_blocks, Q_BLOCK, *a.shape[2:]), 1, 0)
    starts = jnp.arange(n_blocks, dtype=jnp.int32) * Q_BLOCK
    out = lax.map(block, (to_blocks(q), to_blocks(iq), to_blocks(iw), starts))
    return jnp.moveaxis(out, 0, 1).reshape(B, S, Q_W)


def _sample_sparse_attention(q, k, v, iq, iw, ik, cache_k, cache_v, cache_idx_k, page_table):
    DB, T = q.shape[:2]
    past = page_table.shape[1] * PAGE_SIZE
    L = past + T
    topk = min(IDX_TOPK_MAX, L // 4)
    ik_past = cache_idx_k[page_table].reshape(DB, past, IDX_DIM)
    ik_all = jnp.concatenate([ik_past, ik.astype(ik_past.dtype)], axis=1)
    qpos = past + jnp.arange(T, dtype=jnp.int32)
    sc = _index_scores(iq, iw, ik_all)
    sc = jnp.where((jnp.arange(L, dtype=jnp.int32)[None, :] <= qpos[:, None])[None], sc, -jnp.inf)
    _, sel = lax.top_k(sc, topk)
    valid = sel <= qpos[None, :, None]
    in_past = sel < past
    sel_p = jnp.minimum(sel, past - 1)
    page = _gather_rows(page_table, sel_p // PAGE_SIZE)
    phys = page * PAGE_SIZE + sel_p % PAGE_SIZE
    new_idx = jnp.clip(sel - past, 0, T - 1)
    k_pool = cache_k.reshape(-1, KV_HEADS, HEAD_DIM)
    v_pool = cache_v.reshape(-1, KV_HEADS, HEAD_DIM)
    m = in_past[..., None, None]
    k_sel = jnp.where(m, k_pool[phys], _gather_rows(k, new_idx).astype(k_pool.dtype))
    v_sel = jnp.where(m, v_pool[phys], _gather_rows(v, new_idx).astype(v_pool.dtype))
    return _sparse_attend(q, k_sel, v_sel, valid)


def _causal_conv(xr, conv_state, w, b):
    T = xr.shape[1]
    xp = jnp.concatenate([conv_state.astype(xr.dtype), xr], axis=1)
    out = b + sum(xp[:, j:j + T] * w[j] for j in range(CONV_W))
    return out, xp[:, -(CONV_W - 1):]


def _block_diag(x, w, b):
    B, T, _ = x.shape
    y = jnp.einsum('btnc,ncd->btnd', x.reshape(B, T, RG_BLOCKS, RG_BLOCK_W), w)
    return y.reshape(B, T, D_RNN) + b


def _rg_lru(x, h0, w_a, b_a, w_i, b_i, lam):
    xf = x.astype(jnp.float32)
    r = jax.nn.sigmoid(_block_diag(x, w_a, b_a).astype(jnp.float32))
    i = jax.nn.sigmoid(_block_diag(x, w_i, b_i).astype(jnp.float32))
    log_a = RG_C * r * jax.nn.log_sigmoid(lam.astype(jnp.float32))
    a = jnp.exp(log_a)
    u = jnp.sqrt(-jnp.expm1(2.0 * log_a)) * (i * xf)

    def step(h, au):
        h = au[0] * h + au[1]
        return h, h

    h_last, hs = lax.scan(step, h0.astype(jnp.float32), (jnp.moveaxis(a, 1, 0), jnp.moveaxis(u, 1, 0)))
    return jnp.moveaxis(hs, 0, 1).astype(x.dtype), h_last.astype(x.dtype)


def _rms_norm_rows(x, g):
    return x * lax.rsqrt(jnp.mean(x * x, axis=-1, keepdims=True) + EPS) * g


def _router_kernel(x_ref, g_ref, wr_ref, br_ref, tri_ref, e_ref, gate_ref, rank_ref, cnt_ref, run_ref):
    i = pl.program_id(0)
    tm = x_ref.shape[0]

    @pl.when(i == 0)
    def _():
        run_ref[...] = jnp.zeros_like(run_ref)

    xn_bf = _rms_norm_rows(x_ref[...], g_ref[...]).astype(jnp.bfloat16)
    logits = lax.dot_general(wr_ref[...], xn_bf, (((1,), (1,)), ((), ())), preferred_element_type=jnp.float32)
    logits = logits + br_ref[...]

    expert = lax.broadcasted_iota(jnp.int32, (N_EXPERTS, tm), 0)
    member = jnp.zeros((N_EXPERTS, tm), jnp.float32)
    picked, values = [], []
    for k in range(TOP_K):
        mx = jnp.max(logits, axis=0, keepdims=True)
        idx = jnp.min(jnp.where(logits == mx, expert, N_EXPERTS), axis=0, keepdims=True)
        hit = expert == idx
        member = jnp.where(hit, 1.0, member)
        logits = jnp.where(hit, -jnp.inf, logits)
        picked.append(idx)
        values.append(mx)
        e_ref[k:k + 1, :] = idx

    ex = [jnp.exp(v - values[0]) for v in values]
    denom = ex[0] + ex[1] + ex[2] + ex[3]
    for k in range(TOP_K):
        gate_ref[k:k + 1, :] = ex[k] / denom

    before = jnp.dot(member.astype(jnp.bfloat16), tri_ref[...], preferred_element_type=jnp.float32)
    before = before + jnp.tile(run_ref[...], (1, tm // LANE))
    for k in range(TOP_K):
        r = jnp.sum(jnp.where(expert == picked[k], before, 0.0), axis=0, keepdims=True)
        rank_ref[k:k + 1, :] = r.astype(jnp.int32)
    run = run_ref[...] + jnp.sum(member, axis=1, keepdims=True)
    run_ref[...] = run
    cnt_ref[...] = run.astype(jnp.int32)


def _route(x2d, norm_g, w_router, b_router, tm=512):
    n, d = x2d.shape
    tri = (jnp.arange(tm)[:, None] < jnp.arange(tm)[None, :]).astype(jnp.bfloat16)
    const = lambda shape: pl.BlockSpec(shape, lambda i: (0,) * len(shape))
    rows4 = pl.BlockSpec((TOP_K, tm), lambda i: (0, i))
    top_e, gates, rank, cnt = pl.pallas_call(
        _router_kernel,
        grid=(n // tm,),
        in_specs=[pl.BlockSpec((tm, d), lambda i: (i, 0)), const((1, d)), const((N_EXPERTS, d)),
                  const((N_EXPERTS, 1)), const((tm, tm))],
        out_specs=[rows4, rows4, rows4, const((N_EXPERTS, LANE))],
        out_shape=[jax.ShapeDtypeStruct((TOP_K, n), jnp.int32),
                   jax.ShapeDtypeStruct((TOP_K, n), jnp.float32), jax.ShapeDtypeStruct((TOP_K, n), jnp.int32),
                   jax.ShapeDtypeStruct((N_EXPERTS, LANE), jnp.int32)],
        scratch_shapes=[pltpu.VMEM((N_EXPERTS, LANE), jnp.float32)],
        compiler_params=pltpu.CompilerParams(dimension_semantics=("arbitrary",)),
        name="moe_router",
    )(x2d, norm_g.reshape(1, d), w_router.T.astype(jnp.bfloat16), b_router.reshape(N_EXPERTS, 1), tri)
    return top_e, gates, rank, cnt[:, 0]


def _expert_kernel(be_ref, used_ref, xs_ref, g_ref, wgu_ref, bgu_ref, wd_ref, bd_ref, o_ref, *, blk):
    i = pl.program_id(0)

    @pl.when(i * blk < used_ref[0])
    def _():
        xn = _rms_norm_rows(xs_ref[...], g_ref[...]).astype(jnp.bfloat16)
        gu = jnp.dot(xn, wgu_ref[0], preferred_element_type=jnp.float32) + bgu_ref[0]
        g = jnp.minimum(gu[:, :D_FF], SWIGLU_LIMIT)
        u = jnp.clip(gu[:, D_FF:], -SWIGLU_LIMIT, SWIGLU_LIMIT)
        act = (u + 1.0) * (g * jax.nn.sigmoid(SWIGLU_ALPHA * g))
        o_ref[...] = jnp.dot(act.astype(jnp.bfloat16), wd_ref[0], preferred_element_type=jnp.float32) + bd_ref[0]

    @pl.when(i * blk >= used_ref[0])
    def _():
        o_ref[...] = jnp.zeros_like(o_ref)


def _experts(xs, norm_g, blk_exp, n_used, w_gate_up, b_gate_up, w_down, b_down, *, blk):
    n_slots, d = xs.shape
    n_blocks = n_slots // blk
    bf = jnp.bfloat16
    return pl.pallas_call(
        functools.partial(_expert_kernel, blk=blk),
        grid_spec=pltpu.PrefetchScalarGridSpec(
            num_scalar_prefetch=2, grid=(n_blocks,),
            in_specs=[pl.BlockSpec((blk, d), lambda i, be, nu: (i, 0)),
                      pl.BlockSpec((1, d), lambda i, be, nu: (0, 0)),
                      pl.BlockSpec((1, d, 2 * D_FF), lambda i, be, nu: (be[i], 0, 0)),
                      pl.BlockSpec((1, 1, 2 * D_FF), lambda i, be, nu: (be[i], 0, 0)),
                      pl.BlockSpec((1, D_FF, d), lambda i, be, nu: (be[i], 0, 0)),
                      pl.BlockSpec((1, 1, d), lambda i, be, nu: (be[i], 0, 0))],
            out_specs=pl.BlockSpec((blk, d), lambda i, be, nu: (i, 0))),
        out_shape=jax.ShapeDtypeStruct((n_slots, d), jnp.float32),
        compiler_params=pltpu.CompilerParams(dimension_semantics=("arbitrary",), vmem_limit_bytes=VMEM_LIMIT_BYTES),
        name="moe_experts",
    )(blk_exp, n_used, xs, norm_g.reshape(1, d), w_gate_up.astype(bf), b_gate_up.reshape(N_EXPERTS, 1, 2 * D_FF),
      w_down.astype(bf), b_down.reshape(N_EXPERTS, 1, d))


def _row_copy(src_hbm, src_row, dst_ref, dst_row, sem):
    return pltpu.make_async_copy(src_hbm.at[pl.ds(src_row, 1)], dst_ref.at[pl.ds(dst_row, 1)], sem)


def _dispatch_kernel(dest_ref, x_hbm, xs_in_hbm, xs_hbm, sem, *, tm):
    del xs_in_hbm
    base = pl.program_id(0) * tm

    def issue(t, carry):
        for k in range(TOP_K):
            _row_copy(x_hbm, base + t, xs_hbm, dest_ref[k, t], sem).start()
        return carry

    lax.fori_loop(0, tm, issue, 0)

    def drain(t, carry):
        for k in range(TOP_K):
            _row_copy(x_hbm, 0, xs_hbm, 0, sem).wait()
        return carry

    lax.fori_loop(0, tm, drain, 0)


def _dispatch(x2d, dest, n_slots, tm=512):
    n, d = x2d.shape
    return pl.pallas_call(
        functools.partial(_dispatch_kernel, tm=tm),
        grid=(n // tm,),
        in_specs=[pl.BlockSpec((TOP_K, tm), lambda i: (0, i), memory_space=pltpu.SMEM),
                  pl.BlockSpec(memory_space=pl.ANY), pl.BlockSpec(memory_space=pl.ANY)],
        out_specs=pl.BlockSpec(memory_space=pl.ANY),
        out_shape=jax.ShapeDtypeStruct((n_slots, d), x2d.dtype),
        scratch_shapes=[pltpu.SemaphoreType.DMA(())],
        input_output_aliases={2: 0},
        compiler_params=pltpu.CompilerParams(dimension_semantics=("arbitrary",)),
        name="moe_dispatch",
    )(dest, x2d, jnp.zeros((n_slots, d), x2d.dtype))


def _combine_kernel(dest_ref, x_ref, gate_ref, ys_hbm, o_ref, buf_ref, sem, *, tm):
    def issue(t, carry):
        for k in range(TOP_K):
            _row_copy(ys_hbm, dest_ref[k, t], buf_ref.at[k], t, sem).start()
        return carry

    lax.fori_loop(0, tm, issue, 0)

    def drain(t, carry):
        for k in range(TOP_K):
            _row_copy(ys_hbm, 0, buf_ref.at[k], 0, sem).wait()
        return carry

    lax.fori_loop(0, tm, drain, 0)
    gate = gate_ref[...]
    acc = x_ref[...]
    for k in range(TOP_K):
        acc = acc + gate[:, k:k + 1] * buf_ref[k]
    o_ref[...] = acc


def _combine(x2d, ys, dest, gates_t, tm=128):
    n, d = x2d.shape
    return pl.pallas_call(
        functools.partial(_combine_kernel, tm=tm),
        grid=(n // tm,),
        in_specs=[pl.BlockSpec((TOP_K, tm), lambda i: (0, i), memory_space=pltpu.SMEM),
                  pl.BlockSpec((tm, d), lambda i: (i, 0)), pl.BlockSpec((tm, TOP_K), lambda i: (i, 0)),
                  pl.BlockSpec(memory_space=pl.ANY)],
        out_specs=pl.BlockSpec((tm, d), lambda i: (i, 0)),
        out_shape=jax.ShapeDtypeStruct((n, d), jnp.float32),
        scratch_shapes=[pltpu.VMEM((TOP_K, tm, d), jnp.float32), pltpu.SemaphoreType.DMA(())],
        compiler_params=pltpu.CompilerParams(dimension_semantics=("arbitrary",)),
        name="moe_combine",
    )(dest, x2d, gates_t, ys)


MOE_ROWS = 512


def _moe(x2d, norm_g, w_router, b_router, w_gate_up, b_gate_up, w_down, b_down, *, blk=MOE_ROWS, tm=512):
    n, d = x2d.shape
    top_e, gates, rank, counts = _route(x2d, norm_g, w_router, b_router, tm=tm)
    padded = (counts + blk - 1) // blk * blk
    pad_end = jnp.cumsum(padded)
    pad_start = pad_end - padded
    dest = pad_start[top_e] + rank
    n_blocks = -(-(n * TOP_K + N_EXPERTS * (blk - 1)) // blk)
    n_slots = n_blocks * blk
    blk_exp = jnp.minimum(jnp.searchsorted(pad_end, jnp.arange(n_blocks, dtype=jnp.int32) * blk, side='right'),
                          N_EXPERTS - 1).astype(jnp.int32)
    n_used = pad_end[-1:].astype(jnp.int32)
    xs = _dispatch(x2d, dest, n_slots)
    ys = _experts(xs, norm_g, blk_exp, n_used, w_gate_up, b_gate_up, w_down, b_down, blk=blk)
    return _combine(x2d, ys, dest, gates.T)


def _trunk_layer(x, pos, attend, conv_state, h0, norm1_g, w_in, q_norm_g, k_norm_g, conv_w, conv_b,
                 rg_w_a, rg_b_a, rg_w_i, rg_b_i, rg_lambda, w_attn_out, w_rnn_out, w_out, norm2_g,
                 w_router, b_router, w_gate_up, b_gate_up, w_down, b_down):
    B, T, _ = x.shape
    n_pad = (-D_IN) % 896
    w_pad = jnp.pad(w_in, ((0, 0), (0, n_pad))).astype(jnp.bfloat16)
    z = _norm_proj(x.reshape(B * T, D_MODEL), norm1_g, w_pad)[:, :D_IN].reshape(B, T, D_IN)
    q, k, v, iq, ik, iw, xr, xg, ga, gr = jnp.split(z, np.cumsum(IN_WIDTHS)[:-1].tolist(), axis=-1)
    q = _rope(_rms_norm(q.reshape(B, T, N_HEADS, HEAD_DIM), q_norm_g), pos)
    k = _rope(_rms_norm(k.reshape(B, T, KV_HEADS, HEAD_DIM), k_norm_g), pos)
    v = v.reshape(B, T, KV_HEADS, HEAD_DIM)
    iq = _rope(iq.reshape(B, T, IDX_HEADS, IDX_DIM), pos)
    ik = _rope(ik[:, :, None, :], pos)[:, :, 0]
    iw = iw * (IDX_HEADS ** -0.5 * IDX_DIM ** -0.5)
    rg = (conv_w, conv_b, rg_w_a, rg_b_a, rg_w_i, rg_b_i, rg_lambda)
    if attend is None:
        assert B == 1
        attn = _prompt_dsa_attention(q[0], k[0], v[0], iq[0], iw[0], ik[0])[None]
        rnn, h_new = _rglru_prompt(xr[0], xg[0], *rg)
        rnn, h_new = rnn[None], h_new[None]
        conv_new = xr[:, -(CONV_W - 1):]
    else:
        attn = attend(q, k, v, iq, iw, ik)
        rnn, h_new = _rglru_sample(xr, xg, conv_state, h0, *rg)
        conv_new = jnp.concatenate([conv_state, xr], axis=1)[:, -(CONV_W - 1):]
    merged = jax.nn.sigmoid(ga) * (attn @ w_attn_out) + jax.nn.sigmoid(gr) * (rnn @ w_rnn_out)
    x = x + merged @ w_out
    return x, (k, v, ik, conv_new, h_new)


def kernel(x_prompt, x_sample, cache_k, cache_v, cache_idx_k, state_conv, state_h, page_table,
           norm1_g, w_in, q_norm_g, k_norm_g, conv_w, conv_b, rg_w_a, rg_b_a, rg_w_i, rg_b_i, rg_lambda,
           w_attn_out, w_rnn_out, w_out, norm2_g, w_router, b_router, w_gate_up, b_gate_up, w_down, b_down):
    weights = (norm1_g, w_in, q_norm_g, k_norm_g, conv_w, conv_b, rg_w_a, rg_b_a, rg_w_i, rg_b_i, rg_lambda,
               w_attn_out, w_rnn_out, w_out, norm2_g, w_router, b_router, w_gate_up, b_gate_up, w_down, b_down)
    B, S = x_prompt.shape[:2]
    T = x_sample.shape[1]
    past = page_table.shape[1] * PAGE_SIZE
    pos_p = jnp.arange(S, dtype=jnp.int32)
    pos_s = past + jnp.arange(T, dtype=jnp.int32)
    conv0 = jnp.zeros((B, CONV_W - 1, D_RNN), x_prompt.dtype)
    h0 = jnp.zeros((B, D_RNN), jnp.float32)
    wl = [w[0] for w in weights]
    yp, st_p = _trunk_layer(x_prompt, pos_p, None, conv0, h0, *wl)
    attend_s = functools.partial(_sample_dsa_attention, cache_k=cache_k[0], cache_v=cache_v[0],
                                 cache_idx_k=cache_idx_k[0], page_table=page_table)
    ys, st_s = _trunk_layer(x_sample, pos_s, attend_s, state_conv[0], state_h[0], *wl)
    x_all = jnp.concatenate([yp.reshape(-1, D_MODEL), ys.reshape(-1, D_MODEL)], axis=0)
    x_all = _moe(x_all, norm2_g[0], w_router[0], b_router[0], w_gate_up[0], b_gate_up[0], w_down[0], b_down[0])
    yp = x_all[:B * S].reshape(x_prompt.shape)
    ys = x_all[B * S:].reshape(x_sample.shape)
    k_p, v_p, ik_p, conv_p, h_p = [a[None] for a in st_p]
    k_s, v_s, ik_s, conv_s, h_s = [a[None] for a in st_s]
    return (yp, ys, k_p, v_p, ik_p, conv_p, h_p, k_s, v_s, ik_s, conv_s, h_s)
```

```python
import functools
import math

import jax
import jax.numpy as jnp
import numpy as np
from jax import lax
from jax.experimental import pallas as pl
from jax.experimental.pallas import tpu as pltpu

D_MODEL = 1024
PAGE_SIZE = 128
N_HEADS = 16
HEAD_DIM = 64
KV_HEADS = 4
IDX_HEADS = 8
IDX_DIM = 64
IDX_TOPK_MAX = 256
Q_BLOCK = 128
ROPE_THETA = 10000.0
D_RNN = D_MODEL
RG_BLOCKS = 4
RG_BLOCK_W = D_RNN // RG_BLOCKS
CONV_W = 4
RG_C = 8.0
N_EXPERTS = 32
TOP_K = 4
D_FF = D_MODEL
SWIGLU_LIMIT = 7.0
SWIGLU_ALPHA = 1.702
MOE_BLOCK = 128
EPS = 1e-6

Q_W = N_HEADS * HEAD_DIM
KV_W = KV_HEADS * HEAD_DIM
IQ_W = IDX_HEADS * IDX_DIM
IN_WIDTHS = (Q_W, KV_W, KV_W, IQ_W, IDX_DIM, IDX_HEADS, D_RNN, D_RNN, D_MODEL, D_MODEL)
D_IN = sum(IN_WIDTHS)

LANE = 128
SUBLANE = 8
VMEM_LIMIT_BYTES = 48 * 1024 * 1024


def _norm_proj_kernel(x_ref, g_ref, w_ref, o_ref):
    x = x_ref[...]
    y = x * lax.rsqrt(jnp.mean(x * x, axis=-1, keepdims=True) + EPS) * g_ref[...]
    o_ref[...] = jnp.dot(y.astype(jnp.bfloat16), w_ref[...], preferred_element_type=jnp.float32)


def _norm_proj(x2d, g, w_bf16, tm=512, tn=896):
    m, d = x2d.shape
    n = w_bf16.shape[1]
    return pl.pallas_call(
        _norm_proj_kernel,
        grid=(m // tm, n // tn),
        in_specs=[
            pl.BlockSpec((tm, d), lambda i, j: (i, 0)),
            pl.BlockSpec((1, d), lambda i, j: (0, 0)),
            pl.BlockSpec((d, tn), lambda i, j: (0, j)),
        ],
        out_specs=pl.BlockSpec((tm, tn), lambda i, j: (i, j)),
        out_shape=jax.ShapeDtypeStruct((m, n), jnp.float32),
        name="norm_proj",
    )(x2d, g.reshape(1, d), w_bf16)


INT_MIN = -(2 ** 31)
INT_MAX = 2 ** 31 - 1
MASK_BIAS = -1e30
NEG_INF_BITS_MASK = 0x7FFFFFFF


def _ordered_key(x):
    bits = pltpu.bitcast(x, jnp.int32)
    return bits ^ ((bits >> 31) & NEG_INF_BITS_MASK)


def _topk_threshold(key_ref, cut_ref, nk, *, rows, kb, topk, n_idx_bits):
    n_lane_tiles = kb // LANE
    lane = lax.broadcasted_iota(jnp.int32, (rows, kb), 1)

    def count(pred_of_chunk):
        def body(c, cnt):
            p = pred_of_chunk(c, key_ref[c])
            for j in range(n_lane_tiles):
                cnt = cnt + jnp.where(p[:, j * LANE:(j + 1) * LANE], 1, 0)
            return cnt
        cnt = lax.fori_loop(0, nk, body, jnp.zeros((rows, LANE), jnp.int32))
        return jnp.sum(cnt, axis=-1, keepdims=True)

    def bit_step(b, carry):
        thr, n_ge = carry
        cand = thr ^ (jnp.int32(1) << (31 - b))
        cand_b = jnp.broadcast_to(cand, (rows, kb))
        cnt = count(lambda c, key: key >= cand_b)
        ok = cnt >= topk
        return jnp.where(ok, cand, thr), jnp.where(ok, cnt, n_ge)

    thr0 = jnp.full((rows, 1), INT_MIN, jnp.int32)
    n0 = jnp.full((rows, 1), 0, jnp.int32) + nk * kb
    thr, n_ge = lax.fori_loop(0, 32, bit_step, (thr0, n0))
    thr_b = jnp.broadcast_to(thr, (rows, kb))

    cut_ref[...] = jnp.full((rows, LANE), INT_MAX, jnp.int32)

    @pl.when(jnp.max(n_ge) > topk)
    def _():
        n_eq = count(lambda c, key: key == thr_b)
        need = topk - (n_ge - n_eq)

        def idx_step(b, lo):
            step = jnp.int32(1) << (n_idx_bits - 1 - b)
            mid_b = jnp.broadcast_to(lo + step - 1, (rows, kb))
            f = count(lambda c, key: (key == thr_b) & (c * kb + lane <= mid_b))
            return jnp.where(f < need, lo + step, lo)

        lo = lax.fori_loop(0, n_idx_bits, idx_step, jnp.zeros((rows, 1), jnp.int32))
        cut = jnp.where(n_ge > topk, lo, INT_MAX)
        cut_ref[...] = jnp.broadcast_to(cut, (rows, LANE))

    return thr


def _dsa_kernel(iq_ref, iw_ref, q_ref, ikT_ref, kT_ref, v_ref, o_ref,
                key_ref, wb_ref, cut_ref, m_ref, l_ref, acc_ref, *, qb, kb, topk, n_idx_bits, q_pos0, n_chunks_of):
    i = pl.program_id(0)
    nk = n_chunks_of(i)
    n_lane_tiles = kb // LANE
    row = lax.broadcasted_iota(jnp.int32, (qb, kb), 0)
    lane = lax.broadcasted_iota(jnp.int32, (qb, kb), 1)
    qpos = q_pos0 + i * qb + row

    iw = iw_ref[...]
    for h in range(IDX_HEADS):
        wb_ref[h] = jnp.broadcast_to(iw[:, h:h + 1], (qb, LANE))

    def score_chunk(c, carry):
        ikc = ikT_ref[c]
        acc = jnp.zeros((qb, kb), jnp.float32)
        for h in range(IDX_HEADS):
            s = jnp.dot(iq_ref[0, h], ikc, preferred_element_type=jnp.float32)
            acc = acc + jnp.tile(wb_ref[h], (1, n_lane_tiles)) * jnp.maximum(s, 0.0)
        acc = jnp.where(c * kb + lane <= qpos, acc, -jnp.inf)
        key_ref[c] = _ordered_key(acc)
        return carry

    lax.fori_loop(0, nk, score_chunk, 0)

    thr = _topk_threshold(key_ref, cut_ref, nk, rows=qb, kb=kb, topk=topk, n_idx_bits=n_idx_bits)
    thr_b = jnp.broadcast_to(thr, (qb, kb))

    m_ref[...] = jnp.full(m_ref.shape, MASK_BIAS, jnp.float32)
    l_ref[...] = jnp.zeros(l_ref.shape, jnp.float32)
    acc_ref[...] = jnp.zeros(acc_ref.shape, jnp.float32)
    cut_b = jnp.tile(cut_ref[...], (1, n_lane_tiles))
    rep = N_HEADS // KV_HEADS

    def attend_chunk(c, carry):
        key = key_ref[c]
        kpos = c * kb + lane
        sel = (key > thr_b) | ((key == thr_b) & (kpos <= cut_b))
        sel = sel & (kpos <= qpos)
        bias = jnp.where(sel, 0.0, MASK_BIAS)
        kc = kT_ref[c]
        for g in range(KV_HEADS):
            s = jnp.dot(q_ref[0, g], kc[g * HEAD_DIM:(g + 1) * HEAD_DIM, :], preferred_element_type=jnp.float32)
            s = (s.reshape(rep, qb, kb) + bias[None]).reshape(rep * qb, kb)
            m_prev = m_ref[g]
            m_new = jnp.maximum(m_prev, jnp.max(s, axis=-1, keepdims=True))
            alpha = jnp.exp(m_prev - m_new)
            p = jnp.exp(s - jnp.tile(m_new, (1, n_lane_tiles)))
            l_ref[g] = alpha * l_ref[g] + jnp.sum(p, axis=-1, keepdims=True)
            pv = jnp.dot(p.astype(jnp.bfloat16), v_ref[c, g], preferred_element_type=jnp.float32)
            acc_ref[g] = acc_ref[g] * alpha[:, :HEAD_DIM] + pv
            m_ref[g] = m_new
        return carry

    lax.fori_loop(0, nk, attend_chunk, 0)
    for g in range(KV_HEADS):
        o_ref[0, g] = acc_ref[g] / l_ref[g][:, :HEAD_DIM]


def _prompt_dsa_attention(q, k, v, iq, iw, ik, *, qb=128, kb=512):
    s_len = q.shape[0]
    topk = min(IDX_TOPK_MAX, s_len // 4)
    nqb, nkc = s_len // qb, s_len // kb
    rep = N_HEADS // KV_HEADS
    bf = jnp.bfloat16
    scale = HEAD_DIM ** -0.5
    q_t = (q * scale).astype(bf).reshape(nqb, qb, KV_HEADS, rep, HEAD_DIM).transpose(0, 2, 3, 1, 4)
    q_t = q_t.reshape(nqb, KV_HEADS, rep * qb, HEAD_DIM)
    iq_t = iq.astype(bf).reshape(nqb, qb, IDX_HEADS, IDX_DIM).transpose(0, 2, 1, 3)
    ikT = ik.astype(bf).reshape(nkc, kb, IDX_DIM).transpose(0, 2, 1)
    kT = k.astype(bf).reshape(nkc, kb, KV_HEADS * HEAD_DIM).transpose(0, 2, 1)
    v_t = v.astype(bf).reshape(nkc, kb, KV_HEADS, HEAD_DIM).transpose(0, 2, 1, 3)
    n_idx_bits = max(1, (s_len - 1).bit_length())
    body = functools.partial(
        _dsa_kernel, qb=qb, kb=kb, topk=topk, n_idx_bits=n_idx_bits, q_pos0=0,
        n_chunks_of=lambda i: ((i + 1) * qb + kb - 1) // kb)
    whole = lambda shape: pl.BlockSpec(shape, lambda i: (0,) * len(shape), pipeline_mode=pl.Buffered(1))
    out = pl.pallas_call(
        body,
        grid=(nqb,),
        in_specs=[
            pl.BlockSpec((1, IDX_HEADS, qb, IDX_DIM), lambda i: (i, 0, 0, 0)),
            pl.BlockSpec((qb, IDX_HEADS), lambda i: (i, 0)),
            pl.BlockSpec((1, KV_HEADS, rep * qb, HEAD_DIM), lambda i: (i, 0, 0, 0)),
            whole((nkc, IDX_DIM, kb)),
            whole((nkc, KV_HEADS * HEAD_DIM, kb)),
            whole((nkc, KV_HEADS, kb, HEAD_DIM)),
        ],
        out_specs=pl.BlockSpec((1, KV_HEADS, rep * qb, HEAD_DIM), lambda i: (i, 0, 0, 0)),
        out_shape=jax.ShapeDtypeStruct((nqb, KV_HEADS, rep * qb, HEAD_DIM), jnp.float32),
        scratch_shapes=[
            pltpu.VMEM((nkc, qb, kb), jnp.int32),
            pltpu.VMEM((IDX_HEADS, qb, LANE), jnp.float32),
            pltpu.VMEM((qb, LANE), jnp.int32),
            pltpu.VMEM((KV_HEADS, rep * qb, LANE), jnp.float32),
            pltpu.VMEM((KV_HEADS, rep * qb, LANE), jnp.float32),
            pltpu.VMEM((KV_HEADS, rep * qb, HEAD_DIM), jnp.float32),
        ],
        compiler_params=pltpu.CompilerParams(dimension_semantics=("arbitrary",), vmem_limit_bytes=VMEM_LIMIT_BYTES),
        name="prompt_dsa_attention",
    )(iq_t, iw, q_t, ikT, kT, v_t)
    out = out.reshape(nqb, KV_HEADS, rep, qb, HEAD_DIM).transpose(0, 3, 1, 2, 4)
    return out.reshape(s_len, N_HEADS * HEAD_DIM)


def _sample_index_kernel(pt_ref, iq_ref, iw_ref, iknew_ref, *rest, pp, n_pages, topk, n_idx_bits):
    page_refs, o_ref, (key_ref, wide_ref, wb_ref, cut_ref) = rest[:pp], rest[pp], rest[pp + 1:]
    pg = pl.program_id(1)
    t_new = o_ref.shape[1]
    step_keys = pp * PAGE_SIZE
    n_keys = (n_pages + 1) * PAGE_SIZE

    @pl.when(pg == 0)
    def _():
        wb_ref[...] = jnp.broadcast_to(iw_ref[0], wb_ref.shape)

    iq = iq_ref[0]

    def scores(ik):
        s = lax.dot_general(iq, ik.astype(jnp.bfloat16), (((1,), (1,)), ((), ())),
                            preferred_element_type=jnp.float32)
        s = jnp.tile(wb_ref[...], (1, ik.shape[0] // LANE)) * jnp.maximum(s, 0.0)
        acc = jnp.zeros((t_new, ik.shape[0]), jnp.float32)
        for h in range(IDX_HEADS):
            acc = acc + s[h * t_new:(h + 1) * t_new, :]
        return acc

    key_ref[pg] = _ordered_key(scores(jnp.concatenate([r[0] for r in page_refs], axis=0)))

    @pl.when(pg == pl.num_programs(1) - 1)
    def _():
        for c in range(n_pages // pp):
            wide_ref[0, :, c * step_keys:(c + 1) * step_keys] = key_ref[c]
        row = lax.broadcasted_iota(jnp.int32, (t_new, PAGE_SIZE), 0)
        lane = lax.broadcasted_iota(jnp.int32, (t_new, PAGE_SIZE), 1)
        s_new = jnp.where(lane <= row, scores(iknew_ref[0]), -jnp.inf)
        wide_ref[0, :, n_pages * PAGE_SIZE:] = _ordered_key(s_new)
        thr = _topk_threshold(wide_ref, cut_ref, 1, rows=t_new, kb=n_keys, topk=topk, n_idx_bits=n_idx_bits)
        key = wide_ref[0]
        kpos = lax.broadcasted_iota(jnp.int32, (t_new, n_keys), 1)
        qpos = n_pages * PAGE_SIZE + lax.broadcasted_iota(jnp.int32, (t_new, n_keys), 0)
        thr_b = jnp.broadcast_to(thr, (t_new, n_keys))
        cut_b = jnp.tile(cut_ref[...], (1, n_pages + 1))
        sel = (key > thr_b) | ((key == thr_b) & (kpos <= cut_b))
        o_ref[0] = jnp.where(sel & (kpos <= qpos), 0.0, MASK_BIAS)


def _sample_attend_kernel(pt_ref, q_ref, bias_ref, biasnew_ref, knew_ref, vnew_ref, *rest, pp, n_pages):
    k_refs, v_refs, o_ref, (m_ref, l_ref, acc_ref) = rest[:pp], rest[pp:2 * pp], rest[2 * pp], rest[2 * pp + 1:]
    pg = pl.program_id(1)
    n_rows = q_ref.shape[1]
    t_new = bias_ref.shape[1]

    @pl.when(pg == 0)
    def _():
        m_ref[...] = jnp.full(m_ref.shape, MASK_BIAS, jnp.float32)
        l_ref[...] = jnp.zeros(l_ref.shape, jnp.float32)
        acc_ref[...] = jnp.zeros(acc_ref.shape, jnp.float32)

    q = q_ref[0]

    def step(bias, kp, vp):
        n_keys = kp.shape[0]
        s = lax.dot_general(q, kp.astype(jnp.bfloat16), (((1,), (1,)), ((), ())), preferred_element_type=jnp.float32)
        s = s + jnp.tile(bias, (n_rows // t_new, 1))
        m_prev = m_ref[...]
        m_new = jnp.maximum(m_prev, jnp.max(s, axis=-1, keepdims=True))
        alpha = jnp.exp(m_prev - m_new)
        p = jnp.exp(s - jnp.tile(m_new, (1, n_keys // LANE)))
        l_ref[...] = alpha * l_ref[...] + jnp.sum(p, axis=-1, keepdims=True)
        pv = jnp.dot(p.astype(jnp.bfloat16), vp.astype(jnp.bfloat16), preferred_element_type=jnp.float32)
        acc_ref[...] = acc_ref[...] * jnp.tile(alpha, (1, KV_W // LANE)) + pv
        m_ref[...] = m_new

    step(bias_ref[0], jnp.concatenate([r[0] for r in k_refs], axis=0), jnp.concatenate([r[0] for r in v_refs], axis=0))

    @pl.when(pg == pl.num_programs(1) - 1)
    def _():
        step(biasnew_ref[0], knew_ref[0], vnew_ref[0])
        acc = acc_ref[...]
        row_head = lax.broadcasted_iota(jnp.int32, acc.shape, 0) // (n_rows // KV_HEADS)
        col_head = lax.broadcasted_iota(jnp.int32, acc.shape, 1) // HEAD_DIM
        own = jnp.where(row_head == col_head, acc, 0.0)
        folded = own[:, :LANE] + own[:, LANE:]
        folded = folded + pltpu.roll(folded, HEAD_DIM, axis=1)
        o_ref[0] = folded / l_ref[...]


def _sample_dsa_attention(q, k, v, iq, iw, ik, cache_k, cache_v, cache_idx_k, page_table, *, pp=8):
    b, t = q.shape[:2]
    n_pages = page_table.shape[1]
    n_pool = cache_k.shape[0]
    past = n_pages * PAGE_SIZE
    topk = min(IDX_TOPK_MAX, (past + t) // 4)
    n_idx_bits = max(1, (past + PAGE_SIZE - 1).bit_length())
    assert n_pages % pp == 0 and t == SUBLANE and KV_W == 2 * LANE
    npg = n_pages // pp
    n_keys = past + PAGE_SIZE
    bf = jnp.bfloat16
    rep = N_HEADS // KV_HEADS
    n_rows = N_HEADS * t

    iq_r = iq.astype(bf).transpose(0, 2, 1, 3).reshape(b, IDX_HEADS * t, IDX_DIM)
    iw_r = iw.transpose(0, 2, 1).reshape(b, IDX_HEADS * t, 1)
    pad_rows = lambda a: jnp.pad(a, ((0, 0), (0, PAGE_SIZE - t), (0, 0)))
    ik_new = pad_rows(ik)
    k_new = pad_rows(k.reshape(b, t, KV_W))
    v_new = pad_rows(v.reshape(b, t, KV_W))
    qs = (q * HEAD_DIM ** -0.5).astype(bf).reshape(b, t, KV_HEADS, rep, HEAD_DIM).transpose(0, 2, 3, 1, 4)
    q_bd = (qs[:, :, :, :, None, :] * jnp.eye(KV_HEADS, dtype=bf)[None, :, None, None, :, None])
    q_bd = q_bd.reshape(b, n_rows, KV_W)

    page_map = lambda j: (lambda bi, pg, pt: (pt[bi, pg * pp + j], 0, 0))
    per_seq = lambda shape: pl.BlockSpec((1,) + shape, lambda bi, pg, pt: (bi,) + (0,) * len(shape))

    bias = pl.pallas_call(
        functools.partial(_sample_index_kernel, pp=pp, n_pages=n_pages, topk=topk, n_idx_bits=n_idx_bits),
        grid_spec=pltpu.PrefetchScalarGridSpec(
            num_scalar_prefetch=1, grid=(b, npg),
            in_specs=[per_seq((IDX_HEADS * t, IDX_DIM)), per_seq((IDX_HEADS * t, 1)), per_seq((PAGE_SIZE, IDX_DIM))]
            + [pl.BlockSpec((1, PAGE_SIZE, IDX_DIM), page_map(j)) for j in range(pp)],
            out_specs=per_seq((t, n_keys)),
            scratch_shapes=[pltpu.VMEM((npg, t, pp * PAGE_SIZE), jnp.int32),
                            pltpu.VMEM((1, t, n_keys), jnp.int32),
                            pltpu.VMEM((IDX_HEADS * t, PAGE_SIZE), jnp.float32),
                            pltpu.VMEM((t, LANE), jnp.int32)]),
        out_shape=jax.ShapeDtypeStruct((b, t, n_keys), jnp.float32),
        compiler_params=pltpu.CompilerParams(dimension_semantics=("arbitrary", "arbitrary")),
        name="sample_index",
    )(page_table, iq_r, iw_r, ik_new, *([cache_idx_k] * pp))

    ck = cache_k.reshape(n_pool, PAGE_SIZE, KV_W)
    cv = cache_v.reshape(n_pool, PAGE_SIZE, KV_W)
    out = pl.pallas_call(
        functools.partial(_sample_attend_kernel, pp=pp, n_pages=n_pages),
        grid_spec=pltpu.PrefetchScalarGridSpec(
            num_scalar_prefetch=1, grid=(b, npg),
            in_specs=[per_seq((n_rows, KV_W)),
                      pl.BlockSpec((1, t, pp * PAGE_SIZE), lambda bi, pg, pt: (bi, 0, pg)),
                      pl.BlockSpec((1, t, PAGE_SIZE), lambda bi, pg, pt: (bi, 0, n_pages)),
                      per_seq((PAGE_SIZE, KV_W)), per_seq((PAGE_SIZE, KV_W))]
            + [pl.BlockSpec((1, PAGE_SIZE, KV_W), page_map(j)) for j in range(pp)] * 2,
            out_specs=per_seq((n_rows, LANE)),
            scratch_shapes=[pltpu.VMEM((n_rows, LANE), jnp.float32), pltpu.VMEM((n_rows, LANE), jnp.float32),
                            pltpu.VMEM((n_rows, KV_W), jnp.float32)]),
        out_shape=jax.ShapeDtypeStruct((b, n_rows, LANE), jnp.float32),
        compiler_params=pltpu.CompilerParams(dimension_semantics=("arbitrary", "arbitrary")),
        name="sample_attend",
    )(page_table, q_bd, bias, bias, k_new, v_new, *([ck] * pp), *([cv] * pp))
    out = out[:, :, :HEAD_DIM].reshape(b, N_HEADS, t, HEAD_DIM).transpose(0, 2, 1, 3)
    return out.reshape(b, t, N_HEADS * HEAD_DIM)


GELU_C = math.sqrt(2.0 / math.pi)


def _rglru_kernel(x_ref, xg_ref, prev_ref, h0_ref, cw_ref, cb_ref, wa_ref, ba_ref, wi_ref, bi_ref, lam_ref,
                  o_ref, hl_ref, conv_ref, a_ref, u_ref, xprev_ref, h_ref, *, rows, per_group_state):
    i = pl.program_id(0)
    n_groups = rows // SUBLANE
    d_rnn = x_ref.shape[1]
    r8 = lax.broadcasted_iota(jnp.int32, (SUBLANE, d_rnn), 0)
    grp = lambda g: pl.ds(pl.multiple_of(g * SUBLANE, SUBLANE), SUBLANE)

    if not per_group_state:
        @pl.when(i == 0)
        def _():
            xprev_ref[...] = jnp.zeros_like(xprev_ref)
            h_ref[...] = jnp.zeros_like(h_ref)

    cw = cw_ref[...]
    cb = cb_ref[...]

    def conv_group(g, prev):
        x8 = x_ref[grp(g), :]
        if per_group_state:
            prev = prev_ref[grp(g), :]
        out = cb + x8 * cw[CONV_W - 1:CONV_W, :]
        for d in range(1, CONV_W):
            shifted = jnp.where(r8 < d, pltpu.roll(prev, d, axis=0), pltpu.roll(x8, d, axis=0))
            out = out + shifted * cw[CONV_W - 1 - d:CONV_W - d, :]
        conv_ref[grp(g), :] = out
        return x8

    zeros8 = jnp.zeros((SUBLANE, d_rnn), jnp.float32)
    xlast = lax.fori_loop(0, n_groups, conv_group, zeros8 if per_group_state else xprev_ref[...])
    if not per_group_state:
        xprev_ref[...] = xlast

    xc = conv_ref[...]
    xb = xc.astype(jnp.bfloat16)
    r_parts, i_parts = [], []
    for n in range(RG_BLOCKS):
        xn = xb[:, n * RG_BLOCK_W:(n + 1) * RG_BLOCK_W]
        r_parts.append(jnp.dot(xn, wa_ref[n], preferred_element_type=jnp.float32))
        i_parts.append(jnp.dot(xn, wi_ref[n], preferred_element_type=jnp.float32))
    r = jax.nn.sigmoid(jnp.concatenate(r_parts, axis=-1) + ba_ref[...])
    ig = jax.nn.sigmoid(jnp.concatenate(i_parts, axis=-1) + bi_ref[...])
    lam = lam_ref[...]
    log_sig_lam = -(jnp.maximum(-lam, 0.0) + jnp.log1p(jnp.exp(-jnp.abs(lam))))
    log_a = RG_C * r * log_sig_lam
    a = jnp.exp(log_a)
    a_ref[...] = a
    u_ref[...] = jnp.sqrt(-jnp.tanh(log_a) * (a * a + 1.0)) * (ig * xc)

    def scan_group(g, hprev):
        a = a_ref[grp(g), :]
        u = u_ref[grp(g), :]
        if per_group_state:
            hprev = jnp.broadcast_to(h0_ref[pl.ds(g, 1), :], (SUBLANE, d_rnn))
        for d in (1, 2, 4):
            u = jnp.where(r8 >= d, a * pltpu.roll(u, d, axis=0) + u, u)
            a = jnp.where(r8 >= d, a * pltpu.roll(a, d, axis=0), a)
        h = a * hprev + u
        u_ref[grp(g), :] = h
        hlast = jnp.broadcast_to(h[SUBLANE - 1:SUBLANE, :], (SUBLANE, d_rnn))
        if per_group_state:
            hl_ref[pl.ds(g, 1), :] = h[SUBLANE - 1:SUBLANE, :]
        return hlast

    hlast = lax.fori_loop(0, n_groups, scan_group, zeros8 if per_group_state else h_ref[...])
    if not per_group_state:
        h_ref[...] = hlast
        hl_ref[...] = hlast

    xg = xg_ref[...]
    gelu = 0.5 * xg * (1.0 + jnp.tanh(GELU_C * (xg + 0.044715 * (xg * xg * xg))))
    o_ref[...] = u_ref[...] * gelu


def _rglru(x, xg, prev, h0, conv_w, conv_b, w_a, b_a, w_i, b_i, lam, *, per_group_state, tile_rows):
    n_rows, d = x.shape
    row = lambda a: a.reshape(1, d)
    const = lambda shape: pl.BlockSpec(shape, lambda i: (0,) * len(shape))
    n_hl = n_rows // SUBLANE if per_group_state else SUBLANE
    body = functools.partial(_rglru_kernel, rows=tile_rows, per_group_state=per_group_state)
    tile = pl.BlockSpec((tile_rows, d), lambda i: (i, 0))
    if per_group_state:
        assert tile_rows == n_rows
        prev_spec, h0_spec = const(prev.shape), const(h0.shape)
    else:
        prev_spec, h0_spec = const(prev.shape), const(h0.shape)
    return pl.pallas_call(
        body,
        grid=(n_rows // tile_rows,),
        in_specs=[tile, tile, prev_spec, h0_spec, const((CONV_W, d)), const((1, d)),
                  const(w_a.shape), const((1, d)), const(w_i.shape), const((1, d)), const((1, d))],
        out_specs=[tile, const((n_hl, d))],
        out_shape=[jax.ShapeDtypeStruct((n_rows, d), jnp.float32), jax.ShapeDtypeStruct((n_hl, d), jnp.float32)],
        scratch_shapes=[pltpu.VMEM((tile_rows, d), jnp.float32), pltpu.VMEM((tile_rows, d), jnp.float32),
                        pltpu.VMEM((tile_rows, d), jnp.float32), pltpu.VMEM((SUBLANE, d), jnp.float32),
                        pltpu.VMEM((SUBLANE, d), jnp.float32)],
        compiler_params=pltpu.CompilerParams(dimension_semantics=("arbitrary",), vmem_limit_bytes=VMEM_LIMIT_BYTES),
        name="rglru_sample" if per_group_state else "rglru_prompt",
    )(x, xg, prev, h0, conv_w, row(conv_b), w_a.astype(jnp.bfloat16), row(b_a), w_i.astype(jnp.bfloat16), row(b_i), row(lam))


def _rglru_prompt(x, xg, conv_w, conv_b, w_a, b_a, w_i, b_i, lam, tile_rows=512):
    dummy = jnp.zeros((SUBLANE, x.shape[1]), jnp.float32)
    rnn, hl = _rglru(x, xg, dummy, dummy, conv_w, conv_b, w_a, b_a, w_i, b_i, lam,
                     per_group_state=False, tile_rows=tile_rows)
    return rnn, hl[0]


def _rglru_sample(x, xg, state_conv, state_h, conv_w, conv_b, w_a, b_a, w_i, b_i, lam):
    b, t, d = x.shape
    assert t == SUBLANE
    prev = jnp.concatenate([jnp.zeros((b, SUBLANE - (CONV_W - 1), d), jnp.float32), state_conv], axis=1)
    rnn, hl = _rglru(x.reshape(b * t, d), xg.reshape(b * t, d), prev.reshape(b * t, d), state_h,
                     conv_w, conv_b, w_a, b_a, w_i, b_i, lam, per_group_state=True, tile_rows=b * t)
    return rnn.reshape(b, t, d), hl


def _rms_norm(x, g):
    xf = x.astype(jnp.float32)
    y = xf * lax.rsqrt(jnp.mean(xf * xf, axis=-1, keepdims=True) + EPS)
    return (y * g.astype(jnp.float32)).astype(x.dtype)


def _rope(x, pos):
    half = x.shape[-1] // 2
    inv_freq = ROPE_THETA ** (-jnp.arange(half, dtype=jnp.float32) / half)
    ang = pos.astype(jnp.float32)[:, None] * inv_freq[None, :]
    cos = jnp.cos(ang)[:, None, :]
    sin = jnp.sin(ang)[:, None, :]
    xf = x.astype(jnp.float32)
    x1, x2 = xf[..., :half], xf[..., half:]
    return jnp.concatenate([x1 * cos - x2 * sin, x2 * cos + x1 * sin], axis=-1).astype(x.dtype)


def _index_scores(iq, iw, ik):
    s = jnp.einsum('bthd,bld->bthl', iq, ik, preferred_element_type=jnp.float32)
    return jnp.einsum('bthl,bth->btl', jax.nn.relu(s), iw.astype(jnp.float32))


def _sparse_attend(q, k_sel, v_sel, valid):
    B, T = q.shape[:2]
    qg = q.reshape(B, T, KV_HEADS, N_HEADS // KV_HEADS, HEAD_DIM)
    logits = jnp.einsum('btkgd,btskd->btkgs', qg, k_sel, preferred_element_type=jnp.float32) * (HEAD_DIM ** -0.5)
    logits = jnp.where(valid[:, :, None, None, :], logits, -jnp.inf)
    p = jax.nn.softmax(logits, axis=-1)
    out = jnp.einsum('btkgs,btskd->btkgd', p.astype(v_sel.dtype), v_sel)
    return out.reshape(B, T, Q_W)


def _gather_rows(rows, idx):
    return jax.vmap(lambda r, i: r[i])(rows, idx)


def _prompt_sparse_attention(q, k, v, iq, iw, ik):
    B, S = q.shape[:2]
    topk = min(IDX_TOPK_MAX, S // 4)
    n_blocks = S // Q_BLOCK
    key_pos = jnp.arange(S, dtype=jnp.int32)

    def block(args):
        qb, iqb, iwb, start = args
        qpos = start + jnp.arange(Q_BLOCK, dtype=jnp.int32)
        sc = _index_scores(iqb, iwb, ik)
        sc = jnp.where((key_pos[None, :] <= qpos[:, None])[None], sc, -jnp.inf)
        _, sel = lax.top_k(sc, topk)
        valid = sel <= qpos[None, :, None]
        return _sparse_attend(qb, _gather_rows(k, sel), _gather_rows(v, sel), valid)

    to_blocks = lambda a: jnp.moveaxis(a.reshape(B, n_blocks, Q_BLOCK, *a.shape[2:]), 1, 0)
    starts = jnp.arange(n_blocks, dtype=jnp.int32) * Q_BLOCK
    out = lax.map(block, (to_blocks(q), to_blocks(iq), to_blocks(iw), starts))
    return jnp.moveaxis(out, 0, 1).reshape(B, S, Q_W)


def _sample_sparse_attention(q, k, v, iq, iw, ik, cache_k, cache_v, cache_idx_k, page_table):
    DB, T = q.shape[:2]
    past = page_table.shape[1] * PAGE_SIZE
    L = past + T
    topk = min(IDX_TOPK_MAX, L // 4)
    ik_past = cache_idx_k[page_table].reshape(DB, past, IDX_DIM)
    ik_all = jnp.concatenate([ik_past, ik.astype(ik_past.dtype)], axis=1)
    qpos = past + jnp.arange(T, dtype=jnp.int32)
    sc = _index_scores(iq, iw, ik_all)
    sc = jnp.where((jnp.arange(L, dtype=jnp.int32)[None, :] <= qpos[:, None])[None], sc, -jnp.inf)
    _, sel = lax.top_k(sc, topk)
    valid = sel <= qpos[None, :, None]
    in_past = sel < past
    sel_p = jnp.minimum(sel, past - 1)
    page = _gather_rows(page_table, sel_p // PAGE_SIZE)
    phys = page * PAGE_SIZE + sel_p % PAGE_SIZE
    new_idx = jnp.clip(sel - past, 0, T - 1)
    k_pool = cache_k.reshape(-1, KV_HEADS, HEAD_DIM)
    v_pool = cache_v.reshape(-1, KV_HEADS, HEAD_DIM)
    m = in_past[..., None, None]
    k_sel = jnp.where(m, k_pool[phys], _gather_rows(k, new_idx).astype(k_pool.dtype))
    v_sel = jnp.where(m, v_pool[phys], _gather_rows(v, new_idx).astype(v_pool.dtype))
    return _sparse_attend(q, k_sel, v_sel, valid)


def _causal_conv(xr, conv_state, w, b):
    T = xr.shape[1]
    xp = jnp.concatenate([conv_state.astype(xr.dtype), xr], axis=1)
    out = b + sum(xp[:, j:j + T] * w[j] for j in range(CONV_W))
    return out, xp[:, -(CONV_W - 1):]


def _block_diag(x, w, b):
    B, T, _ = x.shape
    y = jnp.einsum('btnc,ncd->btnd', x.reshape(B, T, RG_BLOCKS, RG_BLOCK_W), w)
    return y.reshape(B, T, D_RNN) + b


def _rg_lru(x, h0, w_a, b_a, w_i, b_i, lam):
    xf = x.astype(jnp.float32)
    r = jax.nn.sigmoid(_block_diag(x, w_a, b_a).astype(jnp.float32))
    i = jax.nn.sigmoid(_block_diag(x, w_i, b_i).astype(jnp.float32))
    log_a = RG_C * r * jax.nn.log_sigmoid(lam.astype(jnp.float32))
    a = jnp.exp(log_a)
    u = jnp.sqrt(-jnp.expm1(2.0 * log_a)) * (i * xf)

    def step(h, au):
        h = au[0] * h + au[1]
        return h, h

    h_last, hs = lax.scan(step, h0.astype(jnp.float32), (jnp.moveaxis(a, 1, 0), jnp.moveaxis(u, 1, 0)))
    return jnp.moveaxis(hs, 0, 1).astype(x.dtype), h_last.astype(x.dtype)


def _rms_norm_rows(x, g):
    return x * lax.rsqrt(jnp.mean(x * x, axis=-1, keepdims=True) + EPS) * g


def _router_kernel(x_ref, g_ref, wr_ref, br_ref, tri_ref, e_ref, gate_ref, rank_ref, cnt_ref, run_ref):
    i = pl.program_id(0)
    tm = x_ref.shape[0]

    @pl.when(i == 0)
    def _():
        run_ref[...] = jnp.zeros_like(run_ref)

    xn_bf = _rms_norm_rows(x_ref[...], g_ref[...]).astype(jnp.bfloat16)
    logits = lax.dot_general(wr_ref[...], xn_bf, (((1,), (1,)), ((), ())), preferred_element_type=jnp.float32)
    logits = logits + br_ref[...]

    expert = lax.broadcasted_iota(jnp.int32, (N_EXPERTS, tm), 0)
    member = jnp.zeros((N_EXPERTS, tm), jnp.float32)
    picked, values = [], []
    for k in range(TOP_K):
        mx = jnp.max(logits, axis=0, keepdims=True)
        idx = jnp.min(jnp.where(logits == mx, expert, N_EXPERTS), axis=0, keepdims=True)
        hit = expert == idx
        member = jnp.where(hit, 1.0, member)
        logits = jnp.where(hit, -jnp.inf, logits)
        picked.append(idx)
        values.append(mx)
        e_ref[k:k + 1, :] = idx

    ex = [jnp.exp(v - values[0]) for v in values]
    denom = ex[0] + ex[1] + ex[2] + ex[3]
    for k in range(TOP_K):
        gate_ref[k:k + 1, :] = ex[k] / denom

    before = jnp.dot(member.astype(jnp.bfloat16), tri_ref[...], preferred_element_type=jnp.float32)
    before = before + jnp.tile(run_ref[...], (1, tm // LANE))
    for k in range(TOP_K):
        r = jnp.sum(jnp.where(expert == picked[k], before, 0.0), axis=0, keepdims=True)
        rank_ref[k:k + 1, :] = r.astype(jnp.int32)
    run = run_ref[...] + jnp.sum(member, axis=1, keepdims=True)
    run_ref[...] = run
    cnt_ref[...] = run.astype(jnp.int32)


def _route(x2d, norm_g, w_router, b_router, tm=512):
    n, d = x2d.shape
    tri = (jnp.arange(tm)[:, None] < jnp.arange(tm)[None, :]).astype(jnp.bfloat16)
    const = lambda shape: pl.BlockSpec(shape, lambda i: (0,) * len(shape))
    rows4 = pl.BlockSpec((TOP_K, tm), lambda i: (0, i))
    top_e, gates, rank, cnt = pl.pallas_call(
        _router_kernel,
        grid=(n // tm,),
        in_specs=[pl.BlockSpec((tm, d), lambda i: (i, 0)), const((1, d)), const((N_EXPERTS, d)),
                  const((N_EXPERTS, 1)), const((tm, tm))],
        out_specs=[rows4, rows4, rows4, const((N_EXPERTS, LANE))],
        out_shape=[jax.ShapeDtypeStruct((TOP_K, n), jnp.int32),
                   jax.ShapeDtypeStruct((TOP_K, n), jnp.float32), jax.ShapeDtypeStruct((TOP_K, n), jnp.int32),
                   jax.ShapeDtypeStruct((N_EXPERTS, LANE), jnp.int32)],
        scratch_shapes=[pltpu.VMEM((N_EXPERTS, LANE), jnp.float32)],
        compiler_params=pltpu.CompilerParams(dimension_semantics=("arbitrary",)),
        name="moe_router",
    )(x2d, norm_g.reshape(1, d), w_router.T.astype(jnp.bfloat16), b_router.reshape(N_EXPERTS, 1), tri)
    return top_e, gates, rank, cnt[:, 0]


def _expert_kernel(be_ref, used_ref, xs_ref, g_ref, wgu_ref, bgu_ref, wd_ref, bd_ref, o_ref, *, blk):
    i = pl.program_id(0)

    @pl.when(i * blk < used_ref[0])
    def _():
        xn = _rms_norm_rows(xs_ref[...], g_ref[...]).astype(jnp.bfloat16)
        gu = jnp.dot(xn, wgu_ref[0], preferred_element_type=jnp.float32) + bgu_ref[0]
        g = jnp.minimum(gu[:, :D_FF], SWIGLU_LIMIT)
        u = jnp.clip(gu[:, D_FF:], -SWIGLU_LIMIT, SWIGLU_LIMIT)
        act = (u + 1.0) * (g * jax.nn.sigmoid(SWIGLU_ALPHA * g))
        o_ref[...] = jnp.dot(act.astype(jnp.bfloat16), wd_ref[0], preferred_element_type=jnp.float32) + bd_ref[0]

    @pl.when(i * blk >= used_ref[0])
    def _():
        o_ref[...] = jnp.zeros_like(o_ref)


def _experts(xs, norm_g, blk_exp, n_used, w_gate_up, b_gate_up, w_down, b_down, *, blk):
    n_slots, d = xs.shape
    n_blocks = n_slots // blk
    bf = jnp.bfloat16
    return pl.pallas_call(
        functools.partial(_expert_kernel, blk=blk),
        grid_spec=pltpu.PrefetchScalarGridSpec(
            num_scalar_prefetch=2, grid=(n_blocks,),
            in_specs=[pl.BlockSpec((blk, d), lambda i, be, nu: (i, 0)),
                      pl.BlockSpec((1, d), lambda i, be, nu: (0, 0)),
                      pl.BlockSpec((1, d, 2 * D_FF), lambda i, be, nu: (be[i], 0, 0)),
                      pl.BlockSpec((1, 1, 2 * D_FF), lambda i, be, nu: (be[i], 0, 0)),
                      pl.BlockSpec((1, D_FF, d), lambda i, be, nu: (be[i], 0, 0)),
                      pl.BlockSpec((1, 1, d), lambda i, be, nu: (be[i], 0, 0))],
            out_specs=pl.BlockSpec((blk, d), lambda i, be, nu: (i, 0))),
        out_shape=jax.ShapeDtypeStruct((n_slots, d), jnp.float32),
        compiler_params=pltpu.CompilerParams(dimension_semantics=("arbitrary",), vmem_limit_bytes=VMEM_LIMIT_BYTES),
        name="moe_experts",
    )(blk_exp, n_used, xs, norm_g.reshape(1, d), w_gate_up.astype(bf), b_gate_up.reshape(N_EXPERTS, 1, 2 * D_FF),
      w_down.astype(bf), b_down.reshape(N_EXPERTS, 1, d))


def _row_copy(src_hbm, src_row, dst_ref, dst_row, sem):
    return pltpu.make_async_copy(src_hbm.at[pl.ds(src_row, 1)], dst_ref.at[pl.ds(dst_row, 1)], sem)


def _dispatch_kernel(dest_ref, x_ref, xs_in_hbm, xs_hbm, sem, *, tm):
    del xs_in_hbm

    def issue(t, carry):
        for k in range(TOP_K):
            _row_copy(x_ref, t, xs_hbm, dest_ref[k, t], sem).start()
        return carry

    lax.fori_loop(0, tm, issue, 0)

    def drain(t, carry):
        for k in range(TOP_K):
            _row_copy(x_ref, 0, xs_hbm, 0, sem).wait()
        return carry

    lax.fori_loop(0, tm, drain, 0)


def _dispatch(x2d, dest, n_slots, tm=128):
    n, d = x2d.shape
    return pl.pallas_call(
        functools.partial(_dispatch_kernel, tm=tm),
        grid=(n // tm,),
        in_specs=[pl.BlockSpec((TOP_K, tm), lambda i: (0, i), memory_space=pltpu.SMEM),
                  pl.BlockSpec((tm, d), lambda i: (i, 0)), pl.BlockSpec(memory_space=pl.ANY)],
        out_specs=pl.BlockSpec(memory_space=pl.ANY),
        out_shape=jax.ShapeDtypeStruct((n_slots, d), x2d.dtype),
        scratch_shapes=[pltpu.SemaphoreType.DMA(())],
        input_output_aliases={2: 0},
        compiler_params=pltpu.CompilerParams(dimension_semantics=("arbitrary",)),
        name="moe_dispatch",
    )(dest, x2d, jnp.zeros((n_slots, d), x2d.dtype))


def _combine_kernel(dest_ref, x_ref, gate_ref, ys_hbm, o_ref, buf_ref, sem, *, tm):
    def issue(t, carry):
        for k in range(TOP_K):
            _row_copy(ys_hbm, dest_ref[k, t], buf_ref.at[k], t, sem).start()
        return carry

    lax.fori_loop(0, tm, issue, 0)

    def drain(t, carry):
        for k in range(TOP_K):
            _row_copy(ys_hbm, 0, buf_ref.at[k], 0, sem).wait()
        return carry

    lax.fori_loop(0, tm, drain, 0)
    gate = gate_ref[...]
    acc = x_ref[...]
    for k in range(TOP_K):
        acc = acc + gate[:, k:k + 1] * buf_ref[k]
    o_ref[...] = acc


def _combine(x2d, ys, dest, gates_t, tm=128):
    n, d = x2d.shape
    return pl.pallas_call(
        functools.partial(_combine_kernel, tm=tm),
        grid=(n // tm,),
        in_specs=[pl.BlockSpec((TOP_K, tm), lambda i: (0, i), memory_space=pltpu.SMEM),
                  pl.BlockSpec((tm, d), lambda i: (i, 0)), pl.BlockSpec((tm, TOP_K), lambda i: (i, 0)),
                  pl.BlockSpec(memory_space=pl.ANY)],
        out_specs=pl.BlockSpec((tm, d), lambda i: (i, 0)),
        out_shape=jax.ShapeDtypeStruct((n, d), jnp.float32),
        scratch_shapes=[pltpu.VMEM((TOP_K, tm, d), jnp.float32), pltpu.SemaphoreType.DMA(())],
        compiler_params=pltpu.CompilerParams(dimension_semantics=("arbitrary",)),
        name="moe_combine",
    )(dest, x2d, gates_t, ys)


MOE_ROWS = 512


def _moe(x2d, norm_g, w_router, b_router, w_gate_up, b_gate_up, w_down, b_down, *, blk=MOE_ROWS, tm=512):
    n, d = x2d.shape
    top_e, gates, rank, counts = _route(x2d, norm_g, w_router, b_router, tm=tm)
    padded = (counts + blk - 1) // blk * blk
    pad_end = jnp.cumsum(padded)
    pad_start = pad_end - padded
    dest = pad_start[top_e] + rank
    n_blocks = -(-(n * TOP_K + N_EXPERTS * (blk - 1)) // blk)
    n_slots = n_blocks * blk
    blk_exp = jnp.minimum(jnp.searchsorted(pad_end, jnp.arange(n_blocks, dtype=jnp.int32) * blk, side='right'),
                          N_EXPERTS - 1).astype(jnp.int32)
    n_used = pad_end[-1:].astype(jnp.int32)
    xs = _dispatch(x2d, dest, n_slots)
    ys = _experts(xs, norm_g, blk_exp, n_used, w_gate_up, b_gate_up, w_down, b_down, blk=blk)
    return _combine(x2d, ys, dest, gates.T)


def _trunk_layer(x, pos, attend, conv_state, h0, norm1_g, w_in, q_norm_g, k_norm_g, conv_w, conv_b,
                 rg_w_a, rg_b_a, rg_w_i, rg_b_i, rg_lambda, w_attn_out, w_rnn_out, w_out, norm2_g,
                 w_router, b_router, w_gate_up, b_gate_up, w_down, b_down):
    B, T, _ = x.shape
    n_pad = (-D_IN) % 896
    w_pad = jnp.pad(w_in, ((0, 0), (0, n_pad))).astype(jnp.bfloat16)
    z = _norm_proj(x.reshape(B * T, D_MODEL), norm1_g, w_pad)[:, :D_IN].reshape(B, T, D_IN)
    q, k, v, iq, ik, iw, xr, xg, ga, gr = jnp.split(z, np.cumsum(IN_WIDTHS)[:-1].tolist(), axis=-1)
    q = _rope(_rms_norm(q.reshape(B, T, N_HEADS, HEAD_DIM), q_norm_g), pos)
    k = _rope(_rms_norm(k.reshape(B, T, KV_HEADS, HEAD_DIM), k_norm_g), pos)
    v = v.reshape(B, T, KV_HEADS, HEAD_DIM)
    iq = _rope(iq.reshape(B, T, IDX_HEADS, IDX_DIM), pos)
    ik = _rope(ik[:, :, None, :], pos)[:, :, 0]
    iw = iw * (IDX_HEADS ** -0.5 * IDX_DIM ** -0.5)
    rg = (conv_w, conv_b, rg_w_a, rg_b_a, rg_w_i, rg_b_i, rg_lambda)
    if attend is None:
        assert B == 1
        attn = _prompt_dsa_attention(q[0], k[0], v[0], iq[0], iw[0], ik[0])[None]
        rnn, h_new = _rglru_prompt(xr[0], xg[0], *rg)
        rnn, h_new = rnn[None], h_new[None]
        conv_new = xr[:, -(CONV_W - 1):]
    else:
        attn = attend(q, k, v, iq, iw, ik)
        rnn, h_new = _rglru_sample(xr, xg, conv_state, h0, *rg)
        conv_new = jnp.concatenate([conv_state, xr], axis=1)[:, -(CONV_W - 1):]
    merged = jax.nn.sigmoid(ga) * (attn @ w_attn_out) + jax.nn.sigmoid(gr) * (rnn @ w_rnn_out)
    x = x + merged @ w_out
    return x, (k, v, ik, conv_new, h_new)


def kernel(x_prompt, x_sample, cache_k, cache_v, cache_idx_k, state_conv, state_h, page_table,
           norm1_g, w_in, q_norm_g, k_norm_g, conv_w, conv_b, rg_w_a, rg_b_a, rg_w_i, rg_b_i, rg_lambda,
           w_attn_out, w_rnn_out, w_out, norm2_g, w_router, b_router, w_gate_up, b_gate_up, w_down, b_down):
    weights = (norm1_g, w_in, q_norm_g, k_norm_g, conv_w, conv_b, rg_w_a, rg_b_a, rg_w_i, rg_b_i, rg_lambda,
               w_attn_out, w_rnn_out, w_out, norm2_g, w_router, b_router, w_gate_up, b_gate_up, w_down, b_down)
    B, S = x_prompt.shape[:2]
    T = x_sample.shape[1]
    past = page_table.shape[1] * PAGE_SIZE
    pos_p = jnp.arange(S, dtype=jnp.int32)
    pos_s = past + jnp.arange(T, dtype=jnp.int32)
    conv0 = jnp.zeros((B, CONV_W - 1, D_RNN), x_prompt.dtype)
    h0 = jnp.zeros((B, D_RNN), jnp.float32)
    wl = [w[0] for w in weights]
    yp, st_p = _trunk_layer(x_prompt, pos_p, None, conv0, h0, *wl)
    attend_s = functools.partial(_sample_dsa_attention, cache_k=cache_k[0], cache_v=cache_v[0],
                                 cache_idx_k=cache_idx_k[0], page_table=page_table)
    ys, st_s = _trunk_layer(x_sample, pos_s, attend_s, state_conv[0], state_h[0], *wl)
    x_all = jnp.concatenate([yp.reshape(-1, D_MODEL), ys.reshape(-1, D_MODEL)], axis=0)
    x_all = _moe(x_all, norm2_g[0], w_router[0], b_router[0], w_gate_up[0], b_gate_up[0], w_down[0], b_down[0])
    yp = x_all[:B * S].reshape(x_prompt.shape)
    ys = x_all[B * S:].reshape(x_sample.shape)
    k_p, v_p, ik_p, conv_p, h_p = [a[None] for a in st_p]
    k_s, v_s, ik_s, conv_s, h_s = [a[None] for a in st_s]
    return (yp, ys, k_p, v_p, ik_p, conv_p, h_p, k_s, v_s, ik_s, conv_s, h_s)
```

```python
import functools
import math

import jax
import jax.numpy as jnp
import numpy as np
from jax import lax
from jax.experimental import pallas as pl
from jax.experimental.pallas import tpu as pltpu

D_MODEL = 1024
PAGE_SIZE = 128
N_HEADS = 16
HEAD_DIM = 64
KV_HEADS = 4
IDX_HEADS = 8
IDX_DIM = 64
IDX_TOPK_MAX = 256
Q_BLOCK = 128
ROPE_THETA = 10000.0
D_RNN = D_MODEL
RG_BLOCKS = 4
RG_BLOCK_W = D_RNN // RG_BLOCKS
CONV_W = 4
RG_C = 8.0
N_EXPERTS = 32
TOP_K = 4
D_FF = D_MODEL
SWIGLU_LIMIT = 7.0
SWIGLU_ALPHA = 1.702
MOE_BLOCK = 128
EPS = 1e-6

Q_W = N_HEADS * HEAD_DIM
KV_W = KV_HEADS * HEAD_DIM
IQ_W = IDX_HEADS * IDX_DIM
IN_WIDTHS = (Q_W, KV_W, KV_W, IQ_W, IDX_DIM, IDX_HEADS, D_RNN, D_RNN, D_MODEL, D_MODEL)
D_IN = sum(IN_WIDTHS)

LANE = 128
SUBLANE = 8
VMEM_LIMIT_BYTES = 48 * 1024 * 1024


def _norm_proj_kernel(x_ref, g_ref, w_ref, o_ref):
    x = x_ref[...]
    y = x * lax.rsqrt(jnp.mean(x * x, axis=-1, keepdims=True) + EPS) * g_ref[...]
    o_ref[...] = jnp.dot(y.astype(jnp.bfloat16), w_ref[...], preferred_element_type=jnp.float32)


def _norm_proj(x2d, g, w_bf16, tm=512, tn=896):
    m, d = x2d.shape
    n = w_bf16.shape[1]
    return pl.pallas_call(
        _norm_proj_kernel,
        grid=(m // tm, n // tn),
        in_specs=[
            pl.BlockSpec((tm, d), lambda i, j: (i, 0)),
            pl.BlockSpec((1, d), lambda i, j: (0, 0)),
            pl.BlockSpec((d, tn), lambda i, j: (0, j)),
        ],
        out_specs=pl.BlockSpec((tm, tn), lambda i, j: (i, j)),
        out_shape=jax.ShapeDtypeStruct((m, n), jnp.float32),
        name="norm_proj",
    )(x2d, g.reshape(1, d), w_bf16)


INT_MIN = -(2 ** 31)
INT_MAX = 2 ** 31 - 1
MASK_BIAS = -1e30
NEG_INF_BITS_MASK = 0x7FFFFFFF


def _ordered_key(x):
    bits = pltpu.bitcast(x, jnp.int32)
    return bits ^ ((bits >> 31) & NEG_INF_BITS_MASK)


def _topk_threshold(key_ref, cut_ref, nk, *, rows, kb, topk, n_idx_bits):
    n_lane_tiles = kb // LANE
    lane = lax.broadcasted_iota(jnp.int32, (rows, kb), 1)

    def count(pred_of_chunk):
        def body(c, cnt):
            p = pred_of_chunk(c, key_ref[c])
            for j in range(n_lane_tiles):
                cnt = cnt + jnp.where(p[:, j * LANE:(j + 1) * LANE], 1, 0)
            return cnt
        cnt = lax.fori_loop(0, nk, body, jnp.zeros((rows, LANE), jnp.int32))
        return jnp.sum(cnt, axis=-1, keepdims=True)

    def bit_step(b, carry):
        thr, n_ge = carry
        cand = thr ^ (jnp.int32(1) << (31 - b))
        cand_b = jnp.broadcast_to(cand, (rows, kb))
        cnt = count(lambda c, key: key >= cand_b)
        ok = cnt >= topk
        return jnp.where(ok, cand, thr), jnp.where(ok, cnt, n_ge)

    thr0 = jnp.full((rows, 1), INT_MIN, jnp.int32)
    n0 = jnp.full((rows, 1), 0, jnp.int32) + nk * kb
    thr, n_ge = lax.fori_loop(0, 32, bit_step, (thr0, n0))
    thr_b = jnp.broadcast_to(thr, (rows, kb))

    cut_ref[...] = jnp.full((rows, LANE), INT_MAX, jnp.int32)

    @pl.when(jnp.max(n_ge) > topk)
    def _():
        n_eq = count(lambda c, key: key == thr_b)
        need = topk - (n_ge - n_eq)

        def idx_step(b, lo):
            step = jnp.int32(1) << (n_idx_bits - 1 - b)
            mid_b = jnp.broadcast_to(lo + step - 1, (rows, kb))
            f = count(lambda c, key: (key == thr_b) & (c * kb + lane <= mid_b))
            return jnp.where(f < need, lo + step, lo)

        lo = lax.fori_loop(0, n_idx_bits, idx_step, jnp.zeros((rows, 1), jnp.int32))
        cut = jnp.where(n_ge > topk, lo, INT_MAX)
        cut_ref[...] = jnp.broadcast_to(cut, (rows, LANE))

    return thr


def _dsa_kernel(iq_ref, iw_ref, q_ref, ikT_ref, kT_ref, v_ref, o_ref,
                key_ref, wb_ref, cut_ref, m_ref, l_ref, acc_ref, *, qb, kb, topk, n_idx_bits, q_pos0, n_chunks_of):
    i = pl.program_id(0)
    nk = n_chunks_of(i)
    n_lane_tiles = kb // LANE
    row = lax.broadcasted_iota(jnp.int32, (qb, kb), 0)
    lane = lax.broadcasted_iota(jnp.int32, (qb, kb), 1)
    qpos = q_pos0 + i * qb + row

    iw = iw_ref[...]
    for h in range(IDX_HEADS):
        wb_ref[h] = jnp.broadcast_to(iw[:, h:h + 1], (qb, LANE))

    def score_chunk(c, carry):
        ikc = ikT_ref[c]
        acc = jnp.zeros((qb, kb), jnp.float32)
        for h in range(IDX_HEADS):
            s = jnp.dot(iq_ref[0, h], ikc, preferred_element_type=jnp.float32)
            acc = acc + jnp.tile(wb_ref[h], (1, n_lane_tiles)) * jnp.maximum(s, 0.0)
        acc = jnp.where(c * kb + lane <= qpos, acc, -jnp.inf)
        key_ref[c] = _ordered_key(acc)
        return carry

    lax.fori_loop(0, nk, score_chunk, 0)

    thr = _topk_threshold(key_ref, cut_ref, nk, rows=qb, kb=kb, topk=topk, n_idx_bits=n_idx_bits)
    thr_b = jnp.broadcast_to(thr, (qb, kb))

    m_ref[...] = jnp.full(m_ref.shape, MASK_BIAS, jnp.float32)
    l_ref[...] = jnp.zeros(l_ref.shape, jnp.float32)
    acc_ref[...] = jnp.zeros(acc_ref.shape, jnp.float32)
    cut_b = jnp.tile(cut_ref[...], (1, n_lane_tiles))
    rep = N_HEADS // KV_HEADS

    def attend_chunk(c, carry):
        key = key_ref[c]
        kpos = c * kb + lane
        sel = (key > thr_b) | ((key == thr_b) & (kpos <= cut_b))
        sel = sel & (kpos <= qpos)
        bias = jnp.where(sel, 0.0, MASK_BIAS)
        kc = kT_ref[c]
        for g in range(KV_HEADS):
            s = jnp.dot(q_ref[0, g], kc[g * HEAD_DIM:(g + 1) * HEAD_DIM, :], preferred_element_type=jnp.float32)
            s = (s.reshape(rep, qb, kb) + bias[None]).reshape(rep * qb, kb)
            m_prev = m_ref[g]
            m_new = jnp.maximum(m_prev, jnp.max(s, axis=-1, keepdims=True))
            alpha = jnp.exp(m_prev - m_new)
            p = jnp.exp(s - jnp.tile(m_new, (1, n_lane_tiles)))
            l_ref[g] = alpha * l_ref[g] + jnp.sum(p, axis=-1, keepdims=True)
            pv = jnp.dot(p.astype(jnp.bfloat16), v_ref[c, g], preferred_element_type=jnp.float32)
            acc_ref[g] = acc_ref[g] * alpha[:, :HEAD_DIM] + pv
            m_ref[g] = m_new
        return carry

    lax.fori_loop(0, nk, attend_chunk, 0)
    for g in range(KV_HEADS):
        o_ref[0, g] = acc_ref[g] / l_ref[g][:, :HEAD_DIM]


def _prompt_dsa_attention(q, k, v, iq, iw, ik, *, qb=128, kb=512):
    s_len = q.shape[0]
    topk = min(IDX_TOPK_MAX, s_len // 4)
    nqb, nkc = s_len // qb, s_len // kb
    rep = N_HEADS // KV_HEADS
    bf = jnp.bfloat16
    scale = HEAD_DIM ** -0.5
    q_t = (q * scale).astype(bf).reshape(nqb, qb, KV_HEADS, rep, HEAD_DIM).transpose(0, 2, 3, 1, 4)
    q_t = q_t.reshape(nqb, KV_HEADS, rep * qb, HEAD_DIM)
    iq_t = iq.astype(bf).reshape(nqb, qb, IDX_HEADS, IDX_DIM).transpose(0, 2, 1, 3)
    ikT = ik.astype(bf).reshape(nkc, kb, IDX_DIM).transpose(0, 2, 1)
    kT = k.astype(bf).reshape(nkc, kb, KV_HEADS * HEAD_DIM).transpose(0, 2, 1)
    v_t = v.astype(bf).reshape(nkc, kb, KV_HEADS, HEAD_DIM).transpose(0, 2, 1, 3)
    n_idx_bits = max(1, (s_len - 1).bit_length())
    body = functools.partial(
        _dsa_kernel, qb=qb, kb=kb, topk=topk, n_idx_bits=n_idx_bits, q_pos0=0,
        n_chunks_of=lambda i: ((i + 1) * qb + kb - 1) // kb)
    whole = lambda shape: pl.BlockSpec(shape, lambda i: (0,) * len(shape), pipeline_mode=pl.Buffered(1))
    out = pl.pallas_call(
        body,
        grid=(nqb,),
        in_specs=[
            pl.BlockSpec((1, IDX_HEADS, qb, IDX_DIM), lambda i: (i, 0, 0, 0)),
            pl.BlockSpec((qb, IDX_HEADS), lambda i: (i, 0)),
            pl.BlockSpec((1, KV_HEADS, rep * qb, HEAD_DIM), lambda i: (i, 0, 0, 0)),
            whole((nkc, IDX_DIM, kb)),
            whole((nkc, KV_HEADS * HEAD_DIM, kb)),
            whole((nkc, KV_HEADS, kb, HEAD_DIM)),
        ],
        out_specs=pl.BlockSpec((1, KV_HEADS, rep * qb, HEAD_DIM), lambda i: (i, 0, 0, 0)),
        out_shape=jax.ShapeDtypeStruct((nqb, KV_HEADS, rep * qb, HEAD_DIM), jnp.float32),
        scratch_shapes=[
            pltpu.VMEM((nkc, qb, kb), jnp.int32),
            pltpu.VMEM((IDX_HEADS, qb, LANE), jnp.float32),
            pltpu.VMEM((qb, LANE), jnp.int32),
            pltpu.VMEM((KV_HEADS, rep * qb, LANE), jnp.float32),
            pltpu.VMEM((KV_HEADS, rep * qb, LANE), jnp.float32),
            pltpu.VMEM((KV_HEADS, rep * qb, HEAD_DIM), jnp.float32),
        ],
        compiler_params=pltpu.CompilerParams(dimension_semantics=("arbitrary",), vmem_limit_bytes=VMEM_LIMIT_BYTES),
        name="prompt_dsa_attention",
    )(iq_t, iw, q_t, ikT, kT, v_t)
    out = out.reshape(nqb, KV_HEADS, rep, qb, HEAD_DIM).transpose(0, 3, 1, 2, 4)
    return out.reshape(s_len, N_HEADS * HEAD_DIM)


def _sample_index_kernel(pt_ref, iq_ref, iw_ref, iknew_ref, *rest, pp, n_pages, topk, n_idx_bits):
    page_refs, o_ref, (key_ref, wide_ref, wb_ref, cut_ref) = rest[:pp], rest[pp], rest[pp + 1:]
    pg = pl.program_id(1)
    t_new = o_ref.shape[1]
    step_keys = pp * PAGE_SIZE
    n_keys = (n_pages + 1) * PAGE_SIZE

    @pl.when(pg == 0)
    def _():
        wb_ref[...] = jnp.broadcast_to(iw_ref[0], wb_ref.shape)

    iq = iq_ref[0]

    def scores(ik_t):
        s = jnp.dot(iq, ik_t.astype(jnp.bfloat16), preferred_element_type=jnp.float32)
        s = jnp.tile(wb_ref[...], (1, ik_t.shape[1] // LANE)) * jnp.maximum(s, 0.0)
        acc = jnp.zeros((t_new, ik_t.shape[1]), jnp.float32)
        for h in range(IDX_HEADS):
            acc = acc + s[h * t_new:(h + 1) * t_new, :]
        return acc

    key_ref[pg] = _ordered_key(scores(jnp.concatenate([r[0] for r in page_refs], axis=1)))

    @pl.when(pg == pl.num_programs(1) - 1)
    def _():
        for c in range(n_pages // pp):
            wide_ref[0, :, c * step_keys:(c + 1) * step_keys] = key_ref[c]
        row = lax.broadcasted_iota(jnp.int32, (t_new, PAGE_SIZE), 0)
        lane = lax.broadcasted_iota(jnp.int32, (t_new, PAGE_SIZE), 1)
        s_new = jnp.where(lane <= row, scores(iknew_ref[0]), -jnp.inf)
        wide_ref[0, :, n_pages * PAGE_SIZE:] = _ordered_key(s_new)
        thr = _topk_threshold(wide_ref, cut_ref, 1, rows=t_new, kb=n_keys, topk=topk, n_idx_bits=n_idx_bits)
        key = wide_ref[0]
        kpos = lax.broadcasted_iota(jnp.int32, (t_new, n_keys), 1)
        qpos = n_pages * PAGE_SIZE + lax.broadcasted_iota(jnp.int32, (t_new, n_keys), 0)
        thr_b = jnp.broadcast_to(thr, (t_new, n_keys))
        cut_b = jnp.tile(cut_ref[...], (1, n_pages + 1))
        sel = (key > thr_b) | ((key == thr_b) & (kpos <= cut_b))
        o_ref[0] = jnp.where(sel & (kpos <= qpos), 0.0, MASK_BIAS)


def _sample_attend_kernel(pt_ref, q_ref, bias_ref, biasnew_ref, knew_ref, vnew_ref, *rest, pp, n_pages):
    k_refs, v_refs, o_ref, (m_ref, l_ref, acc_ref) = rest[:pp], rest[pp:2 * pp], rest[2 * pp], rest[2 * pp + 1:]
    pg = pl.program_id(1)
    n_rows = q_ref.shape[1]
    t_new = bias_ref.shape[1]

    @pl.when(pg == 0)
    def _():
        m_ref[...] = jnp.full(m_ref.shape, MASK_BIAS, jnp.float32)
        l_ref[...] = jnp.zeros(l_ref.shape, jnp.float32)
        acc_ref[...] = jnp.zeros(acc_ref.shape, jnp.float32)

    q = q_ref[0]

    def step(bias, k_t, v_t):
        n_keys = k_t.shape[1]
        s = jnp.dot(q, k_t.astype(jnp.bfloat16), preferred_element_type=jnp.float32)
        s = s + jnp.tile(bias, (n_rows // t_new, 1))
        m_prev = m_ref[...]
        m_new = jnp.maximum(m_prev, jnp.max(s, axis=-1, keepdims=True))
        alpha = jnp.exp(m_prev - m_new)
        p = jnp.exp(s - jnp.tile(m_new, (1, n_keys // LANE)))
        l_ref[...] = alpha * l_ref[...] + jnp.sum(p, axis=-1, keepdims=True)
        pv = lax.dot_general(p.astype(jnp.bfloat16), v_t.astype(jnp.bfloat16), (((1,), (1,)), ((), ())),
                             preferred_element_type=jnp.float32)
        acc_ref[...] = acc_ref[...] * jnp.tile(alpha, (1, KV_W // LANE)) + pv
        m_ref[...] = m_new

    step(bias_ref[0], jnp.concatenate([r[0] for r in k_refs], axis=1), jnp.concatenate([r[0] for r in v_refs], axis=1))

    @pl.when(pg == pl.num_programs(1) - 1)
    def _():
        step(biasnew_ref[0], knew_ref[0], vnew_ref[0])
        acc = acc_ref[...]
        row_head = lax.broadcasted_iota(jnp.int32, acc.shape, 0) // (n_rows // KV_HEADS)
        col_head = lax.broadcasted_iota(jnp.int32, acc.shape, 1) // HEAD_DIM
        own = jnp.where(row_head == col_head, acc, 0.0)
        folded = own[:, :LANE] + own[:, LANE:]
        folded = folded + pltpu.roll(folded, HEAD_DIM, axis=1)
        o_ref[0] = folded / l_ref[...]


def _sample_dsa_attention(q, k, v, iq, iw, ik, cache_k, cache_v, cache_idx_k, page_table, *, pp_index=64, pp=32):
    b, t = q.shape[:2]
    n_pages = page_table.shape[1]
    n_pool = cache_k.shape[0]
    past = n_pages * PAGE_SIZE
    topk = min(IDX_TOPK_MAX, (past + t) // 4)
    n_idx_bits = max(1, (past + PAGE_SIZE - 1).bit_length())
    pp_index, pp = min(pp_index, n_pages), min(pp, n_pages)
    assert n_pages % pp == 0 and n_pages % pp_index == 0 and t == SUBLANE and KV_W == 2 * LANE
    npg = n_pages // pp
    n_keys = past + PAGE_SIZE
    bf = jnp.bfloat16
    rep = N_HEADS // KV_HEADS
    n_rows = N_HEADS * t

    iq_r = iq.astype(bf).transpose(0, 2, 1, 3).reshape(b, IDX_HEADS * t, IDX_DIM)
    iw_r = iw.transpose(0, 2, 1).reshape(b, IDX_HEADS * t, 1)
    new_page = lambda a: jnp.pad(jnp.swapaxes(a, 1, 2), ((0, 0), (0, 0), (0, PAGE_SIZE - t)))
    pages_t = lambda c: jnp.swapaxes(c.reshape(n_pool, PAGE_SIZE, -1), 1, 2)
    ik_new = new_page(ik)
    k_new = new_page(k.reshape(b, t, KV_W))
    v_new = new_page(v.reshape(b, t, KV_W))
    qs = (q * HEAD_DIM ** -0.5).astype(bf).reshape(b, t, KV_HEADS, rep, HEAD_DIM).transpose(0, 2, 3, 1, 4)
    q_bd = (qs[:, :, :, :, None, :] * jnp.eye(KV_HEADS, dtype=bf)[None, :, None, None, :, None])
    q_bd = q_bd.reshape(b, n_rows, KV_W)

    page_map = lambda per_step, j: (lambda bi, pg, pt: (pt[bi, pg * per_step + j], 0, 0))
    per_seq = lambda shape: pl.BlockSpec((1,) + shape, lambda bi, pg, pt: (bi,) + (0,) * len(shape))
    params = pltpu.CompilerParams(dimension_semantics=("arbitrary", "arbitrary"), vmem_limit_bytes=VMEM_LIMIT_BYTES)

    bias = pl.pallas_call(
        functools.partial(_sample_index_kernel, pp=pp_index, n_pages=n_pages, topk=topk, n_idx_bits=n_idx_bits),
        grid_spec=pltpu.PrefetchScalarGridSpec(
            num_scalar_prefetch=1, grid=(b, n_pages // pp_index),
            in_specs=[per_seq((IDX_HEADS * t, IDX_DIM)), per_seq((IDX_HEADS * t, 1)), per_seq((IDX_DIM, PAGE_SIZE))]
            + [pl.BlockSpec((1, IDX_DIM, PAGE_SIZE), page_map(pp_index, j)) for j in range(pp_index)],
            out_specs=per_seq((t, n_keys)),
            scratch_shapes=[pltpu.VMEM((n_pages // pp_index, t, pp_index * PAGE_SIZE), jnp.int32),
                            pltpu.VMEM((1, t, n_keys), jnp.int32),
                            pltpu.VMEM((IDX_HEADS * t, PAGE_SIZE), jnp.float32),
                            pltpu.VMEM((t, LANE), jnp.int32)]),
        out_shape=jax.ShapeDtypeStruct((b, t, n_keys), jnp.float32),
        compiler_params=params,
        name="sample_index",
    )(page_table, iq_r, iw_r, ik_new, *([pages_t(cache_idx_k)] * pp_index))

    ck = pages_t(cache_k)
    cv = pages_t(cache_v)
    out = pl.pallas_call(
        functools.partial(_sample_attend_kernel, pp=pp, n_pages=n_pages),
        grid_spec=pltpu.PrefetchScalarGridSpec(
            num_scalar_prefetch=1, grid=(b, npg),
            in_specs=[per_seq((n_rows, KV_W)),
                      pl.BlockSpec((1, t, pp * PAGE_SIZE), lambda bi, pg, pt: (bi, 0, pg)),
                      pl.BlockSpec((1, t, PAGE_SIZE), lambda bi, pg, pt: (bi, 0, n_pages)),
                      per_seq((KV_W, PAGE_SIZE)), per_seq((KV_W, PAGE_SIZE))]
            + [pl.BlockSpec((1, KV_W, PAGE_SIZE), page_map(pp, j)) for j in range(pp)] * 2,
            out_specs=per_seq((n_rows, LANE)),
            scratch_shapes=[pltpu.VMEM((n_rows, LANE), jnp.float32), pltpu.VMEM((n_rows, LANE), jnp.float32),
                            pltpu.VMEM((n_rows, KV_W), jnp.float32)]),
        out_shape=jax.ShapeDtypeStruct((b, n_rows, LANE), jnp.float32),
        compiler_params=params,
        name="sample_attend",
    )(page_table, q_bd, bias, bias, k_new, v_new, *([ck] * pp), *([cv] * pp))
    out = out[:, :, :HEAD_DIM].reshape(b, N_HEADS, t, HEAD_DIM).transpose(0, 2, 1, 3)
    return out.reshape(b, t, N_HEADS * HEAD_DIM)


GELU_C = math.sqrt(2.0 / math.pi)


def _rglru_kernel(x_ref, xg_ref, prev_ref, h0_ref, cw_ref, cb_ref, wa_ref, ba_ref, wi_ref, bi_ref, lam_ref,
                  o_ref, hl_ref, conv_ref, a_ref, u_ref, xprev_ref, h_ref, *, rows, per_group_state):
    i = pl.program_id(0)
    n_groups = rows // SUBLANE
    d_rnn = x_ref.shape[1]
    r8 = lax.broadcasted_iota(jnp.int32, (SUBLANE, d_rnn), 0)
    grp = lambda g: pl.ds(pl.multiple_of(g * SUBLANE, SUBLANE), SUBLANE)

    if not per_group_state:
        @pl.when(i == 0)
        def _():
            xprev_ref[...] = jnp.zeros_like(xprev_ref)
            h_ref[...] = jnp.zeros_like(h_ref)

    cw = cw_ref[...]
    cb = cb_ref[...]

    def conv_group(g, prev):
        x8 = x_ref[grp(g), :]
        if per_group_state:
            prev = prev_ref[grp(g), :]
        out = cb + x8 * cw[CONV_W - 1:CONV_W, :]
        for d in range(1, CONV_W):
            shifted = jnp.where(r8 < d, pltpu.roll(prev, d, axis=0), pltpu.roll(x8, d, axis=0))
            out = out + shifted * cw[CONV_W - 1 - d:CONV_W - d, :]
        conv_ref[grp(g), :] = out
        return x8

    zeros8 = jnp.zeros((SUBLANE, d_rnn), jnp.float32)
    xlast = lax.fori_loop(0, n_groups, conv_group, zeros8 if per_group_state else xprev_ref[...])
    if not per_group_state:
        xprev_ref[...] = xlast

    xc = conv_ref[...]
    xb = xc.astype(jnp.bfloat16)
    r_parts, i_parts = [], []
    for n in range(RG_BLOCKS):
        xn = xb[:, n * RG_BLOCK_W:(n + 1) * RG_BLOCK_W]
        r_parts.append(jnp.dot(xn, wa_ref[n], preferred_element_type=jnp.float32))
        i_parts.append(jnp.dot(xn, wi_ref[n], preferred_element_type=jnp.float32))
    r = jax.nn.sigmoid(jnp.concatenate(r_parts, axis=-1) + ba_ref[...])
    ig = jax.nn.sigmoid(jnp.concatenate(i_parts, axis=-1) + bi_ref[...])
    lam = lam_ref[...]
    log_sig_lam = -(jnp.maximum(-lam, 0.0) + jnp.log1p(jnp.exp(-jnp.abs(lam))))
    log_a = RG_C * r * log_sig_lam
    a = jnp.exp(log_a)
    a_ref[...] = a
    u_ref[...] = jnp.sqrt(-jnp.tanh(log_a) * (a * a + 1.0)) * (ig * xc)

    def scan_group(g, hprev):
        a = a_ref[grp(g), :]
        u = u_ref[grp(g), :]
        if per_group_state:
            hprev = jnp.broadcast_to(h0_ref[pl.ds(g, 1), :], (SUBLANE, d_rnn))
        for d in (1, 2, 4):
            u = jnp.where(r8 >= d, a * pltpu.roll(u, d, axis=0) + u, u)
            a = jnp.where(r8 >= d, a * pltpu.roll(a, d, axis=0), a)
        h = a * hprev + u
        u_ref[grp(g), :] = h
        hlast = jnp.broadcast_to(h[SUBLANE - 1:SUBLANE, :], (SUBLANE, d_rnn))
        if per_group_state:
            hl_ref[pl.ds(g, 1), :] = h[SUBLANE - 1:SUBLANE, :]
        return hlast

    hlast = lax.fori_loop(0, n_groups, scan_group, zeros8 if per_group_state else h_ref[...])
    if not per_group_state:
        h_ref[...] = hlast
        hl_ref[...] = hlast

    xg = xg_ref[...]
    gelu = 0.5 * xg * (1.0 + jnp.tanh(GELU_C * (xg + 0.044715 * (xg * xg * xg))))
    o_ref[...] = u_ref[...] * gelu


def _rglru(x, xg, prev, h0, conv_w, conv_b, w_a, b_a, w_i, b_i, lam, *, per_group_state, tile_rows):
    n_rows, d = x.shape
    row = lambda a: a.reshape(1, d)
    const = lambda shape: pl.BlockSpec(shape, lambda i: (0,) * len(shape))
    n_hl = n_rows // SUBLANE if per_group_state else SUBLANE
    body = functools.partial(_rglru_kernel, rows=tile_rows, per_group_state=per_group_state)
    tile = pl.BlockSpec((tile_rows, d), lambda i: (i, 0))
    if per_group_state:
        assert tile_rows == n_rows
        prev_spec, h0_spec = const(prev.shape), const(h0.shape)
    else:
        prev_spec, h0_spec = const(prev.shape), const(h0.shape)
    return pl.pallas_call(
        body,
        grid=(n_rows // tile_rows,),
        in_specs=[tile, tile, prev_spec, h0_spec, const((CONV_W, d)), const((1, d)),
                  const(w_a.shape), const((1, d)), const(w_i.shape), const((1, d)), const((1, d))],
        out_specs=[tile, const((n_hl, d))],
        out_shape=[jax.ShapeDtypeStruct((n_rows, d), jnp.float32), jax.ShapeDtypeStruct((n_hl, d), jnp.float32)],
        scratch_shapes=[pltpu.VMEM((tile_rows, d), jnp.float32), pltpu.VMEM((tile_rows, d), jnp.float32),
                        pltpu.VMEM((tile_rows, d), jnp.float32), pltpu.VMEM((SUBLANE, d), jnp.float32),
                        pltpu.VMEM((SUBLANE, d), jnp.float32)],
        compiler_params=pltpu.CompilerParams(dimension_semantics=("arbitrary",), vmem_limit_bytes=VMEM_LIMIT_BYTES),
        name="rglru_sample" if per_group_state else "rglru_prompt",
    )(x, xg, prev, h0, conv_w, row(conv_b), w_a.astype(jnp.bfloat16), row(b_a), w_i.astype(jnp.bfloat16), row(b_i), row(lam))


def _rglru_prompt(x, xg, conv_w, conv_b, w_a, b_a, w_i, b_i, lam, tile_rows=512):
    dummy = jnp.zeros((SUBLANE, x.shape[1]), jnp.float32)
    rnn, hl = _rglru(x, xg, dummy, dummy, conv_w, conv_b, w_a, b_a, w_i, b_i, lam,
                     per_group_state=False, tile_rows=tile_rows)
    return rnn, hl[0]


def _rglru_sample(x, xg, state_conv, state_h, conv_w, conv_b, w_a, b_a, w_i, b_i, lam):
    b, t, d = x.shape
    assert t == SUBLANE
    prev = jnp.concatenate([jnp.zeros((b, SUBLANE - (CONV_W - 1), d), jnp.float32), state_conv], axis=1)
    rnn, hl = _rglru(x.reshape(b * t, d), xg.reshape(b * t, d), prev.reshape(b * t, d), state_h,
                     conv_w, conv_b, w_a, b_a, w_i, b_i, lam, per_group_state=True, tile_rows=b * t)
    return rnn.reshape(b, t, d), hl


def _rms_norm(x, g):
    xf = x.astype(jnp.float32)
    y = xf * lax.rsqrt(jnp.mean(xf * xf, axis=-1, keepdims=True) + EPS)
    return (y * g.astype(jnp.float32)).astype(x.dtype)


def _rope(x, pos):
    half = x.shape[-1] // 2
    inv_freq = ROPE_THETA ** (-jnp.arange(half, dtype=jnp.float32) / half)
    ang = pos.astype(jnp.float32)[:, None] * inv_freq[None, :]
    cos = jnp.cos(ang)[:, None, :]
    sin = jnp.sin(ang)[:, None, :]
    xf = x.astype(jnp.float32)
    x1, x2 = xf[..., :half], xf[..., half:]
    return jnp.concatenate([x1 * cos - x2 * sin, x2 * cos + x1 * sin], axis=-1).astype(x.dtype)


def _index_scores(iq, iw, ik):
    s = jnp.einsum('bthd,bld->bthl', iq, ik, preferred_element_type=jnp.float32)
    return jnp.einsum('bthl,bth->btl', jax.nn.relu(s), iw.astype(jnp.float32))


def _sparse_attend(q, k_sel, v_sel, valid):
    B, T = q.shape[:2]
    qg = q.reshape(B, T, KV_HEADS, N_HEADS // KV_HEADS, HEAD_DIM)
    logits = jnp.einsum('btkgd,btskd->btkgs', qg, k_sel, preferred_element_type=jnp.float32) * (HEAD_DIM ** -0.5)
    logits = jnp.where(valid[:, :, None, None, :], logits, -jnp.inf)
    p = jax.nn.softmax(logits, axis=-1)
    out = jnp.einsum('btkgs,btskd->btkgd', p.astype(v_sel.dtype), v_sel)
    return out.reshape(B, T, Q_W)


def _gather_rows(rows, idx):
    return jax.vmap(lambda r, i: r[i])(rows, idx)


def _prompt_sparse_attention(q, k, v, iq, iw, ik):
    B, S = q.shape[:2]
    topk = min(IDX_TOPK_MAX, S // 4)
    n_blocks = S // Q_BLOCK
    key_pos = jnp.arange(S, dtype=jnp.int32)

    def block(args):
        qb, iqb, iwb, start = args
        qpos = start + jnp.arange(Q_BLOCK, dtype=jnp.int32)
        sc = _index_scores(iqb, iwb, ik)
        sc = jnp.where((key_pos[None, :] <= qpos[:, None])[None], sc, -jnp.inf)
        _, sel = lax.top_k(sc, topk)
        valid = sel <= qpos[None, :, None]
        return _sparse_attend(qb, _gather_rows(k, sel), _gather_rows(v, sel), valid)

    to_blocks = lambda a: jnp.moveaxis(a.reshape(B, n_blocks, Q_BLOCK, *a.shape[2:]), 1, 0)
    starts = jnp.arange(n_blocks, dtype=jnp.int32) * Q_BLOCK
    out = lax.map(block, (to_blocks(q), to_blocks(iq), to_blocks(iw), starts))
    return jnp.moveaxis(out, 0, 1).reshape(B, S, Q_W)


def _sample_sparse_attention(q, k, v, iq, iw, ik, cache_k, cache_v, cache_idx_k, page_table):
    DB, T = q.shape[:2]
    past = page_table.shape[1] * PAGE_SIZE
    L = past + T
    topk = min(IDX_TOPK_MAX, L // 4)
    ik_past = cache_idx_k[page_table].reshape(DB, past, IDX_DIM)
    ik_all = jnp.concatenate([ik_past, ik.astype(ik_past.dtype)], axis=1)
    qpos = past + jnp.arange(T, dtype=jnp.int32)
    sc = _index_scores(iq, iw, ik_all)
    sc = jnp.where((jnp.arange(L, dtype=jnp.int32)[None, :] <= qpos[:, None])[None], sc, -jnp.inf)
    _, sel = lax.top_k(sc, topk)
    valid = sel <= qpos[None, :, None]
    in_past = sel < past
    sel_p = jnp.minimum(sel, past - 1)
    page = _gather_rows(page_table, sel_p // PAGE_SIZE)
    phys = page * PAGE_SIZE + sel_p % PAGE_SIZE
    new_idx = jnp.clip(sel - past, 0, T - 1)
    k_pool = cache_k.reshape(-1, KV_HEADS, HEAD_DIM)
    v_pool = cache_v.reshape(-1, KV_HEADS, HEAD_DIM)
    m = in_past[..., None, None]
    k_sel = jnp.where(m, k_pool[phys], _gather_rows(k, new_idx).astype(k_pool.dtype))
    v_sel = jnp.where(m, v_pool[phys], _gather_rows(v, new_idx).astype(v_pool.dtype))
    return _sparse_attend(q, k_sel, v_sel, valid)


def _causal_conv(xr, conv_state, w, b):
    T = xr.shape[1]
    xp = jnp.concatenate([conv_state.astype(xr.dtype), xr], axis=1)
    out = b + sum(xp[:, j:j + T] * w[j] for j in range(CONV_W))
    return out, xp[:, -(CONV_W - 1):]


def _block_diag(x, w, b):
    B, T, _ = x.shape
    y = jnp.einsum('btnc,ncd->btnd', x.reshape(B, T, RG_BLOCKS, RG_BLOCK_W), w)
    return y.reshape(B, T, D_RNN) + b


def _rg_lru(x, h0, w_a, b_a, w_i, b_i, lam):
    xf = x.astype(jnp.float32)
    r = jax.nn.sigmoid(_block_diag(x, w_a, b_a).astype(jnp.float32))
    i = jax.nn.sigmoid(_block_diag(x, w_i, b_i).astype(jnp.float32))
    log_a = RG_C * r * jax.nn.log_sigmoid(lam.astype(jnp.float32))
    a = jnp.exp(log_a)
    u = jnp.sqrt(-jnp.expm1(2.0 * log_a)) * (i * xf)

    def step(h, au):
        h = au[0] * h + au[1]
        return h, h

    h_last, hs = lax.scan(step, h0.astype(jnp.float32), (jnp.moveaxis(a, 1, 0), jnp.moveaxis(u, 1, 0)))
    return jnp.moveaxis(hs, 0, 1).astype(x.dtype), h_last.astype(x.dtype)


def _rms_norm_rows(x, g):
    return x * lax.rsqrt(jnp.mean(x * x, axis=-1, keepdims=True) + EPS) * g


def _router_kernel(x_ref, g_ref, wr_ref, br_ref, tri_ref, e_ref, gate_ref, rank_ref, cnt_ref, run_ref):
    i = pl.program_id(0)
    tm = x_ref.shape[0]

    @pl.when(i == 0)
    def _():
        run_ref[...] = jnp.zeros_like(run_ref)

    xn_bf = _rms_norm_rows(x_ref[...], g_ref[...]).astype(jnp.bfloat16)
    logits = lax.dot_general(wr_ref[...], xn_bf, (((1,), (1,)), ((), ())), preferred_element_type=jnp.float32)
    logits = logits + br_ref[...]

    expert = lax.broadcasted_iota(jnp.int32, (N_EXPERTS, tm), 0)
    member = jnp.zeros((N_EXPERTS, tm), jnp.float32)
    picked, values = [], []
    for k in range(TOP_K):
        mx = jnp.max(logits, axis=0, keepdims=True)
        idx = jnp.min(jnp.where(logits == mx, expert, N_EXPERTS), axis=0, keepdims=True)
        hit = expert == idx
        member = jnp.where(hit, 1.0, member)
        logits = jnp.where(hit, -jnp.inf, logits)
        picked.append(idx)
        values.append(mx)
        e_ref[k:k + 1, :] = idx

    ex = [jnp.exp(v - values[0]) for v in values]
    denom = ex[0] + ex[1] + ex[2] + ex[3]
    for k in range(TOP_K):
        gate_ref[k:k + 1, :] = ex[k] / denom

    before = jnp.dot(member.astype(jnp.bfloat16), tri_ref[...], preferred_element_type=jnp.float32)
    before = before + jnp.tile(run_ref[...], (1, tm // LANE))
    for k in range(TOP_K):
        r = jnp.sum(jnp.where(expert == picked[k], before, 0.0), axis=0, keepdims=True)
        rank_ref[k:k + 1, :] = r.astype(jnp.int32)
    run = run_ref[...] + jnp.sum(member, axis=1, keepdims=True)
    run_ref[...] = run
    cnt_ref[...] = run.astype(jnp.int32)


def _route(x2d, norm_g, w_router, b_router, tm=512):
    n, d = x2d.shape
    tri = (jnp.arange(tm)[:, None] < jnp.arange(tm)[None, :]).astype(jnp.bfloat16)
    const = lambda shape: pl.BlockSpec(shape, lambda i: (0,) * len(shape))
    rows4 = pl.BlockSpec((TOP_K, tm), lambda i: (0, i))
    top_e, gates, rank, cnt = pl.pallas_call(
        _router_kernel,
        grid=(n // tm,),
        in_specs=[pl.BlockSpec((tm, d), lambda i: (i, 0)), const((1, d)), const((N_EXPERTS, d)),
                  const((N_EXPERTS, 1)), const((tm, tm))],
        out_specs=[rows4, rows4, rows4, const((N_EXPERTS, LANE))],
        out_shape=[jax.ShapeDtypeStruct((TOP_K, n), jnp.int32),
                   jax.ShapeDtypeStruct((TOP_K, n), jnp.float32), jax.ShapeDtypeStruct((TOP_K, n), jnp.int32),
                   jax.ShapeDtypeStruct((N_EXPERTS, LANE), jnp.int32)],
        scratch_shapes=[pltpu.VMEM((N_EXPERTS, LANE), jnp.float32)],
        compiler_params=pltpu.CompilerParams(dimension_semantics=("arbitrary",)),
        name="moe_router",
    )(x2d, norm_g.reshape(1, d), w_router.T.astype(jnp.bfloat16), b_router.reshape(N_EXPERTS, 1), tri)
    return top_e, gates, rank, cnt[:, 0]


def _expert_kernel(be_ref, used_ref, xs_ref, g_ref, wgu_ref, bgu_ref, wd_ref, bd_ref, o_ref, *, blk):
    i = pl.program_id(0)

    @pl.when(i * blk < used_ref[0])
    def _():
        xn = _rms_norm_rows(xs_ref[...], g_ref[...]).astype(jnp.bfloat16)
        gu = jnp.dot(xn, wgu_ref[0], preferred_element_type=jnp.float32) + bgu_ref[0]
        g = jnp.minimum(gu[:, :D_FF], SWIGLU_LIMIT)
        u = jnp.clip(gu[:, D_FF:], -SWIGLU_LIMIT, SWIGLU_LIMIT)
        act = (u + 1.0) * (g * jax.nn.sigmoid(SWIGLU_ALPHA * g))
        o_ref[...] = jnp.dot(act.astype(jnp.bfloat16), wd_ref[0], preferred_element_type=jnp.float32) + bd_ref[0]

    @pl.when(i * blk >= used_ref[0])
    def _():
        o_ref[...] = jnp.zeros_like(o_ref)


def _experts(xs, norm_g, blk_exp, n_used, w_gate_up, b_gate_up, w_down, b_down, *, blk):
    n_slots, d = xs.shape
    n_blocks = n_slots // blk
    bf = jnp.bfloat16
    return pl.pallas_call(
        functools.partial(_expert_kernel, blk=blk),
        grid_spec=pltpu.PrefetchScalarGridSpec(
            num_scalar_prefetch=2, grid=(n_blocks,),
            in_specs=[pl.BlockSpec((blk, d), lambda i, be, nu: (i, 0)),
                      pl.BlockSpec((1, d), lambda i, be, nu: (0, 0)),
                      pl.BlockSpec((1, d, 2 * D_FF), lambda i, be, nu: (be[i], 0, 0)),
                      pl.BlockSpec((1, 1, 2 * D_FF), lambda i, be, nu: (be[i], 0, 0)),
                      pl.BlockSpec((1, D_FF, d), lambda i, be, nu: (be[i], 0, 0)),
                      pl.BlockSpec((1, 1, d), lambda i, be, nu: (be[i], 0, 0))],
            out_specs=pl.BlockSpec((blk, d), lambda i, be, nu: (i, 0))),
        out_shape=jax.ShapeDtypeStruct((n_slots, d), jnp.float32),
        compiler_params=pltpu.CompilerParams(dimension_semantics=("arbitrary",), vmem_limit_bytes=VMEM_LIMIT_BYTES),
        name="moe_experts",
    )(blk_exp, n_used, xs, norm_g.reshape(1, d), w_gate_up.astype(bf), b_gate_up.reshape(N_EXPERTS, 1, 2 * D_FF),
      w_down.astype(bf), b_down.reshape(N_EXPERTS, 1, d))


def _row_copy(src_hbm, src_row, dst_ref, dst_row, sem):
    return pltpu.make_async_copy(src_hbm.at[pl.ds(src_row, 1)], dst_ref.at[pl.ds(dst_row, 1)], sem)


def _dispatch_kernel(dest_ref, x_ref, xs_in_hbm, xs_hbm, sem, *, tm):
    del xs_in_hbm

    def issue(t, carry):
        for k in range(TOP_K):
            _row_copy(x_ref, t, xs_hbm, dest_ref[k, t], sem).start()
        return carry

    lax.fori_loop(0, tm, issue, 0)

    def drain(t, carry):
        for k in range(TOP_K):
            _row_copy(x_ref, 0, xs_hbm, 0, sem).wait()
        return carry

    lax.fori_loop(0, tm, drain, 0)


def _dispatch(x2d, dest, n_slots, tm=128):
    n, d = x2d.shape
    return pl.pallas_call(
        functools.partial(_dispatch_kernel, tm=tm),
        grid=(n // tm,),
        in_specs=[pl.BlockSpec((TOP_K, tm), lambda i: (0, i), memory_space=pltpu.SMEM),
                  pl.BlockSpec((tm, d), lambda i: (i, 0)), pl.BlockSpec(memory_space=pl.ANY)],
        out_specs=pl.BlockSpec(memory_space=pl.ANY),
        out_shape=jax.ShapeDtypeStruct((n_slots, d), x2d.dtype),
        scratch_shapes=[pltpu.SemaphoreType.DMA(())],
        input_output_aliases={2: 0},
        compiler_params=pltpu.CompilerParams(dimension_semantics=("arbitrary",)),
        name="moe_dispatch",
    )(dest, x2d, jnp.zeros((n_slots, d), x2d.dtype))


def _combine_kernel(dest_ref, x_ref, gate_ref, ys_hbm, o_ref, buf_ref, sem, *, tm):
    def issue(t, carry):
        for k in range(TOP_K):
            _row_copy(ys_hbm, dest_ref[k, t], buf_ref.at[k], t, sem).start()
        return carry

    lax.fori_loop(0, tm, issue, 0)

    def drain(t, carry):
        for k in range(TOP_K):
            _row_copy(ys_hbm, 0, buf_ref.at[k], 0, sem).wait()
        return carry

    lax.fori_loop(0, tm, drain, 0)
    gate = gate_ref[...]
    acc = x_ref[...]
    for k in range(TOP_K):
        acc = acc + gate[:, k:k + 1] * buf_ref[k]
    o_ref[...] = acc


def _combine(x2d, ys, dest, gates_t, tm=128):
    n, d = x2d.shape
    return pl.pallas_call(
        functools.partial(_combine_kernel, tm=tm),
        grid=(n // tm,),
        in_specs=[pl.BlockSpec((TOP_K, tm), lambda i: (0, i), memory_space=pltpu.SMEM),
                  pl.BlockSpec((tm, d), lambda i: (i, 0)), pl.BlockSpec((tm, TOP_K), lambda i: (i, 0)),
                  pl.BlockSpec(memory_space=pl.ANY)],
        out_specs=pl.BlockSpec((tm, d), lambda i: (i, 0)),
        out_shape=jax.ShapeDtypeStruct((n, d), jnp.float32),
        scratch_shapes=[pltpu.VMEM((TOP_K, tm, d), jnp.float32), pltpu.SemaphoreType.DMA(())],
        compiler_params=pltpu.CompilerParams(dimension_semantics=("arbitrary",)),
        name="moe_combine",
    )(dest, x2d, gates_t, ys)


MOE_ROWS = 512


def _moe(x2d, norm_g, w_router, b_router, w_gate_up, b_gate_up, w_down, b_down, *, blk=MOE_ROWS, tm=512):
    n, d = x2d.shape
    top_e, gates, rank, counts = _route(x2d, norm_g, w_router, b_router, tm=tm)
    padded = (counts + blk - 1) // blk * blk
    pad_end = jnp.cumsum(padded)
    pad_start = pad_end - padded
    dest = pad_start[top_e] + rank
    n_blocks = -(-(n * TOP_K + N_EXPERTS * (blk - 1)) // blk)
    n_slots = n_blocks * blk
    blk_exp = jnp.minimum(jnp.searchsorted(pad_end, jnp.arange(n_blocks, dtype=jnp.int32) * blk, side='right'),
                          N_EXPERTS - 1).astype(jnp.int32)
    n_used = pad_end[-1:].astype(jnp.int32)
    xs = _dispatch(x2d, dest, n_slots)
    ys = _experts(xs, norm_g, blk_exp, n_used, w_gate_up, b_gate_up, w_down, b_down, blk=blk)
    return _combine(x2d, ys, dest, gates.T)


def _trunk_layer(x, pos, attend, conv_state, h0, norm1_g, w_in, q_norm_g, k_norm_g, conv_w, conv_b,
                 rg_w_a, rg_b_a, rg_w_i, rg_b_i, rg_lambda, w_attn_out, w_rnn_out, w_out, norm2_g,
                 w_router, b_router, w_gate_up, b_gate_up, w_down, b_down):
    B, T, _ = x.shape
    n_pad = (-D_IN) % 896
    w_pad = jnp.pad(w_in, ((0, 0), (0, n_pad))).astype(jnp.bfloat16)
    z = _norm_proj(x.reshape(B * T, D_MODEL), norm1_g, w_pad)[:, :D_IN].reshape(B, T, D_IN)
    q, k, v, iq, ik, iw, xr, xg, ga, gr = jnp.split(z, np.cumsum(IN_WIDTHS)[:-1].tolist(), axis=-1)
    q = _rope(_rms_norm(q.reshape(B, T, N_HEADS, HEAD_DIM), q_norm_g), pos)
    k = _rope(_rms_norm(k.reshape(B, T, KV_HEADS, HEAD_DIM), k_norm_g), pos)
    v = v.reshape(B, T, KV_HEADS, HEAD_DIM)
    iq = _rope(iq.reshape(B, T, IDX_HEADS, IDX_DIM), pos)
    ik = _rope(ik[:, :, None, :], pos)[:, :, 0]
    iw = iw * (IDX_HEADS ** -0.5 * IDX_DIM ** -0.5)
    rg = (conv_w, conv_b, rg_w_a, rg_b_a, rg_w_i, rg_b_i, rg_lambda)
    if attend is None:
        assert B == 1
        attn = _prompt_dsa_attention(q[0], k[0], v[0], iq[0], iw[0], ik[0])[None]
        rnn, h_new = _rglru_prompt(xr[0], xg[0], *rg)
        rnn, h_new = rnn[None], h_new[None]
        conv_new = xr[:, -(CONV_W - 1):]
    else:
        attn = attend(q, k, v, iq, iw, ik)
        rnn, h_new = _rglru_sample(xr, xg, conv_state, h0, *rg)
        conv_new = jnp.concatenate([conv_state, xr], axis=1)[:, -(CONV_W - 1):]
    merged = jax.nn.sigmoid(ga) * (attn @ w_attn_out) + jax.nn.sigmoid(gr) * (rnn @ w_rnn_out)
    x = x + merged @ w_out
    return x, (k, v, ik, conv_new, h_new)


def kernel(x_prompt, x_sample, cache_k, cache_v, cache_idx_k, state_conv, state_h, page_table,
           norm1_g, w_in, q_norm_g, k_norm_g, conv_w, conv_b, rg_w_a, rg_b_a, rg_w_i, rg_b_i, rg_lambda,
           w_attn_out, w_rnn_out, w_out, norm2_g, w_router, b_router, w_gate_up, b_gate_up, w_down, b_down):
    weights = (norm1_g, w_in, q_norm_g, k_norm_g, conv_w, conv_b, rg_w_a, rg_b_a, rg_w_i, rg_b_i, rg_lambda,
               w_attn_out, w_rnn_out, w_out, norm2_g, w_router, b_router, w_gate_up, b_gate_up, w_down, b_down)
    B, S = x_prompt.shape[:2]
    T = x_sample.shape[1]
    past = page_table.shape[1] * PAGE_SIZE
    pos_p = jnp.arange(S, dtype=jnp.int32)
    pos_s = past + jnp.arange(T, dtype=jnp.int32)
    conv0 = jnp.zeros((B, CONV_W - 1, D_RNN), x_prompt.dtype)
    h0 = jnp.zeros((B, D_RNN), jnp.float32)
    wl = [w[0] for w in weights]
    yp, st_p = _trunk_layer(x_prompt, pos_p, None, conv0, h0, *wl)
    attend_s = functools.partial(_sample_dsa_attention, cache_k=cache_k[0], cache_v=cache_v[0],
                                 cache_idx_k=cache_idx_k[0], page_table=page_table)
    ys, st_s = _trunk_layer(x_sample, pos_s, attend_s, state_conv[0], state_h[0], *wl)
    x_all = jnp.concatenate([yp.reshape(-1, D_MODEL), ys.reshape(-1, D_MODEL)], axis=0)
    x_all = _moe(x_all, norm2_g[0], w_router[0], b_router[0], w_gate_up[0], b_gate_up[0], w_down[0], b_down[0])
    yp = x_all[:B * S].reshape(x_prompt.shape)
    ys = x_all[B * S:].reshape(x_sample.shape)
    k_p, v_p, ik_p, conv_p, h_p = [a[None] for a in st_p]
    k_s, v_s, ik_s, conv_s, h_s = [a[None] for a in st_s]
    return (yp, ys, k_p, v_p, ik_p, conv_p, h_p, k_s, v_s, ik_s, conv_s, h_s)
```

```python
import functools
import math

import jax
import jax.numpy as jnp
import numpy as np
from jax import lax
from jax.experimental import pallas as pl
from jax.experimental.pallas import tpu as pltpu

D_MODEL = 1024
PAGE_SIZE = 128
N_HEADS = 16
HEAD_DIM = 64
KV_HEADS = 4
IDX_HEADS = 8
IDX_DIM = 64
IDX_TOPK_MAX = 256
Q_BLOCK = 128
ROPE_THETA = 10000.0
D_RNN = D_MODEL
RG_BLOCKS = 4
RG_BLOCK_W = D_RNN // RG_BLOCKS
CONV_W = 4
RG_C = 8.0
N_EXPERTS = 32
TOP_K = 4
D_FF = D_MODEL
SWIGLU_LIMIT = 7.0
SWIGLU_ALPHA = 1.702
MOE_BLOCK = 128
EPS = 1e-6

Q_W = N_HEADS * HEAD_DIM
KV_W = KV_HEADS * HEAD_DIM
IQ_W = IDX_HEADS * IDX_DIM
IN_WIDTHS = (Q_W, KV_W, KV_W, IQ_W, IDX_DIM, IDX_HEADS, D_RNN, D_RNN, D_MODEL, D_MODEL)
D_IN = sum(IN_WIDTHS)

LANE = 128
SUBLANE = 8
VMEM_LIMIT_BYTES = 48 * 1024 * 1024


def _norm_proj_kernel(x_ref, g_ref, w_ref, o_ref):
    x = x_ref[...]
    y = x * lax.rsqrt(jnp.mean(x * x, axis=-1, keepdims=True) + EPS) * g_ref[...]
    o_ref[...] = jnp.dot(y.astype(jnp.bfloat16), w_ref[...], preferred_element_type=jnp.float32)


def _norm_proj(x2d, g, w_bf16, tm=512, tn=896):
    m, d = x2d.shape
    n = w_bf16.shape[1]
    return pl.pallas_call(
        _norm_proj_kernel,
        grid=(m // tm, n // tn),
        in_specs=[
            pl.BlockSpec((tm, d), lambda i, j: (i, 0)),
            pl.BlockSpec((1, d), lambda i, j: (0, 0)),
            pl.BlockSpec((d, tn), lambda i, j: (0, j)),
        ],
        out_specs=pl.BlockSpec((tm, tn), lambda i, j: (i, j)),
        out_shape=jax.ShapeDtypeStruct((m, n), jnp.float32),
        name="norm_proj",
    )(x2d, g.reshape(1, d), w_bf16)


INT_MIN = -(2 ** 31)
INT_MAX = 2 ** 31 - 1
MASK_BIAS = -1e30
NEG_INF_BITS_MASK = 0x7FFFFFFF


def _ordered_key(x):
    bits = pltpu.bitcast(x, jnp.int32)
    return bits ^ ((bits >> 31) & NEG_INF_BITS_MASK)


def _topk_threshold(key_ref, cut_ref, nk, *, rows, kb, topk, n_idx_bits):
    n_lane_tiles = kb // LANE
    lane = lax.broadcasted_iota(jnp.int32, (rows, kb), 1)

    def count(pred_of_chunk):
        def body(c, cnt):
            p = pred_of_chunk(c, key_ref[c])
            for j in range(n_lane_tiles):
                cnt = cnt + jnp.where(p[:, j * LANE:(j + 1) * LANE], 1, 0)
            return cnt
        cnt = lax.fori_loop(0, nk, body, jnp.zeros((rows, LANE), jnp.int32))
        return jnp.sum(cnt, axis=-1, keepdims=True)

    def bit_step(b, carry):
        thr, n_ge = carry
        cand = thr ^ (jnp.int32(1) << (31 - b))
        cand_b = jnp.broadcast_to(cand, (rows, kb))
        cnt = count(lambda c, key: key >= cand_b)
        ok = cnt >= topk
        return jnp.where(ok, cand, thr), jnp.where(ok, cnt, n_ge)

    thr0 = jnp.full((rows, 1), INT_MIN, jnp.int32)
    n0 = jnp.full((rows, 1), 0, jnp.int32) + nk * kb
    thr, n_ge = lax.fori_loop(0, 32, bit_step, (thr0, n0))
    thr_b = jnp.broadcast_to(thr, (rows, kb))

    cut_ref[...] = jnp.full((rows, LANE), INT_MAX, jnp.int32)

    @pl.when(jnp.max(n_ge) > topk)
    def _():
        n_eq = count(lambda c, key: key == thr_b)
        need = topk - (n_ge - n_eq)

        def idx_step(b, lo):
            step = jnp.int32(1) << (n_idx_bits - 1 - b)
            mid_b = jnp.broadcast_to(lo + step - 1, (rows, kb))
            f = count(lambda c, key: (key == thr_b) & (c * kb + lane <= mid_b))
            return jnp.where(f < need, lo + step, lo)

        lo = lax.fori_loop(0, n_idx_bits, idx_step, jnp.zeros((rows, 1), jnp.int32))
        cut = jnp.where(n_ge > topk, lo, INT_MAX)
        cut_ref[...] = jnp.broadcast_to(cut, (rows, LANE))

    return thr


def _dsa_kernel(iq_ref, iw_ref, q_ref, ikT_ref, kT_ref, v_ref, o_ref,
                key_ref, wb_ref, cut_ref, m_ref, l_ref, acc_ref, *, qb, kb, topk, n_idx_bits, q_pos0, n_chunks_of):
    i = pl.program_id(0)
    nk = n_chunks_of(i)
    n_lane_tiles = kb // LANE
    row = lax.broadcasted_iota(jnp.int32, (qb, kb), 0)
    lane = lax.broadcasted_iota(jnp.int32, (qb, kb), 1)
    qpos = q_pos0 + i * qb + row

    iw = iw_ref[...]
    for h in range(IDX_HEADS):
        wb_ref[h] = jnp.broadcast_to(iw[:, h:h + 1], (qb, LANE))

    def score_chunk(c, carry):
        ikc = ikT_ref[c]
        acc = jnp.zeros((qb, kb), jnp.float32)
        for h in range(IDX_HEADS):
            s = jnp.dot(iq_ref[0, h], ikc, preferred_element_type=jnp.float32)
            acc = acc + jnp.tile(wb_ref[h], (1, n_lane_tiles)) * jnp.maximum(s, 0.0)
        acc = jnp.where(c * kb + lane <= qpos, acc, -jnp.inf)
        key_ref[c] = _ordered_key(acc)
        return carry

    lax.fori_loop(0, nk, score_chunk, 0)

    thr = _topk_threshold(key_ref, cut_ref, nk, rows=qb, kb=kb, topk=topk, n_idx_bits=n_idx_bits)
    thr_b = jnp.broadcast_to(thr, (qb, kb))

    m_ref[...] = jnp.full(m_ref.shape, MASK_BIAS, jnp.float32)
    l_ref[...] = jnp.zeros(l_ref.shape, jnp.float32)
    acc_ref[...] = jnp.zeros(acc_ref.shape, jnp.float32)
    cut_b = jnp.tile(cut_ref[...], (1, n_lane_tiles))
    rep = N_HEADS // KV_HEADS

    def attend_chunk(c, carry):
        key = key_ref[c]
        kpos = c * kb + lane
        sel = (key > thr_b) | ((key == thr_b) & (kpos <= cut_b))
        sel = sel & (kpos <= qpos)
        bias = jnp.where(sel, 0.0, MASK_BIAS)
        kc = kT_ref[c]
        for g in range(KV_HEADS):
            s = jnp.dot(q_ref[0, g], kc[g * HEAD_DIM:(g + 1) * HEAD_DIM, :], preferred_element_type=jnp.float32)
            s = (s.reshape(rep, qb, kb) + bias[None]).reshape(rep * qb, kb)
            m_prev = m_ref[g]
            m_new = jnp.maximum(m_prev, jnp.max(s, axis=-1, keepdims=True))
            alpha = jnp.exp(m_prev - m_new)
            p = jnp.exp(s - jnp.tile(m_new, (1, n_lane_tiles)))
            l_ref[g] = alpha * l_ref[g] + jnp.sum(p, axis=-1, keepdims=True)
            pv = jnp.dot(p.astype(jnp.bfloat16), v_ref[c, g], preferred_element_type=jnp.float32)
            acc_ref[g] = acc_ref[g] * alpha[:, :HEAD_DIM] + pv
            m_ref[g] = m_new
        return carry

    lax.fori_loop(0, nk, attend_chunk, 0)
    for g in range(KV_HEADS):
        o_ref[0, g] = acc_ref[g] / l_ref[g][:, :HEAD_DIM]


def _prompt_dsa_attention(q, k, v, iq, iw, ik, *, qb=128, kb=512):
    s_len = q.shape[0]
    topk = min(IDX_TOPK_MAX, s_len // 4)
    nqb, nkc = s_len // qb, s_len // kb
    rep = N_HEADS // KV_HEADS
    bf = jnp.bfloat16
    scale = HEAD_DIM ** -0.5
    q_t = (q * scale).astype(bf).reshape(nqb, qb, KV_HEADS, rep, HEAD_DIM).transpose(0, 2, 3, 1, 4)
    q_t = q_t.reshape(nqb, KV_HEADS, rep * qb, HEAD_DIM)
    iq_t = iq.astype(bf).reshape(nqb, qb, IDX_HEADS, IDX_DIM).transpose(0, 2, 1, 3)
    ikT = ik.astype(bf).reshape(nkc, kb, IDX_DIM).transpose(0, 2, 1)
    kT = k.astype(bf).reshape(nkc, kb, KV_HEADS * HEAD_DIM).transpose(0, 2, 1)
    v_t = v.astype(bf).reshape(nkc, kb, KV_HEADS, HEAD_DIM).transpose(0, 2, 1, 3)
    n_idx_bits = max(1, (s_len - 1).bit_length())
    body = functools.partial(
        _dsa_kernel, qb=qb, kb=kb, topk=topk, n_idx_bits=n_idx_bits, q_pos0=0,
        n_chunks_of=lambda i: ((i + 1) * qb + kb - 1) // kb)
    whole = lambda shape: pl.BlockSpec(shape, lambda i: (0,) * len(shape), pipeline_mode=pl.Buffered(1))
    out = pl.pallas_call(
        body,
        grid=(nqb,),
        in_specs=[
            pl.BlockSpec((1, IDX_HEADS, qb, IDX_DIM), lambda i: (i, 0, 0, 0)),
            pl.BlockSpec((qb, IDX_HEADS), lambda i: (i, 0)),
            pl.BlockSpec((1, KV_HEADS, rep * qb, HEAD_DIM), lambda i: (i, 0, 0, 0)),
            whole((nkc, IDX_DIM, kb)),
            whole((nkc, KV_HEADS * HEAD_DIM, kb)),
            whole((nkc, KV_HEADS, kb, HEAD_DIM)),
        ],
        out_specs=pl.BlockSpec((1, KV_HEADS, rep * qb, HEAD_DIM), lambda i: (i, 0, 0, 0)),
        out_shape=jax.ShapeDtypeStruct((nqb, KV_HEADS, rep * qb, HEAD_DIM), jnp.float32),
        scratch_shapes=[
            pltpu.VMEM((nkc, qb, kb), jnp.int32),
            pltpu.VMEM((IDX_HEADS, qb, LANE), jnp.float32),
            pltpu.VMEM((qb, LANE), jnp.int32),
            pltpu.VMEM((KV_HEADS, rep * qb, LANE), jnp.float32),
            pltpu.VMEM((KV_HEADS, rep * qb, LANE), jnp.float32),
            pltpu.VMEM((KV_HEADS, rep * qb, HEAD_DIM), jnp.float32),
        ],
        compiler_params=pltpu.CompilerParams(dimension_semantics=("arbitrary",), vmem_limit_bytes=VMEM_LIMIT_BYTES),
        name="prompt_dsa_attention",
    )(iq_t, iw, q_t, ikT, kT, v_t)
    out = out.reshape(nqb, KV_HEADS, rep, qb, HEAD_DIM).transpose(0, 3, 1, 2, 4)
    return out.reshape(s_len, N_HEADS * HEAD_DIM)


def _topk_threshold_t(key_ref, cut_ref, nk, *, kb, cols, topk, n_idx_bits):
    acc_rows = 8 * SUBLANE
    assert kb % acc_rows == 0
    pos0 = lax.broadcasted_iota(jnp.int32, (kb, cols), 0)

    def count(pred_of_chunk):
        def body(c, cnt):
            p = pred_of_chunk(c, key_ref[c])
            for j in range(kb // acc_rows):
                cnt = cnt + jnp.where(p[j * acc_rows:(j + 1) * acc_rows, :], 1, 0)
            return cnt
        cnt = lax.fori_loop(0, nk, body, jnp.zeros((acc_rows, cols), jnp.int32))
        return jnp.sum(cnt, axis=0, keepdims=True)

    def bit_cond(carry):
        b, _, n_ge = carry
        return jnp.logical_and(b < 32, jnp.max(jnp.abs(n_ge - topk)) > 0)

    def bit_step(carry):
        b, thr, n_ge = carry
        cand = thr ^ (jnp.int32(1) << (31 - b))
        cand_b = jnp.broadcast_to(cand, (kb, cols))
        cnt = count(lambda c, key: key >= cand_b)
        ok = cnt >= topk
        return b + 1, jnp.where(ok, cand, thr), jnp.where(ok, cnt, n_ge)

    thr0 = jnp.full((1, cols), INT_MIN, jnp.int32)
    n0 = jnp.full((1, cols), 0, jnp.int32) + nk * kb
    _, thr, n_ge = lax.while_loop(bit_cond, bit_step, (jnp.int32(0), thr0, n0))
    thr_b = jnp.broadcast_to(thr, (kb, cols))

    cut_ref[...] = jnp.full((SUBLANE, cols), INT_MAX, jnp.int32)

    @pl.when(jnp.max(n_ge) > topk)
    def _():
        n_eq = count(lambda c, key: key == thr_b)
        need = topk - (n_ge - n_eq)

        def idx_step(b, lo):
            step = jnp.int32(1) << (n_idx_bits - 1 - b)
            mid_b = jnp.broadcast_to(lo + step - 1, (kb, cols))
            f = count(lambda c, key: (key == thr_b) & (c * kb + pos0 <= mid_b))
            return jnp.where(f < need, lo + step, lo)

        lo = lax.fori_loop(0, n_idx_bits, idx_step, jnp.zeros((1, cols), jnp.int32))
        cut = jnp.where(n_ge > topk, lo, INT_MAX)
        cut_ref[...] = jnp.broadcast_to(cut, (SUBLANE, cols))

    return thr


UNDERFLOW_GUARD = 1e-30
LOG2_E = math.log2(math.e)


def _dsa_t_kernel(iq_ref, iw_ref, q_ref, mb_ref, ik_ref, k_ref, vt_ref, o_ref,
                  key_ref, cut_ref, m_ref, l_ref, acc_ref, *, qb, kb, topk, n_idx_bits):
    i = pl.program_id(0)
    nk = ((i + 1) * qb + kb - 1) // kb
    rep = N_HEADS // KV_HEADS
    kpos0 = lax.broadcasted_iota(jnp.int32, (kb, qb), 0)
    qpos = i * qb + lax.broadcasted_iota(jnp.int32, (kb, qb), 1)
    iw = iw_ref[0]

    def score_chunk(c, carry):
        ikc = ik_ref[c]
        acc = jnp.zeros((kb, qb), jnp.float32)
        for hp in range(IDX_HEADS // 2):
            s2 = jnp.dot(ikc, iq_ref[0, hp], preferred_element_type=jnp.float32)
            for hh in range(2):
                h = 2 * hp + hh
                acc = acc + iw[h:h + 1, :] * jnp.maximum(s2[:, hh * qb:(hh + 1) * qb], 0.0)
        acc = jnp.where(c * kb + kpos0 <= qpos, acc, -jnp.inf)
        key_ref[c] = _ordered_key(acc)
        return carry

    lax.fori_loop(0, nk, score_chunk, 0)

    thr = _topk_threshold_t(key_ref, cut_ref, nk, kb=kb, cols=qb, topk=topk, n_idx_bits=n_idx_bits)
    thr_b = jnp.broadcast_to(thr, (kb, qb))
    cut_b = jnp.broadcast_to(cut_ref[0:1, :], (kb, qb))

    def mask_bias(c):
        key = key_ref[c]
        kpos = c * kb + kpos0
        sel = (key > thr_b) | ((key == thr_b) & (kpos <= cut_b))
        sel = sel & (kpos <= qpos)
        return jnp.tile(jnp.where(sel, 0.0, MASK_BIAS), (1, rep))

    l_ref[...] = jnp.zeros(l_ref.shape, jnp.float32)
    acc_ref[...] = jnp.zeros(acc_ref.shape, jnp.float32)

    def attend_chunk(c, carry):
        bias = mask_bias(c)
        kc = k_ref[c]
        for g in range(KV_HEADS):
            s = jnp.dot(kc, q_ref[0, g], preferred_element_type=jnp.float32) + (bias - mb_ref[0, g])
            p = jnp.exp2(s)
            l_ref[g] = l_ref[g] + jnp.sum(p, axis=0, keepdims=True)
            acc_ref[g] = acc_ref[g] + jnp.dot(vt_ref[c, g], p.astype(jnp.bfloat16), preferred_element_type=jnp.float32)
        return carry

    lax.fori_loop(0, nk, attend_chunk, 0)

    @pl.when(jnp.min(l_ref[...]) < UNDERFLOW_GUARD)
    def _():
        m_ref[...] = jnp.full(m_ref.shape, MASK_BIAS, jnp.float32)
        l_ref[...] = jnp.zeros(l_ref.shape, jnp.float32)
        acc_ref[...] = jnp.zeros(acc_ref.shape, jnp.float32)

        def attend_chunk_running_max(c, carry):
            bias = mask_bias(c)
            kc = k_ref[c]
            for g in range(KV_HEADS):
                s = jnp.dot(kc, q_ref[0, g], preferred_element_type=jnp.float32) + bias
                m_prev = m_ref[g]
                m_new = jnp.maximum(m_prev, jnp.max(s, axis=0, keepdims=True))
                alpha = jnp.exp2(m_prev - m_new)
                p = jnp.exp2(s - m_new)
                l_ref[g] = alpha * l_ref[g] + jnp.sum(p, axis=0, keepdims=True)
                pv = jnp.dot(vt_ref[c, g], p.astype(jnp.bfloat16), preferred_element_type=jnp.float32)
                acc_ref[g] = acc_ref[g] * alpha + pv
                m_ref[g] = m_new
            return carry

        lax.fori_loop(0, nk, attend_chunk_running_max, 0)

    for g in range(KV_HEADS):
        o_ref[0, g] = acc_ref[g] / l_ref[g]


def _prompt_dsa_attention_t(q, k, v, iq, iw, ik, *, qb=128, kb=512):
    s_len = q.shape[0]
    topk = min(IDX_TOPK_MAX, s_len // 4)
    nqb, nkc = s_len // qb, s_len // kb
    rep = N_HEADS // KV_HEADS
    bf = jnp.bfloat16
    q_s = (q * (HEAD_DIM ** -0.5 * LOG2_E)).astype(bf)
    k_bf = k.astype(bf)
    q_norm = jnp.sqrt(jnp.sum(jnp.square(q_s.astype(jnp.float32)), axis=-1))
    k_norm = jnp.sqrt(jnp.max(jnp.sum(jnp.square(k_bf.astype(jnp.float32)), axis=-1), axis=0))
    bound = q_norm.reshape(nqb, qb, KV_HEADS, rep) * k_norm[None, None, :, None]
    bound = bound.transpose(0, 2, 3, 1).reshape(nqb, KV_HEADS, 1, rep * qb)
    qs = q_s.reshape(nqb, qb, KV_HEADS, rep, HEAD_DIM).transpose(0, 2, 4, 3, 1)
    q_bd = qs[:, :, None, :, :, :] * jnp.eye(KV_HEADS, dtype=bf)[None, :, :, None, None, None]
    q_bd = q_bd.reshape(nqb, KV_HEADS, KV_W, rep * qb)
    iq_t = iq.astype(bf).reshape(nqb, qb, IDX_HEADS // 2, 2, IDX_DIM).transpose(0, 2, 4, 3, 1)
    iq_t = iq_t.reshape(nqb, IDX_HEADS // 2, IDX_DIM, 2 * qb)
    iw_t = iw.reshape(nqb, qb, IDX_HEADS).transpose(0, 2, 1)
    ik_c = ik.astype(bf).reshape(nkc, kb, IDX_DIM)
    k_c = k_bf.reshape(nkc, kb, KV_W)
    v_t = v.astype(bf).reshape(nkc, kb, KV_HEADS, HEAD_DIM).transpose(0, 2, 3, 1)
    n_idx_bits = max(1, (s_len - 1).bit_length())
    body = functools.partial(_dsa_t_kernel, qb=qb, kb=kb, topk=topk, n_idx_bits=n_idx_bits)
    whole = lambda shape: pl.BlockSpec(shape, lambda i: (0,) * len(shape), pipeline_mode=pl.Buffered(1))
    out = pl.pallas_call(
        body,
        grid=(nqb,),
        in_specs=[
            pl.BlockSpec((1, IDX_HEADS // 2, IDX_DIM, 2 * qb), lambda i: (i, 0, 0, 0)),
            pl.BlockSpec((1, IDX_HEADS, qb), lambda i: (i, 0, 0)),
            pl.BlockSpec((1, KV_HEADS, KV_W, rep * qb), lambda i: (i, 0, 0, 0)),
            pl.BlockSpec((1, KV_HEADS, 1, rep * qb), lambda i: (i, 0, 0, 0)),
            whole((nkc, kb, IDX_DIM)),
            whole((nkc, kb, KV_W)),
            whole((nkc, KV_HEADS, HEAD_DIM, kb)),
        ],
        out_specs=pl.BlockSpec((1, KV_HEADS, HEAD_DIM, rep * qb), lambda i: (i, 0, 0, 0)),
        out_shape=jax.ShapeDtypeStruct((nqb, KV_HEADS, HEAD_DIM, rep * qb), jnp.float32),
        scratch_shapes=[
            pltpu.VMEM((nkc, kb, qb), jnp.int32),
            pltpu.VMEM((SUBLANE, qb), jnp.int32),
            pltpu.VMEM((KV_HEADS, 1, rep * qb), jnp.float32),
            pltpu.VMEM((KV_HEADS, 1, rep * qb), jnp.float32),
            pltpu.VMEM((KV_HEADS, HEAD_DIM, rep * qb), jnp.float32),
        ],
        compiler_params=pltpu.CompilerParams(dimension_semantics=("arbitrary",), vmem_limit_bytes=VMEM_LIMIT_BYTES),
        name="prompt_dsa_attention",
    )(iq_t, iw_t, q_bd, bound, ik_c, k_c, v_t)
    out = out.reshape(nqb, KV_HEADS, HEAD_DIM, rep, qb).transpose(0, 4, 1, 3, 2)
    return out.reshape(s_len, N_HEADS * HEAD_DIM)


def _sample_index_kernel(pt_ref, iq_ref, iw_ref, iknew_ref, *rest, pp, n_pages, topk, n_idx_bits):
    page_refs, o_ref, (key_ref, wide_ref, wb_ref, cut_ref) = rest[:pp], rest[pp], rest[pp + 1:]
    pg = pl.program_id(1)
    t_new = o_ref.shape[1]
    step_keys = pp * PAGE_SIZE
    n_keys = (n_pages + 1) * PAGE_SIZE

    @pl.when(pg == 0)
    def _():
        wb_ref[...] = jnp.broadcast_to(iw_ref[0], wb_ref.shape)

    iq = iq_ref[0]

    def scores(ik_t):
        s = jnp.dot(iq, ik_t.astype(jnp.bfloat16), preferred_element_type=jnp.float32)
        s = jnp.tile(wb_ref[...], (1, ik_t.shape[1] // LANE)) * jnp.maximum(s, 0.0)
        acc = jnp.zeros((t_new, ik_t.shape[1]), jnp.float32)
        for h in range(IDX_HEADS):
            acc = acc + s[h * t_new:(h + 1) * t_new, :]
        return acc

    key_ref[pg] = _ordered_key(scores(jnp.concatenate([r[0] for r in page_refs], axis=1)))

    @pl.when(pg == pl.num_programs(1) - 1)
    def _():
        for c in range(n_pages // pp):
            wide_ref[0, :, c * step_keys:(c + 1) * step_keys] = key_ref[c]
        row = lax.broadcasted_iota(jnp.int32, (t_new, PAGE_SIZE), 0)
        lane = lax.broadcasted_iota(jnp.int32, (t_new, PAGE_SIZE), 1)
        s_new = jnp.where(lane <= row, scores(iknew_ref[0]), -jnp.inf)
        wide_ref[0, :, n_pages * PAGE_SIZE:] = _ordered_key(s_new)
        thr = _topk_threshold(wide_ref, cut_ref, 1, rows=t_new, kb=n_keys, topk=topk, n_idx_bits=n_idx_bits)
        key = wide_ref[0]
        kpos = lax.broadcasted_iota(jnp.int32, (t_new, n_keys), 1)
        qpos = n_pages * PAGE_SIZE + lax.broadcasted_iota(jnp.int32, (t_new, n_keys), 0)
        thr_b = jnp.broadcast_to(thr, (t_new, n_keys))
        cut_b = jnp.tile(cut_ref[...], (1, n_pages + 1))
        sel = (key > thr_b) | ((key == thr_b) & (kpos <= cut_b))
        o_ref[0] = jnp.where(sel & (kpos <= qpos), 0.0, MASK_BIAS)


def _sample_attend_kernel(pt_ref, q_ref, bias_ref, biasnew_ref, knew_ref, vnew_ref, *rest, pp, n_pages):
    k_refs, v_refs, o_ref, (m_ref, l_ref, acc_ref) = rest[:pp], rest[pp:2 * pp], rest[2 * pp], rest[2 * pp + 1:]
    pg = pl.program_id(1)
    n_rows = q_ref.shape[1]
    t_new = bias_ref.shape[1]

    @pl.when(pg == 0)
    def _():
        m_ref[...] = jnp.full(m_ref.shape, MASK_BIAS, jnp.float32)
        l_ref[...] = jnp.zeros(l_ref.shape, jnp.float32)
        acc_ref[...] = jnp.zeros(acc_ref.shape, jnp.float32)

    q = q_ref[0]

    def step(bias, k_t, v_t):
        n_keys = k_t.shape[1]
        s = jnp.dot(q, k_t.astype(jnp.bfloat16), preferred_element_type=jnp.float32)
        s = s + jnp.tile(bias, (n_rows // t_new, 1))
        m_prev = m_ref[...]
        m_new = jnp.maximum(m_prev, jnp.max(s, axis=-1, keepdims=True))
        alpha = jnp.exp(m_prev - m_new)
        p = jnp.exp(s - jnp.tile(m_new, (1, n_keys // LANE)))
        l_ref[...] = alpha * l_ref[...] + jnp.sum(p, axis=-1, keepdims=True)
        pv = lax.dot_general(p.astype(jnp.bfloat16), v_t.astype(jnp.bfloat16), (((1,), (1,)), ((), ())),
                             preferred_element_type=jnp.float32)
        acc_ref[...] = acc_ref[...] * jnp.tile(alpha, (1, KV_W // LANE)) + pv
        m_ref[...] = m_new

    step(bias_ref[0], jnp.concatenate([r[0] for r in k_refs], axis=1), jnp.concatenate([r[0] for r in v_refs], axis=1))

    @pl.when(pg == pl.num_programs(1) - 1)
    def _():
        step(biasnew_ref[0], knew_ref[0], vnew_ref[0])
        acc = acc_ref[...]
        row_head = lax.broadcasted_iota(jnp.int32, acc.shape, 0) // (n_rows // KV_HEADS)
        col_head = lax.broadcasted_iota(jnp.int32, acc.shape, 1) // HEAD_DIM
        own = jnp.where(row_head == col_head, acc, 0.0)
        folded = own[:, :LANE] + own[:, LANE:]
        folded = folded + pltpu.roll(folded, HEAD_DIM, axis=1)
        o_ref[0] = folded / l_ref[...]


def _sample_dsa_attention(q, k, v, iq, iw, ik, cache_k, cache_v, cache_idx_k, page_table, *, pp_index=64, pp=32):
    b, t = q.shape[:2]
    n_pages = page_table.shape[1]
    n_pool = cache_k.shape[0]
    past = n_pages * PAGE_SIZE
    topk = min(IDX_TOPK_MAX, (past + t) // 4)
    n_idx_bits = max(1, (past + PAGE_SIZE - 1).bit_length())
    pp_index, pp = min(pp_index, n_pages), min(pp, n_pages)
    assert n_pages % pp == 0 and n_pages % pp_index == 0 and t == SUBLANE and KV_W == 2 * LANE
    npg = n_pages // pp
    n_keys = past + PAGE_SIZE
    bf = jnp.bfloat16
    rep = N_HEADS // KV_HEADS
    n_rows = N_HEADS * t

    iq_r = iq.astype(bf).transpose(0, 2, 1, 3).reshape(b, IDX_HEADS * t, IDX_DIM)
    iw_r = iw.transpose(0, 2, 1).reshape(b, IDX_HEADS * t, 1)
    new_page = lambda a: jnp.pad(jnp.swapaxes(a, 1, 2), ((0, 0), (0, 0), (0, PAGE_SIZE - t)))
    pages_t = lambda c: jnp.swapaxes(c.reshape(n_pool, PAGE_SIZE, -1), 1, 2)
    ik_new = new_page(ik)
    k_new = new_page(k.reshape(b, t, KV_W))
    v_new = new_page(v.reshape(b, t, KV_W))
    qs = (q * HEAD_DIM ** -0.5).astype(bf).reshape(b, t, KV_HEADS, rep, HEAD_DIM).transpose(0, 2, 3, 1, 4)
    q_bd = (qs[:, :, :, :, None, :] * jnp.eye(KV_HEADS, dtype=bf)[None, :, None, None, :, None])
    q_bd = q_bd.reshape(b, n_rows, KV_W)

    page_map = lambda per_step, j: (lambda bi, pg, pt: (pt[bi, pg * per_step + j], 0, 0))
    per_seq = lambda shape: pl.BlockSpec((1,) + shape, lambda bi, pg, pt: (bi,) + (0,) * len(shape))
    params = pltpu.CompilerParams(dimension_semantics=("arbitrary", "arbitrary"), vmem_limit_bytes=VMEM_LIMIT_BYTES)

    bias = pl.pallas_call(
        functools.partial(_sample_index_kernel, pp=pp_index, n_pages=n_pages, topk=topk, n_idx_bits=n_idx_bits),
        grid_spec=pltpu.PrefetchScalarGridSpec(
            num_scalar_prefetch=1, grid=(b, n_pages // pp_index),
            in_specs=[per_seq((IDX_HEADS * t, IDX_DIM)), per_seq((IDX_HEADS * t, 1)), per_seq((IDX_DIM, PAGE_SIZE))]
            + [pl.BlockSpec((1, IDX_DIM, PAGE_SIZE), page_map(pp_index, j)) for j in range(pp_index)],
            out_specs=per_seq((t, n_keys)),
            scratch_shapes=[pltpu.VMEM((n_pages // pp_index, t, pp_index * PAGE_SIZE), jnp.int32),
                            pltpu.VMEM((1, t, n_keys), jnp.int32),
                            pltpu.VMEM((IDX_HEADS * t, PAGE_SIZE), jnp.float32),
                            pltpu.VMEM((t, LANE), jnp.int32)]),
        out_shape=jax.ShapeDtypeStruct((b, t, n_keys), jnp.float32),
        compiler_params=params,
        name="sample_index",
    )(page_table, iq_r, iw_r, ik_new, *([pages_t(cache_idx_k)] * pp_index))

    ck = pages_t(cache_k)
    cv = pages_t(cache_v)
    out = pl.pallas_call(
        functools.partial(_sample_attend_kernel, pp=pp, n_pages=n_pages),
        grid_spec=pltpu.PrefetchScalarGridSpec(
            num_scalar_prefetch=1, grid=(b, npg),
            in_specs=[per_seq((n_rows, KV_W)),
                      pl.BlockSpec((1, t, pp * PAGE_SIZE), lambda bi, pg, pt: (bi, 0, pg)),
                      pl.BlockSpec((1, t, PAGE_SIZE), lambda bi, pg, pt: (bi, 0, n_pages)),
                      per_seq((KV_W, PAGE_SIZE)), per_seq((KV_W, PAGE_SIZE))]
            + [pl.BlockSpec((1, KV_W, PAGE_SIZE), page_map(pp, j)) for j in range(pp)] * 2,
            out_specs=per_seq((n_rows, LANE)),
            scratch_shapes=[pltpu.VMEM((n_rows, LANE), jnp.float32), pltpu.VMEM((n_rows, LANE), jnp.float32),
                            pltpu.VMEM((n_rows, KV_W), jnp.float32)]),
        out_shape=jax.ShapeDtypeStruct((b, n_rows, LANE), jnp.float32),
        compiler_params=params,
        name="sample_attend",
    )(page_table, q_bd, bias, bias, k_new, v_new, *([ck] * pp), *([cv] * pp))
    out = out[:, :, :HEAD_DIM].reshape(b, N_HEADS, t, HEAD_DIM).transpose(0, 2, 1, 3)
    return out.reshape(b, t, N_HEADS * HEAD_DIM)


GELU_C = math.sqrt(2.0 / math.pi)


def _rglru_kernel(x_ref, xg_ref, prev_ref, h0_ref, cw_ref, cb_ref, wa_ref, ba_ref, wi_ref, bi_ref, lam_ref,
                  o_ref, hl_ref, conv_ref, a_ref, u_ref, xprev_ref, h_ref, *, rows, per_group_state):
    i = pl.program_id(0)
    n_groups = rows // SUBLANE
    d_rnn = x_ref.shape[1]
    r8 = lax.broadcasted_iota(jnp.int32, (SUBLANE, d_rnn), 0)
    grp = lambda g: pl.ds(pl.multiple_of(g * SUBLANE, SUBLANE), SUBLANE)

    if not per_group_state:
        @pl.when(i == 0)
        def _():
            xprev_ref[...] = jnp.zeros_like(xprev_ref)
            h_ref[...] = jnp.zeros_like(h_ref)

    cw = cw_ref[...]
    cb = cb_ref[...]

    def conv_group(g, prev):
        x8 = x_ref[grp(g), :]
        if per_group_state:
            prev = prev_ref[grp(g), :]
        out = cb + x8 * cw[CONV_W - 1:CONV_W, :]
        for d in range(1, CONV_W):
            shifted = jnp.where(r8 < d, pltpu.roll(prev, d, axis=0), pltpu.roll(x8, d, axis=0))
            out = out + shifted * cw[CONV_W - 1 - d:CONV_W - d, :]
        conv_ref[grp(g), :] = out
        return x8

    zeros8 = jnp.zeros((SUBLANE, d_rnn), jnp.float32)
    xlast = lax.fori_loop(0, n_groups, conv_group, zeros8 if per_group_state else xprev_ref[...])
    if not per_group_state:
        xprev_ref[...] = xlast

    xc = conv_ref[...]
    xb = xc.astype(jnp.bfloat16)
    r_parts, i_parts = [], []
    for n in range(RG_BLOCKS):
        xn = xb[:, n * RG_BLOCK_W:(n + 1) * RG_BLOCK_W]
        r_parts.append(jnp.dot(xn, wa_ref[n], preferred_element_type=jnp.float32))
        i_parts.append(jnp.dot(xn, wi_ref[n], preferred_element_type=jnp.float32))
    r = jax.nn.sigmoid(jnp.concatenate(r_parts, axis=-1) + ba_ref[...])
    ig = jax.nn.sigmoid(jnp.concatenate(i_parts, axis=-1) + bi_ref[...])
    lam = lam_ref[...]
    log_sig_lam = -(jnp.maximum(-lam, 0.0) + jnp.log1p(jnp.exp(-jnp.abs(lam))))
    log_a = RG_C * r * log_sig_lam
    a = jnp.exp(log_a)
    a_ref[...] = a
    u_ref[...] = jnp.sqrt(-jnp.tanh(log_a) * (a * a + 1.0)) * (ig * xc)

    def scan_group(g, hprev):
        a = a_ref[grp(g), :]
        u = u_ref[grp(g), :]
        if per_group_state:
            hprev = jnp.broadcast_to(h0_ref[pl.ds(g, 1), :], (SUBLANE, d_rnn))
        for d in (1, 2, 4):
            u = jnp.where(r8 >= d, a * pltpu.roll(u, d, axis=0) + u, u)
            a = jnp.where(r8 >= d, a * pltpu.roll(a, d, axis=0), a)
        h = a * hprev + u
        u_ref[grp(g), :] = h
        hlast = jnp.broadcast_to(h[SUBLANE - 1:SUBLANE, :], (SUBLANE, d_rnn))
        if per_group_state:
            hl_ref[pl.ds(g, 1), :] = h[SUBLANE - 1:SUBLANE, :]
        return hlast

    hlast = lax.fori_loop(0, n_groups, scan_group, zeros8 if per_group_state else h_ref[...])
    if not per_group_state:
        h_ref[...] = hlast
        hl_ref[...] = hlast

    xg = xg_ref[...]
    gelu = 0.5 * xg * (1.0 + jnp.tanh(GELU_C * (xg + 0.044715 * (xg * xg * xg))))
    o_ref[...] = u_ref[...] * gelu


def _rglru(x, xg, prev, h0, conv_w, conv_b, w_a, b_a, w_i, b_i, lam, *, per_group_state, tile_rows):
    n_rows, d = x.shape
    row = lambda a: a.reshape(1, d)
    const = lambda shape: pl.BlockSpec(shape, lambda i: (0,) * len(shape))
    n_hl = n_rows // SUBLANE if per_group_state else SUBLANE
    body = functools.partial(_rglru_kernel, rows=tile_rows, per_group_state=per_group_state)
    tile = pl.BlockSpec((tile_rows, d), lambda i: (i, 0))
    if per_group_state:
        assert tile_rows == n_rows
        prev_spec, h0_spec = const(prev.shape), const(h0.shape)
    else:
        prev_spec, h0_spec = const(prev.shape), const(h0.shape)
    return pl.pallas_call(
        body,
        grid=(n_rows // tile_rows,),
        in_specs=[tile, tile, prev_spec, h0_spec, const((CONV_W, d)), const((1, d)),
                  const(w_a.shape), const((1, d)), const(w_i.shape), const((1, d)), const((1, d))],
        out_specs=[tile, const((n_hl, d))],
        out_shape=[jax.ShapeDtypeStruct((n_rows, d), jnp.float32), jax.ShapeDtypeStruct((n_hl, d), jnp.float32)],
        scratch_shapes=[pltpu.VMEM((tile_rows, d), jnp.float32), pltpu.VMEM((tile_rows, d), jnp.float32),
                        pltpu.VMEM((tile_rows, d), jnp.float32), pltpu.VMEM((SUBLANE, d), jnp.float32),
                        pltpu.VMEM((SUBLANE, d), jnp.float32)],
        compiler_params=pltpu.CompilerParams(dimension_semantics=("arbitrary",), vmem_limit_bytes=VMEM_LIMIT_BYTES),
        name="rglru_sample" if per_group_state else "rglru_prompt",
    )(x, xg, prev, h0, conv_w, row(conv_b), w_a.astype(jnp.bfloat16), row(b_a), w_i.astype(jnp.bfloat16), row(b_i), row(lam))


def _rglru_prompt(x, xg, conv_w, conv_b, w_a, b_a, w_i, b_i, lam, tile_rows=512):
    dummy = jnp.zeros((SUBLANE, x.shape[1]), jnp.float32)
    rnn, hl = _rglru(x, xg, dummy, dummy, conv_w, conv_b, w_a, b_a, w_i, b_i, lam,
                     per_group_state=False, tile_rows=tile_rows)
    return rnn, hl[0]


def _rglru_sample(x, xg, state_conv, state_h, conv_w, conv_b, w_a, b_a, w_i, b_i, lam):
    b, t, d = x.shape
    assert t == SUBLANE
    prev = jnp.concatenate([jnp.zeros((b, SUBLANE - (CONV_W - 1), d), jnp.float32), state_conv], axis=1)
    rnn, hl = _rglru(x.reshape(b * t, d), xg.reshape(b * t, d), prev.reshape(b * t, d), state_h,
                     conv_w, conv_b, w_a, b_a, w_i, b_i, lam, per_group_state=True, tile_rows=b * t)
    return rnn.reshape(b, t, d), hl


def _rms_norm(x, g):
    xf = x.astype(jnp.float32)
    y = xf * lax.rsqrt(jnp.mean(xf * xf, axis=-1, keepdims=True) + EPS)
    return (y * g.astype(jnp.float32)).astype(x.dtype)


def _rope(x, pos):
    half = x.shape[-1] // 2
    inv_freq = ROPE_THETA ** (-jnp.arange(half, dtype=jnp.float32) / half)
    ang = pos.astype(jnp.float32)[:, None] * inv_freq[None, :]
    cos = jnp.cos(ang)[:, None, :]
    sin = jnp.sin(ang)[:, None, :]
    xf = x.astype(jnp.float32)
    x1, x2 = xf[..., :half], xf[..., half:]
    return jnp.concatenate([x1 * cos - x2 * sin, x2 * cos + x1 * sin], axis=-1).astype(x.dtype)


def _index_scores(iq, iw, ik):
    s = jnp.einsum('bthd,bld->bthl', iq, ik, preferred_element_type=jnp.float32)
    return jnp.einsum('bthl,bth->btl', jax.nn.relu(s), iw.astype(jnp.float32))


def _sparse_attend(q, k_sel, v_sel, valid):
    B, T = q.shape[:2]
    qg = q.reshape(B, T, KV_HEADS, N_HEADS // KV_HEADS, HEAD_DIM)
    logits = jnp.einsum('btkgd,btskd->btkgs', qg, k_sel, preferred_element_type=jnp.float32) * (HEAD_DIM ** -0.5)
    logits = jnp.where(valid[:, :, None, None, :], logits, -jnp.inf)
    p = jax.nn.softmax(logits, axis=-1)
    out = jnp.einsum('btkgs,btskd->btkgd', p.astype(v_sel.dtype), v_sel)
    return out.reshape(B, T, Q_W)


def _gather_rows(rows, idx):
    return jax.vmap(lambda r, i: r[i])(rows, idx)


def _prompt_sparse_attention(q, k, v, iq, iw, ik):
    B, S = q.shape[:2]
    topk = min(IDX_TOPK_MAX, S // 4)
    n_blocks = S // Q_BLOCK
    key_pos = jnp.arange(S, dtype=jnp.int32)

    def block(args):
        qb, iqb, iwb, start = args
        qpos = start + jnp.arange(Q_BLOCK, dtype=jnp.int32)
        sc = _index_scores(iqb, iwb, ik)
        sc = jnp.where((key_pos[None, :] <= qpos[:, None])[None], sc, -jnp.inf)
        _, sel = lax.top_k(sc, topk)
        valid = sel <= qpos[None, :, None]
        return _sparse_attend(qb, _gather_rows(k, sel), _gather_rows(v, sel), valid)

    to_blocks = lambda a: jnp.moveaxis(a.reshape(B, n_blocks, Q_BLOCK, *a.shape[2:]), 1, 0)
    starts = jnp.arange(n_blocks, dtype=jnp.int32) * Q_BLOCK
    out = lax.map(block, (to_blocks(q), to_blocks(iq), to_blocks(iw), starts))
    return jnp.moveaxis(out, 0, 1).reshape(B, S, Q_W)


def _sample_sparse_attention(q, k, v, iq, iw, ik, cache_k, cache_v, cache_idx_k, page_table):
    DB, T = q.shape[:2]
    past = page_table.shape[1] * PAGE_SIZE
    L = past + T
    topk = min(IDX_TOPK_MAX, L // 4)
    ik_past = cache_idx_k[page_table].reshape(DB, past, IDX_DIM)
    ik_all = jnp.concatenate([ik_past, ik.astype(ik_past.dtype)], axis=1)
    qpos = past + jnp.arange(T, dtype=jnp.int32)
    sc = _index_scores(iq, iw, ik_all)
    sc = jnp.where((jnp.arange(L, dtype=jnp.int32)[None, :] <= qpos[:, None])[None], sc, -jnp.inf)
    _, sel = lax.top_k(sc, topk)
    valid = sel <= qpos[None, :, None]
    in_past = sel < past
    sel_p = jnp.minimum(sel, past - 1)
    page = _gather_rows(page_table, sel_p // PAGE_SIZE)
    phys = page * PAGE_SIZE + sel_p % PAGE_SIZE
    new_idx = jnp.clip(sel - past, 0, T - 1)
    k_pool = cache_k.reshape(-1, KV_HEADS, HEAD_DIM)
    v_pool = cache_v.reshape(-1, KV_HEADS, HEAD_DIM)
    m = in_past[..., None, None]
    k_sel = jnp.where(m, k_pool[phys], _gather_rows(k, new_idx).astype(k_pool.dtype))
    v_sel = jnp.where(m, v_pool[phys], _gather_rows(v, new_idx).astype(v_pool.dtype))
    return _sparse_attend(q, k_sel, v_sel, valid)


def _causal_conv(xr, conv_state, w, b):
    T = xr.shape[1]
    xp = jnp.concatenate([conv_state.astype(xr.dtype), xr], axis=1)
    out = b + sum(xp[:, j:j + T] * w[j] for j in range(CONV_W))
    return out, xp[:, -(CONV_W - 1):]


def _block_diag(x, w, b):
    B, T, _ = x.shape
    y = jnp.einsum('btnc,ncd->btnd', x.reshape(B, T, RG_BLOCKS, RG_BLOCK_W), w)
    return y.reshape(B, T, D_RNN) + b


def _rg_lru(x, h0, w_a, b_a, w_i, b_i, lam):
    xf = x.astype(jnp.float32)
    r = jax.nn.sigmoid(_block_diag(x, w_a, b_a).astype(jnp.float32))
    i = jax.nn.sigmoid(_block_diag(x, w_i, b_i).astype(jnp.float32))
    log_a = RG_C * r * jax.nn.log_sigmoid(lam.astype(jnp.float32))
    a = jnp.exp(log_a)
    u = jnp.sqrt(-jnp.expm1(2.0 * log_a)) * (i * xf)

    def step(h, au):
        h = au[0] * h + au[1]
        return h, h

    h_last, hs = lax.scan(step, h0.astype(jnp.float32), (jnp.moveaxis(a, 1, 0), jnp.moveaxis(u, 1, 0)))
    return jnp.moveaxis(hs, 0, 1).astype(x.dtype), h_last.astype(x.dtype)


def _rms_norm_rows(x, g):
    return x * lax.rsqrt(jnp.mean(x * x, axis=-1, keepdims=True) + EPS) * g


def _router_kernel(x_ref, g_ref, wr_ref, br_ref, tri_ref, e_ref, gate_ref, rank_ref, cnt_ref, run_ref):
    i = pl.program_id(0)
    tm = x_ref.shape[0]

    @pl.when(i == 0)
    def _():
        run_ref[...] = jnp.zeros_like(run_ref)

    xn_bf = _rms_norm_rows(x_ref[...], g_ref[...]).astype(jnp.bfloat16)
    logits = lax.dot_general(wr_ref[...], xn_bf, (((1,), (1,)), ((), ())), preferred_element_type=jnp.float32)
    logits = logits + br_ref[...]

    expert = lax.broadcasted_iota(jnp.int32, (N_EXPERTS, tm), 0)
    member = jnp.zeros((N_EXPERTS, tm), jnp.float32)
    picked, values = [], []
    for k in range(TOP_K):
        mx = jnp.max(logits, axis=0, keepdims=True)
        idx = jnp.min(jnp.where(logits == mx, expert, N_EXPERTS), axis=0, keepdims=True)
        hit = expert == idx
        member = jnp.where(hit, 1.0, member)
        logits = jnp.where(hit, -jnp.inf, logits)
        picked.append(idx)
        values.append(mx)
        e_ref[k:k + 1, :] = idx

    ex = [jnp.exp(v - values[0]) for v in values]
    denom = ex[0] + ex[1] + ex[2] + ex[3]
    for k in range(TOP_K):
        gate_ref[k:k + 1, :] = ex[k] / denom

    before = jnp.dot(member.astype(jnp.bfloat16), tri_ref[...], preferred_element_type=jnp.float32)
    before = before + jnp.tile(run_ref[...], (1, tm // LANE))
    for k in range(TOP_K):
        r = jnp.sum(jnp.where(expert == picked[k], before, 0.0), axis=0, keepdims=True)
        rank_ref[k:k + 1, :] = r.astype(jnp.int32)
    run = run_ref[...] + jnp.sum(member, axis=1, keepdims=True)
    run_ref[...] = run
    cnt_ref[...] = run.astype(jnp.int32)


def _route(x2d, norm_g, w_router, b_router, tm=512):
    n, d = x2d.shape
    tri = (jnp.arange(tm)[:, None] < jnp.arange(tm)[None, :]).astype(jnp.bfloat16)
    const = lambda shape: pl.BlockSpec(shape, lambda i: (0,) * len(shape))
    rows4 = pl.BlockSpec((TOP_K, tm), lambda i: (0, i))
    top_e, gates, rank, cnt = pl.pallas_call(
        _router_kernel,
        grid=(n // tm,),
        in_specs=[pl.BlockSpec((tm, d), lambda i: (i, 0)), const((1, d)), const((N_EXPERTS, d)),
                  const((N_EXPERTS, 1)), const((tm, tm))],
        out_specs=[rows4, rows4, rows4, const((N_EXPERTS, LANE))],
        out_shape=[jax.ShapeDtypeStruct((TOP_K, n), jnp.int32),
                   jax.ShapeDtypeStruct((TOP_K, n), jnp.float32), jax.ShapeDtypeStruct((TOP_K, n), jnp.int32),
                   jax.ShapeDtypeStruct((N_EXPERTS, LANE), jnp.int32)],
        scratch_shapes=[pltpu.VMEM((N_EXPERTS, LANE), jnp.float32)],
        compiler_params=pltpu.CompilerParams(dimension_semantics=("arbitrary",)),
        name="moe_router",
    )(x2d, norm_g.reshape(1, d), w_router.T.astype(jnp.bfloat16), b_router.reshape(N_EXPERTS, 1), tri)
    return top_e, gates, rank, cnt[:, 0]


def _expert_kernel(be_ref, used_ref, xs_ref, g_ref, wgu_ref, bgu_ref, wd_ref, bd_ref, o_ref, *, blk):
    i = pl.program_id(0)

    @pl.when(i * blk < used_ref[0])
    def _():
        xn = _rms_norm_rows(xs_ref[...], g_ref[...]).astype(jnp.bfloat16)
        gu = jnp.dot(xn, wgu_ref[0], preferred_element_type=jnp.float32) + bgu_ref[0]
        g = jnp.minimum(gu[:, :D_FF], SWIGLU_LIMIT)
        u = jnp.clip(gu[:, D_FF:], -SWIGLU_LIMIT, SWIGLU_LIMIT)
        act = (u + 1.0) * (g * jax.nn.sigmoid(SWIGLU_ALPHA * g))
        o_ref[...] = jnp.dot(act.astype(jnp.bfloat16), wd_ref[0], preferred_element_type=jnp.float32) + bd_ref[0]

    @pl.when(i * blk >= used_ref[0])
    def _():
        o_ref[...] = jnp.zeros_like(o_ref)


def _experts(xs, norm_g, blk_exp, n_used, w_gate_up, b_gate_up, w_down, b_down, *, blk):
    n_slots, d = xs.shape
    n_blocks = n_slots // blk
    bf = jnp.bfloat16
    return pl.pallas_call(
        functools.partial(_expert_kernel, blk=blk),
        grid_spec=pltpu.PrefetchScalarGridSpec(
            num_scalar_prefetch=2, grid=(n_blocks,),
            in_specs=[pl.BlockSpec((blk, d), lambda i, be, nu: (i, 0)),
                      pl.BlockSpec((1, d), lambda i, be, nu: (0, 0)),
                      pl.BlockSpec((1, d, 2 * D_FF), lambda i, be, nu: (be[i], 0, 0)),
                      pl.BlockSpec((1, 1, 2 * D_FF), lambda i, be, nu: (be[i], 0, 0)),
                      pl.BlockSpec((1, D_FF, d), lambda i, be, nu: (be[i], 0, 0)),
                      pl.BlockSpec((1, 1, d), lambda i, be, nu: (be[i], 0, 0))],
            out_specs=pl.BlockSpec((blk, d), lambda i, be, nu: (i, 0))),
        out_shape=jax.ShapeDtypeStruct((n_slots, d), jnp.float32),
        compiler_params=pltpu.CompilerParams(dimension_semantics=("arbitrary",), vmem_limit_bytes=VMEM_LIMIT_BYTES),
        name="moe_experts",
    )(blk_exp, n_used, xs, norm_g.reshape(1, d), w_gate_up.astype(bf), b_gate_up.reshape(N_EXPERTS, 1, 2 * D_FF),
      w_down.astype(bf), b_down.reshape(N_EXPERTS, 1, d))


def _row_copy(src_hbm, src_row, dst_ref, dst_row, sem):
    return pltpu.make_async_copy(src_hbm.at[pl.ds(src_row, 1)], dst_ref.at[pl.ds(dst_row, 1)], sem)


def _dispatch_kernel(dest_ref, x_ref, xs_in_hbm, xs_hbm, sem, *, tm):
    del xs_in_hbm

    def issue(t, carry):
        for k in range(TOP_K):
            _row_copy(x_ref, t, xs_hbm, dest_ref[k, t], sem).start()
        return carry

    lax.fori_loop(0, tm, issue, 0)

    def drain(t, carry):
        for k in range(TOP_K):
            _row_copy(x_ref, 0, xs_hbm, 0, sem).wait()
        return carry

    lax.fori_loop(0, tm, drain, 0)


def _dispatch(x2d, dest, n_slots, tm=128):
    n, d = x2d.shape
    return pl.pallas_call(
        functools.partial(_dispatch_kernel, tm=tm),
        grid=(n // tm,),
        in_specs=[pl.BlockSpec((TOP_K, tm), lambda i: (0, i), memory_space=pltpu.SMEM),
                  pl.BlockSpec((tm, d), lambda i: (i, 0)), pl.BlockSpec(memory_space=pl.ANY)],
        out_specs=pl.BlockSpec(memory_space=pl.ANY),
        out_shape=jax.ShapeDtypeStruct((n_slots, d), x2d.dtype),
        scratch_shapes=[pltpu.SemaphoreType.DMA(())],
        input_output_aliases={2: 0},
        compiler_params=pltpu.CompilerParams(dimension_semantics=("arbitrary",)),
        name="moe_dispatch",
    )(dest, x2d, jnp.zeros((n_slots, d), x2d.dtype))


def _combine_kernel(dest_ref, x_ref, gate_ref, ys_hbm, o_ref, buf_ref, sem, *, tm):
    def issue(t, carry):
        for k in range(TOP_K):
            _row_copy(ys_hbm, dest_ref[k, t], buf_ref.at[k], t, sem).start()
        return carry

    lax.fori_loop(0, tm, issue, 0)

    def drain(t, carry):
        for k in range(TOP_K):
            _row_copy(ys_hbm, 0, buf_ref.at[k], 0, sem).wait()
        return carry

    lax.fori_loop(0, tm, drain, 0)
    gate = gate_ref[...]
    acc = x_ref[...]
    for k in range(TOP_K):
        acc = acc + gate[:, k:k + 1] * buf_ref[k]
    o_ref[...] = acc


def _combine(x2d, ys, dest, gates_t, tm=128):
    n, d = x2d.shape
    return pl.pallas_call(
        functools.partial(_combine_kernel, tm=tm),
        grid=(n // tm,),
        in_specs=[pl.BlockSpec((TOP_K, tm), lambda i: (0, i), memory_space=pltpu.SMEM),
                  pl.BlockSpec((tm, d), lambda i: (i, 0)), pl.BlockSpec((tm, TOP_K), lambda i: (i, 0)),
                  pl.BlockSpec(memory_space=pl.ANY)],
        out_specs=pl.BlockSpec((tm, d), lambda i: (i, 0)),
        out_shape=jax.ShapeDtypeStruct((n, d), jnp.float32),
        scratch_shapes=[pltpu.VMEM((TOP_K, tm, d), jnp.float32), pltpu.SemaphoreType.DMA(())],
        compiler_params=pltpu.CompilerParams(dimension_semantics=("arbitrary",)),
        name="moe_combine",
    )(dest, x2d, gates_t, ys)


MOE_ROWS = 512


def _moe(x2d, norm_g, w_router, b_router, w_gate_up, b_gate_up, w_down, b_down, *, blk=MOE_ROWS, tm=512):
    n, d = x2d.shape
    top_e, gates, rank, counts = _route(x2d, norm_g, w_router, b_router, tm=tm)
    padded = (counts + blk - 1) // blk * blk
    pad_end = jnp.cumsum(padded)
    pad_start = pad_end - padded
    dest = pad_start[top_e] + rank
    n_blocks = -(-(n * TOP_K + N_EXPERTS * (blk - 1)) // blk)
    n_slots = n_blocks * blk
    blk_exp = jnp.minimum(jnp.searchsorted(pad_end, jnp.arange(n_blocks, dtype=jnp.int32) * blk, side='right'),
                          N_EXPERTS - 1).astype(jnp.int32)
    n_used = pad_end[-1:].astype(jnp.int32)
    xs = _dispatch(x2d, dest, n_slots)
    ys = _experts(xs, norm_g, blk_exp, n_used, w_gate_up, b_gate_up, w_down, b_down, blk=blk)
    return _combine(x2d, ys, dest, gates.T)


def _trunk_layer(x, pos, attend, conv_state, h0, norm1_g, w_in, q_norm_g, k_norm_g, conv_w, conv_b,
                 rg_w_a, rg_b_a, rg_w_i, rg_b_i, rg_lambda, w_attn_out, w_rnn_out, w_out, norm2_g,
                 w_router, b_router, w_gate_up, b_gate_up, w_down, b_down):
    B, T, _ = x.shape
    n_pad = (-D_IN) % 896
    w_pad = jnp.pad(w_in, ((0, 0), (0, n_pad))).astype(jnp.bfloat16)
    z = _norm_proj(x.reshape(B * T, D_MODEL), norm1_g, w_pad)[:, :D_IN].reshape(B, T, D_IN)
    q, k, v, iq, ik, iw, xr, xg, ga, gr = jnp.split(z, np.cumsum(IN_WIDTHS)[:-1].tolist(), axis=-1)
    q = _rope(_rms_norm(q.reshape(B, T, N_HEADS, HEAD_DIM), q_norm_g), pos)
    k = _rope(_rms_norm(k.reshape(B, T, KV_HEADS, HEAD_DIM), k_norm_g), pos)
    v = v.reshape(B, T, KV_HEADS, HEAD_DIM)
    iq = _rope(iq.reshape(B, T, IDX_HEADS, IDX_DIM), pos)
    ik = _rope(ik[:, :, None, :], pos)[:, :, 0]
    iw = iw * (IDX_HEADS ** -0.5 * IDX_DIM ** -0.5)
    rg = (conv_w, conv_b, rg_w_a, rg_b_a, rg_w_i, rg_b_i, rg_lambda)
    if attend is None:
        assert B == 1
        attn = _prompt_dsa_attention_t(q[0], k[0], v[0], iq[0], iw[0], ik[0])[None]
        rnn, h_new = _rglru_prompt(xr[0], xg[0], *rg)
        rnn, h_new = rnn[None], h_new[None]
        conv_new = xr[:, -(CONV_W - 1):]
    else:
        attn = attend(q, k, v, iq, iw, ik)
        rnn, h_new = _rglru_sample(xr, xg, conv_state, h0, *rg)
        conv_new = jnp.concatenate([conv_state, xr], axis=1)[:, -(CONV_W - 1):]
    merged = jax.nn.sigmoid(ga) * (attn @ w_attn_out) + jax.nn.sigmoid(gr) * (rnn @ w_rnn_out)
    x = x + merged @ w_out
    return x, (k, v, ik, conv_new, h_new)


def kernel(x_prompt, x_sample, cache_k, cache_v, cache_idx_k, state_conv, state_h, page_table,
           norm1_g, w_in, q_norm_g, k_norm_g, conv_w, conv_b, rg_w_a, rg_b_a, rg_w_i, rg_b_i, rg_lambda,
           w_attn_out, w_rnn_out, w_out, norm2_g, w_router, b_router, w_gate_up, b_gate_up, w_down, b_down):
    weights = (norm1_g, w_in, q_norm_g, k_norm_g, conv_w, conv_b, rg_w_a, rg_b_a, rg_w_i, rg_b_i, rg_lambda,
               w_attn_out, w_rnn_out, w_out, norm2_g, w_router, b_router, w_gate_up, b_gate_up, w_down, b_down)
    B, S = x_prompt.shape[:2]
    T = x_sample.shape[1]
    past = page_table.shape[1] * PAGE_SIZE
    pos_p = jnp.arange(S, dtype=jnp.int32)
    pos_s = past + jnp.arange(T, dtype=jnp.int32)
    conv0 = jnp.zeros((B, CONV_W - 1, D_RNN), x_prompt.dtype)
    h0 = jnp.zeros((B, D_RNN), jnp.float32)
    wl = [w[0] for w in weights]
    yp, st_p = _trunk_layer(x_prompt, pos_p, None, conv0, h0, *wl)
    attend_s = functools.partial(_sample_dsa_attention, cache_k=cache_k[0], cache_v=cache_v[0],
                                 cache_idx_k=cache_idx_k[0], page_table=page_table)
    ys, st_s = _trunk_layer(x_sample, pos_s, attend_s, state_conv[0], state_h[0], *wl)
    x_all = jnp.concatenate([yp.reshape(-1, D_MODEL), ys.reshape(-1, D_MODEL)], axis=0)
    x_all = _moe(x_all, norm2_g[0], w_router[0], b_router[0], w_gate_up[0], b_gate_up[0], w_down[0], b_down[0])
    yp = x_all[:B * S].reshape(x_prompt.shape)
    ys = x_all[B * S:].reshape(x_sample.shape)
    k_p, v_p, ik_p, conv_p, h_p = [a[None] for a in st_p]
    k_s, v_s, ik_s, conv_s, h_s = [a[None] for a in st_s]
    return (yp, ys, k_p, v_p, ik_p, conv_p, h_p, k_s, v_s, ik_s, conv_s, h_s)
```

```python
import functools
import math

import jax
import jax.numpy as jnp
import numpy as np
from jax import lax
from jax.experimental import pallas as pl
from jax.experimental.pallas import tpu as pltpu

D_MODEL = 1024
PAGE_SIZE = 128
N_HEADS = 16
HEAD_DIM = 64
KV_HEADS = 4
IDX_HEADS = 8
IDX_DIM = 64
IDX_TOPK_MAX = 256
Q_BLOCK = 128
ROPE_THETA = 10000.0
D_RNN = D_MODEL
RG_BLOCKS = 4
RG_BLOCK_W = D_RNN // RG_BLOCKS
CONV_W = 4
RG_C = 8.0
N_EXPERTS = 32
TOP_K = 4
D_FF = D_MODEL
SWIGLU_LIMIT = 7.0
SWIGLU_ALPHA = 1.702
MOE_BLOCK = 128
EPS = 1e-6

Q_W = N_HEADS * HEAD_DIM
KV_W = KV_HEADS * HEAD_DIM
IQ_W = IDX_HEADS * IDX_DIM
IN_WIDTHS = (Q_W, KV_W, KV_W, IQ_W, IDX_DIM, IDX_HEADS, D_RNN, D_RNN, D_MODEL, D_MODEL)
D_IN = sum(IN_WIDTHS)

LANE = 128
SUBLANE = 8
VMEM_LIMIT_BYTES = 48 * 1024 * 1024


def _norm_proj_kernel(x_ref, g_ref, w_ref, o_ref):
    x = x_ref[...]
    y = x * lax.rsqrt(jnp.mean(x * x, axis=-1, keepdims=True) + EPS) * g_ref[...]
    o_ref[...] = jnp.dot(y.astype(jnp.bfloat16), w_ref[...], preferred_element_type=jnp.float32)


def _norm_proj(x2d, g, w_bf16, tm=512, tn=896):
    m, d = x2d.shape
    n = w_bf16.shape[1]
    return pl.pallas_call(
        _norm_proj_kernel,
        grid=(m // tm, n // tn),
        in_specs=[
            pl.BlockSpec((tm, d), lambda i, j: (i, 0)),
            pl.BlockSpec((1, d), lambda i, j: (0, 0)),
            pl.BlockSpec((d, tn), lambda i, j: (0, j)),
        ],
        out_specs=pl.BlockSpec((tm, tn), lambda i, j: (i, j)),
        out_shape=jax.ShapeDtypeStruct((m, n), jnp.float32),
        name="norm_proj",
    )(x2d, g.reshape(1, d), w_bf16)


INT_MIN = -(2 ** 31)
INT_MAX = 2 ** 31 - 1
MASK_BIAS = -1e30
NEG_INF_BITS_MASK = 0x7FFFFFFF


def _ordered_key(x):
    bits = pltpu.bitcast(x, jnp.int32)
    return bits ^ ((bits >> 31) & NEG_INF_BITS_MASK)


def _topk_threshold(key_ref, cut_ref, nk, *, rows, kb, topk, n_idx_bits):
    n_lane_tiles = kb // LANE
    lane = lax.broadcasted_iota(jnp.int32, (rows, kb), 1)

    def count(pred_of_chunk):
        def body(c, cnt):
            p = pred_of_chunk(c, key_ref[c])
            for j in range(n_lane_tiles):
                cnt = cnt + jnp.where(p[:, j * LANE:(j + 1) * LANE], 1, 0)
            return cnt
        cnt = lax.fori_loop(0, nk, body, jnp.zeros((rows, LANE), jnp.int32))
        return jnp.sum(cnt, axis=-1, keepdims=True)

    def bit_step(b, carry):
        thr, n_ge = carry
        cand = thr ^ (jnp.int32(1) << (31 - b))
        cand_b = jnp.broadcast_to(cand, (rows, kb))
        cnt = count(lambda c, key: key >= cand_b)
        ok = cnt >= topk
        return jnp.where(ok, cand, thr), jnp.where(ok, cnt, n_ge)

    thr0 = jnp.full((rows, 1), INT_MIN, jnp.int32)
    n0 = jnp.full((rows, 1), 0, jnp.int32) + nk * kb
    thr, n_ge = lax.fori_loop(0, 32, bit_step, (thr0, n0))
    thr_b = jnp.broadcast_to(thr, (rows, kb))

    cut_ref[...] = jnp.full((rows, LANE), INT_MAX, jnp.int32)

    @pl.when(jnp.max(n_ge) > topk)
    def _():
        n_eq = count(lambda c, key: key == thr_b)
        need = topk - (n_ge - n_eq)

        def idx_step(b, lo):
            step = jnp.int32(1) << (n_idx_bits - 1 - b)
            mid_b = jnp.broadcast_to(lo + step - 1, (rows, kb))
            f = count(lambda c, key: (key == thr_b) & (c * kb + lane <= mid_b))
            return jnp.where(f < need, lo + step, lo)

        lo = lax.fori_loop(0, n_idx_bits, idx_step, jnp.zeros((rows, 1), jnp.int32))
        cut = jnp.where(n_ge > topk, lo, INT_MAX)
        cut_ref[...] = jnp.broadcast_to(cut, (rows, LANE))

    return thr


def _dsa_kernel(iq_ref, iw_ref, q_ref, ikT_ref, kT_ref, v_ref, o_ref,
                key_ref, wb_ref, cut_ref, m_ref, l_ref, acc_ref, *, qb, kb, topk, n_idx_bits, q_pos0, n_chunks_of):
    i = pl.program_id(0)
    nk = n_chunks_of(i)
    n_lane_tiles = kb // LANE
    row = lax.broadcasted_iota(jnp.int32, (qb, kb), 0)
    lane = lax.broadcasted_iota(jnp.int32, (qb, kb), 1)
    qpos = q_pos0 + i * qb + row

    iw = iw_ref[...]
    for h in range(IDX_HEADS):
        wb_ref[h] = jnp.broadcast_to(iw[:, h:h + 1], (qb, LANE))

    def score_chunk(c, carry):
        ikc = ikT_ref[c]
        acc = jnp.zeros((qb, kb), jnp.float32)
        for h in range(IDX_HEADS):
            s = jnp.dot(iq_ref[0, h], ikc, preferred_element_type=jnp.float32)
            acc = acc + jnp.tile(wb_ref[h], (1, n_lane_tiles)) * jnp.maximum(s, 0.0)
        acc = jnp.where(c * kb + lane <= qpos, acc, -jnp.inf)
        key_ref[c] = _ordered_key(acc)
        return carry

    lax.fori_loop(0, nk, score_chunk, 0)

    thr = _topk_threshold(key_ref, cut_ref, nk, rows=qb, kb=kb, topk=topk, n_idx_bits=n_idx_bits)
    thr_b = jnp.broadcast_to(thr, (qb, kb))

    m_ref[...] = jnp.full(m_ref.shape, MASK_BIAS, jnp.float32)
    l_ref[...] = jnp.zeros(l_ref.shape, jnp.float32)
    acc_ref[...] = jnp.zeros(acc_ref.shape, jnp.float32)
    cut_b = jnp.tile(cut_ref[...], (1, n_lane_tiles))
    rep = N_HEADS // KV_HEADS

    def attend_chunk(c, carry):
        key = key_ref[c]
        kpos = c * kb + lane
        sel = (key > thr_b) | ((key == thr_b) & (kpos <= cut_b))
        sel = sel & (kpos <= qpos)
        bias = jnp.where(sel, 0.0, MASK_BIAS)
        kc = kT_ref[c]
        for g in range(KV_HEADS):
            s = jnp.dot(q_ref[0, g], kc[g * HEAD_DIM:(g + 1) * HEAD_DIM, :], preferred_element_type=jnp.float32)
            s = (s.reshape(rep, qb, kb) + bias[None]).reshape(rep * qb, kb)
            m_prev = m_ref[g]
            m_new = jnp.maximum(m_prev, jnp.max(s, axis=-1, keepdims=True))
            alpha = jnp.exp(m_prev - m_new)
            p = jnp.exp(s - jnp.tile(m_new, (1, n_lane_tiles)))
            l_ref[g] = alpha * l_ref[g] + jnp.sum(p, axis=-1, keepdims=True)
            pv = jnp.dot(p.astype(jnp.bfloat16), v_ref[c, g], preferred_element_type=jnp.float32)
            acc_ref[g] = acc_ref[g] * alpha[:, :HEAD_DIM] + pv
            m_ref[g] = m_new
        return carry

    lax.fori_loop(0, nk, attend_chunk, 0)
    for g in range(KV_HEADS):
        o_ref[0, g] = acc_ref[g] / l_ref[g][:, :HEAD_DIM]


def _prompt_dsa_attention(q, k, v, iq, iw, ik, *, qb=128, kb=512):
    s_len = q.shape[0]
    topk = min(IDX_TOPK_MAX, s_len // 4)
    nqb, nkc = s_len // qb, s_len // kb
    rep = N_HEADS // KV_HEADS
    bf = jnp.bfloat16
    scale = HEAD_DIM ** -0.5
    q_t = (q * scale).astype(bf).reshape(nqb, qb, KV_HEADS, rep, HEAD_DIM).transpose(0, 2, 3, 1, 4)
    q_t = q_t.reshape(nqb, KV_HEADS, rep * qb, HEAD_DIM)
    iq_t = iq.astype(bf).reshape(nqb, qb, IDX_HEADS, IDX_DIM).transpose(0, 2, 1, 3)
    ikT = ik.astype(bf).reshape(nkc, kb, IDX_DIM).transpose(0, 2, 1)
    kT = k.astype(bf).reshape(nkc, kb, KV_HEADS * HEAD_DIM).transpose(0, 2, 1)
    v_t = v.astype(bf).reshape(nkc, kb, KV_HEADS, HEAD_DIM).transpose(0, 2, 1, 3)
    n_idx_bits = max(1, (s_len - 1).bit_length())
    body = functools.partial(
        _dsa_kernel, qb=qb, kb=kb, topk=topk, n_idx_bits=n_idx_bits, q_pos0=0,
        n_chunks_of=lambda i: ((i + 1) * qb + kb - 1) // kb)
    whole = lambda shape: pl.BlockSpec(shape, lambda i: (0,) * len(shape), pipeline_mode=pl.Buffered(1))
    out = pl.pallas_call(
        body,
        grid=(nqb,),
        in_specs=[
            pl.BlockSpec((1, IDX_HEADS, qb, IDX_DIM), lambda i: (i, 0, 0, 0)),
            pl.BlockSpec((qb, IDX_HEADS), lambda i: (i, 0)),
            pl.BlockSpec((1, KV_HEADS, rep * qb, HEAD_DIM), lambda i: (i, 0, 0, 0)),
            whole((nkc, IDX_DIM, kb)),
            whole((nkc, KV_HEADS * HEAD_DIM, kb)),
            whole((nkc, KV_HEADS, kb, HEAD_DIM)),
        ],
        out_specs=pl.BlockSpec((1, KV_HEADS, rep * qb, HEAD_DIM), lambda i: (i, 0, 0, 0)),
        out_shape=jax.ShapeDtypeStruct((nqb, KV_HEADS, rep * qb, HEAD_DIM), jnp.float32),
        scratch_shapes=[
            pltpu.VMEM((nkc, qb, kb), jnp.int32),
            pltpu.VMEM((IDX_HEADS, qb, LANE), jnp.float32),
            pltpu.VMEM((qb, LANE), jnp.int32),
            pltpu.VMEM((KV_HEADS, rep * qb, LANE), jnp.float32),
            pltpu.VMEM((KV_HEADS, rep * qb, LANE), jnp.float32),
            pltpu.VMEM((KV_HEADS, rep * qb, HEAD_DIM), jnp.float32),
        ],
        compiler_params=pltpu.CompilerParams(dimension_semantics=("arbitrary",), vmem_limit_bytes=VMEM_LIMIT_BYTES),
        name="prompt_dsa_attention",
    )(iq_t, iw, q_t, ikT, kT, v_t)
    out = out.reshape(nqb, KV_HEADS, rep, qb, HEAD_DIM).transpose(0, 3, 1, 2, 4)
    return out.reshape(s_len, N_HEADS * HEAD_DIM)


def _topk_threshold_t(key_ref, cut_ref, nk, *, kb, cols, topk, n_idx_bits):
    acc_rows = 8 * SUBLANE
    assert kb % acc_rows == 0
    pos0 = lax.broadcasted_iota(jnp.int32, (kb, cols), 0)

    def count(pred_of_chunk):
        def body(c, cnt):
            p = pred_of_chunk(c, key_ref[c])
            for j in range(kb // acc_rows):
                cnt = cnt + jnp.where(p[j * acc_rows:(j + 1) * acc_rows, :], 1, 0)
            return cnt
        cnt = lax.fori_loop(0, nk, body, jnp.zeros((acc_rows, cols), jnp.int32))
        return jnp.sum(cnt, axis=0, keepdims=True)

    def bit_cond(carry):
        b, _, n_ge = carry
        return jnp.logical_and(b < 32, jnp.max(jnp.abs(n_ge - topk)) > 0)

    def bit_step(carry):
        b, thr, n_ge = carry
        cand = thr ^ (jnp.int32(1) << (31 - b))
        cand_b = jnp.broadcast_to(cand, (kb, cols))
        cnt = count(lambda c, key: key >= cand_b)
        ok = cnt >= topk
        return b + 1, jnp.where(ok, cand, thr), jnp.where(ok, cnt, n_ge)

    thr0 = jnp.full((1, cols), INT_MIN, jnp.int32)
    n0 = jnp.full((1, cols), 0, jnp.int32) + nk * kb
    _, thr, n_ge = lax.while_loop(bit_cond, bit_step, (jnp.int32(0), thr0, n0))
    thr_b = jnp.broadcast_to(thr, (kb, cols))

    cut_ref[...] = jnp.full((SUBLANE, cols), INT_MAX, jnp.int32)

    @pl.when(jnp.max(n_ge) > topk)
    def _():
        n_eq = count(lambda c, key: key == thr_b)
        need = topk - (n_ge - n_eq)

        def idx_step(b, lo):
            step = jnp.int32(1) << (n_idx_bits - 1 - b)
            mid_b = jnp.broadcast_to(lo + step - 1, (kb, cols))
            f = count(lambda c, key: (key == thr_b) & (c * kb + pos0 <= mid_b))
            return jnp.where(f < need, lo + step, lo)

        lo = lax.fori_loop(0, n_idx_bits, idx_step, jnp.zeros((1, cols), jnp.int32))
        cut = jnp.where(n_ge > topk, lo, INT_MAX)
        cut_ref[...] = jnp.broadcast_to(cut, (SUBLANE, cols))

    return thr


UNDERFLOW_GUARD = 1e-30
LOG2_E = math.log2(math.e)


def _dsa_t_kernel(iq_ref, iw_ref, q_ref, mb_ref, ik_ref, k_ref, vt_ref, o_ref,
                  key_ref, cut_ref, m_ref, l_ref, acc_ref, *, qb, kb, topk, n_idx_bits):
    i = pl.program_id(0)
    nk = ((i + 1) * qb + kb - 1) // kb
    rep = N_HEADS // KV_HEADS
    kpos0 = lax.broadcasted_iota(jnp.int32, (kb, qb), 0)
    qpos = i * qb + lax.broadcasted_iota(jnp.int32, (kb, qb), 1)
    iw = iw_ref[0]

    def score_chunk(c, carry):
        ikc = ik_ref[c]
        acc = jnp.zeros((kb, qb), jnp.float32)
        for hp in range(IDX_HEADS // 2):
            s2 = jnp.dot(ikc, iq_ref[0, hp], preferred_element_type=jnp.float32)
            for hh in range(2):
                h = 2 * hp + hh
                acc = acc + iw[h:h + 1, :] * jnp.maximum(s2[:, hh * qb:(hh + 1) * qb], 0.0)
        acc = jnp.where(c * kb + kpos0 <= qpos, acc, -jnp.inf)
        key_ref[c] = _ordered_key(acc)
        return carry

    lax.fori_loop(0, nk, score_chunk, 0)

    thr = _topk_threshold_t(key_ref, cut_ref, nk, kb=kb, cols=qb, topk=topk, n_idx_bits=n_idx_bits)
    thr_b = jnp.broadcast_to(thr, (kb, qb))
    cut_b = jnp.broadcast_to(cut_ref[0:1, :], (kb, qb))

    def mask_bias(c):
        key = key_ref[c]
        kpos = c * kb + kpos0
        sel = (key > thr_b) | ((key == thr_b) & (kpos <= cut_b))
        sel = sel & (kpos <= qpos)
        return jnp.tile(jnp.where(sel, 0.0, MASK_BIAS), (1, rep))

    l_ref[...] = jnp.zeros(l_ref.shape, jnp.float32)
    acc_ref[...] = jnp.zeros(acc_ref.shape, jnp.float32)

    def attend_chunk(c, carry):
        bias = mask_bias(c)
        kc = k_ref[c]
        for g in range(KV_HEADS):
            s = jnp.dot(kc, q_ref[0, g], preferred_element_type=jnp.float32) + (bias - mb_ref[0, g])
            p = jnp.exp2(s)
            l_ref[g] = l_ref[g] + jnp.sum(p, axis=0, keepdims=True)
            acc_ref[g] = acc_ref[g] + jnp.dot(vt_ref[c, g], p.astype(jnp.bfloat16), preferred_element_type=jnp.float32)
        return carry

    lax.fori_loop(0, nk, attend_chunk, 0)

    @pl.when(jnp.min(l_ref[...]) < UNDERFLOW_GUARD)
    def _():
        m_ref[...] = jnp.full(m_ref.shape, MASK_BIAS, jnp.float32)
        l_ref[...] = jnp.zeros(l_ref.shape, jnp.float32)
        acc_ref[...] = jnp.zeros(acc_ref.shape, jnp.float32)

        def attend_chunk_running_max(c, carry):
            bias = mask_bias(c)
            kc = k_ref[c]
            for g in range(KV_HEADS):
                s = jnp.dot(kc, q_ref[0, g], preferred_element_type=jnp.float32) + bias
                m_prev = m_ref[g]
                m_new = jnp.maximum(m_prev, jnp.max(s, axis=0, keepdims=True))
                alpha = jnp.exp2(m_prev - m_new)
                p = jnp.exp2(s - m_new)
                l_ref[g] = alpha * l_ref[g] + jnp.sum(p, axis=0, keepdims=True)
                pv = jnp.dot(vt_ref[c, g], p.astype(jnp.bfloat16), preferred_element_type=jnp.float32)
                acc_ref[g] = acc_ref[g] * alpha + pv
                m_ref[g] = m_new
            return carry

        lax.fori_loop(0, nk, attend_chunk_running_max, 0)

    for g in range(KV_HEADS):
        o_ref[0, g] = acc_ref[g] / l_ref[g]


def _prompt_dsa_attention_t(q, k, v, iq, iw, ik, *, qb=128, kb=1024):
    s_len = q.shape[0]
    topk = min(IDX_TOPK_MAX, s_len // 4)
    nqb, nkc = s_len // qb, s_len // kb
    rep = N_HEADS // KV_HEADS
    bf = jnp.bfloat16
    q_s = (q * (HEAD_DIM ** -0.5 * LOG2_E)).astype(bf)
    k_bf = k.astype(bf)
    q_norm = jnp.sqrt(jnp.sum(jnp.square(q_s.astype(jnp.float32)), axis=-1))
    k_norm = jnp.sqrt(jnp.max(jnp.sum(jnp.square(k_bf.astype(jnp.float32)), axis=-1), axis=0))
    bound = q_norm.reshape(nqb, qb, KV_HEADS, rep) * k_norm[None, None, :, None]
    bound = bound.transpose(0, 2, 3, 1).reshape(nqb, KV_HEADS, 1, rep * qb)
    qs = q_s.reshape(nqb, qb, KV_HEADS, rep, HEAD_DIM).transpose(0, 2, 4, 3, 1)
    q_bd = qs[:, :, None, :, :, :] * jnp.eye(KV_HEADS, dtype=bf)[None, :, :, None, None, None]
    q_bd = q_bd.reshape(nqb, KV_HEADS, KV_W, rep * qb)
    iq_t = iq.astype(bf).reshape(nqb, qb, IDX_HEADS // 2, 2, IDX_DIM).transpose(0, 2, 4, 3, 1)
    iq_t = iq_t.reshape(nqb, IDX_HEADS // 2, IDX_DIM, 2 * qb)
    iw_t = iw.reshape(nqb, qb, IDX_HEADS).transpose(0, 2, 1)
    ik_c = ik.astype(bf).reshape(nkc, kb, IDX_DIM)
    k_c = k_bf.reshape(nkc, kb, KV_W)
    v_t = v.astype(bf).reshape(nkc, kb, KV_HEADS, HEAD_DIM).transpose(0, 2, 3, 1)
    n_idx_bits = max(1, (s_len - 1).bit_length())
    body = functools.partial(_dsa_t_kernel, qb=qb, kb=kb, topk=topk, n_idx_bits=n_idx_bits)
    whole = lambda shape: pl.BlockSpec(shape, lambda i: (0,) * len(shape), pipeline_mode=pl.Buffered(1))
    out = pl.pallas_call(
        body,
        grid=(nqb,),
        in_specs=[
            pl.BlockSpec((1, IDX_HEADS // 2, IDX_DIM, 2 * qb), lambda i: (i, 0, 0, 0)),
            pl.BlockSpec((1, IDX_HEADS, qb), lambda i: (i, 0, 0)),
            pl.BlockSpec((1, KV_HEADS, KV_W, rep * qb), lambda i: (i, 0, 0, 0)),
            pl.BlockSpec((1, KV_HEADS, 1, rep * qb), lambda i: (i, 0, 0, 0)),
            whole((nkc, kb, IDX_DIM)),
            whole((nkc, kb, KV_W)),
            whole((nkc, KV_HEADS, HEAD_DIM, kb)),
        ],
        out_specs=pl.BlockSpec((1, KV_HEADS, HEAD_DIM, rep * qb), lambda i: (i, 0, 0, 0)),
        out_shape=jax.ShapeDtypeStruct((nqb, KV_HEADS, HEAD_DIM, rep * qb), jnp.float32),
        scratch_shapes=[
            pltpu.VMEM((nkc, kb, qb), jnp.int32),
            pltpu.VMEM((SUBLANE, qb), jnp.int32),
            pltpu.VMEM((KV_HEADS, 1, rep * qb), jnp.float32),
            pltpu.VMEM((KV_HEADS, 1, rep * qb), jnp.float32),
            pltpu.VMEM((KV_HEADS, HEAD_DIM, rep * qb), jnp.float32),
        ],
        compiler_params=pltpu.CompilerParams(dimension_semantics=("arbitrary",), vmem_limit_bytes=VMEM_LIMIT_BYTES),
        name="prompt_dsa_attention",
    )(iq_t, iw_t, q_bd, bound, ik_c, k_c, v_t)
    out = out.reshape(nqb, KV_HEADS, HEAD_DIM, rep, qb).transpose(0, 4, 1, 3, 2)
    return out.reshape(s_len, N_HEADS * HEAD_DIM)


def _sample_index_kernel(pt_ref, iq_ref, iw_ref, iknew_ref, *rest, pp, n_pages, topk, n_idx_bits):
    page_refs, o_ref, (key_ref, wide_ref, wb_ref, cut_ref) = rest[:pp], rest[pp], rest[pp + 1:]
    pg = pl.program_id(1)
    t_new = o_ref.shape[1]
    step_keys = pp * PAGE_SIZE
    n_keys = (n_pages + 1) * PAGE_SIZE

    @pl.when(pg == 0)
    def _():
        wb_ref[...] = jnp.broadcast_to(iw_ref[0], wb_ref.shape)

    iq = iq_ref[0]

    def scores(ik_t):
        s = jnp.dot(iq, ik_t.astype(jnp.bfloat16), preferred_element_type=jnp.float32)
        s = jnp.tile(wb_ref[...], (1, ik_t.shape[1] // LANE)) * jnp.maximum(s, 0.0)
        acc = jnp.zeros((t_new, ik_t.shape[1]), jnp.float32)
        for h in range(IDX_HEADS):
            acc = acc + s[h * t_new:(h + 1) * t_new, :]
        return acc

    key_ref[pg] = _ordered_key(scores(jnp.concatenate([r[0] for r in page_refs], axis=1)))

    @pl.when(pg == pl.num_programs(1) - 1)
    def _():
        for c in range(n_pages // pp):
            wide_ref[0, :, c * step_keys:(c + 1) * step_keys] = key_ref[c]
        row = lax.broadcasted_iota(jnp.int32, (t_new, PAGE_SIZE), 0)
        lane = lax.broadcasted_iota(jnp.int32, (t_new, PAGE_SIZE), 1)
        s_new = jnp.where(lane <= row, scores(iknew_ref[0]), -jnp.inf)
        wide_ref[0, :, n_pages * PAGE_SIZE:] = _ordered_key(s_new)
        thr = _topk_threshold(wide_ref, cut_ref, 1, rows=t_new, kb=n_keys, topk=topk, n_idx_bits=n_idx_bits)
        key = wide_ref[0]
        kpos = lax.broadcasted_iota(jnp.int32, (t_new, n_keys), 1)
        qpos = n_pages * PAGE_SIZE + lax.broadcasted_iota(jnp.int32, (t_new, n_keys), 0)
        thr_b = jnp.broadcast_to(thr, (t_new, n_keys))
        cut_b = jnp.tile(cut_ref[...], (1, n_pages + 1))
        sel = (key > thr_b) | ((key == thr_b) & (kpos <= cut_b))
        o_ref[0] = jnp.where(sel & (kpos <= qpos), 0.0, MASK_BIAS)


def _sample_attend_kernel(pt_ref, q_ref, bias_ref, biasnew_ref, knew_ref, vnew_ref, *rest, pp, n_pages):
    k_refs, v_refs, o_ref, (m_ref, l_ref, acc_ref) = rest[:pp], rest[pp:2 * pp], rest[2 * pp], rest[2 * pp + 1:]
    pg = pl.program_id(1)
    n_rows = q_ref.shape[1]
    t_new = bias_ref.shape[1]

    @pl.when(pg == 0)
    def _():
        m_ref[...] = jnp.full(m_ref.shape, MASK_BIAS, jnp.float32)
        l_ref[...] = jnp.zeros(l_ref.shape, jnp.float32)
        acc_ref[...] = jnp.zeros(acc_ref.shape, jnp.float32)

    q = q_ref[0]

    def step(bias, k_t, v_t):
        n_keys = k_t.shape[1]
        s = jnp.dot(q, k_t.astype(jnp.bfloat16), preferred_element_type=jnp.float32)
        s = s + jnp.tile(bias, (n_rows // t_new, 1))
        m_prev = m_ref[...]
        m_new = jnp.maximum(m_prev, jnp.max(s, axis=-1, keepdims=True))
        alpha = jnp.exp(m_prev - m_new)
        p = jnp.exp(s - jnp.tile(m_new, (1, n_keys // LANE)))
        l_ref[...] = alpha * l_ref[...] + jnp.sum(p, axis=-1, keepdims=True)
        pv = lax.dot_general(p.astype(jnp.bfloat16), v_t.astype(jnp.bfloat16), (((1,), (1,)), ((), ())),
                             preferred_element_type=jnp.float32)
        acc_ref[...] = acc_ref[...] * jnp.tile(alpha, (1, KV_W // LANE)) + pv
        m_ref[...] = m_new

    step(bias_ref[0], jnp.concatenate([r[0] for r in k_refs], axis=1), jnp.concatenate([r[0] for r in v_refs], axis=1))

    @pl.when(pg == pl.num_programs(1) - 1)
    def _():
        step(biasnew_ref[0], knew_ref[0], vnew_ref[0])
        acc = acc_ref[...]
        row_head = lax.broadcasted_iota(jnp.int32, acc.shape, 0) // (n_rows // KV_HEADS)
        col_head = lax.broadcasted_iota(jnp.int32, acc.shape, 1) // HEAD_DIM
        own = jnp.where(row_head == col_head, acc, 0.0)
        folded = own[:, :LANE] + own[:, LANE:]
        folded = folded + pltpu.roll(folded, HEAD_DIM, axis=1)
        o_ref[0] = folded / l_ref[...]


def _sample_dsa_attention(q, k, v, iq, iw, ik, cache_k, cache_v, cache_idx_k, page_table, *, pp_index=64, pp=32):
    b, t = q.shape[:2]
    n_pages = page_table.shape[1]
    n_pool = cache_k.shape[0]
    past = n_pages * PAGE_SIZE
    topk = min(IDX_TOPK_MAX, (past + t) // 4)
    n_idx_bits = max(1, (past + PAGE_SIZE - 1).bit_length())
    pp_index, pp = min(pp_index, n_pages), min(pp, n_pages)
    assert n_pages % pp == 0 and n_pages % pp_index == 0 and t == SUBLANE and KV_W == 2 * LANE
    npg = n_pages // pp
    n_keys = past + PAGE_SIZE
    bf = jnp.bfloat16
    rep = N_HEADS // KV_HEADS
    n_rows = N_HEADS * t

    iq_r = iq.astype(bf).transpose(0, 2, 1, 3).reshape(b, IDX_HEADS * t, IDX_DIM)
    iw_r = iw.transpose(0, 2, 1).reshape(b, IDX_HEADS * t, 1)
    new_page = lambda a: jnp.pad(jnp.swapaxes(a, 1, 2), ((0, 0), (0, 0), (0, PAGE_SIZE - t)))
    pages_t = lambda c: jnp.swapaxes(c.reshape(n_pool, PAGE_SIZE, -1), 1, 2)
    ik_new = new_page(ik)
    k_new = new_page(k.reshape(b, t, KV_W))
    v_new = new_page(v.reshape(b, t, KV_W))
    qs = (q * HEAD_DIM ** -0.5).astype(bf).reshape(b, t, KV_HEADS, rep, HEAD_DIM).transpose(0, 2, 3, 1, 4)
    q_bd = (qs[:, :, :, :, None, :] * jnp.eye(KV_HEADS, dtype=bf)[None, :, None, None, :, None])
    q_bd = q_bd.reshape(b, n_rows, KV_W)

    page_map = lambda per_step, j: (lambda bi, pg, pt: (pt[bi, pg * per_step + j], 0, 0))
    per_seq = lambda shape: pl.BlockSpec((1,) + shape, lambda bi, pg, pt: (bi,) + (0,) * len(shape))
    params = pltpu.CompilerParams(dimension_semantics=("arbitrary", "arbitrary"), vmem_limit_bytes=VMEM_LIMIT_BYTES)

    bias = pl.pallas_call(
        functools.partial(_sample_index_kernel, pp=pp_index, n_pages=n_pages, topk=topk, n_idx_bits=n_idx_bits),
        grid_spec=pltpu.PrefetchScalarGridSpec(
            num_scalar_prefetch=1, grid=(b, n_pages // pp_index),
            in_specs=[per_seq((IDX_HEADS * t, IDX_DIM)), per_seq((IDX_HEADS * t, 1)), per_seq((IDX_DIM, PAGE_SIZE))]
            + [pl.BlockSpec((1, IDX_DIM, PAGE_SIZE), page_map(pp_index, j)) for j in range(pp_index)],
            out_specs=per_seq((t, n_keys)),
            scratch_shapes=[pltpu.VMEM((n_pages // pp_index, t, pp_index * PAGE_SIZE), jnp.int32),
                            pltpu.VMEM((1, t, n_keys), jnp.int32),
                            pltpu.VMEM((IDX_HEADS * t, PAGE_SIZE), jnp.float32),
                            pltpu.VMEM((t, LANE), jnp.int32)]),
        out_shape=jax.ShapeDtypeStruct((b, t, n_keys), jnp.float32),
        compiler_params=params,
        name="sample_index",
    )(page_table, iq_r, iw_r, ik_new, *([pages_t(cache_idx_k)] * pp_index))

    ck = pages_t(cache_k)
    cv = pages_t(cache_v)
    out = pl.pallas_call(
        functools.partial(_sample_attend_kernel, pp=pp, n_pages=n_pages),
        grid_spec=pltpu.PrefetchScalarGridSpec(
            num_scalar_prefetch=1, grid=(b, npg),
            in_specs=[per_seq((n_rows, KV_W)),
                      pl.BlockSpec((1, t, pp * PAGE_SIZE), lambda bi, pg, pt: (bi, 0, pg)),
                      pl.BlockSpec((1, t, PAGE_SIZE), lambda bi, pg, pt: (bi, 0, n_pages)),
                      per_seq((KV_W, PAGE_SIZE)), per_seq((KV_W, PAGE_SIZE))]
            + [pl.BlockSpec((1, KV_W, PAGE_SIZE), page_map(pp, j)) for j in range(pp)] * 2,
            out_specs=per_seq((n_rows, LANE)),
            scratch_shapes=[pltpu.VMEM((n_rows, LANE), jnp.float32), pltpu.VMEM((n_rows, LANE), jnp.float32),
                            pltpu.VMEM((n_rows, KV_W), jnp.float32)]),
        out_shape=jax.ShapeDtypeStruct((b, n_rows, LANE), jnp.float32),
        compiler_params=params,
        name="sample_attend",
    )(page_table, q_bd, bias, bias, k_new, v_new, *([ck] * pp), *([cv] * pp))
    out = out[:, :, :HEAD_DIM].reshape(b, N_HEADS, t, HEAD_DIM).transpose(0, 2, 1, 3)
    return out.reshape(b, t, N_HEADS * HEAD_DIM)


GELU_C = math.sqrt(2.0 / math.pi)


def _rglru_kernel(x_ref, xg_ref, prev_ref, h0_ref, cw_ref, cb_ref, wa_ref, ba_ref, wi_ref, bi_ref, lam_ref,
                  o_ref, hl_ref, conv_ref, a_ref, u_ref, xprev_ref, h_ref, *, rows, per_group_state):
    i = pl.program_id(0)
    n_groups = rows // SUBLANE
    d_rnn = x_ref.shape[1]
    r8 = lax.broadcasted_iota(jnp.int32, (SUBLANE, d_rnn), 0)
    grp = lambda g: pl.ds(pl.multiple_of(g * SUBLANE, SUBLANE), SUBLANE)

    if not per_group_state:
        @pl.when(i == 0)
        def _():
            xprev_ref[...] = jnp.zeros_like(xprev_ref)
            h_ref[...] = jnp.zeros_like(h_ref)

    cw = cw_ref[...]
    cb = cb_ref[...]

    def conv_group(g, prev):
        x8 = x_ref[grp(g), :]
        if per_group_state:
            prev = prev_ref[grp(g), :]
        out = cb + x8 * cw[CONV_W - 1:CONV_W, :]
        for d in range(1, CONV_W):
            shifted = jnp.where(r8 < d, pltpu.roll(prev, d, axis=0), pltpu.roll(x8, d, axis=0))
            out = out + shifted * cw[CONV_W - 1 - d:CONV_W - d, :]
        conv_ref[grp(g), :] = out
        return x8

    zeros8 = jnp.zeros((SUBLANE, d_rnn), jnp.float32)
    xlast = lax.fori_loop(0, n_groups, conv_group, zeros8 if per_group_state else xprev_ref[...])
    if not per_group_state:
        xprev_ref[...] = xlast

    xc = conv_ref[...]
    xb = xc.astype(jnp.bfloat16)
    r_parts, i_parts = [], []
    for n in range(RG_BLOCKS):
        xn = xb[:, n * RG_BLOCK_W:(n + 1) * RG_BLOCK_W]
        r_parts.append(jnp.dot(xn, wa_ref[n], preferred_element_type=jnp.float32))
        i_parts.append(jnp.dot(xn, wi_ref[n], preferred_element_type=jnp.float32))
    r = jax.nn.sigmoid(jnp.concatenate(r_parts, axis=-1) + ba_ref[...])
    ig = jax.nn.sigmoid(jnp.concatenate(i_parts, axis=-1) + bi_ref[...])
    lam = lam_ref[...]
    log_sig_lam = -(jnp.maximum(-lam, 0.0) + jnp.log1p(jnp.exp(-jnp.abs(lam))))
    log_a = RG_C * r * log_sig_lam
    a = jnp.exp(log_a)
    a_ref[...] = a
    u_ref[...] = jnp.sqrt(-jnp.tanh(log_a) * (a * a + 1.0)) * (ig * xc)

    def scan_group(g, hprev):
        a = a_ref[grp(g), :]
        u = u_ref[grp(g), :]
        if per_group_state:
            hprev = jnp.broadcast_to(h0_ref[pl.ds(g, 1), :], (SUBLANE, d_rnn))
        for d in (1, 2, 4):
            u = jnp.where(r8 >= d, a * pltpu.roll(u, d, axis=0) + u, u)
            a = jnp.where(r8 >= d, a * pltpu.roll(a, d, axis=0), a)
        h = a * hprev + u
        u_ref[grp(g), :] = h
        hlast = jnp.broadcast_to(h[SUBLANE - 1:SUBLANE, :], (SUBLANE, d_rnn))
        if per_group_state:
            hl_ref[pl.ds(g, 1), :] = h[SUBLANE - 1:SUBLANE, :]
        return hlast

    hlast = lax.fori_loop(0, n_groups, scan_group, zeros8 if per_group_state else h_ref[...])
    if not per_group_state:
        h_ref[...] = hlast
        hl_ref[...] = hlast

    xg = xg_ref[...]
    gelu = 0.5 * xg * (1.0 + jnp.tanh(GELU_C * (xg + 0.044715 * (xg * xg * xg))))
    o_ref[...] = u_ref[...] * gelu


def _rglru(x, xg, prev, h0, conv_w, conv_b, w_a, b_a, w_i, b_i, lam, *, per_group_state, tile_rows):
    n_rows, d = x.shape
    row = lambda a: a.reshape(1, d)
    const = lambda shape: pl.BlockSpec(shape, lambda i: (0,) * len(shape))
    n_hl = n_rows // SUBLANE if per_group_state else SUBLANE
    body = functools.partial(_rglru_kernel, rows=tile_rows, per_group_state=per_group_state)
    tile = pl.BlockSpec((tile_rows, d), lambda i: (i, 0))
    if per_group_state:
        assert tile_rows == n_rows
        prev_spec, h0_spec = const(prev.shape), const(h0.shape)
    else:
        prev_spec, h0_spec = const(prev.shape), const(h0.shape)
    return pl.pallas_call(
        body,
        grid=(n_rows // tile_rows,),
        in_specs=[tile, tile, prev_spec, h0_spec, const((CONV_W, d)), const((1, d)),
                  const(w_a.shape), const((1, d)), const(w_i.shape), const((1, d)), const((1, d))],
        out_specs=[tile, const((n_hl, d))],
        out_shape=[jax.ShapeDtypeStruct((n_rows, d), jnp.float32), jax.ShapeDtypeStruct((n_hl, d), jnp.float32)],
        scratch_shapes=[pltpu.VMEM((tile_rows, d), jnp.float32), pltpu.VMEM((tile_rows, d), jnp.float32),
                        pltpu.VMEM((tile_rows, d), jnp.float32), pltpu.VMEM((SUBLANE, d), jnp.float32),
                        pltpu.VMEM((SUBLANE, d), jnp.float32)],
        compiler_params=pltpu.CompilerParams(dimension_semantics=("arbitrary",), vmem_limit_bytes=VMEM_LIMIT_BYTES),
        name="rglru_sample" if per_group_state else "rglru_prompt",
    )(x, xg, prev, h0, conv_w, row(conv_b), w_a.astype(jnp.bfloat16), row(b_a), w_i.astype(jnp.bfloat16), row(b_i), row(lam))


def _rglru_prompt(x, xg, conv_w, conv_b, w_a, b_a, w_i, b_i, lam, tile_rows=512):
    dummy = jnp.zeros((SUBLANE, x.shape[1]), jnp.float32)
    rnn, hl = _rglru(x, xg, dummy, dummy, conv_w, conv_b, w_a, b_a, w_i, b_i, lam,
                     per_group_state=False, tile_rows=tile_rows)
    return rnn, hl[0]


def _rglru_sample(x, xg, state_conv, state_h, conv_w, conv_b, w_a, b_a, w_i, b_i, lam):
    b, t, d = x.shape
    assert t == SUBLANE
    prev = jnp.concatenate([jnp.zeros((b, SUBLANE - (CONV_W - 1), d), jnp.float32), state_conv], axis=1)
    rnn, hl = _rglru(x.reshape(b * t, d), xg.reshape(b * t, d), prev.reshape(b * t, d), state_h,
                     conv_w, conv_b, w_a, b_a, w_i, b_i, lam, per_group_state=True, tile_rows=b * t)
    return rnn.reshape(b, t, d), hl


def _rms_norm(x, g):
    xf = x.astype(jnp.float32)
    y = xf * lax.rsqrt(jnp.mean(xf * xf, axis=-1, keepdims=True) + EPS)
    return (y * g.astype(jnp.float32)).astype(x.dtype)


def _rope(x, pos):
    half = x.shape[-1] // 2
    inv_freq = ROPE_THETA ** (-jnp.arange(half, dtype=jnp.float32) / half)
    ang = pos.astype(jnp.float32)[:, None] * inv_freq[None, :]
    cos = jnp.cos(ang)[:, None, :]
    sin = jnp.sin(ang)[:, None, :]
    xf = x.astype(jnp.float32)
    x1, x2 = xf[..., :half], xf[..., half:]
    return jnp.concatenate([x1 * cos - x2 * sin, x2 * cos + x1 * sin], axis=-1).astype(x.dtype)


def _index_scores(iq, iw, ik):
    s = jnp.einsum('bthd,bld->bthl', iq, ik, preferred_element_type=jnp.float32)
    return jnp.einsum('bthl,bth->btl', jax.nn.relu(s), iw.astype(jnp.float32))


def _sparse_attend(q, k_sel, v_sel, valid):
    B, T = q.shape[:2]
    qg = q.reshape(B, T, KV_HEADS, N_HEADS // KV_HEADS, HEAD_DIM)
    logits = jnp.einsum('btkgd,btskd->btkgs', qg, k_sel, preferred_element_type=jnp.float32) * (HEAD_DIM ** -0.5)
    logits = jnp.where(valid[:, :, None, None, :], logits, -jnp.inf)
    p = jax.nn.softmax(logits, axis=-1)
    out = jnp.einsum('btkgs,btskd->btkgd', p.astype(v_sel.dtype), v_sel)
    return out.reshape(B, T, Q_W)


def _gather_rows(rows, idx):
    return jax.vmap(lambda r, i: r[i])(rows, idx)


def _prompt_sparse_attention(q, k, v, iq, iw, ik):
    B, S = q.shape[:2]
    topk = min(IDX_TOPK_MAX, S // 4)
    n_blocks = S // Q_BLOCK
    key_pos = jnp.arange(S, dtype=jnp.int32)

    def block(args):
        qb, iqb, iwb, start = args
        qpos = start + jnp.arange(Q_BLOCK, dtype=jnp.int32)
        sc = _index_scores(iqb, iwb, ik)
        sc = jnp.where((key_pos[None, :] <= qpos[:, None])[None], sc, -jnp.inf)
        _, sel = lax.top_k(sc, topk)
        valid = sel <= qpos[None, :, None]
        return _sparse_attend(qb, _gather_rows(k, sel), _gather_rows(v, sel), valid)

    to_blocks = lambda a: jnp.moveaxis(a.reshape(B, n_blocks, Q_BLOCK, *a.shape[2:]), 1, 0)
    starts = jnp.arange(n_blocks, dtype=jnp.int32) * Q_BLOCK
    out = lax.map(block, (to_blocks(q), to_blocks(iq), to_blocks(iw), starts))
    return jnp.moveaxis(out, 0, 1).reshape(B, S, Q_W)


def _sample_sparse_attention(q, k, v, iq, iw, ik, cache_k, cache_v, cache_idx_k, page_table):
    DB, T = q.shape[:2]
    past = page_table.shape[1] * PAGE_SIZE
    L = past + T
    topk = min(IDX_TOPK_MAX, L // 4)
    ik_past = cache_idx_k[page_table].reshape(DB, past, IDX_DIM)
    ik_all = jnp.concatenate([ik_past, ik.astype(ik_past.dtype)], axis=1)
    qpos = past + jnp.arange(T, dtype=jnp.int32)
    sc = _index_scores(iq, iw, ik_all)
    sc = jnp.where((jnp.arange(L, dtype=jnp.int32)[None, :] <= qpos[:, None])[None], sc, -jnp.inf)
    _, sel = lax.top_k(sc, topk)
    valid = sel <= qpos[None, :, None]
    in_past = sel < past
    sel_p = jnp.minimum(sel, past - 1)
    page = _gather_rows(page_table, sel_p // PAGE_SIZE)
    phys = page * PAGE_SIZE + sel_p % PAGE_SIZE
    new_idx = jnp.clip(sel - past, 0, T - 1)
    k_pool = cache_k.reshape(-1, KV_HEADS, HEAD_DIM)
    v_pool = cache_v.reshape(-1, KV_HEADS, HEAD_DIM)
    m = in_past[..., None, None]
    k_sel = jnp.where(m, k_pool[phys], _gather_rows(k, new_idx).astype(k_pool.dtype))
    v_sel = jnp.where(m, v_pool[phys], _gather_rows(v, new_idx).astype(v_pool.dtype))
    return _sparse_attend(q, k_sel, v_sel, valid)


def _causal_conv(xr, conv_state, w, b):
    T = xr.shape[1]
    xp = jnp.concatenate([conv_state.astype(xr.dtype), xr], axis=1)
    out = b + sum(xp[:, j:j + T] * w[j] for j in range(CONV_W))
    return out, xp[:, -(CONV_W - 1):]


def _block_diag(x, w, b):
    B, T, _ = x.shape
    y = jnp.einsum('btnc,ncd->btnd', x.reshape(B, T, RG_BLOCKS, RG_BLOCK_W), w)
    return y.reshape(B, T, D_RNN) + b


def _rg_lru(x, h0, w_a, b_a, w_i, b_i, lam):
    xf = x.astype(jnp.float32)
    r = jax.nn.sigmoid(_block_diag(x, w_a, b_a).astype(jnp.float32))
    i = jax.nn.sigmoid(_block_diag(x, w_i, b_i).astype(jnp.float32))
    log_a = RG_C * r * jax.nn.log_sigmoid(lam.astype(jnp.float32))
    a = jnp.exp(log_a)
    u = jnp.sqrt(-jnp.expm1(2.0 * log_a)) * (i * xf)

    def step(h, au):
        h = au[0] * h + au[1]
        return h, h

    h_last, hs = lax.scan(step, h0.astype(jnp.float32), (jnp.moveaxis(a, 1, 0), jnp.moveaxis(u, 1, 0)))
    return jnp.moveaxis(hs, 0, 1).astype(x.dtype), h_last.astype(x.dtype)


def _rms_norm_rows(x, g):
    return x * lax.rsqrt(jnp.mean(x * x, axis=-1, keepdims=True) + EPS) * g


def _router_kernel(x_ref, g_ref, wr_ref, br_ref, tri_ref, e_ref, gate_ref, rank_ref, cnt_ref, run_ref):
    i = pl.program_id(0)
    tm = x_ref.shape[0]

    @pl.when(i == 0)
    def _():
        run_ref[...] = jnp.zeros_like(run_ref)

    xn_bf = _rms_norm_rows(x_ref[...], g_ref[...]).astype(jnp.bfloat16)
    logits = lax.dot_general(wr_ref[...], xn_bf, (((1,), (1,)), ((), ())), preferred_element_type=jnp.float32)
    logits = logits + br_ref[...]

    expert = lax.broadcasted_iota(jnp.int32, (N_EXPERTS, tm), 0)
    member = jnp.zeros((N_EXPERTS, tm), jnp.float32)
    picked, values = [], []
    for k in range(TOP_K):
        mx = jnp.max(logits, axis=0, keepdims=True)
        idx = jnp.min(jnp.where(logits == mx, expert, N_EXPERTS), axis=0, keepdims=True)
        hit = expert == idx
        member = jnp.where(hit, 1.0, member)
        logits = jnp.where(hit, -jnp.inf, logits)
        picked.append(idx)
        values.append(mx)
        e_ref[k:k + 1, :] = idx

    ex = [jnp.exp(v - values[0]) for v in values]
    denom = ex[0] + ex[1] + ex[2] + ex[3]
    for k in range(TOP_K):
        gate_ref[k:k + 1, :] = ex[k] / denom

    before = jnp.dot(member.astype(jnp.bfloat16), tri_ref[...], preferred_element_type=jnp.float32)
    before = before + jnp.tile(run_ref[...], (1, tm // LANE))
    for k in range(TOP_K):
        r = jnp.sum(jnp.where(expert == picked[k], before, 0.0), axis=0, keepdims=True)
        rank_ref[k:k + 1, :] = r.astype(jnp.int32)
    run = run_ref[...] + jnp.sum(member, axis=1, keepdims=True)
    run_ref[...] = run
    cnt_ref[...] = run.astype(jnp.int32)


def _route(x2d, norm_g, w_router, b_router, tm=512):
    n, d = x2d.shape
    tri = (jnp.arange(tm)[:, None] < jnp.arange(tm)[None, :]).astype(jnp.bfloat16)
    const = lambda shape: pl.BlockSpec(shape, lambda i: (0,) * len(shape))
    rows4 = pl.BlockSpec((TOP_K, tm), lambda i: (0, i))
    top_e, gates, rank, cnt = pl.pallas_call(
        _router_kernel,
        grid=(n // tm,),
        in_specs=[pl.BlockSpec((tm, d), lambda i: (i, 0)), const((1, d)), const((N_EXPERTS, d)),
                  const((N_EXPERTS, 1)), const((tm, tm))],
        out_specs=[rows4, rows4, rows4, const((N_EXPERTS, LANE))],
        out_shape=[jax.ShapeDtypeStruct((TOP_K, n), jnp.int32),
                   jax.ShapeDtypeStruct((TOP_K, n), jnp.float32), jax.ShapeDtypeStruct((TOP_K, n), jnp.int32),
                   jax.ShapeDtypeStruct((N_EXPERTS, LANE), jnp.int32)],
        scratch_shapes=[pltpu.VMEM((N_EXPERTS, LANE), jnp.float32)],
        compiler_params=pltpu.CompilerParams(dimension_semantics=("arbitrary",)),
        name="moe_router",
    )(x2d, norm_g.reshape(1, d), w_router.T.astype(jnp.bfloat16), b_router.reshape(N_EXPERTS, 1), tri)
    return top_e, gates, rank, cnt[:, 0]


def _expert_kernel(be_ref, used_ref, xs_ref, g_ref, wgu_ref, bgu_ref, wd_ref, bd_ref, o_ref, *, blk):
    i = pl.program_id(0)

    @pl.when(i * blk < used_ref[0])
    def _():
        xn = _rms_norm_rows(xs_ref[...], g_ref[...]).astype(jnp.bfloat16)
        gu = jnp.dot(xn, wgu_ref[0], preferred_element_type=jnp.float32) + bgu_ref[0]
        g = jnp.minimum(gu[:, :D_FF], SWIGLU_LIMIT)
        u = jnp.clip(gu[:, D_FF:], -SWIGLU_LIMIT, SWIGLU_LIMIT)
        act = (u + 1.0) * (g * jax.nn.sigmoid(SWIGLU_ALPHA * g))
        o_ref[...] = jnp.dot(act.astype(jnp.bfloat16), wd_ref[0], preferred_element_type=jnp.float32) + bd_ref[0]

    @pl.when(i * blk >= used_ref[0])
    def _():
        o_ref[...] = jnp.zeros_like(o_ref)


def _experts(xs, norm_g, blk_exp, n_used, w_gate_up, b_gate_up, w_down, b_down, *, blk):
    n_slots, d = xs.shape
    n_blocks = n_slots // blk
    bf = jnp.bfloat16
    return pl.pallas_call(
        functools.partial(_expert_kernel, blk=blk),
        grid_spec=pltpu.PrefetchScalarGridSpec(
            num_scalar_prefetch=2, grid=(n_blocks,),
            in_specs=[pl.BlockSpec((blk, d), lambda i, be, nu: (i, 0)),
                      pl.BlockSpec((1, d), lambda i, be, nu: (0, 0)),
                      pl.BlockSpec((1, d, 2 * D_FF), lambda i, be, nu: (be[i], 0, 0)),
                      pl.BlockSpec((1, 1, 2 * D_FF), lambda i, be, nu: (be[i], 0, 0)),
                      pl.BlockSpec((1, D_FF, d), lambda i, be, nu: (be[i], 0, 0)),
                      pl.BlockSpec((1, 1, d), lambda i, be, nu: (be[i], 0, 0))],
            out_specs=pl.BlockSpec((blk, d), lambda i, be, nu: (i, 0))),
        out_shape=jax.ShapeDtypeStruct((n_slots, d), jnp.float32),
        compiler_params=pltpu.CompilerParams(dimension_semantics=("arbitrary",), vmem_limit_bytes=VMEM_LIMIT_BYTES),
        name="moe_experts",
    )(blk_exp, n_used, xs, norm_g.reshape(1, d), w_gate_up.astype(bf), b_gate_up.reshape(N_EXPERTS, 1, 2 * D_FF),
      w_down.astype(bf), b_down.reshape(N_EXPERTS, 1, d))


def _row_copy(src_hbm, src_row, dst_ref, dst_row, sem):
    return pltpu.make_async_copy(src_hbm.at[pl.ds(src_row, 1)], dst_ref.at[pl.ds(dst_row, 1)], sem)


def _dispatch_kernel(dest_ref, x_ref, xs_in_hbm, xs_hbm, sem, *, tm):
    del xs_in_hbm

    def issue(t, carry):
        for k in range(TOP_K):
            _row_copy(x_ref, t, xs_hbm, dest_ref[k, t], sem).start()
        return carry

    lax.fori_loop(0, tm, issue, 0)

    def drain(t, carry):
        for k in range(TOP_K):
            _row_copy(x_ref, 0, xs_hbm, 0, sem).wait()
        return carry

    lax.fori_loop(0, tm, drain, 0)


def _dispatch(x2d, dest, n_slots, tm=128):
    n, d = x2d.shape
    return pl.pallas_call(
        functools.partial(_dispatch_kernel, tm=tm),
        grid=(n // tm,),
        in_specs=[pl.BlockSpec((TOP_K, tm), lambda i: (0, i), memory_space=pltpu.SMEM),
                  pl.BlockSpec((tm, d), lambda i: (i, 0)), pl.BlockSpec(memory_space=pl.ANY)],
        out_specs=pl.BlockSpec(memory_space=pl.ANY),
        out_shape=jax.ShapeDtypeStruct((n_slots, d), x2d.dtype),
        scratch_shapes=[pltpu.SemaphoreType.DMA(())],
        input_output_aliases={2: 0},
        compiler_params=pltpu.CompilerParams(dimension_semantics=("arbitrary",)),
        name="moe_dispatch",
    )(dest, x2d, jnp.zeros((n_slots, d), x2d.dtype))


def _combine_kernel(dest_ref, x_ref, gate_ref, ys_hbm, o_ref, buf_ref, sem, *, tm):
    def issue(t, carry):
        for k in range(TOP_K):
            _row_copy(ys_hbm, dest_ref[k, t], buf_ref.at[k], t, sem).start()
        return carry

    lax.fori_loop(0, tm, issue, 0)

    def drain(t, carry):
        for k in range(TOP_K):
            _row_copy(ys_hbm, 0, buf_ref.at[k], 0, sem).wait()
        return carry

    lax.fori_loop(0, tm, drain, 0)
    gate = gate_ref[...]
    acc = x_ref[...]
    for k in range(TOP_K):
        acc = acc + gate[:, k:k + 1] * buf_ref[k]
    o_ref[...] = acc


def _combine(x2d, ys, dest, gates_t, tm=128):
    n, d = x2d.shape
    return pl.pallas_call(
        functools.partial(_combine_kernel, tm=tm),
        grid=(n // tm,),
        in_specs=[pl.BlockSpec((TOP_K, tm), lambda i: (0, i), memory_space=pltpu.SMEM),
                  pl.BlockSpec((tm, d), lambda i: (i, 0)), pl.BlockSpec((tm, TOP_K), lambda i: (i, 0)),
                  pl.BlockSpec(memory_space=pl.ANY)],
        out_specs=pl.BlockSpec((tm, d), lambda i: (i, 0)),
        out_shape=jax.ShapeDtypeStruct((n, d), jnp.float32),
        scratch_shapes=[pltpu.VMEM((TOP_K, tm, d), jnp.float32), pltpu.SemaphoreType.DMA(())],
        compiler_params=pltpu.CompilerParams(dimension_semantics=("arbitrary",)),
        name="moe_combine",
    )(dest, x2d, gates_t, ys)


MOE_ROWS = 512


def _moe(x2d, norm_g, w_router, b_router, w_gate_up, b_gate_up, w_down, b_down, *, blk=MOE_ROWS, tm=512):
    n, d = x2d.shape
    top_e, gates, rank, counts = _route(x2d, norm_g, w_router, b_router, tm=tm)
    padded = (counts + blk - 1) // blk * blk
    pad_end = jnp.cumsum(padded)
    pad_start = pad_end - padded
    experts = jnp.arange(N_EXPERTS, dtype=jnp.int32)
    start_of = jnp.sum(jnp.where(top_e[..., None] == experts, pad_start.astype(jnp.int32), 0), axis=-1)
    dest = start_of + rank
    n_blocks = -(-(n * TOP_K + N_EXPERTS * (blk - 1)) // blk)
    n_slots = n_blocks * blk
    block_start = jnp.arange(n_blocks, dtype=jnp.int32) * blk
    blk_exp = jnp.minimum(jnp.sum(pad_end[None, :] <= block_start[:, None], axis=1), N_EXPERTS - 1).astype(jnp.int32)
    n_used = pad_end[-1:].astype(jnp.int32)
    xs = _dispatch(x2d, dest, n_slots)
    ys = _experts(xs, norm_g, blk_exp, n_used, w_gate_up, b_gate_up, w_down, b_down, blk=blk)
    return _combine(x2d, ys, dest, gates.T)


def _trunk_layer(x, pos, attend, conv_state, h0, norm1_g, w_in, q_norm_g, k_norm_g, conv_w, conv_b,
                 rg_w_a, rg_b_a, rg_w_i, rg_b_i, rg_lambda, w_attn_out, w_rnn_out, w_out, norm2_g,
                 w_router, b_router, w_gate_up, b_gate_up, w_down, b_down):
    B, T, _ = x.shape
    n_pad = (-D_IN) % 896
    w_pad = jnp.pad(w_in, ((0, 0), (0, n_pad))).astype(jnp.bfloat16)
    z = _norm_proj(x.reshape(B * T, D_MODEL), norm1_g, w_pad)[:, :D_IN].reshape(B, T, D_IN)
    q, k, v, iq, ik, iw, xr, xg, ga, gr = jnp.split(z, np.cumsum(IN_WIDTHS)[:-1].tolist(), axis=-1)
    q = _rope(_rms_norm(q.reshape(B, T, N_HEADS, HEAD_DIM), q_norm_g), pos)
    k = _rope(_rms_norm(k.reshape(B, T, KV_HEADS, HEAD_DIM), k_norm_g), pos)
    v = v.reshape(B, T, KV_HEADS, HEAD_DIM)
    iq = _rope(iq.reshape(B, T, IDX_HEADS, IDX_DIM), pos)
    ik = _rope(ik[:, :, None, :], pos)[:, :, 0]
    iw = iw * (IDX_HEADS ** -0.5 * IDX_DIM ** -0.5)
    rg = (conv_w, conv_b, rg_w_a, rg_b_a, rg_w_i, rg_b_i, rg_lambda)
    if attend is None:
        assert B == 1
        attn = _prompt_dsa_attention_t(q[0], k[0], v[0], iq[0], iw[0], ik[0])[None]
        rnn, h_new = _rglru_prompt(xr[0], xg[0], *rg)
        rnn, h_new = rnn[None], h_new[None]
        conv_new = xr[:, -(CONV_W - 1):]
    else:
        attn = attend(q, k, v, iq, iw, ik)
        rnn, h_new = _rglru_sample(xr, xg, conv_state, h0, *rg)
        conv_new = jnp.concatenate([conv_state, xr], axis=1)[:, -(CONV_W - 1):]
    merged = jax.nn.sigmoid(ga) * (attn @ w_attn_out) + jax.nn.sigmoid(gr) * (rnn @ w_rnn_out)
    x = x + merged @ w_out
    return x, (k, v, ik, conv_new, h_new)


def kernel(x_prompt, x_sample, cache_k, cache_v, cache_idx_k, state_conv, state_h, page_table,
           norm1_g, w_in, q_norm_g, k_norm_g, conv_w, conv_b, rg_w_a, rg_b_a, rg_w_i, rg_b_i, rg_lambda,
           w_attn_out, w_rnn_out, w_out, norm2_g, w_router, b_router, w_gate_up, b_gate_up, w_down, b_down):
    weights = (norm1_g, w_in, q_norm_g, k_norm_g, conv_w, conv_b, rg_w_a, rg_b_a, rg_w_i, rg_b_i, rg_lambda,
               w_attn_out, w_rnn_out, w_out, norm2_g, w_router, b_router, w_gate_up, b_gate_up, w_down, b_down)
    B, S = x_prompt.shape[:2]
    T = x_sample.shape[1]
    past = page_table.shape[1] * PAGE_SIZE
    pos_p = jnp.arange(S, dtype=jnp.int32)
    pos_s = past + jnp.arange(T, dtype=jnp.int32)
    conv0 = jnp.zeros((B, CONV_W - 1, D_RNN), x_prompt.dtype)
    h0 = jnp.zeros((B, D_RNN), jnp.float32)
    wl = [w[0] for w in weights]
    yp, st_p = _trunk_layer(x_prompt, pos_p, None, conv0, h0, *wl)
    attend_s = functools.partial(_sample_dsa_attention, cache_k=cache_k[0], cache_v=cache_v[0],
                                 cache_idx_k=cache_idx_k[0], page_table=page_table)
    ys, st_s = _trunk_layer(x_sample, pos_s, attend_s, state_conv[0], state_h[0], *wl)
    x_all = jnp.concatenate([yp.reshape(-1, D_MODEL), ys.reshape(-1, D_MODEL)], axis=0)
    x_all = _moe(x_all, norm2_g[0], w_router[0], b_router[0], w_gate_up[0], b_gate_up[0], w_down[0], b_down[0])
    yp = x_all[:B * S].reshape(x_prompt.shape)
    ys = x_all[B * S:].reshape(x_sample.shape)
    k_p, v_p, ik_p, conv_p, h_p = [a[None] for a in st_p]
    k_s, v_s, ik_s, conv_s, h_s = [a[None] for a in st_s]
    return (yp, ys, k_p, v_p, ik_p, conv_p, h_p, k_s, v_s, ik_s, conv_s, h_s)
```

```python
import functools
import math

import jax
import jax.numpy as jnp
import numpy as np
from jax import lax
from jax.experimental import pallas as pl
from jax.experimental.pallas import tpu as pltpu

D_MODEL = 1024
PAGE_SIZE = 128
N_HEADS = 16
HEAD_DIM = 64
KV_HEADS = 4
IDX_HEADS = 8
IDX_DIM = 64
IDX_TOPK_MAX = 256
Q_BLOCK = 128
ROPE_THETA = 10000.0
D_RNN = D_MODEL
RG_BLOCKS = 4
RG_BLOCK_W = D_RNN // RG_BLOCKS
CONV_W = 4
RG_C = 8.0
N_EXPERTS = 32
TOP_K = 4
D_FF = D_MODEL
SWIGLU_LIMIT = 7.0
SWIGLU_ALPHA = 1.702
MOE_BLOCK = 128
EPS = 1e-6

Q_W = N_HEADS * HEAD_DIM
KV_W = KV_HEADS * HEAD_DIM
IQ_W = IDX_HEADS * IDX_DIM
IN_NAMES = ("q", "k", "v", "iq", "ik", "iw", "xr", "xg", "ga", "gr")
IN_WIDTHS = (Q_W, KV_W, KV_W, IQ_W, IDX_DIM, IDX_HEADS, D_RNN, D_RNN, D_MODEL, D_MODEL)

LANE = 128
SUBLANE = 8
VMEM_LIMIT_BYTES = 48 * 1024 * 1024

Z_ORDER = ("q", "k", "v", "iq", "xr", "xg", "ga", "gr")
Z_COLS = {}
_col = 0
for _name in Z_ORDER:
    Z_COLS[_name] = _col
    _col += IN_WIDTHS[IN_NAMES.index(_name)]
Z_COLS["ikw"] = _col
Z_W = _col + LANE
PROJ_TN = 896


def _permuted_w_in(w_in):
    offs = np.concatenate([[0], np.cumsum(IN_WIDTHS)])
    seg = {name: w_in[:, offs[j]:offs[j + 1]] for j, name in enumerate(IN_NAMES)}
    pad = jnp.zeros((w_in.shape[0], LANE - IDX_DIM - IDX_HEADS), w_in.dtype)
    return jnp.concatenate([seg[name] for name in Z_ORDER] + [seg["ik"], seg["iw"], pad], axis=1).astype(jnp.bfloat16)


def _norm_proj_kernel(x_ref, g_ref, w_ref, o_ref):
    x = x_ref[...]
    y = x * lax.rsqrt(jnp.mean(x * x, axis=-1, keepdims=True) + EPS) * g_ref[...]
    o_ref[...] = jnp.dot(y.astype(jnp.bfloat16), w_ref[...], preferred_element_type=jnp.float32)


def _norm_proj(x2d, g, w_bf16, tm=512, tn=PROJ_TN):
    m, d = x2d.shape
    n = w_bf16.shape[1]
    return pl.pallas_call(
        _norm_proj_kernel,
        grid=(m // tm, n // tn),
        in_specs=[
            pl.BlockSpec((tm, d), lambda i, j: (i, 0)),
            pl.BlockSpec((1, d), lambda i, j: (0, 0)),
            pl.BlockSpec((d, tn), lambda i, j: (0, j)),
        ],
        out_specs=pl.BlockSpec((tm, tn), lambda i, j: (i, j)),
        out_shape=jax.ShapeDtypeStruct((m, n), jnp.float32),
        name="norm_proj",
    )(x2d, g.reshape(1, d), w_bf16)


INT_MIN = -(2 ** 31)
INT_MAX = 2 ** 31 - 1
MASK_BIAS = -1e30
NEG_INF_BITS_MASK = 0x7FFFFFFF


def _ordered_key(x):
    bits = pltpu.bitcast(x, jnp.int32)
    return bits ^ ((bits >> 31) & NEG_INF_BITS_MASK)


def _topk_threshold(key_ref, cut_ref, nk, *, rows, kb, topk, n_idx_bits):
    n_lane_tiles = kb // LANE
    lane = lax.broadcasted_iota(jnp.int32, (rows, kb), 1)

    def count(pred_of_chunk):
        def body(c, cnt):
            p = pred_of_chunk(c, key_ref[c])
            for j in range(n_lane_tiles):
                cnt = cnt + jnp.where(p[:, j * LANE:(j + 1) * LANE], 1, 0)
            return cnt
        cnt = lax.fori_loop(0, nk, body, jnp.zeros((rows, LANE), jnp.int32))
        return jnp.sum(cnt, axis=-1, keepdims=True)

    def bit_step(b, carry):
        thr, n_ge = carry
        cand = thr ^ (jnp.int32(1) << (31 - b))
        cand_b = jnp.broadcast_to(cand, (rows, kb))
        cnt = count(lambda c, key: key >= cand_b)
        ok = cnt >= topk
        return jnp.where(ok, cand, thr), jnp.where(ok, cnt, n_ge)

    thr0 = jnp.full((rows, 1), INT_MIN, jnp.int32)
    n0 = jnp.full((rows, 1), 0, jnp.int32) + nk * kb
    thr, n_ge = lax.fori_loop(0, 32, bit_step, (thr0, n0))
    thr_b = jnp.broadcast_to(thr, (rows, kb))

    cut_ref[...] = jnp.full((rows, LANE), INT_MAX, jnp.int32)

    @pl.when(jnp.max(n_ge) > topk)
    def _():
        n_eq = count(lambda c, key: key == thr_b)
        need = topk - (n_ge - n_eq)

        def idx_step(b, lo):
            step = jnp.int32(1) << (n_idx_bits - 1 - b)
            mid_b = jnp.broadcast_to(lo + step - 1, (rows, kb))
            f = count(lambda c, key: (key == thr_b) & (c * kb + lane <= mid_b))
            return jnp.where(f < need, lo + step, lo)

        lo = lax.fori_loop(0, n_idx_bits, idx_step, jnp.zeros((rows, 1), jnp.int32))
        cut = jnp.where(n_ge > topk, lo, INT_MAX)
        cut_ref[...] = jnp.broadcast_to(cut, (rows, LANE))

    return thr


def _topk_threshold_t(key_ref, cut_ref, nk, *, kb, cols, topk, n_idx_bits):
    acc_rows = 8 * SUBLANE
    assert kb % acc_rows == 0
    pos0 = lax.broadcasted_iota(jnp.int32, (kb, cols), 0)

    def count(pred_of_chunk):
        def body(c, cnt):
            p = pred_of_chunk(c, key_ref[c])
            for j in range(kb // acc_rows):
                cnt = cnt + jnp.where(p[j * acc_rows:(j + 1) * acc_rows, :], 1, 0)
            return cnt
        cnt = lax.fori_loop(0, nk, body, jnp.zeros((acc_rows, cols), jnp.int32))
        return jnp.sum(cnt, axis=0, keepdims=True)

    def bit_cond(carry):
        b, _, n_ge = carry
        return jnp.logical_and(b < 32, jnp.max(jnp.abs(n_ge - topk)) > 0)

    def bit_step(carry):
        b, thr, n_ge = carry
        cand = thr ^ (jnp.int32(1) << (31 - b))
        cand_b = jnp.broadcast_to(cand, (kb, cols))
        cnt = count(lambda c, key: key >= cand_b)
        ok = cnt >= topk
        return b + 1, jnp.where(ok, cand, thr), jnp.where(ok, cnt, n_ge)

    thr0 = jnp.full((1, cols), INT_MIN, jnp.int32)
    n0 = jnp.full((1, cols), 0, jnp.int32) + nk * kb
    _, thr, n_ge = lax.while_loop(bit_cond, bit_step, (jnp.int32(0), thr0, n0))
    thr_b = jnp.broadcast_to(thr, (kb, cols))

    cut_ref[...] = jnp.full((SUBLANE, cols), INT_MAX, jnp.int32)

    @pl.when(jnp.max(n_ge) > topk)
    def _():
        n_eq = count(lambda c, key: key == thr_b)
        need = topk - (n_ge - n_eq)

        def idx_step(b, lo):
            step = jnp.int32(1) << (n_idx_bits - 1 - b)
            mid_b = jnp.broadcast_to(lo + step - 1, (kb, cols))
            f = count(lambda c, key: (key == thr_b) & (c * kb + pos0 <= mid_b))
            return jnp.where(f < need, lo + step, lo)

        lo = lax.fori_loop(0, n_idx_bits, idx_step, jnp.zeros((1, cols), jnp.int32))
        cut = jnp.where(n_ge > topk, lo, INT_MAX)
        cut_ref[...] = jnp.broadcast_to(cut, (SUBLANE, cols))

    return thr


UNDERFLOW_GUARD = 1e-30
LOG2_E = math.log2(math.e)


def _dsa_t_kernel(iq_ref, iw_ref, q_ref, mb_ref, ik_ref, k_ref, vt_ref, o_ref,
                  key_ref, cut_ref, m_ref, l_ref, acc_ref, *, qb, kb, topk, n_idx_bits):
    i = pl.program_id(0)
    nk = ((i + 1) * qb + kb - 1) // kb
    rep = N_HEADS // KV_HEADS
    kpos0 = lax.broadcasted_iota(jnp.int32, (kb, qb), 0)
    qpos = i * qb + lax.broadcasted_iota(jnp.int32, (kb, qb), 1)
    iw = iw_ref[0]

    def score_chunk(c, carry):
        ikc = ik_ref[c]
        acc = jnp.zeros((kb, qb), jnp.float32)
        for hp in range(IDX_HEADS // 2):
            s2 = jnp.dot(ikc, iq_ref[0, hp], preferred_element_type=jnp.float32)
            for hh in range(2):
                h = 2 * hp + hh
                acc = acc + iw[h:h + 1, :] * jnp.maximum(s2[:, hh * qb:(hh + 1) * qb], 0.0)
        acc = jnp.where(c * kb + kpos0 <= qpos, acc, -jnp.inf)
        key_ref[c] = _ordered_key(acc)
        return carry

    lax.fori_loop(0, nk, score_chunk, 0)

    thr = _topk_threshold_t(key_ref, cut_ref, nk, kb=kb, cols=qb, topk=topk, n_idx_bits=n_idx_bits)
    thr_b = jnp.broadcast_to(thr, (kb, qb))
    cut_b = jnp.broadcast_to(cut_ref[0:1, :], (kb, qb))

    def mask_bias(c):
        key = key_ref[c]
        kpos = c * kb + kpos0
        sel = (key > thr_b) | ((key == thr_b) & (kpos <= cut_b))
        sel = sel & (kpos <= qpos)
        return jnp.tile(jnp.where(sel, 0.0, MASK_BIAS), (1, rep))

    l_ref[...] = jnp.zeros(l_ref.shape, jnp.float32)
    acc_ref[...] = jnp.zeros(acc_ref.shape, jnp.float32)

    def attend_chunk(c, carry):
        bias = mask_bias(c)
        kc = k_ref[c]
        for g in range(KV_HEADS):
            s = jnp.dot(kc, q_ref[0, g], preferred_element_type=jnp.float32) + (bias - mb_ref[0, g])
            p = jnp.exp2(s)
            l_ref[g] = l_ref[g] + jnp.sum(p, axis=0, keepdims=True)
            acc_ref[g] = acc_ref[g] + jnp.dot(vt_ref[c, g], p.astype(jnp.bfloat16), preferred_element_type=jnp.float32)
        return carry

    lax.fori_loop(0, nk, attend_chunk, 0)

    @pl.when(jnp.min(l_ref[...]) < UNDERFLOW_GUARD)
    def _():
        m_ref[...] = jnp.full(m_ref.shape, MASK_BIAS, jnp.float32)
        l_ref[...] = jnp.zeros(l_ref.shape, jnp.float32)
        acc_ref[...] = jnp.zeros(acc_ref.shape, jnp.float32)

        def attend_chunk_running_max(c, carry):
            bias = mask_bias(c)
            kc = k_ref[c]
            for g in range(KV_HEADS):
                s = jnp.dot(kc, q_ref[0, g], preferred_element_type=jnp.float32) + bias
                m_prev = m_ref[g]
                m_new = jnp.maximum(m_prev, jnp.max(s, axis=0, keepdims=True))
                alpha = jnp.exp2(m_prev - m_new)
                p = jnp.exp2(s - m_new)
                l_ref[g] = alpha * l_ref[g] + jnp.sum(p, axis=0, keepdims=True)
                pv = jnp.dot(vt_ref[c, g], p.astype(jnp.bfloat16), preferred_element_type=jnp.float32)
                acc_ref[g] = acc_ref[g] * alpha + pv
                m_ref[g] = m_new
            return carry

        lax.fori_loop(0, nk, attend_chunk_running_max, 0)

    for g in range(KV_HEADS):
        o_ref[0, g] = acc_ref[g] / l_ref[g]


def _prompt_dsa_attention_t(q, k, v, iq, iw, ik, *, qb=128, kb=1024):
    s_len = q.shape[0]
    topk = min(IDX_TOPK_MAX, s_len // 4)
    nqb, nkc = s_len // qb, s_len // kb
    rep = N_HEADS // KV_HEADS
    bf = jnp.bfloat16
    q_s = (q * (HEAD_DIM ** -0.5 * LOG2_E)).astype(bf)
    k_bf = k.astype(bf)
    q_norm = jnp.sqrt(jnp.sum(jnp.square(q_s.astype(jnp.float32)), axis=-1))
    k_norm = jnp.sqrt(jnp.max(jnp.sum(jnp.square(k_bf.astype(jnp.float32)), axis=-1), axis=0))
    bound = q_norm.reshape(nqb, qb, KV_HEADS, rep) * k_norm[None, None, :, None]
    bound = bound.transpose(0, 2, 3, 1).reshape(nqb, KV_HEADS, 1, rep * qb)
    qs = q_s.reshape(nqb, qb, KV_HEADS, rep, HEAD_DIM).transpose(0, 2, 4, 3, 1)
    q_bd = qs[:, :, None, :, :, :] * jnp.eye(KV_HEADS, dtype=bf)[None, :, :, None, None, None]
    q_bd = q_bd.reshape(nqb, KV_HEADS, KV_W, rep * qb)
    iq_t = iq.astype(bf).reshape(nqb, qb, IDX_HEADS // 2, 2, IDX_DIM).transpose(0, 2, 4, 3, 1)
    iq_t = iq_t.reshape(nqb, IDX_HEADS // 2, IDX_DIM, 2 * qb)
    iw_t = iw.reshape(nqb, qb, IDX_HEADS).transpose(0, 2, 1)
    ik_c = ik.astype(bf).reshape(nkc, kb, IDX_DIM)
    k_c = k_bf.reshape(nkc, kb, KV_W)
    v_t = v.astype(bf).reshape(nkc, kb, KV_HEADS, HEAD_DIM).transpose(0, 2, 3, 1)
    n_idx_bits = max(1, (s_len - 1).bit_length())
    body = functools.partial(_dsa_t_kernel, qb=qb, kb=kb, topk=topk, n_idx_bits=n_idx_bits)
    whole = lambda shape: pl.BlockSpec(shape, lambda i: (0,) * len(shape), pipeline_mode=pl.Buffered(1))
    out = pl.pallas_call(
        body,
        grid=(nqb,),
        in_specs=[
            pl.BlockSpec((1, IDX_HEADS // 2, IDX_DIM, 2 * qb), lambda i: (i, 0, 0, 0)),
            pl.BlockSpec((1, IDX_HEADS, qb), lambda i: (i, 0, 0)),
            pl.BlockSpec((1, KV_HEADS, KV_W, rep * qb), lambda i: (i, 0, 0, 0)),
            pl.BlockSpec((1, KV_HEADS, 1, rep * qb), lambda i: (i, 0, 0, 0)),
            whole((nkc, kb, IDX_DIM)),
            whole((nkc, kb, KV_W)),
            whole((nkc, KV_HEADS, HEAD_DIM, kb)),
        ],
        out_specs=pl.BlockSpec((1, KV_HEADS, HEAD_DIM, rep * qb), lambda i: (i, 0, 0, 0)),
        out_shape=jax.ShapeDtypeStruct((nqb, KV_HEADS, HEAD_DIM, rep * qb), jnp.float32),
        scratch_shapes=[
            pltpu.VMEM((nkc, kb, qb), jnp.int32),
            pltpu.VMEM((SUBLANE, qb), jnp.int32),
            pltpu.VMEM((KV_HEADS, 1, rep * qb), jnp.float32),
            pltpu.VMEM((KV_HEADS, 1, rep * qb), jnp.float32),
            pltpu.VMEM((KV_HEADS, HEAD_DIM, rep * qb), jnp.float32),
        ],
        compiler_params=pltpu.CompilerParams(dimension_semantics=("arbitrary",), vmem_limit_bytes=VMEM_LIMIT_BYTES),
        name="prompt_dsa_attention",
    )(iq_t, iw_t, q_bd, bound, ik_c, k_c, v_t)
    out = out.reshape(nqb, KV_HEADS, HEAD_DIM, rep, qb).transpose(0, 4, 1, 3, 2)
    return out.reshape(s_len, N_HEADS * HEAD_DIM)


def _sample_index_kernel(pt_ref, iq_ref, iw_ref, iknew_ref, *rest, pp, n_pages, topk, n_idx_bits):
    page_refs, o_ref, (key_ref, wide_ref, wb_ref, cut_ref) = rest[:pp], rest[pp], rest[pp + 1:]
    pg = pl.program_id(1)
    t_new = o_ref.shape[1]
    step_keys = pp * PAGE_SIZE
    n_keys = (n_pages + 1) * PAGE_SIZE

    @pl.when(pg == 0)
    def _():
        wb_ref[...] = jnp.broadcast_to(iw_ref[0], wb_ref.shape)

    iq = iq_ref[0]

    def scores(ik_t):
        s = jnp.dot(iq, ik_t.astype(jnp.bfloat16), preferred_element_type=jnp.float32)
        s = jnp.tile(wb_ref[...], (1, ik_t.shape[1] // LANE)) * jnp.maximum(s, 0.0)
        acc = jnp.zeros((t_new, ik_t.shape[1]), jnp.float32)
        for h in range(IDX_HEADS):
            acc = acc + s[h * t_new:(h + 1) * t_new, :]
        return acc

    key_ref[pg] = _ordered_key(scores(jnp.concatenate([r[0] for r in page_refs], axis=1)))

    @pl.when(pg == pl.num_programs(1) - 1)
    def _():
        for c in range(n_pages // pp):
            wide_ref[0, :, c * step_keys:(c + 1) * step_keys] = key_ref[c]
        row = lax.broadcasted_iota(jnp.int32, (t_new, PAGE_SIZE), 0)
        lane = lax.broadcasted_iota(jnp.int32, (t_new, PAGE_SIZE), 1)
        s_new = jnp.where(lane <= row, scores(iknew_ref[0]), -jnp.inf)
        wide_ref[0, :, n_pages * PAGE_SIZE:] = _ordered_key(s_new)
        thr = _topk_threshold(wide_ref, cut_ref, 1, rows=t_new, kb=n_keys, topk=topk, n_idx_bits=n_idx_bits)
        key = wide_ref[0]
        kpos = lax.broadcasted_iota(jnp.int32, (t_new, n_keys), 1)
        qpos = n_pages * PAGE_SIZE + lax.broadcasted_iota(jnp.int32, (t_new, n_keys), 0)
        thr_b = jnp.broadcast_to(thr, (t_new, n_keys))
        cut_b = jnp.tile(cut_ref[...], (1, n_pages + 1))
        sel = (key > thr_b) | ((key == thr_b) & (kpos <= cut_b))
        o_ref[0] = jnp.where(sel & (kpos <= qpos), 0.0, MASK_BIAS)


def _sample_attend_kernel(pt_ref, q_ref, bias_ref, biasnew_ref, knew_ref, vnew_ref, *rest, pp, n_pages):
    k_refs, v_refs, o_ref, (m_ref, l_ref, acc_ref) = rest[:pp], rest[pp:2 * pp], rest[2 * pp], rest[2 * pp + 1:]
    pg = pl.program_id(1)
    n_rows = q_ref.shape[1]
    t_new = bias_ref.shape[1]

    @pl.when(pg == 0)
    def _():
        m_ref[...] = jnp.full(m_ref.shape, MASK_BIAS, jnp.float32)
        l_ref[...] = jnp.zeros(l_ref.shape, jnp.float32)
        acc_ref[...] = jnp.zeros(acc_ref.shape, jnp.float32)

    q = q_ref[0]

    def step(bias, k_t, v_t):
        n_keys = k_t.shape[1]
        s = jnp.dot(q, k_t.astype(jnp.bfloat16), preferred_element_type=jnp.float32)
        s = s + jnp.tile(bias, (n_rows // t_new, 1))
        m_prev = m_ref[...]
        m_new = jnp.maximum(m_prev, jnp.max(s, axis=-1, keepdims=True))
        alpha = jnp.exp(m_prev - m_new)
        p = jnp.exp(s - jnp.tile(m_new, (1, n_keys // LANE)))
        l_ref[...] = alpha * l_ref[...] + jnp.sum(p, axis=-1, keepdims=True)
        pv = lax.dot_general(p.astype(jnp.bfloat16), v_t.astype(jnp.bfloat16), (((1,), (1,)), ((), ())),
                             preferred_element_type=jnp.float32)
        acc_ref[...] = acc_ref[...] * jnp.tile(alpha, (1, KV_W // LANE)) + pv
        m_ref[...] = m_new

    step(bias_ref[0], jnp.concatenate([r[0] for r in k_refs], axis=1), jnp.concatenate([r[0] for r in v_refs], axis=1))

    @pl.when(pg == pl.num_programs(1) - 1)
    def _():
        step(biasnew_ref[0], knew_ref[0], vnew_ref[0])
        acc = acc_ref[...]
        row_head = lax.broadcasted_iota(jnp.int32, acc.shape, 0) // (n_rows // KV_HEADS)
        col_head = lax.broadcasted_iota(jnp.int32, acc.shape, 1) // HEAD_DIM
        own = jnp.where(row_head == col_head, acc, 0.0)
        folded = own[:, :LANE] + own[:, LANE:]
        folded = folded + pltpu.roll(folded, HEAD_DIM, axis=1)
        o_ref[0] = folded / l_ref[...]


def _sample_dsa_attention(q, k, v, iq, iw, ik, cache_k, cache_v, cache_idx_k, page_table, *, pp_index=64, pp=32):
    b, t = q.shape[:2]
    n_pages = page_table.shape[1]
    n_pool = cache_k.shape[0]
    past = n_pages * PAGE_SIZE
    topk = min(IDX_TOPK_MAX, (past + t) // 4)
    n_idx_bits = max(1, (past + PAGE_SIZE - 1).bit_length())
    pp_index, pp = min(pp_index, n_pages), min(pp, n_pages)
    assert n_pages % pp == 0 and n_pages % pp_index == 0 and t == SUBLANE and KV_W == 2 * LANE
    npg = n_pages // pp
    n_keys = past + PAGE_SIZE
    bf = jnp.bfloat16
    rep = N_HEADS // KV_HEADS
    n_rows = N_HEADS * t

    iq_r = iq.astype(bf).transpose(0, 2, 1, 3).reshape(b, IDX_HEADS * t, IDX_DIM)
    iw_r = iw.transpose(0, 2, 1).reshape(b, IDX_HEADS * t, 1)
    new_page = lambda a: jnp.pad(jnp.swapaxes(a, 1, 2), ((0, 0), (0, 0), (0, PAGE_SIZE - t)))
    pages_t = lambda c: jnp.swapaxes(c.reshape(n_pool, PAGE_SIZE, -1), 1, 2)
    ik_new = new_page(ik)
    k_new = new_page(k.reshape(b, t, KV_W))
    v_new = new_page(v.reshape(b, t, KV_W))
    qs = (q * HEAD_DIM ** -0.5).astype(bf).reshape(b, t, KV_HEADS, rep, HEAD_DIM).transpose(0, 2, 3, 1, 4)
    q_bd = (qs[:, :, :, :, None, :] * jnp.eye(KV_HEADS, dtype=bf)[None, :, None, None, :, None])
    q_bd = q_bd.reshape(b, n_rows, KV_W)

    page_map = lambda per_step, j: (lambda bi, pg, pt: (pt[bi, pg * per_step + j], 0, 0))
    per_seq = lambda shape: pl.BlockSpec((1,) + shape, lambda bi, pg, pt: (bi,) + (0,) * len(shape))
    params = pltpu.CompilerParams(dimension_semantics=("arbitrary", "arbitrary"), vmem_limit_bytes=VMEM_LIMIT_BYTES)

    bias = pl.pallas_call(
        functools.partial(_sample_index_kernel, pp=pp_index, n_pages=n_pages, topk=topk, n_idx_bits=n_idx_bits),
        grid_spec=pltpu.PrefetchScalarGridSpec(
            num_scalar_prefetch=1, grid=(b, n_pages // pp_index),
            in_specs=[per_seq((IDX_HEADS * t, IDX_DIM)), per_seq((IDX_HEADS * t, 1)), per_seq((IDX_DIM, PAGE_SIZE))]
            + [pl.BlockSpec((1, IDX_DIM, PAGE_SIZE), page_map(pp_index, j)) for j in range(pp_index)],
            out_specs=per_seq((t, n_keys)),
            scratch_shapes=[pltpu.VMEM((n_pages // pp_index, t, pp_index * PAGE_SIZE), jnp.int32),
                            pltpu.VMEM((1, t, n_keys), jnp.int32),
                            pltpu.VMEM((IDX_HEADS * t, PAGE_SIZE), jnp.float32),
                            pltpu.VMEM((t, LANE), jnp.int32)]),
        out_shape=jax.ShapeDtypeStruct((b, t, n_keys), jnp.float32),
        compiler_params=params,
        name="sample_index",
    )(page_table, iq_r, iw_r, ik_new, *([pages_t(cache_idx_k)] * pp_index))

    ck = pages_t(cache_k)
    cv = pages_t(cache_v)
    out = pl.pallas_call(
        functools.partial(_sample_attend_kernel, pp=pp, n_pages=n_pages),
        grid_spec=pltpu.PrefetchScalarGridSpec(
            num_scalar_prefetch=1, grid=(b, npg),
            in_specs=[per_seq((n_rows, KV_W)),
                      pl.BlockSpec((1, t, pp * PAGE_SIZE), lambda bi, pg, pt: (bi, 0, pg)),
                      pl.BlockSpec((1, t, PAGE_SIZE), lambda bi, pg, pt: (bi, 0, n_pages)),
                      per_seq((KV_W, PAGE_SIZE)), per_seq((KV_W, PAGE_SIZE))]
            + [pl.BlockSpec((1, KV_W, PAGE_SIZE), page_map(pp, j)) for j in range(pp)] * 2,
            out_specs=per_seq((n_rows, LANE)),
            scratch_shapes=[pltpu.VMEM((n_rows, LANE), jnp.float32), pltpu.VMEM((n_rows, LANE), jnp.float32),
                            pltpu.VMEM((n_rows, KV_W), jnp.float32)]),
        out_shape=jax.ShapeDtypeStruct((b, n_rows, LANE), jnp.float32),
        compiler_params=params,
        name="sample_attend",
    )(page_table, q_bd, bias, bias, k_new, v_new, *([ck] * pp), *([cv] * pp))
    out = out[:, :, :HEAD_DIM].reshape(b, N_HEADS, t, HEAD_DIM).transpose(0, 2, 1, 3)
    return out.reshape(b, t, N_HEADS * HEAD_DIM)


GELU_C = math.sqrt(2.0 / math.pi)


def _rglru_kernel(x_ref, xg_ref, prev_ref, h0_ref, cw_ref, cb_ref, wa_ref, ba_ref, wi_ref, bi_ref, lam_ref,
                  o_ref, hl_ref, conv_ref, a_ref, u_ref, xprev_ref, h_ref, *, rows, per_group_state):
    i = pl.program_id(0)
    n_groups = rows // SUBLANE
    d_rnn = x_ref.shape[1]
    r8 = lax.broadcasted_iota(jnp.int32, (SUBLANE, d_rnn), 0)
    grp = lambda g: pl.ds(pl.multiple_of(g * SUBLANE, SUBLANE), SUBLANE)

    if not per_group_state:
        @pl.when(i == 0)
        def _():
            xprev_ref[...] = jnp.zeros_like(xprev_ref)
            h_ref[...] = jnp.zeros_like(h_ref)

    cw = cw_ref[...]
    cb = cb_ref[...]

    def conv_group(g, prev):
        x8 = x_ref[grp(g), :]
        if per_group_state:
            prev = prev_ref[grp(g), :]
        out = cb + x8 * cw[CONV_W - 1:CONV_W, :]
        for d in range(1, CONV_W):
            shifted = jnp.where(r8 < d, pltpu.roll(prev, d, axis=0), pltpu.roll(x8, d, axis=0))
            out = out + shifted * cw[CONV_W - 1 - d:CONV_W - d, :]
        conv_ref[grp(g), :] = out
        return x8

    zeros8 = jnp.zeros((SUBLANE, d_rnn), jnp.float32)
    xlast = lax.fori_loop(0, n_groups, conv_group, zeros8 if per_group_state else xprev_ref[...])
    if not per_group_state:
        xprev_ref[...] = xlast

    xc = conv_ref[...]
    xb = xc.astype(jnp.bfloat16)
    r_parts, i_parts = [], []
    for n in range(RG_BLOCKS):
        xn = xb[:, n * RG_BLOCK_W:(n + 1) * RG_BLOCK_W]
        r_parts.append(jnp.dot(xn, wa_ref[n], preferred_element_type=jnp.float32))
        i_parts.append(jnp.dot(xn, wi_ref[n], preferred_element_type=jnp.float32))
    r = jax.nn.sigmoid(jnp.concatenate(r_parts, axis=-1) + ba_ref[...])
    ig = jax.nn.sigmoid(jnp.concatenate(i_parts, axis=-1) + bi_ref[...])
    lam = lam_ref[...]
    log_sig_lam = -(jnp.maximum(-lam, 0.0) + jnp.log1p(jnp.exp(-jnp.abs(lam))))
    log_a = RG_C * r * log_sig_lam
    a = jnp.exp(log_a)
    a_ref[...] = a
    u_ref[...] = jnp.sqrt(-jnp.tanh(log_a) * (a * a + 1.0)) * (ig * xc)

    def scan_group(g, hprev):
        a = a_ref[grp(g), :]
        u = u_ref[grp(g), :]
        if per_group_state:
            hprev = jnp.broadcast_to(h0_ref[pl.ds(g, 1), :], (SUBLANE, d_rnn))
        for d in (1, 2, 4):
            u = jnp.where(r8 >= d, a * pltpu.roll(u, d, axis=0) + u, u)
            a = jnp.where(r8 >= d, a * pltpu.roll(a, d, axis=0), a)
        h = a * hprev + u
        u_ref[grp(g), :] = h
        hlast = jnp.broadcast_to(h[SUBLANE - 1:SUBLANE, :], (SUBLANE, d_rnn))
        if per_group_state:
            hl_ref[pl.ds(g, 1), :] = h[SUBLANE - 1:SUBLANE, :]
        return hlast

    hlast = lax.fori_loop(0, n_groups, scan_group, zeros8 if per_group_state else h_ref[...])
    if not per_group_state:
        h_ref[...] = hlast
        hl_ref[...] = hlast

    xg = xg_ref[...]
    gelu = 0.5 * xg * (1.0 + jnp.tanh(GELU_C * (xg + 0.044715 * (xg * xg * xg))))
    o_ref[...] = u_ref[...] * gelu


def _rglru(z, row0, n_rows, prev, h0, conv_w, conv_b, w_a, b_a, w_i, b_i, lam, *, per_group_state, tile_rows):
    d = D_RNN
    assert row0 % tile_rows == 0 and n_rows % tile_rows == 0
    row = lambda a: a.reshape(1, d)
    const = lambda shape: pl.BlockSpec(shape, lambda i: (0,) * len(shape))
    n_hl = n_rows // SUBLANE if per_group_state else SUBLANE
    body = functools.partial(_rglru_kernel, rows=tile_rows, per_group_state=per_group_state)
    tile = pl.BlockSpec((tile_rows, d), lambda i: (i, 0))
    z_col = lambda name: pl.BlockSpec((tile_rows, d), lambda i, c=Z_COLS[name] // d: (row0 // tile_rows + i, c))
    if per_group_state:
        assert tile_rows == n_rows
    return pl.pallas_call(
        body,
        grid=(n_rows // tile_rows,),
        in_specs=[z_col("xr"), z_col("xg"), const(prev.shape), const(h0.shape), const((CONV_W, d)), const((1, d)),
                  const(w_a.shape), const((1, d)), const(w_i.shape), const((1, d)), const((1, d))],
        out_specs=[tile, const((n_hl, d))],
        out_shape=[jax.ShapeDtypeStruct((n_rows, d), jnp.float32), jax.ShapeDtypeStruct((n_hl, d), jnp.float32)],
        scratch_shapes=[pltpu.VMEM((tile_rows, d), jnp.float32), pltpu.VMEM((tile_rows, d), jnp.float32),
                        pltpu.VMEM((tile_rows, d), jnp.float32), pltpu.VMEM((SUBLANE, d), jnp.float32),
                        pltpu.VMEM((SUBLANE, d), jnp.float32)],
        compiler_params=pltpu.CompilerParams(dimension_semantics=("arbitrary",), vmem_limit_bytes=VMEM_LIMIT_BYTES),
        name="rglru_sample" if per_group_state else "rglru_prompt",
    )(z, z, prev, h0, conv_w, row(conv_b), w_a.astype(jnp.bfloat16), row(b_a), w_i.astype(jnp.bfloat16), row(b_i), row(lam))


def _rglru_prompt(z, row0, n_rows, conv_w, conv_b, w_a, b_a, w_i, b_i, lam, tile_rows=512):
    dummy = jnp.zeros((SUBLANE, D_RNN), jnp.float32)
    rnn, hl = _rglru(z, row0, n_rows, dummy, dummy, conv_w, conv_b, w_a, b_a, w_i, b_i, lam,
                     per_group_state=False, tile_rows=tile_rows)
    return rnn, hl[0]


def _rglru_sample(z, row0, state_conv, state_h, conv_w, conv_b, w_a, b_a, w_i, b_i, lam):
    b, d = state_h.shape
    prev = jnp.concatenate([jnp.zeros((b, SUBLANE - (CONV_W - 1), d), jnp.float32), state_conv], axis=1)
    return _rglru(z, row0, b * SUBLANE, prev.reshape(b * SUBLANE, d), state_h,
                  conv_w, conv_b, w_a, b_a, w_i, b_i, lam, per_group_state=True, tile_rows=b * SUBLANE)


def _rope_tables(pos):
    half = HEAD_DIM // 2
    inv_freq = ROPE_THETA ** (-jnp.arange(half, dtype=jnp.float32) / half)
    ang = pos.astype(jnp.float32)[:, None] * inv_freq[None, :]
    cos, sin = jnp.cos(ang), jnp.sin(ang)
    cos2 = jnp.concatenate([cos, cos], axis=-1)
    sin2 = jnp.concatenate([-sin, sin], axis=-1)
    return jnp.tile(cos2, (1, LANE // HEAD_DIM)), jnp.tile(sin2, (1, LANE // HEAD_DIM))


def _rotate(x, cos2, sin2):
    width = x.shape[1]
    half = HEAD_DIM // 2
    lane = lax.broadcasted_iota(jnp.int32, x.shape, 1)
    partner = jnp.where((lane & half) == 0, pltpu.roll(x, width - half, axis=1), pltpu.roll(x, half, axis=1))
    reps = width // LANE
    return x * jnp.tile(cos2, (1, reps)) + partner * jnp.tile(sin2, (1, reps))


def _head_mean_square(x, seg_ref):
    sq = x * x
    hi = sq.astype(jnp.bfloat16)
    lo = (sq - hi.astype(jnp.float32)).astype(jnp.bfloat16)
    seg = seg_ref[...]
    return (jnp.dot(hi, seg, preferred_element_type=jnp.float32)
            + jnp.dot(lo, seg, preferred_element_type=jnp.float32))


def _qk_post_kernel(q_ref, k_ref, iq_ref, ikw_ref, cos_ref, sin_ref, qg_ref, kg_ref, segq_ref, segk_ref,
                    qo_ref, ko_ref, iqo_ref, iko_ref, iwo_ref):
    cos2, sin2 = cos_ref[...], sin_ref[...]
    q = q_ref[...]
    qn = q * lax.rsqrt(_head_mean_square(q, segq_ref) + EPS) * qg_ref[...]
    qo_ref[...] = _rotate(qn, cos2, sin2)
    k = k_ref[...]
    kn = k * lax.rsqrt(_head_mean_square(k, segk_ref) + EPS) * kg_ref[...]
    ko_ref[...] = _rotate(kn, cos2, sin2)
    iqo_ref[...] = _rotate(iq_ref[...], cos2, sin2)
    ikw = ikw_ref[...]
    iko_ref[...] = _rotate(ikw, cos2, sin2)
    iwo_ref[...] = ikw * (IDX_HEADS ** -0.5 * IDX_DIM ** -0.5)


def _qk_post(z, pos, q_norm_g, k_norm_g, tm=512):
    n = z.shape[0]
    cos2, sin2 = _rope_tables(pos)
    seg = lambda w: (jnp.kron(jnp.eye(w // HEAD_DIM), jnp.ones((HEAD_DIM, HEAD_DIM))) / HEAD_DIM).astype(jnp.bfloat16)
    col = lambda name, w: pl.BlockSpec((tm, w), lambda i, c=Z_COLS[name] // w: (i, c))
    const = lambda shape: pl.BlockSpec(shape, lambda i: (0,) * len(shape))
    rows = lambda w: pl.BlockSpec((tm, w), lambda i: (i, 0))
    f32 = jnp.float32
    return pl.pallas_call(
        _qk_post_kernel,
        grid=(n // tm,),
        in_specs=[col("q", Q_W), col("k", KV_W), col("iq", IQ_W), col("ikw", LANE), rows(LANE), rows(LANE),
                  const((1, Q_W)), const((1, KV_W)), const((Q_W, Q_W)), const((KV_W, KV_W))],
        out_specs=[rows(Q_W), rows(KV_W), rows(IQ_W), rows(LANE), rows(LANE)],
        out_shape=[jax.ShapeDtypeStruct((n, Q_W), f32), jax.ShapeDtypeStruct((n, KV_W), f32),
                   jax.ShapeDtypeStruct((n, IQ_W), f32), jax.ShapeDtypeStruct((n, LANE), f32),
                   jax.ShapeDtypeStruct((n, LANE), f32)],
        compiler_params=pltpu.CompilerParams(dimension_semantics=("arbitrary",), vmem_limit_bytes=VMEM_LIMIT_BYTES),
        name="qk_post",
    )(z, z, z, z, cos2, sin2, jnp.tile(q_norm_g, N_HEADS).reshape(1, Q_W), jnp.tile(k_norm_g, KV_HEADS).reshape(1, KV_W),
      seg(Q_W), seg(KV_W))


def _out_proj_kernel(x_ref, attn_ref, rnn_ref, ga_ref, gr_ref, wa_ref, wr_ref, wo_ref, o_ref):
    bf = jnp.bfloat16
    a = jnp.dot(attn_ref[...].astype(bf), wa_ref[...], preferred_element_type=jnp.float32)
    r = jnp.dot(rnn_ref[...].astype(bf), wr_ref[...], preferred_element_type=jnp.float32)
    merged = jax.nn.sigmoid(ga_ref[...]) * a + jax.nn.sigmoid(gr_ref[...]) * r
    o_ref[...] = x_ref[...] + jnp.dot(merged.astype(bf), wo_ref[...], preferred_element_type=jnp.float32)


def _out_proj(x, attn, rnn, z, w_attn_out, w_rnn_out, w_out, tm=512):
    n, d = x.shape
    bf = jnp.bfloat16
    rows = pl.BlockSpec((tm, d), lambda i: (i, 0))
    col = lambda name: pl.BlockSpec((tm, d), lambda i, c=Z_COLS[name] // d: (i, c))
    const = pl.BlockSpec((d, d), lambda i: (0, 0))
    return pl.pallas_call(
        _out_proj_kernel,
        grid=(n // tm,),
        in_specs=[rows, rows, rows, col("ga"), col("gr"), const, const, const],
        out_specs=rows,
        out_shape=jax.ShapeDtypeStruct((n, d), jnp.float32),
        compiler_params=pltpu.CompilerParams(dimension_semantics=("arbitrary",), vmem_limit_bytes=VMEM_LIMIT_BYTES),
        name="out_proj",
    )(x, attn, rnn, z, z, w_attn_out.astype(bf), w_rnn_out.astype(bf), w_out.astype(bf))


def _rms_norm_rows(x, g):
    return x * lax.rsqrt(jnp.mean(x * x, axis=-1, keepdims=True) + EPS) * g


def _router_kernel(x_ref, g_ref, wr_ref, br_ref, tri_ref, e_ref, gate_ref, rank_ref, cnt_ref, run_ref):
    i = pl.program_id(0)
    tm = x_ref.shape[0]

    @pl.when(i == 0)
    def _():
        run_ref[...] = jnp.zeros_like(run_ref)

    xn_bf = _rms_norm_rows(x_ref[...], g_ref[...]).astype(jnp.bfloat16)
    logits = lax.dot_general(wr_ref[...], xn_bf, (((1,), (1,)), ((), ())), preferred_element_type=jnp.float32)
    logits = logits + br_ref[...]

    expert = lax.broadcasted_iota(jnp.int32, (N_EXPERTS, tm), 0)
    member = jnp.zeros((N_EXPERTS, tm), jnp.float32)
    picked, values = [], []
    for k in range(TOP_K):
        mx = jnp.max(logits, axis=0, keepdims=True)
        idx = jnp.min(jnp.where(logits == mx, expert, N_EXPERTS), axis=0, keepdims=True)
        hit = expert == idx
        member = jnp.where(hit, 1.0, member)
        logits = jnp.where(hit, -jnp.inf, logits)
        picked.append(idx)
        values.append(mx)
        e_ref[k:k + 1, :] = idx

    ex = [jnp.exp(v - values[0]) for v in values]
    denom = ex[0] + ex[1] + ex[2] + ex[3]
    for k in range(TOP_K):
        gate_ref[k:k + 1, :] = ex[k] / denom

    before = jnp.dot(member.astype(jnp.bfloat16), tri_ref[...], preferred_element_type=jnp.float32)
    before = before + jnp.tile(run_ref[...], (1, tm // LANE))
    for k in range(TOP_K):
        r = jnp.sum(jnp.where(expert == picked[k], before, 0.0), axis=0, keepdims=True)
        rank_ref[k:k + 1, :] = r.astype(jnp.int32)
    run = run_ref[...] + jnp.sum(member, axis=1, keepdims=True)
    run_ref[...] = run
    cnt_ref[...] = run.astype(jnp.int32)


def _route(x2d, norm_g, w_router, b_router, tm=512):
    n, d = x2d.shape
    tri = (jnp.arange(tm)[:, None] < jnp.arange(tm)[None, :]).astype(jnp.bfloat16)
    const = lambda shape: pl.BlockSpec(shape, lambda i: (0,) * len(shape))
    rows4 = pl.BlockSpec((TOP_K, tm), lambda i: (0, i))
    top_e, gates, rank, cnt = pl.pallas_call(
        _router_kernel,
        grid=(n // tm,),
        in_specs=[pl.BlockSpec((tm, d), lambda i: (i, 0)), const((1, d)), const((N_EXPERTS, d)),
                  const((N_EXPERTS, 1)), const((tm, tm))],
        out_specs=[rows4, rows4, rows4, const((N_EXPERTS, LANE))],
        out_shape=[jax.ShapeDtypeStruct((TOP_K, n), jnp.int32),
                   jax.ShapeDtypeStruct((TOP_K, n), jnp.float32), jax.ShapeDtypeStruct((TOP_K, n), jnp.int32),
                   jax.ShapeDtypeStruct((N_EXPERTS, LANE), jnp.int32)],
        scratch_shapes=[pltpu.VMEM((N_EXPERTS, LANE), jnp.float32)],
        compiler_params=pltpu.CompilerParams(dimension_semantics=("arbitrary",)),
        name="moe_router",
    )(x2d, norm_g.reshape(1, d), w_router.T.astype(jnp.bfloat16), b_router.reshape(N_EXPERTS, 1), tri)
    return top_e, gates, rank, cnt[:, 0]


def _expert_kernel(be_ref, used_ref, xs_ref, g_ref, wgu_ref, bgu_ref, wd_ref, bd_ref, o_ref, *, blk):
    i = pl.program_id(0)

    @pl.when(i * blk < used_ref[0])
    def _():
        xn = _rms_norm_rows(xs_ref[...], g_ref[...]).astype(jnp.bfloat16)
        gu = jnp.dot(xn, wgu_ref[0], preferred_element_type=jnp.float32) + bgu_ref[0]
        g = jnp.minimum(gu[:, :D_FF], SWIGLU_LIMIT)
        u = jnp.clip(gu[:, D_FF:], -SWIGLU_LIMIT, SWIGLU_LIMIT)
        act = (u + 1.0) * (g * jax.nn.sigmoid(SWIGLU_ALPHA * g))
        o_ref[...] = jnp.dot(act.astype(jnp.bfloat16), wd_ref[0], preferred_element_type=jnp.float32) + bd_ref[0]

    @pl.when(i * blk >= used_ref[0])
    def _():
        o_ref[...] = jnp.zeros_like(o_ref)


def _experts(xs, norm_g, blk_exp, n_used, w_gate_up, b_gate_up, w_down, b_down, *, blk):
    n_slots, d = xs.shape
    n_blocks = n_slots // blk
    bf = jnp.bfloat16
    return pl.pallas_call(
        functools.partial(_expert_kernel, blk=blk),
        grid_spec=pltpu.PrefetchScalarGridSpec(
            num_scalar_prefetch=2, grid=(n_blocks,),
            in_specs=[pl.BlockSpec((blk, d), lambda i, be, nu: (i, 0)),
                      pl.BlockSpec((1, d), lambda i, be, nu: (0, 0)),
                      pl.BlockSpec((1, d, 2 * D_FF), lambda i, be, nu: (be[i], 0, 0)),
                      pl.BlockSpec((1, 1, 2 * D_FF), lambda i, be, nu: (be[i], 0, 0)),
                      pl.BlockSpec((1, D_FF, d), lambda i, be, nu: (be[i], 0, 0)),
                      pl.BlockSpec((1, 1, d), lambda i, be, nu: (be[i], 0, 0))],
            out_specs=pl.BlockSpec((blk, d), lambda i, be, nu: (i, 0))),
        out_shape=jax.ShapeDtypeStruct((n_slots, d), jnp.float32),
        compiler_params=pltpu.CompilerParams(dimension_semantics=("arbitrary",), vmem_limit_bytes=VMEM_LIMIT_BYTES),
        name="moe_experts",
    )(blk_exp, n_used, xs, norm_g.reshape(1, d), w_gate_up.astype(bf), b_gate_up.reshape(N_EXPERTS, 1, 2 * D_FF),
      w_down.astype(bf), b_down.reshape(N_EXPERTS, 1, d))


def _row_copy(src_hbm, src_row, dst_ref, dst_row, sem):
    return pltpu.make_async_copy(src_hbm.at[pl.ds(src_row, 1)], dst_ref.at[pl.ds(dst_row, 1)], sem)


def _dispatch_kernel(dest_ref, x_ref, xs_in_hbm, xs_hbm, sem, *, tm):
    del xs_in_hbm

    def issue(t, carry):
        for k in range(TOP_K):
            _row_copy(x_ref, t, xs_hbm, dest_ref[k, t], sem).start()
        return carry

    lax.fori_loop(0, tm, issue, 0)

    def drain(t, carry):
        for k in range(TOP_K):
            _row_copy(x_ref, 0, xs_hbm, 0, sem).wait()
        return carry

    lax.fori_loop(0, tm, drain, 0)


def _dispatch(x2d, dest, n_slots, tm=128):
    n, d = x2d.shape
    return pl.pallas_call(
        functools.partial(_dispatch_kernel, tm=tm),
        grid=(n // tm,),
        in_specs=[pl.BlockSpec((TOP_K, tm), lambda i: (0, i), memory_space=pltpu.SMEM),
                  pl.BlockSpec((tm, d), lambda i: (i, 0)), pl.BlockSpec(memory_space=pl.ANY)],
        out_specs=pl.BlockSpec(memory_space=pl.ANY),
        out_shape=jax.ShapeDtypeStruct((n_slots, d), x2d.dtype),
        scratch_shapes=[pltpu.SemaphoreType.DMA(())],
        input_output_aliases={2: 0},
        compiler_params=pltpu.CompilerParams(dimension_semantics=("arbitrary",)),
        name="moe_dispatch",
    )(dest, x2d, jnp.zeros((n_slots, d), x2d.dtype))


def _combine_kernel(dest_ref, x_ref, gate_ref, ys_hbm, o_ref, buf_ref, sem, *, tm):
    def issue(t, carry):
        for k in range(TOP_K):
            _row_copy(ys_hbm, dest_ref[k, t], buf_ref.at[k], t, sem).start()
        return carry

    lax.fori_loop(0, tm, issue, 0)

    def drain(t, carry):
        for k in range(TOP_K):
            _row_copy(ys_hbm, 0, buf_ref.at[k], 0, sem).wait()
        return carry

    lax.fori_loop(0, tm, drain, 0)
    gate = gate_ref[...]
    acc = x_ref[...]
    for k in range(TOP_K):
        acc = acc + gate[:, k:k + 1] * buf_ref[k]
    o_ref[...] = acc


def _combine(x2d, ys, dest, gates_t, tm=128):
    n, d = x2d.shape
    return pl.pallas_call(
        functools.partial(_combine_kernel, tm=tm),
        grid=(n // tm,),
        in_specs=[pl.BlockSpec((TOP_K, tm), lambda i: (0, i), memory_space=pltpu.SMEM),
                  pl.BlockSpec((tm, d), lambda i: (i, 0)), pl.BlockSpec((tm, TOP_K), lambda i: (i, 0)),
                  pl.BlockSpec(memory_space=pl.ANY)],
        out_specs=pl.BlockSpec((tm, d), lambda i: (i, 0)),
        out_shape=jax.ShapeDtypeStruct((n, d), jnp.float32),
        scratch_shapes=[pltpu.VMEM((TOP_K, tm, d), jnp.float32), pltpu.SemaphoreType.DMA(())],
        compiler_params=pltpu.CompilerParams(dimension_semantics=("arbitrary",)),
        name="moe_combine",
    )(dest, x2d, gates_t, ys)


MOE_ROWS = 512


def _moe(x2d, norm_g, w_router, b_router, w_gate_up, b_gate_up, w_down, b_down, *, blk=MOE_ROWS, tm=512):
    n, d = x2d.shape
    top_e, gates, rank, counts = _route(x2d, norm_g, w_router, b_router, tm=tm)
    padded = (counts + blk - 1) // blk * blk
    pad_end = jnp.cumsum(padded)
    pad_start = pad_end - padded
    experts = jnp.arange(N_EXPERTS, dtype=jnp.int32)
    start_of = jnp.sum(jnp.where(top_e[..., None] == experts, pad_start.astype(jnp.int32), 0), axis=-1)
    dest = start_of + rank
    n_blocks = -(-(n * TOP_K + N_EXPERTS * (blk - 1)) // blk)
    n_slots = n_blocks * blk
    block_start = jnp.arange(n_blocks, dtype=jnp.int32) * blk
    blk_exp = jnp.minimum(jnp.sum(pad_end[None, :] <= block_start[:, None], axis=1), N_EXPERTS - 1).astype(jnp.int32)
    n_used = pad_end[-1:].astype(jnp.int32)
    xs = _dispatch(x2d, dest, n_slots)
    ys = _experts(xs, norm_g, blk_exp, n_used, w_gate_up, b_gate_up, w_down, b_down, blk=blk)
    return _combine(x2d, ys, dest, gates.T)


def kernel(x_prompt, x_sample, cache_k, cache_v, cache_idx_k, state_conv, state_h, page_table,
           norm1_g, w_in, q_norm_g, k_norm_g, conv_w, conv_b, rg_w_a, rg_b_a, rg_w_i, rg_b_i, rg_lambda,
           w_attn_out, w_rnn_out, w_out, norm2_g, w_router, b_router, w_gate_up, b_gate_up, w_down, b_down):
    assert x_prompt.shape[0] == 1 and norm1_g.shape[0] == 1 and x_sample.shape[1] == SUBLANE
    s_len = x_prompt.shape[1]
    db, t_new = x_sample.shape[:2]
    n_s = db * t_new
    past = page_table.shape[1] * PAGE_SIZE
    pos = jnp.concatenate([jnp.arange(s_len, dtype=jnp.int32),
                           jnp.tile(past + jnp.arange(t_new, dtype=jnp.int32), db)])
    x_all = jnp.concatenate([x_prompt.reshape(s_len, D_MODEL), x_sample.reshape(n_s, D_MODEL)], axis=0)

    z = _norm_proj(x_all, norm1_g[0], _permuted_w_in(w_in[0]))
    q, k, iq, ik, iw = _qk_post(z, pos, q_norm_g[0], k_norm_g[0])
    v = z[:, Z_COLS["v"]:Z_COLS["v"] + KV_W]
    xr = z[:, Z_COLS["xr"]:Z_COLS["xr"] + D_RNN]
    ik = ik[:, :IDX_DIM]
    iw = iw[:, IDX_DIM:IDX_DIM + IDX_HEADS]
    heads = lambda a, lo, hi, n_heads: a[lo:hi].reshape(hi - lo, n_heads, -1)
    seqs = lambda a: a.reshape((db, t_new) + a.shape[1:])
    p_rows, s_rows = (0, s_len), (s_len, s_len + n_s)

    attn_p = _prompt_dsa_attention_t(heads(q, *p_rows, N_HEADS), heads(k, *p_rows, KV_HEADS), heads(v, *p_rows, KV_HEADS),
                                     heads(iq, *p_rows, IDX_HEADS), iw[:s_len], ik[:s_len])
    attn_s = _sample_dsa_attention(seqs(heads(q, *s_rows, N_HEADS)), seqs(heads(k, *s_rows, KV_HEADS)),
                                   seqs(heads(v, *s_rows, KV_HEADS)), seqs(heads(iq, *s_rows, IDX_HEADS)),
                                   seqs(iw[s_len:]), seqs(ik[s_len:]), cache_k[0], cache_v[0], cache_idx_k[0], page_table)
    rg = (conv_w[0], conv_b[0], rg_w_a[0], rg_b_a[0], rg_w_i[0], rg_b_i[0], rg_lambda[0])
    rnn_p, h_p = _rglru_prompt(z, 0, s_len, *rg)
    rnn_s, h_s = _rglru_sample(z, s_len, state_conv[0], state_h[0], *rg)

    attn = jnp.concatenate([attn_p, attn_s.reshape(n_s, Q_W)], axis=0)
    rnn = jnp.concatenate([rnn_p, rnn_s], axis=0)
    x_mid = _out_proj(x_all, attn, rnn, z, w_attn_out[0], w_rnn_out[0], w_out[0])
    y = _moe(x_mid, norm2_g[0], w_router[0], b_router[0], w_gate_up[0], b_gate_up[0], w_down[0], b_down[0])

    tail = CONV_W - 1
    xr_s = xr[s_len:].reshape(db, t_new, D_RNN)
    conv_s = jnp.concatenate([state_conv[0], xr_s], axis=1)[:, -tail:]
    kv_p = lambda a: a[:s_len].reshape(1, 1, s_len, KV_HEADS, HEAD_DIM)
    kv_s = lambda a: a[s_len:].reshape(1, db, t_new, KV_HEADS, HEAD_DIM)
    return (y[:s_len].reshape(x_prompt.shape), y[s_len:].reshape(x_sample.shape),
            kv_p(k), kv_p(v), ik[:s_len].reshape(1, 1, s_len, IDX_DIM),
            xr[s_len - tail:s_len].reshape(1, 1, tail, D_RNN), h_p.reshape(1, 1, D_RNN),
            kv_s(k), kv_s(v), ik[s_len:].reshape(1, db, t_new, IDX_DIM),
            conv_s[None], h_s[None])
```

```python
import functools
import math

import jax
import jax.numpy as jnp
import numpy as np
from jax import lax
from jax.experimental import pallas as pl
from jax.experimental.pallas import tpu as pltpu

D_MODEL = 1024
PAGE_SIZE = 128
N_HEADS = 16
HEAD_DIM = 64
KV_HEADS = 4
IDX_HEADS = 8
IDX_DIM = 64
IDX_TOPK_MAX = 256
Q_BLOCK = 128
ROPE_THETA = 10000.0
D_RNN = D_MODEL
RG_BLOCKS = 4
RG_BLOCK_W = D_RNN // RG_BLOCKS
CONV_W = 4
RG_C = 8.0
N_EXPERTS = 32
TOP_K = 4
D_FF = D_MODEL
SWIGLU_LIMIT = 7.0
SWIGLU_ALPHA = 1.702
MOE_BLOCK = 128
EPS = 1e-6

Q_W = N_HEADS * HEAD_DIM
KV_W = KV_HEADS * HEAD_DIM
IQ_W = IDX_HEADS * IDX_DIM
IN_NAMES = ("q", "k", "v", "iq", "ik", "iw", "xr", "xg", "ga", "gr")
IN_WIDTHS = (Q_W, KV_W, KV_W, IQ_W, IDX_DIM, IDX_HEADS, D_RNN, D_RNN, D_MODEL, D_MODEL)

LANE = 128
SUBLANE = 8
VMEM_LIMIT_BYTES = 48 * 1024 * 1024

Z_ORDER = ("q", "k", "v", "iq", "xr", "xg", "ga", "gr")
Z_COLS = {}
_col = 0
for _name in Z_ORDER:
    Z_COLS[_name] = _col
    _col += IN_WIDTHS[IN_NAMES.index(_name)]
Z_COLS["ikw"] = _col
Z_W = _col + LANE
PROJ_TN = 896


def _permuted_w_in(w_in):
    offs = np.concatenate([[0], np.cumsum(IN_WIDTHS)])
    seg = {name: w_in[:, offs[j]:offs[j + 1]] for j, name in enumerate(IN_NAMES)}
    pad = jnp.zeros((w_in.shape[0], LANE - IDX_DIM - IDX_HEADS), w_in.dtype)
    return jnp.concatenate([seg[name] for name in Z_ORDER] + [seg["ik"], seg["iw"], pad], axis=1).astype(jnp.bfloat16)


def _norm_proj_kernel(x_ref, g_ref, w_ref, o_ref):
    x = x_ref[...]
    y = x * lax.rsqrt(jnp.mean(x * x, axis=-1, keepdims=True) + EPS) * g_ref[...]
    o_ref[...] = jnp.dot(y.astype(jnp.bfloat16), w_ref[...], preferred_element_type=jnp.float32)


def _norm_proj(x2d, g, w_bf16, tm=512, tn=PROJ_TN):
    m, d = x2d.shape
    n = w_bf16.shape[1]
    return pl.pallas_call(
        _norm_proj_kernel,
        grid=(m // tm, n // tn),
        in_specs=[
            pl.BlockSpec((tm, d), lambda i, j: (i, 0)),
            pl.BlockSpec((1, d), lambda i, j: (0, 0)),
            pl.BlockSpec((d, tn), lambda i, j: (0, j)),
        ],
        out_specs=pl.BlockSpec((tm, tn), lambda i, j: (i, j)),
        out_shape=jax.ShapeDtypeStruct((m, n), jnp.float32),
        name="norm_proj",
    )(x2d, g.reshape(1, d), w_bf16)


INT_MIN = -(2 ** 31)
INT_MAX = 2 ** 31 - 1
MASK_BIAS = -1e30
NEG_INF_BITS_MASK = 0x7FFFFFFF


def _ordered_key(x):
    bits = pltpu.bitcast(x, jnp.int32)
    return bits ^ ((bits >> 31) & NEG_INF_BITS_MASK)


def _topk_threshold(key_ref, cut_ref, nk, *, rows, kb, topk, n_idx_bits):
    n_lane_tiles = kb // LANE
    lane = lax.broadcasted_iota(jnp.int32, (rows, kb), 1)

    def count(pred_of_chunk):
        def body(c, cnt):
            p = pred_of_chunk(c, key_ref[c])
            for j in range(n_lane_tiles):
                cnt = cnt + jnp.where(p[:, j * LANE:(j + 1) * LANE], 1, 0)
            return cnt
        cnt = lax.fori_loop(0, nk, body, jnp.zeros((rows, LANE), jnp.int32))
        return jnp.sum(cnt, axis=-1, keepdims=True)

    def bit_cond(carry):
        b, _, n_ge = carry
        return jnp.logical_and(b < 32, jnp.max(jnp.abs(n_ge - topk)) > 0)

    def bit_step(carry):
        b, thr, n_ge = carry
        cand = thr ^ (jnp.int32(1) << (31 - b))
        cand_b = jnp.broadcast_to(cand, (rows, kb))
        cnt = count(lambda c, key: key >= cand_b)
        ok = cnt >= topk
        return b + 1, jnp.where(ok, cand, thr), jnp.where(ok, cnt, n_ge)

    thr0 = jnp.full((rows, 1), INT_MIN, jnp.int32)
    n0 = jnp.full((rows, 1), 0, jnp.int32) + nk * kb
    _, thr, n_ge = lax.while_loop(bit_cond, bit_step, (jnp.int32(0), thr0, n0))
    thr_b = jnp.broadcast_to(thr, (rows, kb))

    cut_ref[...] = jnp.full((rows, LANE), INT_MAX, jnp.int32)

    @pl.when(jnp.max(n_ge) > topk)
    def _():
        n_eq = count(lambda c, key: key == thr_b)
        need = topk - (n_ge - n_eq)

        def idx_step(b, lo):
            step = jnp.int32(1) << (n_idx_bits - 1 - b)
            mid_b = jnp.broadcast_to(lo + step - 1, (rows, kb))
            f = count(lambda c, key: (key == thr_b) & (c * kb + lane <= mid_b))
            return jnp.where(f < need, lo + step, lo)

        lo = lax.fori_loop(0, n_idx_bits, idx_step, jnp.zeros((rows, 1), jnp.int32))
        cut = jnp.where(n_ge > topk, lo, INT_MAX)
        cut_ref[...] = jnp.broadcast_to(cut, (rows, LANE))

    return thr


def _topk_threshold_t(key_ref, cut_ref, nk, *, kb, cols, topk, n_idx_bits):
    acc_rows = 8 * SUBLANE
    assert kb % acc_rows == 0
    pos0 = lax.broadcasted_iota(jnp.int32, (kb, cols), 0)

    def count(pred_of_chunk):
        def body(c, cnt):
            p = pred_of_chunk(c, key_ref[c])
            for j in range(kb // acc_rows):
                cnt = cnt + jnp.where(p[j * acc_rows:(j + 1) * acc_rows, :], 1, 0)
            return cnt
        cnt = lax.fori_loop(0, nk, body, jnp.zeros((acc_rows, cols), jnp.int32))
        return jnp.sum(cnt, axis=0, keepdims=True)

    def bit_cond(carry):
        b, _, n_ge = carry
        return jnp.logical_and(b < 32, jnp.max(jnp.abs(n_ge - topk)) > 0)

    def bit_step(carry):
        b, thr, n_ge = carry
        cand = thr ^ (jnp.int32(1) << (31 - b))
        cand_b = jnp.broadcast_to(cand, (kb, cols))
        cnt = count(lambda c, key: key >= cand_b)
        ok = cnt >= topk
        return b + 1, jnp.where(ok, cand, thr), jnp.where(ok, cnt, n_ge)

    thr0 = jnp.full((1, cols), INT_MIN, jnp.int32)
    n0 = jnp.full((1, cols), 0, jnp.int32) + nk * kb
    _, thr, n_ge = lax.while_loop(bit_cond, bit_step, (jnp.int32(0), thr0, n0))
    thr_b = jnp.broadcast_to(thr, (kb, cols))

    cut_ref[...] = jnp.full((SUBLANE, cols), INT_MAX, jnp.int32)

    @pl.when(jnp.max(n_ge) > topk)
    def _():
        n_eq = count(lambda c, key: key == thr_b)
        need = topk - (n_ge - n_eq)

        def idx_step(b, lo):
            step = jnp.int32(1) << (n_idx_bits - 1 - b)
            mid_b = jnp.broadcast_to(lo + step - 1, (kb, cols))
            f = count(lambda c, key: (key == thr_b) & (c * kb + pos0 <= mid_b))
            return jnp.where(f < need, lo + step, lo)

        lo = lax.fori_loop(0, n_idx_bits, idx_step, jnp.zeros((1, cols), jnp.int32))
        cut = jnp.where(n_ge > topk, lo, INT_MAX)
        cut_ref[...] = jnp.broadcast_to(cut, (SUBLANE, cols))

    return thr


UNDERFLOW_GUARD = 1e-30
LOG2_E = math.log2(math.e)


def _dsa_t_kernel(iq_ref, iw_ref, q_ref, mb_ref, ik_ref, k_ref, vt_ref, o_ref,
                  key_ref, cut_ref, m_ref, l_ref, acc_ref, *, qb, kb, topk, n_idx_bits):
    i = pl.program_id(0)
    nk = ((i + 1) * qb + kb - 1) // kb
    rep = N_HEADS // KV_HEADS
    kpos0 = lax.broadcasted_iota(jnp.int32, (kb, qb), 0)
    qpos = i * qb + lax.broadcasted_iota(jnp.int32, (kb, qb), 1)
    iw = iw_ref[0]

    def score_chunk(c, carry):
        ikc = ik_ref[c]
        acc = jnp.zeros((kb, qb), jnp.float32)
        for hp in range(IDX_HEADS // 2):
            s2 = jnp.dot(ikc, iq_ref[0, hp], preferred_element_type=jnp.float32)
            for hh in range(2):
                h = 2 * hp + hh
                acc = acc + iw[h:h + 1, :] * jnp.maximum(s2[:, hh * qb:(hh + 1) * qb], 0.0)
        acc = jnp.where(c * kb + kpos0 <= qpos, acc, -jnp.inf)
        key_ref[c] = _ordered_key(acc)
        return carry

    lax.fori_loop(0, nk, score_chunk, 0)

    thr = _topk_threshold_t(key_ref, cut_ref, nk, kb=kb, cols=qb, topk=topk, n_idx_bits=n_idx_bits)
    thr_b = jnp.broadcast_to(thr, (kb, qb))
    cut_b = jnp.broadcast_to(cut_ref[0:1, :], (kb, qb))

    def mask_bias(c):
        key = key_ref[c]
        kpos = c * kb + kpos0
        sel = (key > thr_b) | ((key == thr_b) & (kpos <= cut_b))
        sel = sel & (kpos <= qpos)
        return jnp.tile(jnp.where(sel, 0.0, MASK_BIAS), (1, rep))

    l_ref[...] = jnp.zeros(l_ref.shape, jnp.float32)
    acc_ref[...] = jnp.zeros(acc_ref.shape, jnp.float32)

    def attend_chunk(c, carry):
        bias = mask_bias(c)
        for g in range(KV_HEADS):
            s = jnp.dot(k_ref[c, g], q_ref[0, g], preferred_element_type=jnp.float32) + (bias - mb_ref[0, g])
            p = jnp.exp2(s)
            l_ref[g] = l_ref[g] + jnp.sum(p, axis=0, keepdims=True)
            acc_ref[g] = acc_ref[g] + jnp.dot(vt_ref[c, g], p.astype(jnp.bfloat16), preferred_element_type=jnp.float32)
        return carry

    lax.fori_loop(0, nk, attend_chunk, 0)

    @pl.when(jnp.min(l_ref[...]) < UNDERFLOW_GUARD)
    def _():
        m_ref[...] = jnp.full(m_ref.shape, MASK_BIAS, jnp.float32)
        l_ref[...] = jnp.zeros(l_ref.shape, jnp.float32)
        acc_ref[...] = jnp.zeros(acc_ref.shape, jnp.float32)

        def attend_chunk_running_max(c, carry):
            bias = mask_bias(c)
            for g in range(KV_HEADS):
                s = jnp.dot(k_ref[c, g], q_ref[0, g], preferred_element_type=jnp.float32) + bias
                m_prev = m_ref[g]
                m_new = jnp.maximum(m_prev, jnp.max(s, axis=0, keepdims=True))
                alpha = jnp.exp2(m_prev - m_new)
                p = jnp.exp2(s - m_new)
                l_ref[g] = alpha * l_ref[g] + jnp.sum(p, axis=0, keepdims=True)
                pv = jnp.dot(vt_ref[c, g], p.astype(jnp.bfloat16), preferred_element_type=jnp.float32)
                acc_ref[g] = acc_ref[g] * alpha + pv
                m_ref[g] = m_new
            return carry

        lax.fori_loop(0, nk, attend_chunk_running_max, 0)

    for g in range(KV_HEADS):
        o_ref[0, g] = acc_ref[g] / l_ref[g]


def _prompt_dsa_attention_t(q, k, v, iq, iw, ik, *, qb=128, kb=1024):
    s_len = q.shape[0]
    topk = min(IDX_TOPK_MAX, s_len // 4)
    nqb, nkc = s_len // qb, s_len // kb
    rep = N_HEADS // KV_HEADS
    bf = jnp.bfloat16
    q_s = (q * (HEAD_DIM ** -0.5 * LOG2_E)).astype(bf)
    k_bf = k.astype(bf)
    q_norm = jnp.sqrt(jnp.sum(jnp.square(q_s.astype(jnp.float32)), axis=-1))
    k_norm = jnp.sqrt(jnp.max(jnp.sum(jnp.square(k_bf.astype(jnp.float32)), axis=-1), axis=0))
    bound = q_norm.reshape(nqb, qb, KV_HEADS, rep) * k_norm[None, None, :, None]
    bound = bound.transpose(0, 2, 3, 1).reshape(nqb, KV_HEADS, 1, rep * qb)
    q_t = q_s.reshape(nqb, qb, KV_HEADS, rep, HEAD_DIM).transpose(0, 2, 4, 3, 1)
    q_t = q_t.reshape(nqb, KV_HEADS, HEAD_DIM, rep * qb)
    iq_t = iq.astype(bf).reshape(nqb, qb, IDX_HEADS // 2, 2, IDX_DIM).transpose(0, 2, 4, 3, 1)
    iq_t = iq_t.reshape(nqb, IDX_HEADS // 2, IDX_DIM, 2 * qb)
    iw_t = iw.reshape(nqb, qb, IDX_HEADS).transpose(0, 2, 1)
    ik_c = ik.astype(bf).reshape(nkc, kb, IDX_DIM)
    k_c = k_bf.reshape(nkc, kb, KV_HEADS, HEAD_DIM).transpose(0, 2, 1, 3)
    v_t = v.astype(bf).reshape(nkc, kb, KV_HEADS, HEAD_DIM).transpose(0, 2, 3, 1)
    n_idx_bits = max(1, (s_len - 1).bit_length())
    body = functools.partial(_dsa_t_kernel, qb=qb, kb=kb, topk=topk, n_idx_bits=n_idx_bits)
    whole = lambda shape: pl.BlockSpec(shape, lambda i: (0,) * len(shape), pipeline_mode=pl.Buffered(1))
    out = pl.pallas_call(
        body,
        grid=(nqb,),
        in_specs=[
            pl.BlockSpec((1, IDX_HEADS // 2, IDX_DIM, 2 * qb), lambda i: (i, 0, 0, 0)),
            pl.BlockSpec((1, IDX_HEADS, qb), lambda i: (i, 0, 0)),
            pl.BlockSpec((1, KV_HEADS, HEAD_DIM, rep * qb), lambda i: (i, 0, 0, 0)),
            pl.BlockSpec((1, KV_HEADS, 1, rep * qb), lambda i: (i, 0, 0, 0)),
            whole((nkc, kb, IDX_DIM)),
            whole((nkc, KV_HEADS, kb, HEAD_DIM)),
            whole((nkc, KV_HEADS, HEAD_DIM, kb)),
        ],
        out_specs=pl.BlockSpec((1, KV_HEADS, HEAD_DIM, rep * qb), lambda i: (i, 0, 0, 0)),
        out_shape=jax.ShapeDtypeStruct((nqb, KV_HEADS, HEAD_DIM, rep * qb), jnp.float32),
        scratch_shapes=[
            pltpu.VMEM((nkc, kb, qb), jnp.int32),
            pltpu.VMEM((SUBLANE, qb), jnp.int32),
            pltpu.VMEM((KV_HEADS, 1, rep * qb), jnp.float32),
            pltpu.VMEM((KV_HEADS, 1, rep * qb), jnp.float32),
            pltpu.VMEM((KV_HEADS, HEAD_DIM, rep * qb), jnp.float32),
        ],
        compiler_params=pltpu.CompilerParams(dimension_semantics=("arbitrary",), vmem_limit_bytes=VMEM_LIMIT_BYTES),
        name="prompt_dsa_attention",
    )(iq_t, iw_t, q_t, bound, ik_c, k_c, v_t)
    out = out.reshape(nqb, KV_HEADS, HEAD_DIM, rep, qb).transpose(0, 4, 1, 3, 2)
    return out.reshape(s_len, N_HEADS * HEAD_DIM)


def _sample_index_kernel(pt_ref, iq_ref, iw_ref, iknew_ref, *rest, pp, n_pages, topk, n_idx_bits):
    page_refs, o_ref, (key_ref, wide_ref, wb_ref, cut_ref) = rest[:pp], rest[pp], rest[pp + 1:]
    pg = pl.program_id(1)
    t_new = o_ref.shape[1]
    step_keys = pp * PAGE_SIZE
    n_keys = (n_pages + 1) * PAGE_SIZE

    @pl.when(pg == 0)
    def _():
        wb_ref[...] = jnp.broadcast_to(iw_ref[0], wb_ref.shape)

    iq = iq_ref[0]

    def scores(ik_t):
        s = jnp.dot(iq, ik_t.astype(jnp.bfloat16), preferred_element_type=jnp.float32)
        s = jnp.tile(wb_ref[...], (1, ik_t.shape[1] // LANE)) * jnp.maximum(s, 0.0)
        acc = jnp.zeros((t_new, ik_t.shape[1]), jnp.float32)
        for h in range(IDX_HEADS):
            acc = acc + s[h * t_new:(h + 1) * t_new, :]
        return acc

    key_ref[pg] = _ordered_key(scores(jnp.concatenate([r[0] for r in page_refs], axis=1)))

    @pl.when(pg == pl.num_programs(1) - 1)
    def _():
        for c in range(n_pages // pp):
            wide_ref[0, :, c * step_keys:(c + 1) * step_keys] = key_ref[c]
        row = lax.broadcasted_iota(jnp.int32, (t_new, PAGE_SIZE), 0)
        lane = lax.broadcasted_iota(jnp.int32, (t_new, PAGE_SIZE), 1)
        s_new = jnp.where(lane <= row, scores(iknew_ref[0]), -jnp.inf)
        wide_ref[0, :, n_pages * PAGE_SIZE:] = _ordered_key(s_new)
        thr = _topk_threshold(wide_ref, cut_ref, 1, rows=t_new, kb=n_keys, topk=topk, n_idx_bits=n_idx_bits)
        key = wide_ref[0]
        kpos = lax.broadcasted_iota(jnp.int32, (t_new, n_keys), 1)
        qpos = n_pages * PAGE_SIZE + lax.broadcasted_iota(jnp.int32, (t_new, n_keys), 0)
        thr_b = jnp.broadcast_to(thr, (t_new, n_keys))
        cut_b = jnp.tile(cut_ref[...], (1, n_pages + 1))
        sel = (key > thr_b) | ((key == thr_b) & (kpos <= cut_b))
        o_ref[0] = jnp.where(sel & (kpos <= qpos), 0.0, MASK_BIAS)


def _sample_attend_kernel(pt_ref, q_ref, bias_ref, biasnew_ref, knew_ref, vnew_ref, *rest, pp, n_pages):
    k_refs, v_refs, o_ref, (m_ref, l_ref, acc_ref) = rest[:pp], rest[pp:2 * pp], rest[2 * pp], rest[2 * pp + 1:]
    pg = pl.program_id(1)
    n_rows = q_ref.shape[1]
    t_new = bias_ref.shape[1]

    @pl.when(pg == 0)
    def _():
        m_ref[...] = jnp.full(m_ref.shape, MASK_BIAS, jnp.float32)
        l_ref[...] = jnp.zeros(l_ref.shape, jnp.float32)
        acc_ref[...] = jnp.zeros(acc_ref.shape, jnp.float32)

    q = q_ref[0]

    def step(bias, k_t, v_t):
        n_keys = k_t.shape[1]
        s = jnp.dot(q, k_t.astype(jnp.bfloat16), preferred_element_type=jnp.float32)
        s = s + jnp.tile(bias, (n_rows // t_new, 1))
        m_prev = m_ref[...]
        m_new = jnp.maximum(m_prev, jnp.max(s, axis=-1, keepdims=True))
        alpha = jnp.exp(m_prev - m_new)
        p = jnp.exp(s - jnp.tile(m_new, (1, n_keys // LANE)))
        l_ref[...] = alpha * l_ref[...] + jnp.sum(p, axis=-1, keepdims=True)
        pv = lax.dot_general(p.astype(jnp.bfloat16), v_t.astype(jnp.bfloat16), (((1,), (1,)), ((), ())),
                             preferred_element_type=jnp.float32)
        acc_ref[...] = acc_ref[...] * jnp.tile(alpha, (1, KV_W // LANE)) + pv
        m_ref[...] = m_new

    step(bias_ref[0], jnp.concatenate([r[0] for r in k_refs], axis=1), jnp.concatenate([r[0] for r in v_refs], axis=1))

    @pl.when(pg == pl.num_programs(1) - 1)
    def _():
        step(biasnew_ref[0], knew_ref[0], vnew_ref[0])
        acc = acc_ref[...]
        row_head = lax.broadcasted_iota(jnp.int32, acc.shape, 0) // (n_rows // KV_HEADS)
        col_head = lax.broadcasted_iota(jnp.int32, acc.shape, 1) // HEAD_DIM
        own = jnp.where(row_head == col_head, acc, 0.0)
        folded = own[:, :LANE] + own[:, LANE:]
        folded = folded + pltpu.roll(folded, HEAD_DIM, axis=1)
        o_ref[0] = folded / l_ref[...]


def _sample_dsa_attention(q, k, v, iq, iw, ik, cache_k, cache_v, cache_idx_k, page_table, *, pp_index=64, pp=32):
    b, t = q.shape[:2]
    n_pages = page_table.shape[1]
    n_pool = cache_k.shape[0]
    past = n_pages * PAGE_SIZE
    topk = min(IDX_TOPK_MAX, (past + t) // 4)
    n_idx_bits = max(1, (past + PAGE_SIZE - 1).bit_length())
    pp_index, pp = min(pp_index, n_pages), min(pp, n_pages)
    assert n_pages % pp == 0 and n_pages % pp_index == 0 and t == SUBLANE and KV_W == 2 * LANE
    npg = n_pages // pp
    n_keys = past + PAGE_SIZE
    bf = jnp.bfloat16
    rep = N_HEADS // KV_HEADS
    n_rows = N_HEADS * t

    iq_r = iq.astype(bf).transpose(0, 2, 1, 3).reshape(b, IDX_HEADS * t, IDX_DIM)
    iw_r = iw.transpose(0, 2, 1).reshape(b, IDX_HEADS * t, 1)
    new_page = lambda a: jnp.pad(jnp.swapaxes(a, 1, 2), ((0, 0), (0, 0), (0, PAGE_SIZE - t)))
    pages_t = lambda c: jnp.swapaxes(c.reshape(n_pool, PAGE_SIZE, -1), 1, 2)
    ik_new = new_page(ik)
    k_new = new_page(k.reshape(b, t, KV_W))
    v_new = new_page(v.reshape(b, t, KV_W))
    qs = (q * HEAD_DIM ** -0.5).astype(bf).reshape(b, t, KV_HEADS, rep, HEAD_DIM).transpose(0, 2, 3, 1, 4)
    q_bd = (qs[:, :, :, :, None, :] * jnp.eye(KV_HEADS, dtype=bf)[None, :, None, None, :, None])
    q_bd = q_bd.reshape(b, n_rows, KV_W)

    page_map = lambda per_step, j: (lambda bi, pg, pt: (pt[bi, pg * per_step + j], 0, 0))
    per_seq = lambda shape: pl.BlockSpec((1,) + shape, lambda bi, pg, pt: (bi,) + (0,) * len(shape))
    params = pltpu.CompilerParams(dimension_semantics=("arbitrary", "arbitrary"), vmem_limit_bytes=VMEM_LIMIT_BYTES)

    bias = pl.pallas_call(
        functools.partial(_sample_index_kernel, pp=pp_index, n_pages=n_pages, topk=topk, n_idx_bits=n_idx_bits),
        grid_spec=pltpu.PrefetchScalarGridSpec(
            num_scalar_prefetch=1, grid=(b, n_pages // pp_index),
            in_specs=[per_seq((IDX_HEADS * t, IDX_DIM)), per_seq((IDX_HEADS * t, 1)), per_seq((IDX_DIM, PAGE_SIZE))]
            + [pl.BlockSpec((1, IDX_DIM, PAGE_SIZE), page_map(pp_index, j)) for j in range(pp_index)],
            out_specs=per_seq((t, n_keys)),
            scratch_shapes=[pltpu.VMEM((n_pages // pp_index, t, pp_index * PAGE_SIZE), jnp.int32),
                            pltpu.VMEM((1, t, n_keys), jnp.int32),
                            pltpu.VMEM((IDX_HEADS * t, PAGE_SIZE), jnp.float32),
                            pltpu.VMEM((t, LANE), jnp.int32)]),
        out_shape=jax.ShapeDtypeStruct((b, t, n_keys), jnp.float32),
        compiler_params=params,
        name="sample_index",
    )(page_table, iq_r, iw_r, ik_new, *([pages_t(cache_idx_k)] * pp_index))

    ck = pages_t(cache_k)
    cv = pages_t(cache_v)
    out = pl.pallas_call(
        functools.partial(_sample_attend_kernel, pp=pp, n_pages=n_pages),
        grid_spec=pltpu.PrefetchScalarGridSpec(
            num_scalar_prefetch=1, grid=(b, npg),
            in_specs=[per_seq((n_rows, KV_W)),
                      pl.BlockSpec((1, t, pp * PAGE_SIZE), lambda bi, pg, pt: (bi, 0, pg)),
                      pl.BlockSpec((1, t, PAGE_SIZE), lambda bi, pg, pt: (bi, 0, n_pages)),
                      per_seq((KV_W, PAGE_SIZE)), per_seq((KV_W, PAGE_SIZE))]
            + [pl.BlockSpec((1, KV_W, PAGE_SIZE), page_map(pp, j)) for j in range(pp)] * 2,
            out_specs=per_seq((n_rows, LANE)),
            scratch_shapes=[pltpu.VMEM((n_rows, LANE), jnp.float32), pltpu.VMEM((n_rows, LANE), jnp.float32),
                            pltpu.VMEM((n_rows, KV_W), jnp.float32)]),
        out_shape=jax.ShapeDtypeStruct((b, n_rows, LANE), jnp.float32),
        compiler_params=params,
        name="sample_attend",
    )(page_table, q_bd, bias, bias, k_new, v_new, *([ck] * pp), *([cv] * pp))
    out = out[:, :, :HEAD_DIM].reshape(b, N_HEADS, t, HEAD_DIM).transpose(0, 2, 1, 3)
    return out.reshape(b, t, N_HEADS * HEAD_DIM)


GELU_C = math.sqrt(2.0 / math.pi)


def _rglru_kernel(x_ref, xg_ref, prev_ref, h0_ref, cw_ref, cb_ref, wa_ref, ba_ref, wi_ref, bi_ref, lam_ref,
                  o_ref, hl_ref, conv_ref, a_ref, u_ref, xprev_ref, h_ref, *, rows, per_group_state):
    i = pl.program_id(0)
    n_groups = rows // SUBLANE
    d_rnn = x_ref.shape[1]
    r8 = lax.broadcasted_iota(jnp.int32, (SUBLANE, d_rnn), 0)
    grp = lambda g: pl.ds(pl.multiple_of(g * SUBLANE, SUBLANE), SUBLANE)

    if not per_group_state:
        @pl.when(i == 0)
        def _():
            xprev_ref[...] = jnp.zeros_like(xprev_ref)
            h_ref[...] = jnp.zeros_like(h_ref)

    cw = cw_ref[...]
    cb = cb_ref[...]

    def conv_group(g, prev):
        x8 = x_ref[grp(g), :]
        if per_group_state:
            prev = prev_ref[grp(g), :]
        out = cb + x8 * cw[CONV_W - 1:CONV_W, :]
        for d in range(1, CONV_W):
            shifted = jnp.where(r8 < d, pltpu.roll(prev, d, axis=0), pltpu.roll(x8, d, axis=0))
            out = out + shifted * cw[CONV_W - 1 - d:CONV_W - d, :]
        conv_ref[grp(g), :] = out
        return x8

    zeros8 = jnp.zeros((SUBLANE, d_rnn), jnp.float32)
    xlast = lax.fori_loop(0, n_groups, conv_group, zeros8 if per_group_state else xprev_ref[...])
    if not per_group_state:
        xprev_ref[...] = xlast

    xc = conv_ref[...]
    xb = xc.astype(jnp.bfloat16)
    r_parts, i_parts = [], []
    for n in range(RG_BLOCKS):
        xn = xb[:, n * RG_BLOCK_W:(n + 1) * RG_BLOCK_W]
        r_parts.append(jnp.dot(xn, wa_ref[n], preferred_element_type=jnp.float32))
        i_parts.append(jnp.dot(xn, wi_ref[n], preferred_element_type=jnp.float32))
    r = jax.nn.sigmoid(jnp.concatenate(r_parts, axis=-1) + ba_ref[...])
    ig = jax.nn.sigmoid(jnp.concatenate(i_parts, axis=-1) + bi_ref[...])
    lam = lam_ref[...]
    log_sig_lam = -(jnp.maximum(-lam, 0.0) + jnp.log1p(jnp.exp(-jnp.abs(lam))))
    log_a = RG_C * r * log_sig_lam
    a = jnp.exp(log_a)
    a_ref[...] = a
    u_ref[...] = jnp.sqrt(-jnp.tanh(log_a) * (a * a + 1.0)) * (ig * xc)

    def scan_group(g, hprev):
        a = a_ref[grp(g), :]
        u = u_ref[grp(g), :]
        if per_group_state:
            hprev = jnp.broadcast_to(h0_ref[pl.ds(g, 1), :], (SUBLANE, d_rnn))
        for d in (1, 2, 4):
            u = jnp.where(r8 >= d, a * pltpu.roll(u, d, axis=0) + u, u)
            a = jnp.where(r8 >= d, a * pltpu.roll(a, d, axis=0), a)
        h = a * hprev + u
        u_ref[grp(g), :] = h
        hlast = jnp.broadcast_to(h[SUBLANE - 1:SUBLANE, :], (SUBLANE, d_rnn))
        if per_group_state:
            hl_ref[pl.ds(g, 1), :] = h[SUBLANE - 1:SUBLANE, :]
        return hlast

    hlast = lax.fori_loop(0, n_groups, scan_group, zeros8 if per_group_state else h_ref[...])
    if not per_group_state:
        h_ref[...] = hlast
        hl_ref[...] = hlast

    xg = xg_ref[...]
    gelu = 0.5 * xg * (1.0 + jnp.tanh(GELU_C * (xg + 0.044715 * (xg * xg * xg))))
    o_ref[...] = u_ref[...] * gelu


def _rglru(z, row0, n_rows, prev, h0, conv_w, conv_b, w_a, b_a, w_i, b_i, lam, *, per_group_state, tile_rows):
    d = D_RNN
    assert row0 % tile_rows == 0 and n_rows % tile_rows == 0
    row = lambda a: a.reshape(1, d)
    const = lambda shape: pl.BlockSpec(shape, lambda i: (0,) * len(shape))
    n_hl = n_rows // SUBLANE if per_group_state else SUBLANE
    body = functools.partial(_rglru_kernel, rows=tile_rows, per_group_state=per_group_state)
    tile = pl.BlockSpec((tile_rows, d), lambda i: (i, 0))
    z_col = lambda name: pl.BlockSpec((tile_rows, d), lambda i, c=Z_COLS[name] // d: (row0 // tile_rows + i, c))
    if per_group_state:
        assert tile_rows == n_rows
    return pl.pallas_call(
        body,
        grid=(n_rows // tile_rows,),
        in_specs=[z_col("xr"), z_col("xg"), const(prev.shape), const(h0.shape), const((CONV_W, d)), const((1, d)),
                  const(w_a.shape), const((1, d)), const(w_i.shape), const((1, d)), const((1, d))],
        out_specs=[tile, const((n_hl, d))],
        out_shape=[jax.ShapeDtypeStruct((n_rows, d), jnp.float32), jax.ShapeDtypeStruct((n_hl, d), jnp.float32)],
        scratch_shapes=[pltpu.VMEM((tile_rows, d), jnp.float32), pltpu.VMEM((tile_rows, d), jnp.float32),
                        pltpu.VMEM((tile_rows, d), jnp.float32), pltpu.VMEM((SUBLANE, d), jnp.float32),
                        pltpu.VMEM((SUBLANE, d), jnp.float32)],
        compiler_params=pltpu.CompilerParams(dimension_semantics=("arbitrary",), vmem_limit_bytes=VMEM_LIMIT_BYTES),
        name="rglru_sample" if per_group_state else "rglru_prompt",
    )(z, z, prev, h0, conv_w, row(conv_b), w_a.astype(jnp.bfloat16), row(b_a), w_i.astype(jnp.bfloat16), row(b_i), row(lam))


def _rglru_prompt(z, row0, n_rows, conv_w, conv_b, w_a, b_a, w_i, b_i, lam, tile_rows=512):
    dummy = jnp.zeros((SUBLANE, D_RNN), jnp.float32)
    rnn, hl = _rglru(z, row0, n_rows, dummy, dummy, conv_w, conv_b, w_a, b_a, w_i, b_i, lam,
                     per_group_state=False, tile_rows=tile_rows)
    return rnn, hl[0]


def _rglru_sample(z, row0, state_conv, state_h, conv_w, conv_b, w_a, b_a, w_i, b_i, lam):
    b, d = state_h.shape
    prev = jnp.concatenate([jnp.zeros((b, SUBLANE - (CONV_W - 1), d), jnp.float32), state_conv], axis=1)
    return _rglru(z, row0, b * SUBLANE, prev.reshape(b * SUBLANE, d), state_h,
                  conv_w, conv_b, w_a, b_a, w_i, b_i, lam, per_group_state=True, tile_rows=b * SUBLANE)


def _rope_tables(pos):
    half = HEAD_DIM // 2
    inv_freq = ROPE_THETA ** (-jnp.arange(half, dtype=jnp.float32) / half)
    ang = pos.astype(jnp.float32)[:, None] * inv_freq[None, :]
    cos, sin = jnp.cos(ang), jnp.sin(ang)
    cos2 = jnp.concatenate([cos, cos], axis=-1)
    sin2 = jnp.concatenate([-sin, sin], axis=-1)
    return jnp.tile(cos2, (1, LANE // HEAD_DIM)), jnp.tile(sin2, (1, LANE // HEAD_DIM))


def _rotate(x, cos2, sin2):
    width = x.shape[1]
    half = HEAD_DIM // 2
    lane = lax.broadcasted_iota(jnp.int32, x.shape, 1)
    partner = jnp.where((lane & half) == 0, pltpu.roll(x, width - half, axis=1), pltpu.roll(x, half, axis=1))
    reps = width // LANE
    return x * jnp.tile(cos2, (1, reps)) + partner * jnp.tile(sin2, (1, reps))


def _head_mean_square(x, seg_ref):
    sq = x * x
    hi = sq.astype(jnp.bfloat16)
    lo = (sq - hi.astype(jnp.float32)).astype(jnp.bfloat16)
    seg = seg_ref[...]
    return (jnp.dot(hi, seg, preferred_element_type=jnp.float32)
            + jnp.dot(lo, seg, preferred_element_type=jnp.float32))


def _qk_post_kernel(q_ref, k_ref, iq_ref, ikw_ref, cos_ref, sin_ref, qg_ref, kg_ref, segq_ref, segk_ref,
                    qo_ref, ko_ref, iqo_ref, iko_ref, iwo_ref):
    cos2, sin2 = cos_ref[...], sin_ref[...]
    q = q_ref[...]
    qn = q * lax.rsqrt(_head_mean_square(q, segq_ref) + EPS) * qg_ref[...]
    qo_ref[...] = _rotate(qn, cos2, sin2)
    k = k_ref[...]
    kn = k * lax.rsqrt(_head_mean_square(k, segk_ref) + EPS) * kg_ref[...]
    ko_ref[...] = _rotate(kn, cos2, sin2)
    iqo_ref[...] = _rotate(iq_ref[...], cos2, sin2)
    ikw = ikw_ref[...]
    iko_ref[...] = _rotate(ikw, cos2, sin2)
    iwo_ref[...] = ikw * (IDX_HEADS ** -0.5 * IDX_DIM ** -0.5)


def _qk_post(z, pos, q_norm_g, k_norm_g, tm=512):
    n = z.shape[0]
    cos2, sin2 = _rope_tables(pos)
    seg = lambda w: (jnp.kron(jnp.eye(w // HEAD_DIM), jnp.ones((HEAD_DIM, HEAD_DIM))) / HEAD_DIM).astype(jnp.bfloat16)
    col = lambda name, w: pl.BlockSpec((tm, w), lambda i, c=Z_COLS[name] // w: (i, c))
    const = lambda shape: pl.BlockSpec(shape, lambda i: (0,) * len(shape))
    rows = lambda w: pl.BlockSpec((tm, w), lambda i: (i, 0))
    f32 = jnp.float32
    return pl.pallas_call(
        _qk_post_kernel,
        grid=(n // tm,),
        in_specs=[col("q", Q_W), col("k", KV_W), col("iq", IQ_W), col("ikw", LANE), rows(LANE), rows(LANE),
                  const((1, Q_W)), const((1, KV_W)), const((Q_W, Q_W)), const((KV_W, KV_W))],
        out_specs=[rows(Q_W), rows(KV_W), rows(IQ_W), rows(LANE), rows(LANE)],
        out_shape=[jax.ShapeDtypeStruct((n, Q_W), f32), jax.ShapeDtypeStruct((n, KV_W), f32),
                   jax.ShapeDtypeStruct((n, IQ_W), f32), jax.ShapeDtypeStruct((n, LANE), f32),
                   jax.ShapeDtypeStruct((n, LANE), f32)],
        compiler_params=pltpu.CompilerParams(dimension_semantics=("arbitrary",), vmem_limit_bytes=VMEM_LIMIT_BYTES),
        name="qk_post",
    )(z, z, z, z, cos2, sin2, jnp.tile(q_norm_g, N_HEADS).reshape(1, Q_W), jnp.tile(k_norm_g, KV_HEADS).reshape(1, KV_W),
      seg(Q_W), seg(KV_W))


def _out_proj_kernel(x_ref, attn_ref, rnn_ref, ga_ref, gr_ref, wa_ref, wr_ref, wo_ref, o_ref):
    bf = jnp.bfloat16
    a = jnp.dot(attn_ref[...].astype(bf), wa_ref[...], preferred_element_type=jnp.float32)
    r = jnp.dot(rnn_ref[...].astype(bf), wr_ref[...], preferred_element_type=jnp.float32)
    merged = jax.nn.sigmoid(ga_ref[...]) * a + jax.nn.sigmoid(gr_ref[...]) * r
    o_ref[...] = x_ref[...] + jnp.dot(merged.astype(bf), wo_ref[...], preferred_element_type=jnp.float32)


def _out_proj(x, attn, rnn, z, w_attn_out, w_rnn_out, w_out, tm=512):
    n, d = x.shape
    bf = jnp.bfloat16
    rows = pl.BlockSpec((tm, d), lambda i: (i, 0))
    col = lambda name: pl.BlockSpec((tm, d), lambda i, c=Z_COLS[name] // d: (i, c))
    const = pl.BlockSpec((d, d), lambda i: (0, 0))
    return pl.pallas_call(
        _out_proj_kernel,
        grid=(n // tm,),
        in_specs=[rows, rows, rows, col("ga"), col("gr"), const, const, const],
        out_specs=rows,
        out_shape=jax.ShapeDtypeStruct((n, d), jnp.float32),
        compiler_params=pltpu.CompilerParams(dimension_semantics=("arbitrary",), vmem_limit_bytes=VMEM_LIMIT_BYTES),
        name="out_proj",
    )(x, attn, rnn, z, z, w_attn_out.astype(bf), w_rnn_out.astype(bf), w_out.astype(bf))


def _rms_norm_rows(x, g):
    return x * lax.rsqrt(jnp.mean(x * x, axis=-1, keepdims=True) + EPS) * g


def _router_kernel(x_ref, g_ref, wr_ref, br_ref, tri_ref, e_ref, gate_ref, rank_ref, cnt_ref, run_ref):
    i = pl.program_id(0)
    tm = x_ref.shape[0]

    @pl.when(i == 0)
    def _():
        run_ref[...] = jnp.zeros_like(run_ref)

    xn_bf = _rms_norm_rows(x_ref[...], g_ref[...]).astype(jnp.bfloat16)
    logits = lax.dot_general(wr_ref[...], xn_bf, (((1,), (1,)), ((), ())), preferred_element_type=jnp.float32)
    logits = logits + br_ref[...]

    expert = lax.broadcasted_iota(jnp.int32, (N_EXPERTS, tm), 0)
    member = jnp.zeros((N_EXPERTS, tm), jnp.float32)
    picked, values = [], []
    for k in range(TOP_K):
        mx = jnp.max(logits, axis=0, keepdims=True)
        idx = jnp.min(jnp.where(logits == mx, expert, N_EXPERTS), axis=0, keepdims=True)
        hit = expert == idx
        member = jnp.where(hit, 1.0, member)
        logits = jnp.where(hit, -jnp.inf, logits)
        picked.append(idx)
        values.append(mx)
        e_ref[k:k + 1, :] = idx

    ex = [jnp.exp(v - values[0]) for v in values]
    denom = ex[0] + ex[1] + ex[2] + ex[3]
    for k in range(TOP_K):
        gate_ref[k:k + 1, :] = ex[k] / denom

    before = jnp.dot(member.astype(jnp.bfloat16), tri_ref[...], preferred_element_type=jnp.float32)
    before = before + jnp.tile(run_ref[...], (1, tm // LANE))
    for k in range(TOP_K):
        r = jnp.sum(jnp.where(expert == picked[k], before, 0.0), axis=0, keepdims=True)
        rank_ref[k:k + 1, :] = r.astype(jnp.int32)
    run = run_ref[...] + jnp.sum(member, axis=1, keepdims=True)
    run_ref[...] = run
    cnt_ref[...] = run.astype(jnp.int32)


def _route(x2d, norm_g, w_router, b_router, tm=512):
    n, d = x2d.shape
    tri = (jnp.arange(tm)[:, None] < jnp.arange(tm)[None, :]).astype(jnp.bfloat16)
    const = lambda shape: pl.BlockSpec(shape, lambda i: (0,) * len(shape))
    rows4 = pl.BlockSpec((TOP_K, tm), lambda i: (0, i))
    top_e, gates, rank, cnt = pl.pallas_call(
        _router_kernel,
        grid=(n // tm,),
        in_specs=[pl.BlockSpec((tm, d), lambda i: (i, 0)), const((1, d)), const((N_EXPERTS, d)),
                  const((N_EXPERTS, 1)), const((tm, tm))],
        out_specs=[rows4, rows4, rows4, const((N_EXPERTS, LANE))],
        out_shape=[jax.ShapeDtypeStruct((TOP_K, n), jnp.int32),
                   jax.ShapeDtypeStruct((TOP_K, n), jnp.float32), jax.ShapeDtypeStruct((TOP_K, n), jnp.int32),
                   jax.ShapeDtypeStruct((N_EXPERTS, LANE), jnp.int32)],
        scratch_shapes=[pltpu.VMEM((N_EXPERTS, LANE), jnp.float32)],
        compiler_params=pltpu.CompilerParams(dimension_semantics=("arbitrary",)),
        name="moe_router",
    )(x2d, norm_g.reshape(1, d), w_router.T.astype(jnp.bfloat16), b_router.reshape(N_EXPERTS, 1), tri)
    return top_e, gates, rank, cnt[:, 0]


def _expert_kernel(be_ref, used_ref, xs_ref, g_ref, wgu_ref, bgu_ref, wd_ref, bd_ref, o_ref, *, blk):
    i = pl.program_id(0)

    @pl.when(i * blk < used_ref[0])
    def _():
        xn = _rms_norm_rows(xs_ref[...], g_ref[...]).astype(jnp.bfloat16)
        gu = jnp.dot(xn, wgu_ref[0], preferred_element_type=jnp.float32) + bgu_ref[0]
        g = jnp.minimum(gu[:, :D_FF], SWIGLU_LIMIT)
        u = jnp.clip(gu[:, D_FF:], -SWIGLU_LIMIT, SWIGLU_LIMIT)
        act = (u + 1.0) * (g * jax.nn.sigmoid(SWIGLU_ALPHA * g))
        o_ref[...] = jnp.dot(act.astype(jnp.bfloat16), wd_ref[0], preferred_element_type=jnp.float32) + bd_ref[0]

    @pl.when(i * blk >= used_ref[0])
    def _():
        o_ref[...] = jnp.zeros_like(o_ref)


def _experts(xs, norm_g, blk_exp, n_used, w_gate_up, b_gate_up, w_down, b_down, *, blk):
    n_slots, d = xs.shape
    n_blocks = n_slots // blk
    bf = jnp.bfloat16
    return pl.pallas_call(
        functools.partial(_expert_kernel, blk=blk),
        grid_spec=pltpu.PrefetchScalarGridSpec(
            num_scalar_prefetch=2, grid=(n_blocks,),
            in_specs=[pl.BlockSpec((blk, d), lambda i, be, nu: (i, 0)),
                      pl.BlockSpec((1, d), lambda i, be, nu: (0, 0)),
                      pl.BlockSpec((1, d, 2 * D_FF), lambda i, be, nu: (be[i], 0, 0)),
                      pl.BlockSpec((1, 1, 2 * D_FF), lambda i, be, nu: (be[i], 0, 0)),
                      pl.BlockSpec((1, D_FF, d), lambda i, be, nu: (be[i], 0, 0)),
                      pl.BlockSpec((1, 1, d), lambda i, be, nu: (be[i], 0, 0))],
            out_specs=pl.BlockSpec((blk, d), lambda i, be, nu: (i, 0))),
        out_shape=jax.ShapeDtypeStruct((n_slots, d), jnp.float32),
        compiler_params=pltpu.CompilerParams(dimension_semantics=("arbitrary",), vmem_limit_bytes=VMEM_LIMIT_BYTES),
        name="moe_experts",
    )(blk_exp, n_used, xs, norm_g.reshape(1, d), w_gate_up.astype(bf), b_gate_up.reshape(N_EXPERTS, 1, 2 * D_FF),
      w_down.astype(bf), b_down.reshape(N_EXPERTS, 1, d))


def _row_copy(src_hbm, src_row, dst_ref, dst_row, sem):
    return pltpu.make_async_copy(src_hbm.at[pl.ds(src_row, 1)], dst_ref.at[pl.ds(dst_row, 1)], sem)


def _dispatch_kernel(dest_ref, x_ref, xs_in_hbm, xs_hbm, sem, *, tm):
    del xs_in_hbm

    def issue(t, carry):
        for k in range(TOP_K):
            _row_copy(x_ref, t, xs_hbm, dest_ref[k, t], sem).start()
        return carry

    lax.fori_loop(0, tm, issue, 0)

    def drain(t, carry):
        for k in range(TOP_K):
            _row_copy(x_ref, 0, xs_hbm, 0, sem).wait()
        return carry

    lax.fori_loop(0, tm, drain, 0)


def _dispatch(x2d, dest, n_slots, tm=128):
    n, d = x2d.shape
    return pl.pallas_call(
        functools.partial(_dispatch_kernel, tm=tm),
        grid=(n // tm,),
        in_specs=[pl.BlockSpec((TOP_K, tm), lambda i: (0, i), memory_space=pltpu.SMEM),
                  pl.BlockSpec((tm, d), lambda i: (i, 0)), pl.BlockSpec(memory_space=pl.ANY)],
        out_specs=pl.BlockSpec(memory_space=pl.ANY),
        out_shape=jax.ShapeDtypeStruct((n_slots, d), x2d.dtype),
        scratch_shapes=[pltpu.SemaphoreType.DMA(())],
        input_output_aliases={2: 0},
        compiler_params=pltpu.CompilerParams(dimension_semantics=("arbitrary",)),
        name="moe_dispatch",
    )(dest, x2d, jnp.zeros((n_slots, d), x2d.dtype))


def _combine_kernel(dest_ref, x_ref, gate_ref, ys_hbm, o_ref, buf_ref, sem, *, tm):
    def issue(t, carry):
        for k in range(TOP_K):
            _row_copy(ys_hbm, dest_ref[k, t], buf_ref.at[k], t, sem).start()
        return carry

    lax.fori_loop(0, tm, issue, 0)

    def drain(t, carry):
        for k in range(TOP_K):
            _row_copy(ys_hbm, 0, buf_ref.at[k], 0, sem).wait()
        return carry

    lax.fori_loop(0, tm, drain, 0)
    gate = gate_ref[...]
    acc = x_ref[...]
    for k in range(TOP_K):
        acc = acc + gate[:, k:k + 1] * buf_ref[k]
    o_ref[...] = acc


def _combine(x2d, ys, dest, gates_t, tm=128):
    n, d = x2d.shape
    return pl.pallas_call(
        functools.partial(_combine_kernel, tm=tm),
        grid=(n // tm,),
        in_specs=[pl.BlockSpec((TOP_K, tm), lambda i: (0, i), memory_space=pltpu.SMEM),
                  pl.BlockSpec((tm, d), lambda i: (i, 0)), pl.BlockSpec((tm, TOP_K), lambda i: (i, 0)),
                  pl.BlockSpec(memory_space=pl.ANY)],
        out_specs=pl.BlockSpec((tm, d), lambda i: (i, 0)),
        out_shape=jax.ShapeDtypeStruct((n, d), jnp.float32),
        scratch_shapes=[pltpu.VMEM((TOP_K, tm, d), jnp.float32), pltpu.SemaphoreType.DMA(())],
        compiler_params=pltpu.CompilerParams(dimension_semantics=("arbitrary",)),
        name="moe_combine",
    )(dest, x2d, gates_t, ys)


MOE_ROWS = 512


def _moe(x2d, norm_g, w_router, b_router, w_gate_up, b_gate_up, w_down, b_down, *, blk=MOE_ROWS, tm=512):
    n, d = x2d.shape
    top_e, gates, rank, counts = _route(x2d, norm_g, w_router, b_router, tm=tm)
    padded = (counts + blk - 1) // blk * blk
    pad_end = jnp.cumsum(padded)
    pad_start = pad_end - padded
    experts = jnp.arange(N_EXPERTS, dtype=jnp.int32)
    start_of = jnp.sum(jnp.where(top_e[..., None] == experts, pad_start.astype(jnp.int32), 0), axis=-1)
    dest = start_of + rank
    n_blocks = -(-(n * TOP_K + N_EXPERTS * (blk - 1)) // blk)
    n_slots = n_blocks * blk
    block_start = jnp.arange(n_blocks, dtype=jnp.int32) * blk
    blk_exp = jnp.minimum(jnp.sum(pad_end[None, :] <= block_start[:, None], axis=1), N_EXPERTS - 1).astype(jnp.int32)
    n_used = pad_end[-1:].astype(jnp.int32)
    xs = _dispatch(x2d, dest, n_slots)
    ys = _experts(xs, norm_g, blk_exp, n_used, w_gate_up, b_gate_up, w_down, b_down, blk=blk)
    return _combine(x2d, ys, dest, gates.T)


def kernel(x_prompt, x_sample, cache_k, cache_v, cache_idx_k, state_conv, state_h, page_table,
           norm1_g, w_in, q_norm_g, k_norm_g, conv_w, conv_b, rg_w_a, rg_b_a, rg_w_i, rg_b_i, rg_lambda,
           w_attn_out, w_rnn_out, w_out, norm2_g, w_router, b_router, w_gate_up, b_gate_up, w_down, b_down):
    assert x_prompt.shape[0] == 1 and norm1_g.shape[0] == 1 and x_sample.shape[1] == SUBLANE
    s_len = x_prompt.shape[1]
    db, t_new = x_sample.shape[:2]
    n_s = db * t_new
    past = page_table.shape[1] * PAGE_SIZE
    pos = jnp.concatenate([jnp.arange(s_len, dtype=jnp.int32),
                           jnp.tile(past + jnp.arange(t_new, dtype=jnp.int32), db)])
    x_all = jnp.concatenate([x_prompt.reshape(s_len, D_MODEL), x_sample.reshape(n_s, D_MODEL)], axis=0)

    z = _norm_proj(x_all, norm1_g[0], _permuted_w_in(w_in[0]))
    q, k, iq, ik, iw = _qk_post(z, pos, q_norm_g[0], k_norm_g[0])
    v = z[:, Z_COLS["v"]:Z_COLS["v"] + KV_W]
    xr = z[:, Z_COLS["xr"]:Z_COLS["xr"] + D_RNN]
    ik = ik[:, :IDX_DIM]
    iw = iw[:, IDX_DIM:IDX_DIM + IDX_HEADS]
    heads = lambda a, lo, hi, n_heads: a[lo:hi].reshape(hi - lo, n_heads, -1)
    seqs = lambda a: a.reshape((db, t_new) + a.shape[1:])
    p_rows, s_rows = (0, s_len), (s_len, s_len + n_s)

    attn_p = _prompt_dsa_attention_t(heads(q, *p_rows, N_HEADS), heads(k, *p_rows, KV_HEADS), heads(v, *p_rows, KV_HEADS),
                                     heads(iq, *p_rows, IDX_HEADS), iw[:s_len], ik[:s_len])
    attn_s = _sample_dsa_attention(seqs(heads(q, *s_rows, N_HEADS)), seqs(heads(k, *s_rows, KV_HEADS)),
                                   seqs(heads(v, *s_rows, KV_HEADS)), seqs(heads(iq, *s_rows, IDX_HEADS)),
                                   seqs(iw[s_len:]), seqs(ik[s_len:]), cache_k[0], cache_v[0], cache_idx_k[0], page_table)
    rg = (conv_w[0], conv_b[0], rg_w_a[0], rg_b_a[0], rg_w_i[0], rg_b_i[0], rg_lambda[0])
    rnn_p, h_p = _rglru_prompt(z, 0, s_len, *rg)
    rnn_s, h_s = _rglru_sample(z, s_len, state_conv[0], state_h[0], *rg)

    attn = jnp.concatenate([attn_p, attn_s.reshape(n_s, Q_W)], axis=0)
    rnn = jnp.concatenate([rnn_p, rnn_s], axis=0)
    x_mid = _out_proj(x_all, attn, rnn, z, w_attn_out[0], w_rnn_out[0], w_out[0])
    y = _moe(x_mid, norm2_g[0], w_router[0], b_router[0], w_gate_up[0], b_gate_up[0], w_down[0], b_down[0])

    tail = CONV_W - 1
    xr_s = xr[s_len:].reshape(db, t_new, D_RNN)
    conv_s = jnp.concatenate([state_conv[0], xr_s], axis=1)[:, -tail:]
    kv_p = lambda a: a[:s_len].reshape(1, 1, s_len, KV_HEADS, HEAD_DIM)
    kv_s = lambda a: a[s_len:].reshape(1, db, t_new, KV_HEADS, HEAD_DIM)
    return (y[:s_len].reshape(x_prompt.shape), y[s_len:].reshape(x_sample.shape),
            kv_p(k), kv_p(v), ik[:s_len].reshape(1, 1, s_len, IDX_DIM),
            xr[s_len - tail:s_len].reshape(1, 1, tail, D_RNN), h_p.reshape(1, 1, D_RNN),
            kv_s(k), kv_s(v), ik[s_len:].reshape(1, db, t_new, IDX_DIM),
            conv_s[None], h_s[None])
```

```python
import functools
import math

import jax
import jax.numpy as jnp
import numpy as np
from jax import lax
from jax.experimental import pallas as pl
from jax.experimental.pallas import tpu as pltpu

D_MODEL = 1024
PAGE_SIZE = 128
N_HEADS = 16
HEAD_DIM = 64
KV_HEADS = 4
IDX_HEADS = 8
IDX_DIM = 64
IDX_TOPK_MAX = 256
Q_BLOCK = 128
ROPE_THETA = 10000.0
D_RNN = D_MODEL
RG_BLOCKS = 4
RG_BLOCK_W = D_RNN // RG_BLOCKS
CONV_W = 4
RG_C = 8.0
N_EXPERTS = 32
TOP_K = 4
D_FF = D_MODEL
SWIGLU_LIMIT = 7.0
SWIGLU_ALPHA = 1.702
MOE_BLOCK = 128
EPS = 1e-6

Q_W = N_HEADS * HEAD_DIM
KV_W = KV_HEADS * HEAD_DIM
IQ_W = IDX_HEADS * IDX_DIM
IN_NAMES = ("q", "k", "v", "iq", "ik", "iw", "xr", "xg", "ga", "gr")
IN_WIDTHS = (Q_W, KV_W, KV_W, IQ_W, IDX_DIM, IDX_HEADS, D_RNN, D_RNN, D_MODEL, D_MODEL)

LANE = 128
SUBLANE = 8
VMEM_LIMIT_BYTES = 48 * 1024 * 1024

Z_ORDER = ("q", "k", "v", "iq", "xr", "xg", "ga", "gr")
Z_COLS = {}
_col = 0
for _name in Z_ORDER:
    Z_COLS[_name] = _col
    _col += IN_WIDTHS[IN_NAMES.index(_name)]
Z_COLS["ikw"] = _col
Z_W = _col + LANE
PROJ_TN = 896


def _permuted_w_in(w_in):
    offs = np.concatenate([[0], np.cumsum(IN_WIDTHS)])
    seg = {name: w_in[:, offs[j]:offs[j + 1]] for j, name in enumerate(IN_NAMES)}
    pad = jnp.zeros((w_in.shape[0], LANE - IDX_DIM - IDX_HEADS), w_in.dtype)
    return jnp.concatenate([seg[name] for name in Z_ORDER] + [seg["ik"], seg["iw"], pad], axis=1).astype(jnp.bfloat16)


def _norm_proj_kernel(x_ref, g_ref, w_ref, o_ref):
    x = x_ref[...]
    y = x * lax.rsqrt(jnp.mean(x * x, axis=-1, keepdims=True) + EPS) * g_ref[...]
    o_ref[...] = jnp.dot(y.astype(jnp.bfloat16), w_ref[...], preferred_element_type=jnp.float32)


def _norm_proj(x2d, g, w_bf16, tm=512, tn=PROJ_TN):
    m, d = x2d.shape
    n = w_bf16.shape[1]
    return pl.pallas_call(
        _norm_proj_kernel,
        grid=(m // tm, n // tn),
        in_specs=[
            pl.BlockSpec((tm, d), lambda i, j: (i, 0)),
            pl.BlockSpec((1, d), lambda i, j: (0, 0)),
            pl.BlockSpec((d, tn), lambda i, j: (0, j)),
        ],
        out_specs=pl.BlockSpec((tm, tn), lambda i, j: (i, j)),
        out_shape=jax.ShapeDtypeStruct((m, n), jnp.float32),
        name="norm_proj",
    )(x2d, g.reshape(1, d), w_bf16)


INT_MIN = -(2 ** 31)
INT_MAX = 2 ** 31 - 1
MASK_BIAS = -1e30
NEG_INF_BITS_MASK = 0x7FFFFFFF


def _ordered_key(x):
    bits = pltpu.bitcast(x, jnp.int32)
    return bits ^ ((bits >> 31) & NEG_INF_BITS_MASK)


def _topk_threshold(key_ref, cut_ref, nk, *, rows, kb, topk, n_idx_bits):
    n_lane_tiles = kb // LANE
    lane = lax.broadcasted_iota(jnp.int32, (rows, kb), 1)

    def count(pred_of_chunk):
        def body(c, cnt):
            p = pred_of_chunk(c, key_ref[c])
            for j in range(n_lane_tiles):
                cnt = cnt + jnp.where(p[:, j * LANE:(j + 1) * LANE], 1, 0)
            return cnt
        cnt = lax.fori_loop(0, nk, body, jnp.zeros((rows, LANE), jnp.int32))
        return jnp.sum(cnt, axis=-1, keepdims=True)

    def bit_cond(carry):
        b, _, n_ge = carry
        return jnp.logical_and(b < 32, jnp.max(jnp.abs(n_ge - topk)) > 0)

    def bit_step(carry):
        b, thr, n_ge = carry
        cand = thr ^ (jnp.int32(1) << (31 - b))
        cand_b = jnp.broadcast_to(cand, (rows, kb))
        cnt = count(lambda c, key: key >= cand_b)
        ok = cnt >= topk
        return b + 1, jnp.where(ok, cand, thr), jnp.where(ok, cnt, n_ge)

    thr0 = jnp.full((rows, 1), INT_MIN, jnp.int32)
    n0 = jnp.full((rows, 1), 0, jnp.int32) + nk * kb
    _, thr, n_ge = lax.while_loop(bit_cond, bit_step, (jnp.int32(0), thr0, n0))
    thr_b = jnp.broadcast_to(thr, (rows, kb))

    cut_ref[...] = jnp.full((rows, LANE), INT_MAX, jnp.int32)

    @pl.when(jnp.max(n_ge) > topk)
    def _():
        n_eq = count(lambda c, key: key == thr_b)
        need = topk - (n_ge - n_eq)

        def idx_step(b, lo):
            step = jnp.int32(1) << (n_idx_bits - 1 - b)
            mid_b = jnp.broadcast_to(lo + step - 1, (rows, kb))
            f = count(lambda c, key: (key == thr_b) & (c * kb + lane <= mid_b))
            return jnp.where(f < need, lo + step, lo)

        lo = lax.fori_loop(0, n_idx_bits, idx_step, jnp.zeros((rows, 1), jnp.int32))
        cut = jnp.where(n_ge > topk, lo, INT_MAX)
        cut_ref[...] = jnp.broadcast_to(cut, (rows, LANE))

    return thr


def _topk_threshold_t(key_ref, cut_ref, nk, *, kb, cols, topk, n_idx_bits):
    acc_rows = 8 * SUBLANE
    assert kb % acc_rows == 0
    pos0 = lax.broadcasted_iota(jnp.int32, (kb, cols), 0)

    def count(pred_of_chunk):
        def body(c, cnt):
            p = pred_of_chunk(c, key_ref[c])
            for j in range(kb // acc_rows):
                cnt = cnt + jnp.where(p[j * acc_rows:(j + 1) * acc_rows, :], 1, 0)
            return cnt
        cnt = lax.fori_loop(0, nk, body, jnp.zeros((acc_rows, cols), jnp.int32))
        return jnp.sum(cnt, axis=0, keepdims=True)

    def bit_cond(carry):
        b, _, n_ge = carry
        return jnp.logical_and(b < 32, jnp.max(jnp.abs(n_ge - topk)) > 0)

    def bit_step(carry):
        b, thr, n_ge = carry
        cand = thr ^ (jnp.int32(1) << (31 - b))
        cand_b = jnp.broadcast_to(cand, (kb, cols))
        cnt = count(lambda c, key: key >= cand_b)
        ok = cnt >= topk
        return b + 1, jnp.where(ok, cand, thr), jnp.where(ok, cnt, n_ge)

    thr0 = jnp.full((1, cols), INT_MIN, jnp.int32)
    n0 = jnp.full((1, cols), 0, jnp.int32) + nk * kb
    _, thr, n_ge = lax.while_loop(bit_cond, bit_step, (jnp.int32(0), thr0, n0))
    thr_b = jnp.broadcast_to(thr, (kb, cols))

    cut_ref[...] = jnp.full((SUBLANE, cols), INT_MAX, jnp.int32)

    @pl.when(jnp.max(n_ge) > topk)
    def _():
        n_eq = count(lambda c, key: key == thr_b)
        need = topk - (n_ge - n_eq)

        def idx_step(b, lo):
            step = jnp.int32(1) << (n_idx_bits - 1 - b)
            mid_b = jnp.broadcast_to(lo + step - 1, (kb, cols))
            f = count(lambda c, key: (key == thr_b) & (c * kb + pos0 <= mid_b))
            return jnp.where(f < need, lo + step, lo)

        lo = lax.fori_loop(0, n_idx_bits, idx_step, jnp.zeros((1, cols), jnp.int32))
        cut = jnp.where(n_ge > topk, lo, INT_MAX)
        cut_ref[...] = jnp.broadcast_to(cut, (SUBLANE, cols))

    return thr


UNDERFLOW_GUARD = 1e-30
LOG2_E = math.log2(math.e)


def _dsa_t_kernel(iq_ref, iw_ref, q_ref, mb_ref, ik_ref, k_ref, vt_ref, o_ref,
                  key_ref, cut_ref, m_ref, l_ref, acc_ref, *, qb, kb, topk, n_idx_bits):
    i = pl.program_id(0)
    nk = ((i + 1) * qb + kb - 1) // kb
    rep = N_HEADS // KV_HEADS
    kpos0 = lax.broadcasted_iota(jnp.int32, (kb, qb), 0)
    qpos = i * qb + lax.broadcasted_iota(jnp.int32, (kb, qb), 1)
    iw = iw_ref[0]

    def score_chunk(c, carry):
        ikc = ik_ref[c]
        acc = jnp.zeros((kb, qb), jnp.float32)
        for hp in range(IDX_HEADS // 2):
            s2 = jnp.dot(ikc, iq_ref[0, hp], preferred_element_type=jnp.float32)
            for hh in range(2):
                h = 2 * hp + hh
                acc = acc + iw[h:h + 1, :] * jnp.maximum(s2[:, hh * qb:(hh + 1) * qb], 0.0)
        acc = jnp.where(c * kb + kpos0 <= qpos, acc, -jnp.inf)
        key_ref[c] = _ordered_key(acc)
        return carry

    lax.fori_loop(0, nk, score_chunk, 0)

    thr = _topk_threshold_t(key_ref, cut_ref, nk, kb=kb, cols=qb, topk=topk, n_idx_bits=n_idx_bits)
    thr_b = jnp.broadcast_to(thr, (kb, qb))
    cut_b = jnp.broadcast_to(cut_ref[0:1, :], (kb, qb))

    def mask_bias(c):
        key = key_ref[c]
        kpos = c * kb + kpos0
        sel = (key > thr_b) | ((key == thr_b) & (kpos <= cut_b))
        sel = sel & (kpos <= qpos)
        return jnp.tile(jnp.where(sel, 0.0, MASK_BIAS), (1, rep))

    l_ref[...] = jnp.zeros(l_ref.shape, jnp.float32)
    acc_ref[...] = jnp.zeros(acc_ref.shape, jnp.float32)

    def attend_chunk(c, carry):
        bias = mask_bias(c)
        for g in range(KV_HEADS):
            s = jnp.dot(k_ref[c, g], q_ref[0, g], preferred_element_type=jnp.float32) + (bias - mb_ref[0, g])
            p = jnp.exp2(s)
            l_ref[g] = l_ref[g] + jnp.sum(p, axis=0, keepdims=True)
            acc_ref[g] = acc_ref[g] + jnp.dot(vt_ref[c, g], p.astype(jnp.bfloat16), preferred_element_type=jnp.float32)
        return carry

    lax.fori_loop(0, nk, attend_chunk, 0)

    @pl.when(jnp.min(l_ref[...]) < UNDERFLOW_GUARD)
    def _():
        m_ref[...] = jnp.full(m_ref.shape, MASK_BIAS, jnp.float32)
        l_ref[...] = jnp.zeros(l_ref.shape, jnp.float32)
        acc_ref[...] = jnp.zeros(acc_ref.shape, jnp.float32)

        def attend_chunk_running_max(c, carry):
            bias = mask_bias(c)
            for g in range(KV_HEADS):
                s = jnp.dot(k_ref[c, g], q_ref[0, g], preferred_element_type=jnp.float32) + bias
                m_prev = m_ref[g]
                m_new = jnp.maximum(m_prev, jnp.max(s, axis=0, keepdims=True))
                alpha = jnp.exp2(m_prev - m_new)
                p = jnp.exp2(s - m_new)
                l_ref[g] = alpha * l_ref[g] + jnp.sum(p, axis=0, keepdims=True)
                pv = jnp.dot(vt_ref[c, g], p.astype(jnp.bfloat16), preferred_element_type=jnp.float32)
                acc_ref[g] = acc_ref[g] * alpha + pv
                m_ref[g] = m_new
            return carry

        lax.fori_loop(0, nk, attend_chunk_running_max, 0)

    for g in range(KV_HEADS):
        o_ref[0, g] = acc_ref[g] / l_ref[g]


def _prompt_dsa_attention_t(q, k, v, iq, iw, ik, *, qb=128, kb=1024):
    s_len = q.shape[0]
    topk = min(IDX_TOPK_MAX, s_len // 4)
    nqb, nkc = s_len // qb, s_len // kb
    rep = N_HEADS // KV_HEADS
    bf = jnp.bfloat16
    q_s = (q * (HEAD_DIM ** -0.5 * LOG2_E)).astype(bf)
    k_bf = k.astype(bf)
    q_norm = jnp.sqrt(jnp.sum(jnp.square(q_s.astype(jnp.float32)), axis=-1))
    k_norm = jnp.sqrt(jnp.max(jnp.sum(jnp.square(k_bf.astype(jnp.float32)), axis=-1), axis=0))
    bound = q_norm.reshape(nqb, qb, KV_HEADS, rep) * k_norm[None, None, :, None]
    bound = bound.transpose(0, 2, 3, 1).reshape(nqb, KV_HEADS, 1, rep * qb)
    q_t = q_s.reshape(nqb, qb, KV_HEADS, rep, HEAD_DIM).transpose(0, 2, 4, 3, 1)
    q_t = q_t.reshape(nqb, KV_HEADS, HEAD_DIM, rep * qb)
    iq_t = iq.astype(bf).reshape(nqb, qb, IDX_HEADS // 2, 2, IDX_DIM).transpose(0, 2, 4, 3, 1)
    iq_t = iq_t.reshape(nqb, IDX_HEADS // 2, IDX_DIM, 2 * qb)
    iw_t = iw.reshape(nqb, qb, IDX_HEADS).transpose(0, 2, 1)
    ik_c = ik.astype(bf).reshape(nkc, kb, IDX_DIM)
    k_c = k_bf.reshape(nkc, kb, KV_HEADS, HEAD_DIM).transpose(0, 2, 1, 3)
    v_t = v.astype(bf).reshape(nkc, kb, KV_HEADS, HEAD_DIM).transpose(0, 2, 3, 1)
    n_idx_bits = max(1, (s_len - 1).bit_length())
    body = functools.partial(_dsa_t_kernel, qb=qb, kb=kb, topk=topk, n_idx_bits=n_idx_bits)
    whole = lambda shape: pl.BlockSpec(shape, lambda i: (0,) * len(shape), pipeline_mode=pl.Buffered(1))
    out = pl.pallas_call(
        body,
        grid=(nqb,),
        in_specs=[
            pl.BlockSpec((1, IDX_HEADS // 2, IDX_DIM, 2 * qb), lambda i: (i, 0, 0, 0)),
            pl.BlockSpec((1, IDX_HEADS, qb), lambda i: (i, 0, 0)),
            pl.BlockSpec((1, KV_HEADS, HEAD_DIM, rep * qb), lambda i: (i, 0, 0, 0)),
            pl.BlockSpec((1, KV_HEADS, 1, rep * qb), lambda i: (i, 0, 0, 0)),
            whole((nkc, kb, IDX_DIM)),
            whole((nkc, KV_HEADS, kb, HEAD_DIM)),
            whole((nkc, KV_HEADS, HEAD_DIM, kb)),
        ],
        out_specs=pl.BlockSpec((1, KV_HEADS, HEAD_DIM, rep * qb), lambda i: (i, 0, 0, 0)),
        out_shape=jax.ShapeDtypeStruct((nqb, KV_HEADS, HEAD_DIM, rep * qb), jnp.float32),
        scratch_shapes=[
            pltpu.VMEM((nkc, kb, qb), jnp.int32),
            pltpu.VMEM((SUBLANE, qb), jnp.int32),
            pltpu.VMEM((KV_HEADS, 1, rep * qb), jnp.float32),
            pltpu.VMEM((KV_HEADS, 1, rep * qb), jnp.float32),
            pltpu.VMEM((KV_HEADS, HEAD_DIM, rep * qb), jnp.float32),
        ],
        compiler_params=pltpu.CompilerParams(dimension_semantics=("arbitrary",), vmem_limit_bytes=VMEM_LIMIT_BYTES),
        name="prompt_dsa_attention",
    )(iq_t, iw_t, q_t, bound, ik_c, k_c, v_t)
    out = out.reshape(nqb, KV_HEADS, HEAD_DIM, rep, qb).transpose(0, 4, 1, 3, 2)
    return out.reshape(s_len, N_HEADS * HEAD_DIM)


def _sample_index_kernel(pt_ref, iq_ref, iw_ref, iknew_ref, *rest, pp, n_pages, topk, n_idx_bits):
    page_refs, o_ref, (key_ref, wide_ref, wb_ref, cut_ref) = rest[:pp], rest[pp], rest[pp + 1:]
    pg = pl.program_id(1)
    t_new = o_ref.shape[1]
    step_keys = pp * PAGE_SIZE
    n_keys = (n_pages + 1) * PAGE_SIZE

    @pl.when(pg == 0)
    def _():
        wb_ref[...] = jnp.broadcast_to(iw_ref[0], wb_ref.shape)

    iq = iq_ref[0]

    def scores(ik_t):
        s = jnp.dot(iq, ik_t.astype(jnp.bfloat16), preferred_element_type=jnp.float32)
        s = jnp.tile(wb_ref[...], (1, ik_t.shape[1] // LANE)) * jnp.maximum(s, 0.0)
        acc = jnp.zeros((t_new, ik_t.shape[1]), jnp.float32)
        for h in range(IDX_HEADS):
            acc = acc + s[h * t_new:(h + 1) * t_new, :]
        return acc

    key_ref[pg] = _ordered_key(scores(jnp.concatenate([r[0] for r in page_refs], axis=1)))

    @pl.when(pg == pl.num_programs(1) - 1)
    def _():
        for c in range(n_pages // pp):
            wide_ref[0, :, c * step_keys:(c + 1) * step_keys] = key_ref[c]
        row = lax.broadcasted_iota(jnp.int32, (t_new, PAGE_SIZE), 0)
        lane = lax.broadcasted_iota(jnp.int32, (t_new, PAGE_SIZE), 1)
        s_new = jnp.where(lane <= row, scores(iknew_ref[0]), -jnp.inf)
        wide_ref[0, :, n_pages * PAGE_SIZE:] = _ordered_key(s_new)
        thr = _topk_threshold(wide_ref, cut_ref, 1, rows=t_new, kb=n_keys, topk=topk, n_idx_bits=n_idx_bits)
        key = wide_ref[0]
        kpos = lax.broadcasted_iota(jnp.int32, (t_new, n_keys), 1)
        qpos = n_pages * PAGE_SIZE + lax.broadcasted_iota(jnp.int32, (t_new, n_keys), 0)
        thr_b = jnp.broadcast_to(thr, (t_new, n_keys))
        cut_b = jnp.tile(cut_ref[...], (1, n_pages + 1))
        sel = (key > thr_b) | ((key == thr_b) & (kpos <= cut_b))
        o_ref[0] = jnp.where(sel & (kpos <= qpos), 0.0, MASK_BIAS)


def _sample_attend_kernel(pt_ref, q_ref, bias_ref, biasnew_ref, knew_ref, vnew_ref, *rest, pp, n_pages):
    k_refs, v_refs, o_ref, (m_ref, l_ref, acc_ref) = rest[:pp], rest[pp:2 * pp], rest[2 * pp], rest[2 * pp + 1:]
    pg = pl.program_id(1)
    n_rows = q_ref.shape[1]
    t_new = bias_ref.shape[1]

    @pl.when(pg == 0)
    def _():
        m_ref[...] = jnp.full(m_ref.shape, MASK_BIAS, jnp.float32)
        l_ref[...] = jnp.zeros(l_ref.shape, jnp.float32)
        acc_ref[...] = jnp.zeros(acc_ref.shape, jnp.float32)

    q = q_ref[0]

    def step(bias, k_t, v_t):
        n_keys = k_t.shape[1]
        s = jnp.dot(q, k_t.astype(jnp.bfloat16), preferred_element_type=jnp.float32)
        s = s + jnp.tile(bias, (n_rows // t_new, 1))
        m_prev = m_ref[...]
        m_new = jnp.maximum(m_prev, jnp.max(s, axis=-1, keepdims=True))
        alpha = jnp.exp(m_prev - m_new)
        p = jnp.exp(s - jnp.tile(m_new, (1, n_keys // LANE)))
        l_ref[...] = alpha * l_ref[...] + jnp.sum(p, axis=-1, keepdims=True)
        pv = lax.dot_general(p.astype(jnp.bfloat16), v_t.astype(jnp.bfloat16), (((1,), (1,)), ((), ())),
                             preferred_element_type=jnp.float32)
        acc_ref[...] = acc_ref[...] * jnp.tile(alpha, (1, KV_W // LANE)) + pv
        m_ref[...] = m_new

    step(bias_ref[0], jnp.concatenate([r[0] for r in k_refs], axis=1), jnp.concatenate([r[0] for r in v_refs], axis=1))

    @pl.when(pg == pl.num_programs(1) - 1)
    def _():
        step(biasnew_ref[0], knew_ref[0], vnew_ref[0])
        acc = acc_ref[...]
        row_head = lax.broadcasted_iota(jnp.int32, acc.shape, 0) // (n_rows // KV_HEADS)
        col_head = lax.broadcasted_iota(jnp.int32, acc.shape, 1) // HEAD_DIM
        own = jnp.where(row_head == col_head, acc, 0.0)
        folded = own[:, :LANE] + own[:, LANE:]
        folded = folded + pltpu.roll(folded, HEAD_DIM, axis=1)
        o_ref[0] = folded / l_ref[...]


def _sample_dsa_attention(q, k, v, iq, iw, ik, cache_k, cache_v, cache_idx_k, page_table, *, pp_index=64, pp=32):
    b, t = q.shape[:2]
    n_pages = page_table.shape[1]
    n_pool = cache_k.shape[0]
    past = n_pages * PAGE_SIZE
    topk = min(IDX_TOPK_MAX, (past + t) // 4)
    n_idx_bits = max(1, (past + PAGE_SIZE - 1).bit_length())
    pp_index, pp = min(pp_index, n_pages), min(pp, n_pages)
    assert n_pages % pp == 0 and n_pages % pp_index == 0 and t == SUBLANE and KV_W == 2 * LANE
    npg = n_pages // pp
    n_keys = past + PAGE_SIZE
    bf = jnp.bfloat16
    rep = N_HEADS // KV_HEADS
    n_rows = N_HEADS * t

    iq_r = iq.astype(bf).transpose(0, 2, 1, 3).reshape(b, IDX_HEADS * t, IDX_DIM)
    iw_r = iw.transpose(0, 2, 1).reshape(b, IDX_HEADS * t, 1)
    new_page = lambda a: jnp.pad(jnp.swapaxes(a, 1, 2), ((0, 0), (0, 0), (0, PAGE_SIZE - t)))
    pages_t = lambda c: jnp.swapaxes(c.reshape(n_pool, PAGE_SIZE, -1), 1, 2)
    ik_new = new_page(ik)
    k_new = new_page(k.reshape(b, t, KV_W))
    v_new = new_page(v.reshape(b, t, KV_W))
    qs = (q * HEAD_DIM ** -0.5).astype(bf).reshape(b, t, KV_HEADS, rep, HEAD_DIM).transpose(0, 2, 3, 1, 4)
    q_bd = (qs[:, :, :, :, None, :] * jnp.eye(KV_HEADS, dtype=bf)[None, :, None, None, :, None])
    q_bd = q_bd.reshape(b, n_rows, KV_W)

    page_map = lambda per_step, j: (lambda bi, pg, pt: (pt[bi, pg * per_step + j], 0, 0))
    per_seq = lambda shape: pl.BlockSpec((1,) + shape, lambda bi, pg, pt: (bi,) + (0,) * len(shape))
    params = pltpu.CompilerParams(dimension_semantics=("arbitrary", "arbitrary"), vmem_limit_bytes=VMEM_LIMIT_BYTES)

    bias = pl.pallas_call(
        functools.partial(_sample_index_kernel, pp=pp_index, n_pages=n_pages, topk=topk, n_idx_bits=n_idx_bits),
        grid_spec=pltpu.PrefetchScalarGridSpec(
            num_scalar_prefetch=1, grid=(b, n_pages // pp_index),
            in_specs=[per_seq((IDX_HEADS * t, IDX_DIM)), per_seq((IDX_HEADS * t, 1)), per_seq((IDX_DIM, PAGE_SIZE))]
            + [pl.BlockSpec((1, IDX_DIM, PAGE_SIZE), page_map(pp_index, j)) for j in range(pp_index)],
            out_specs=per_seq((t, n_keys)),
            scratch_shapes=[pltpu.VMEM((n_pages // pp_index, t, pp_index * PAGE_SIZE), jnp.int32),
                            pltpu.VMEM((1, t, n_keys), jnp.int32),
                            pltpu.VMEM((IDX_HEADS * t, PAGE_SIZE), jnp.float32),
                            pltpu.VMEM((t, LANE), jnp.int32)]),
        out_shape=jax.ShapeDtypeStruct((b, t, n_keys), jnp.float32),
        compiler_params=params,
        name="sample_index",
    )(page_table, iq_r, iw_r, ik_new, *([pages_t(cache_idx_k)] * pp_index))

    ck = pages_t(cache_k)
    cv = pages_t(cache_v)
    out = pl.pallas_call(
        functools.partial(_sample_attend_kernel, pp=pp, n_pages=n_pages),
        grid_spec=pltpu.PrefetchScalarGridSpec(
            num_scalar_prefetch=1, grid=(b, npg),
            in_specs=[per_seq((n_rows, KV_W)),
                      pl.BlockSpec((1, t, pp * PAGE_SIZE), lambda bi, pg, pt: (bi, 0, pg)),
                      pl.BlockSpec((1, t, PAGE_SIZE), lambda bi, pg, pt: (bi, 0, n_pages)),
                      per_seq((KV_W, PAGE_SIZE)), per_seq((KV_W, PAGE_SIZE))]
            + [pl.BlockSpec((1, KV_W, PAGE_SIZE), page_map(pp, j)) for j in range(pp)] * 2,
            out_specs=per_seq((n_rows, LANE)),
            scratch_shapes=[pltpu.VMEM((n_rows, LANE), jnp.float32), pltpu.VMEM((n_rows, LANE), jnp.float32),
                            pltpu.VMEM((n_rows, KV_W), jnp.float32)]),
        out_shape=jax.ShapeDtypeStruct((b, n_rows, LANE), jnp.float32),
        compiler_params=params,
        name="sample_attend",
    )(page_table, q_bd, bias, bias, k_new, v_new, *([ck] * pp), *([cv] * pp))
    out = out[:, :, :HEAD_DIM].reshape(b, N_HEADS, t, HEAD_DIM).transpose(0, 2, 1, 3)
    return out.reshape(b, t, N_HEADS * HEAD_DIM)


GELU_C = math.sqrt(2.0 / math.pi)


def _rglru_kernel(x_ref, xg_ref, prev_ref, h0_ref, cw_ref, cb_ref, wa_ref, ba_ref, wi_ref, bi_ref, lam_ref,
                  o_ref, hl_ref, conv_ref, a_ref, u_ref, xprev_ref, h_ref, *, rows, per_group_state):
    i = pl.program_id(0)
    n_groups = rows // SUBLANE
    d_rnn = x_ref.shape[1]
    r8 = lax.broadcasted_iota(jnp.int32, (SUBLANE, d_rnn), 0)
    grp = lambda g: pl.ds(pl.multiple_of(g * SUBLANE, SUBLANE), SUBLANE)

    if not per_group_state:
        @pl.when(i == 0)
        def _():
            xprev_ref[...] = jnp.zeros_like(xprev_ref)
            h_ref[...] = jnp.zeros_like(h_ref)

    cw = cw_ref[...]
    cb = cb_ref[...]

    def conv_group(g, prev):
        x8 = x_ref[grp(g), :]
        if per_group_state:
            prev = prev_ref[grp(g), :]
        out = cb + x8 * cw[CONV_W - 1:CONV_W, :]
        for d in range(1, CONV_W):
            shifted = jnp.where(r8 < d, pltpu.roll(prev, d, axis=0), pltpu.roll(x8, d, axis=0))
            out = out + shifted * cw[CONV_W - 1 - d:CONV_W - d, :]
        conv_ref[grp(g), :] = out
        return x8

    zeros8 = jnp.zeros((SUBLANE, d_rnn), jnp.float32)
    xlast = lax.fori_loop(0, n_groups, conv_group, zeros8 if per_group_state else xprev_ref[...])
    if not per_group_state:
        xprev_ref[...] = xlast

    xc = conv_ref[...]
    xb = xc.astype(jnp.bfloat16)
    r_parts, i_parts = [], []
    for n in range(RG_BLOCKS):
        xn = xb[:, n * RG_BLOCK_W:(n + 1) * RG_BLOCK_W]
        r_parts.append(jnp.dot(xn, wa_ref[n], preferred_element_type=jnp.float32))
        i_parts.append(jnp.dot(xn, wi_ref[n], preferred_element_type=jnp.float32))
    r = jax.nn.sigmoid(jnp.concatenate(r_parts, axis=-1) + ba_ref[...])
    ig = jax.nn.sigmoid(jnp.concatenate(i_parts, axis=-1) + bi_ref[...])
    lam = lam_ref[...]
    log_sig_lam = -(jnp.maximum(-lam, 0.0) + jnp.log1p(jnp.exp(-jnp.abs(lam))))
    log_a = RG_C * r * log_sig_lam
    a = jnp.exp(log_a)
    a_ref[...] = a
    u_ref[...] = jnp.sqrt(-jnp.tanh(log_a) * (a * a + 1.0)) * (ig * xc)

    def scan_group(g, hprev):
        a = a_ref[grp(g), :]
        u = u_ref[grp(g), :]
        if per_group_state:
            hprev = jnp.broadcast_to(h0_ref[pl.ds(g, 1), :], (SUBLANE, d_rnn))
        for d in (1, 2, 4):
            u = jnp.where(r8 >= d, a * pltpu.roll(u, d, axis=0) + u, u)
            a = jnp.where(r8 >= d, a * pltpu.roll(a, d, axis=0), a)
        h = a * hprev + u
        u_ref[grp(g), :] = h
        hlast = jnp.broadcast_to(h[SUBLANE - 1:SUBLANE, :], (SUBLANE, d_rnn))
        if per_group_state:
            hl_ref[pl.ds(g, 1), :] = h[SUBLANE - 1:SUBLANE, :]
        return hlast

    hlast = lax.fori_loop(0, n_groups, scan_group, zeros8 if per_group_state else h_ref[...])
    if not per_group_state:
        h_ref[...] = hlast
        hl_ref[...] = hlast

    xg = xg_ref[...]
    gelu = 0.5 * xg * (1.0 + jnp.tanh(GELU_C * (xg + 0.044715 * (xg * xg * xg))))
    o_ref[...] = u_ref[...] * gelu


def _rglru(z, row0, n_rows, prev, h0, conv_w, conv_b, w_a, b_a, w_i, b_i, lam, *, per_group_state, tile_rows):
    d = D_RNN
    assert row0 % tile_rows == 0 and n_rows % tile_rows == 0
    row = lambda a: a.reshape(1, d)
    const = lambda shape: pl.BlockSpec(shape, lambda i: (0,) * len(shape))
    n_hl = n_rows // SUBLANE if per_group_state else SUBLANE
    body = functools.partial(_rglru_kernel, rows=tile_rows, per_group_state=per_group_state)
    tile = pl.BlockSpec((tile_rows, d), lambda i: (i, 0))
    z_col = lambda name: pl.BlockSpec((tile_rows, d), lambda i, c=Z_COLS[name] // d: (row0 // tile_rows + i, c))
    if per_group_state:
        assert tile_rows == n_rows
    return pl.pallas_call(
        body,
        grid=(n_rows // tile_rows,),
        in_specs=[z_col("xr"), z_col("xg"), const(prev.shape), const(h0.shape), const((CONV_W, d)), const((1, d)),
                  const(w_a.shape), const((1, d)), const(w_i.shape), const((1, d)), const((1, d))],
        out_specs=[tile, const((n_hl, d))],
        out_shape=[jax.ShapeDtypeStruct((n_rows, d), jnp.float32), jax.ShapeDtypeStruct((n_hl, d), jnp.float32)],
        scratch_shapes=[pltpu.VMEM((tile_rows, d), jnp.float32), pltpu.VMEM((tile_rows, d), jnp.float32),
                        pltpu.VMEM((tile_rows, d), jnp.float32), pltpu.VMEM((SUBLANE, d), jnp.float32),
                        pltpu.VMEM((SUBLANE, d), jnp.float32)],
        compiler_params=pltpu.CompilerParams(dimension_semantics=("arbitrary",), vmem_limit_bytes=VMEM_LIMIT_BYTES),
        name="rglru_sample" if per_group_state else "rglru_prompt",
    )(z, z, prev, h0, conv_w, row(conv_b), w_a.astype(jnp.bfloat16), row(b_a), w_i.astype(jnp.bfloat16), row(b_i), row(lam))


def _rglru_prompt(z, row0, n_rows, conv_w, conv_b, w_a, b_a, w_i, b_i, lam, tile_rows=512):
    dummy = jnp.zeros((SUBLANE, D_RNN), jnp.float32)
    rnn, hl = _rglru(z, row0, n_rows, dummy, dummy, conv_w, conv_b, w_a, b_a, w_i, b_i, lam,
                     per_group_state=False, tile_rows=tile_rows)
    return rnn, hl[0]


def _rglru_sample(z, row0, state_conv, state_h, conv_w, conv_b, w_a, b_a, w_i, b_i, lam):
    b, d = state_h.shape
    prev = jnp.concatenate([jnp.zeros((b, SUBLANE - (CONV_W - 1), d), jnp.float32), state_conv], axis=1)
    return _rglru(z, row0, b * SUBLANE, prev.reshape(b * SUBLANE, d), state_h,
                  conv_w, conv_b, w_a, b_a, w_i, b_i, lam, per_group_state=True, tile_rows=b * SUBLANE)


def _rope_tables(pos):
    half = HEAD_DIM // 2
    inv_freq = ROPE_THETA ** (-jnp.arange(half, dtype=jnp.float32) / half)
    ang = pos.astype(jnp.float32)[:, None] * inv_freq[None, :]
    cos, sin = jnp.cos(ang), jnp.sin(ang)
    cos2 = jnp.concatenate([cos, cos], axis=-1)
    sin2 = jnp.concatenate([-sin, sin], axis=-1)
    return jnp.tile(cos2, (1, LANE // HEAD_DIM)), jnp.tile(sin2, (1, LANE // HEAD_DIM))


def _rotate(x, cos2, sin2):
    width = x.shape[1]
    half = HEAD_DIM // 2
    lane = lax.broadcasted_iota(jnp.int32, x.shape, 1)
    partner = jnp.where((lane & half) == 0, pltpu.roll(x, width - half, axis=1), pltpu.roll(x, half, axis=1))
    reps = width // LANE
    return x * jnp.tile(cos2, (1, reps)) + partner * jnp.tile(sin2, (1, reps))


def _head_mean_square(x, seg_ref):
    sq = x * x
    hi = sq.astype(jnp.bfloat16)
    lo = (sq - hi.astype(jnp.float32)).astype(jnp.bfloat16)
    seg = seg_ref[...]
    return (jnp.dot(hi, seg, preferred_element_type=jnp.float32)
            + jnp.dot(lo, seg, preferred_element_type=jnp.float32))


def _qk_post_kernel(q_ref, k_ref, iq_ref, ikw_ref, cos_ref, sin_ref, qg_ref, kg_ref, segq_ref, segk_ref,
                    qo_ref, ko_ref, iqo_ref, iko_ref, iwo_ref):
    cos2, sin2 = cos_ref[...], sin_ref[...]
    q = q_ref[...]
    qn = q * lax.rsqrt(_head_mean_square(q, segq_ref) + EPS) * qg_ref[...]
    qo_ref[...] = _rotate(qn, cos2, sin2)
    k = k_ref[...]
    kn = k * lax.rsqrt(_head_mean_square(k, segk_ref) + EPS) * kg_ref[...]
    ko_ref[...] = _rotate(kn, cos2, sin2)
    iqo_ref[...] = _rotate(iq_ref[...], cos2, sin2)
    ikw = ikw_ref[...]
    iko_ref[...] = _rotate(ikw, cos2, sin2)
    iwo_ref[...] = ikw * (IDX_HEADS ** -0.5 * IDX_DIM ** -0.5)


def _qk_post(z, pos, q_norm_g, k_norm_g, tm=512):
    n = z.shape[0]
    cos2, sin2 = _rope_tables(pos)
    seg = lambda w: (jnp.kron(jnp.eye(w // HEAD_DIM), jnp.ones((HEAD_DIM, HEAD_DIM))) / HEAD_DIM).astype(jnp.bfloat16)
    col = lambda name, w: pl.BlockSpec((tm, w), lambda i, c=Z_COLS[name] // w: (i, c))
    const = lambda shape: pl.BlockSpec(shape, lambda i: (0,) * len(shape))
    rows = lambda w: pl.BlockSpec((tm, w), lambda i: (i, 0))
    f32 = jnp.float32
    return pl.pallas_call(
        _qk_post_kernel,
        grid=(n // tm,),
        in_specs=[col("q", Q_W), col("k", KV_W), col("iq", IQ_W), col("ikw", LANE), rows(LANE), rows(LANE),
                  const((1, Q_W)), const((1, KV_W)), const((Q_W, Q_W)), const((KV_W, KV_W))],
        out_specs=[rows(Q_W), rows(KV_W), rows(IQ_W), rows(LANE), rows(LANE)],
        out_shape=[jax.ShapeDtypeStruct((n, Q_W), f32), jax.ShapeDtypeStruct((n, KV_W), f32),
                   jax.ShapeDtypeStruct((n, IQ_W), f32), jax.ShapeDtypeStruct((n, LANE), f32),
                   jax.ShapeDtypeStruct((n, LANE), f32)],
        compiler_params=pltpu.CompilerParams(dimension_semantics=("arbitrary",), vmem_limit_bytes=VMEM_LIMIT_BYTES),
        name="qk_post",
    )(z, z, z, z, cos2, sin2, jnp.tile(q_norm_g, N_HEADS).reshape(1, Q_W), jnp.tile(k_norm_g, KV_HEADS).reshape(1, KV_W),
      seg(Q_W), seg(KV_W))


def _out_proj_kernel(x_ref, attn_ref, rnn_ref, ga_ref, gr_ref, wa_ref, wr_ref, wo_ref, o_ref):
    bf = jnp.bfloat16
    a = jnp.dot(attn_ref[...].astype(bf), wa_ref[...], preferred_element_type=jnp.float32)
    r = jnp.dot(rnn_ref[...].astype(bf), wr_ref[...], preferred_element_type=jnp.float32)
    merged = jax.nn.sigmoid(ga_ref[...]) * a + jax.nn.sigmoid(gr_ref[...]) * r
    o_ref[...] = x_ref[...] + jnp.dot(merged.astype(bf), wo_ref[...], preferred_element_type=jnp.float32)


def _out_proj(x, attn, rnn, z, w_attn_out, w_rnn_out, w_out, tm=512):
    n, d = x.shape
    bf = jnp.bfloat16
    rows = pl.BlockSpec((tm, d), lambda i: (i, 0))
    col = lambda name: pl.BlockSpec((tm, d), lambda i, c=Z_COLS[name] // d: (i, c))
    const = pl.BlockSpec((d, d), lambda i: (0, 0))
    return pl.pallas_call(
        _out_proj_kernel,
        grid=(n // tm,),
        in_specs=[rows, rows, rows, col("ga"), col("gr"), const, const, const],
        out_specs=rows,
        out_shape=jax.ShapeDtypeStruct((n, d), jnp.float32),
        compiler_params=pltpu.CompilerParams(dimension_semantics=("arbitrary",), vmem_limit_bytes=VMEM_LIMIT_BYTES),
        name="out_proj",
    )(x, attn, rnn, z, z, w_attn_out.astype(bf), w_rnn_out.astype(bf), w_out.astype(bf))


def _rms_norm_rows(x, g):
    return x * lax.rsqrt(jnp.mean(x * x, axis=-1, keepdims=True) + EPS) * g


def _router_kernel(x_ref, g_ref, wr_ref, br_ref, tri_ref, e_ref, gate_ref, rank_ref, cnt_ref, run_ref):
    i = pl.program_id(0)
    tm = x_ref.shape[0]

    @pl.when(i == 0)
    def _():
        run_ref[...] = jnp.zeros_like(run_ref)

    xn_bf = _rms_norm_rows(x_ref[...], g_ref[...]).astype(jnp.bfloat16)
    logits = lax.dot_general(wr_ref[...], xn_bf, (((1,), (1,)), ((), ())), preferred_element_type=jnp.float32)
    logits = logits + br_ref[...]

    expert = lax.broadcasted_iota(jnp.int32, (N_EXPERTS, tm), 0)
    member = jnp.zeros((N_EXPERTS, tm), jnp.float32)
    picked, values = [], []
    for k in range(TOP_K):
        mx = jnp.max(logits, axis=0, keepdims=True)
        idx = jnp.min(jnp.where(logits == mx, expert, N_EXPERTS), axis=0, keepdims=True)
        hit = expert == idx
        member = jnp.where(hit, 1.0, member)
        logits = jnp.where(hit, -jnp.inf, logits)
        picked.append(idx)
        values.append(mx)
        e_ref[k:k + 1, :] = idx

    ex = [jnp.exp(v - values[0]) for v in values]
    denom = ex[0] + ex[1] + ex[2] + ex[3]
    for k in range(TOP_K):
        gate_ref[k:k + 1, :] = ex[k] / denom

    before = jnp.dot(member.astype(jnp.bfloat16), tri_ref[...], preferred_element_type=jnp.float32)
    before = before + jnp.tile(run_ref[...], (1, tm // LANE))
    for k in range(TOP_K):
        r = jnp.sum(jnp.where(expert == picked[k], before, 0.0), axis=0, keepdims=True)
        rank_ref[k:k + 1, :] = r.astype(jnp.int32)
    run = run_ref[...] + jnp.sum(member, axis=1, keepdims=True)
    run_ref[...] = run
    cnt_ref[...] = run.astype(jnp.int32)


def _route(x2d, norm_g, w_router, b_router, tm=512):
    n, d = x2d.shape
    tri = (jnp.arange(tm)[:, None] < jnp.arange(tm)[None, :]).astype(jnp.bfloat16)
    const = lambda shape: pl.BlockSpec(shape, lambda i: (0,) * len(shape))
    rows4 = pl.BlockSpec((TOP_K, tm), lambda i: (0, i))
    top_e, gates, rank, cnt = pl.pallas_call(
        _router_kernel,
        grid=(n // tm,),
        in_specs=[pl.BlockSpec((tm, d), lambda i: (i, 0)), const((1, d)), const((N_EXPERTS, d)),
                  const((N_EXPERTS, 1)), const((tm, tm))],
        out_specs=[rows4, rows4, rows4, const((N_EXPERTS, LANE))],
        out_shape=[jax.ShapeDtypeStruct((TOP_K, n), jnp.int32),
                   jax.ShapeDtypeStruct((TOP_K, n), jnp.float32), jax.ShapeDtypeStruct((TOP_K, n), jnp.int32),
                   jax.ShapeDtypeStruct((N_EXPERTS, LANE), jnp.int32)],
        scratch_shapes=[pltpu.VMEM((N_EXPERTS, LANE), jnp.float32)],
        compiler_params=pltpu.CompilerParams(dimension_semantics=("arbitrary",)),
        name="moe_router",
    )(x2d, norm_g.reshape(1, d), w_router.T.astype(jnp.bfloat16), b_router.reshape(N_EXPERTS, 1), tri)
    return top_e, gates, rank, cnt[:, 0]


def _expert_kernel(be_ref, used_ref, xs_ref, g_ref, wgu_ref, bgu_ref, wd_ref, bd_ref, o_ref, *, blk):
    i = pl.program_id(0)

    @pl.when(i * blk < used_ref[0])
    def _():
        xn = _rms_norm_rows(xs_ref[...], g_ref[...]).astype(jnp.bfloat16)
        gu = jnp.dot(xn, wgu_ref[0].astype(jnp.bfloat16), preferred_element_type=jnp.float32) + bgu_ref[0]
        g = jnp.minimum(gu[:, :D_FF], SWIGLU_LIMIT)
        u = jnp.clip(gu[:, D_FF:], -SWIGLU_LIMIT, SWIGLU_LIMIT)
        act = (u + 1.0) * (g * jax.nn.sigmoid(SWIGLU_ALPHA * g))
        o_ref[...] = (jnp.dot(act.astype(jnp.bfloat16), wd_ref[0].astype(jnp.bfloat16),
                              preferred_element_type=jnp.float32) + bd_ref[0])

    @pl.when(i * blk >= used_ref[0])
    def _():
        o_ref[...] = jnp.zeros_like(o_ref)


def _experts(xs, norm_g, blk_exp, n_used, w_gate_up, b_gate_up, w_down, b_down, *, blk):
    n_slots, d = xs.shape
    n_blocks = n_slots // blk
    return pl.pallas_call(
        functools.partial(_expert_kernel, blk=blk),
        grid_spec=pltpu.PrefetchScalarGridSpec(
            num_scalar_prefetch=2, grid=(n_blocks,),
            in_specs=[pl.BlockSpec((blk, d), lambda i, be, nu: (i, 0)),
                      pl.BlockSpec((1, d), lambda i, be, nu: (0, 0)),
                      pl.BlockSpec((1, d, 2 * D_FF), lambda i, be, nu: (be[i], 0, 0)),
                      pl.BlockSpec((1, 1, 2 * D_FF), lambda i, be, nu: (be[i], 0, 0)),
                      pl.BlockSpec((1, D_FF, d), lambda i, be, nu: (be[i], 0, 0)),
                      pl.BlockSpec((1, 1, d), lambda i, be, nu: (be[i], 0, 0))],
            out_specs=pl.BlockSpec((blk, d), lambda i, be, nu: (i, 0))),
        out_shape=jax.ShapeDtypeStruct((n_slots, d), jnp.float32),
        compiler_params=pltpu.CompilerParams(dimension_semantics=("arbitrary",), vmem_limit_bytes=VMEM_LIMIT_BYTES),
        name="moe_experts",
    )(blk_exp, n_used, xs, norm_g.reshape(1, d), w_gate_up, b_gate_up.reshape(N_EXPERTS, 1, 2 * D_FF),
      w_down, b_down.reshape(N_EXPERTS, 1, d))


def _row_copy(src_hbm, src_row, dst_ref, dst_row, sem):
    return pltpu.make_async_copy(src_hbm.at[pl.ds(src_row, 1)], dst_ref.at[pl.ds(dst_row, 1)], sem)


def _dispatch_kernel(dest_ref, x_ref, xs_in_hbm, xs_hbm, sem, *, tm):
    del xs_in_hbm

    def issue(t, carry):
        for k in range(TOP_K):
            _row_copy(x_ref, t, xs_hbm, dest_ref[k, t], sem).start()
        return carry

    lax.fori_loop(0, tm, issue, 0)

    def drain(t, carry):
        for k in range(TOP_K):
            _row_copy(x_ref, 0, xs_hbm, 0, sem).wait()
        return carry

    lax.fori_loop(0, tm, drain, 0)


def _dispatch(x2d, dest, n_slots, tm=128):
    n, d = x2d.shape
    return pl.pallas_call(
        functools.partial(_dispatch_kernel, tm=tm),
        grid=(n // tm,),
        in_specs=[pl.BlockSpec((TOP_K, tm), lambda i: (0, i), memory_space=pltpu.SMEM),
                  pl.BlockSpec((tm, d), lambda i: (i, 0)), pl.BlockSpec(memory_space=pl.ANY)],
        out_specs=pl.BlockSpec(memory_space=pl.ANY),
        out_shape=jax.ShapeDtypeStruct((n_slots, d), x2d.dtype),
        scratch_shapes=[pltpu.SemaphoreType.DMA(())],
        input_output_aliases={2: 0},
        compiler_params=pltpu.CompilerParams(dimension_semantics=("arbitrary",)),
        name="moe_dispatch",
    )(dest, x2d, jnp.zeros((n_slots, d), x2d.dtype))


def _combine_kernel(dest_ref, x_ref, gate_ref, ys_hbm, o_ref, buf_ref, sem, *, tm):
    def issue(t, carry):
        for k in range(TOP_K):
            _row_copy(ys_hbm, dest_ref[k, t], buf_ref.at[k], t, sem).start()
        return carry

    lax.fori_loop(0, tm, issue, 0)

    def drain(t, carry):
        for k in range(TOP_K):
            _row_copy(ys_hbm, 0, buf_ref.at[k], 0, sem).wait()
        return carry

    lax.fori_loop(0, tm, drain, 0)
    gate = gate_ref[...]
    acc = x_ref[...]
    for k in range(TOP_K):
        acc = acc + gate[:, k:k + 1] * buf_ref[k]
    o_ref[...] = acc


def _combine(x2d, ys, dest, gates_t, tm=128):
    n, d = x2d.shape
    return pl.pallas_call(
        functools.partial(_combine_kernel, tm=tm),
        grid=(n // tm,),
        in_specs=[pl.BlockSpec((TOP_K, tm), lambda i: (0, i), memory_space=pltpu.SMEM),
                  pl.BlockSpec((tm, d), lambda i: (i, 0)), pl.BlockSpec((tm, TOP_K), lambda i: (i, 0)),
                  pl.BlockSpec(memory_space=pl.ANY)],
        out_specs=pl.BlockSpec((tm, d), lambda i: (i, 0)),
        out_shape=jax.ShapeDtypeStruct((n, d), jnp.float32),
        scratch_shapes=[pltpu.VMEM((TOP_K, tm, d), jnp.float32), pltpu.SemaphoreType.DMA(())],
        compiler_params=pltpu.CompilerParams(dimension_semantics=("arbitrary",)),
        name="moe_combine",
    )(dest, x2d, gates_t, ys)


MOE_ROWS = 512


def _moe(x2d, norm_g, w_router, b_router, w_gate_up, b_gate_up, w_down, b_down, *, blk=MOE_ROWS, tm=512):
    n, d = x2d.shape
    top_e, gates, rank, counts = _route(x2d, norm_g, w_router, b_router, tm=tm)
    padded = (counts + blk - 1) // blk * blk
    pad_end = jnp.cumsum(padded)
    pad_start = pad_end - padded
    experts = jnp.arange(N_EXPERTS, dtype=jnp.int32)
    start_of = jnp.sum(jnp.where(top_e[..., None] == experts, pad_start.astype(jnp.int32), 0), axis=-1)
    dest = start_of + rank
    n_blocks = -(-(n * TOP_K + N_EXPERTS * (blk - 1)) // blk)
    n_slots = n_blocks * blk
    block_start = jnp.arange(n_blocks, dtype=jnp.int32) * blk
    blk_exp = jnp.minimum(jnp.sum(pad_end[None, :] <= block_start[:, None], axis=1), N_EXPERTS - 1).astype(jnp.int32)
    n_used = pad_end[-1:].astype(jnp.int32)
    xs = _dispatch(x2d, dest, n_slots)
    ys = _experts(xs, norm_g, blk_exp, n_used, w_gate_up, b_gate_up, w_down, b_down, blk=blk)
    return _combine(x2d, ys, dest, gates.T)


def kernel(x_prompt, x_sample, cache_k, cache_v, cache_idx_k, state_conv, state_h, page_table,
           norm1_g, w_in, q_norm_g, k_norm_g, conv_w, conv_b, rg_w_a, rg_b_a, rg_w_i, rg_b_i, rg_lambda,
           w_attn_out, w_rnn_out, w_out, norm2_g, w_router, b_router, w_gate_up, b_gate_up, w_down, b_down):
    assert x_prompt.shape[0] == 1 and norm1_g.shape[0] == 1 and x_sample.shape[1] == SUBLANE
    s_len = x_prompt.shape[1]
    db, t_new = x_sample.shape[:2]
    n_s = db * t_new
    past = page_table.shape[1] * PAGE_SIZE
    pos = jnp.concatenate([jnp.arange(s_len, dtype=jnp.int32),
                           jnp.tile(past + jnp.arange(t_new, dtype=jnp.int32), db)])
    x_all = jnp.concatenate([x_prompt.reshape(s_len, D_MODEL), x_sample.reshape(n_s, D_MODEL)], axis=0)

    z = _norm_proj(x_all, norm1_g[0], _permuted_w_in(w_in[0]))
    q, k, iq, ik, iw = _qk_post(z, pos, q_norm_g[0], k_norm_g[0])
    v = z[:, Z_COLS["v"]:Z_COLS["v"] + KV_W]
    xr = z[:, Z_COLS["xr"]:Z_COLS["xr"] + D_RNN]
    ik = ik[:, :IDX_DIM]
    iw = iw[:, IDX_DIM:IDX_DIM + IDX_HEADS]
    heads = lambda a, lo, hi, n_heads: a[lo:hi].reshape(hi - lo, n_heads, -1)
    seqs = lambda a: a.reshape((db, t_new) + a.shape[1:])
    p_rows, s_rows = (0, s_len), (s_len, s_len + n_s)

    attn_p = _prompt_dsa_attention_t(heads(q, *p_rows, N_HEADS), heads(k, *p_rows, KV_HEADS), heads(v, *p_rows, KV_HEADS),
                                     heads(iq, *p_rows, IDX_HEADS), iw[:s_len], ik[:s_len])
    attn_s = _sample_dsa_attention(seqs(heads(q, *s_rows, N_HEADS)), seqs(heads(k, *s_rows, KV_HEADS)),
                                   seqs(heads(v, *s_rows, KV_HEADS)), seqs(heads(iq, *s_rows, IDX_HEADS)),
                                   seqs(iw[s_len:]), seqs(ik[s_len:]), cache_k[0], cache_v[0], cache_idx_k[0], page_table)
    rg = (conv_w[0], conv_b[0], rg_w_a[0], rg_b_a[0], rg_w_i[0], rg_b_i[0], rg_lambda[0])
    rnn_p, h_p = _rglru_prompt(z, 0, s_len, *rg)
    rnn_s, h_s = _rglru_sample(z, s_len, state_conv[0], state_h[0], *rg)

    attn = jnp.concatenate([attn_p, attn_s.reshape(n_s, Q_W)], axis=0)
    rnn = jnp.concatenate([rnn_p, rnn_s], axis=0)
    x_mid = _out_proj(x_all, attn, rnn, z, w_attn_out[0], w_rnn_out[0], w_out[0])
    y = _moe(x_mid, norm2_g[0], w_router[0], b_router[0], w_gate_up[0], b_gate_up[0], w_down[0], b_down[0])

    tail = CONV_W - 1
    xr_s = xr[s_len:].reshape(db, t_new, D_RNN)
    conv_s = jnp.concatenate([state_conv[0], xr_s], axis=1)[:, -tail:]
    kv_p = lambda a: a[:s_len].reshape(1, 1, s_len, KV_HEADS, HEAD_DIM)
    kv_s = lambda a: a[s_len:].reshape(1, db, t_new, KV_HEADS, HEAD_DIM)
    return (y[:s_len].reshape(x_prompt.shape), y[s_len:].reshape(x_sample.shape),
            kv_p(k), kv_p(v), ik[:s_len].reshape(1, 1, s_len, IDX_DIM),
            xr[s_len - tail:s_len].reshape(1, 1, tail, D_RNN), h_p.reshape(1, 1, D_RNN),
            kv_s(k), kv_s(v), ik[s_len:].reshape(1, db, t_new, IDX_DIM),
            conv_s[None], h_s[None])
```

```python
import functools
import math

import jax
import jax.numpy as jnp
import numpy as np
from jax import lax
from jax.experimental import pallas as pl
from jax.experimental.pallas import tpu as pltpu

D_MODEL = 1024
PAGE_SIZE = 128
N_HEADS = 16
HEAD_DIM = 64
KV_HEADS = 4
IDX_HEADS = 8
IDX_DIM = 64
IDX_TOPK_MAX = 256
Q_BLOCK = 128
ROPE_THETA = 10000.0
D_RNN = D_MODEL
RG_BLOCKS = 4
RG_BLOCK_W = D_RNN // RG_BLOCKS
CONV_W = 4
RG_C = 8.0
N_EXPERTS = 32
TOP_K = 4
D_FF = D_MODEL
SWIGLU_LIMIT = 7.0
SWIGLU_ALPHA = 1.702
MOE_BLOCK = 128
EPS = 1e-6

Q_W = N_HEADS * HEAD_DIM
KV_W = KV_HEADS * HEAD_DIM
IQ_W = IDX_HEADS * IDX_DIM
IN_NAMES = ("q", "k", "v", "iq", "ik", "iw", "xr", "xg", "ga", "gr")
IN_WIDTHS = (Q_W, KV_W, KV_W, IQ_W, IDX_DIM, IDX_HEADS, D_RNN, D_RNN, D_MODEL, D_MODEL)

LANE = 128
SUBLANE = 8
VMEM_LIMIT_BYTES = 48 * 1024 * 1024

Z_ORDER = ("q", "k", "v", "iq", "xr", "xg", "ga", "gr")
Z_COLS = {}
_col = 0
for _name in Z_ORDER:
    Z_COLS[_name] = _col
    _col += IN_WIDTHS[IN_NAMES.index(_name)]
Z_COLS["ikw"] = _col
Z_W = _col + LANE
PROJ_TN = 896


def _permuted_w_in(w_in):
    offs = np.concatenate([[0], np.cumsum(IN_WIDTHS)])
    seg = {name: w_in[:, offs[j]:offs[j + 1]] for j, name in enumerate(IN_NAMES)}
    pad = jnp.zeros((w_in.shape[0], LANE - IDX_DIM - IDX_HEADS), w_in.dtype)
    return jnp.concatenate([seg[name] for name in Z_ORDER] + [seg["ik"], seg["iw"], pad], axis=1).astype(jnp.bfloat16)


def _norm_proj_kernel(x_ref, g_ref, w_ref, o_ref):
    x = x_ref[...]
    y = x * lax.rsqrt(jnp.mean(x * x, axis=-1, keepdims=True) + EPS) * g_ref[...]
    o_ref[...] = jnp.dot(y.astype(jnp.bfloat16), w_ref[...], preferred_element_type=jnp.float32)


def _norm_proj(x2d, g, w_bf16, tm=512, tn=PROJ_TN):
    m, d = x2d.shape
    n = w_bf16.shape[1]
    return pl.pallas_call(
        _norm_proj_kernel,
        grid=(m // tm, n // tn),
        in_specs=[
            pl.BlockSpec((tm, d), lambda i, j: (i, 0)),
            pl.BlockSpec((1, d), lambda i, j: (0, 0)),
            pl.BlockSpec((d, tn), lambda i, j: (0, j)),
        ],
        out_specs=pl.BlockSpec((tm, tn), lambda i, j: (i, j)),
        out_shape=jax.ShapeDtypeStruct((m, n), jnp.float32),
        name="norm_proj",
    )(x2d, g.reshape(1, d), w_bf16)


INT_MIN = -(2 ** 31)
INT_MAX = 2 ** 31 - 1
MASK_BIAS = -1e30
NEG_INF_BITS_MASK = 0x7FFFFFFF


def _ordered_key(x):
    bits = pltpu.bitcast(x, jnp.int32)
    return bits ^ ((bits >> 31) & NEG_INF_BITS_MASK)


def _topk_threshold(key_ref, cut_ref, nk, *, rows, kb, topk, n_idx_bits):
    n_lane_tiles = kb // LANE
    lane = lax.broadcasted_iota(jnp.int32, (rows, kb), 1)

    def count(pred_of_chunk):
        def body(c, cnt):
            p = pred_of_chunk(c, key_ref[c])
            for j in range(n_lane_tiles):
                cnt = cnt + jnp.where(p[:, j * LANE:(j + 1) * LANE], 1, 0)
            return cnt
        cnt = lax.fori_loop(0, nk, body, jnp.zeros((rows, LANE), jnp.int32))
        return jnp.sum(cnt, axis=-1, keepdims=True)

    def bit_cond(carry):
        b, _, n_ge = carry
        return jnp.logical_and(b < 32, jnp.max(jnp.abs(n_ge - topk)) > 0)

    def bit_step(carry):
        b, thr, n_ge = carry
        cand = thr ^ (jnp.int32(1) << (31 - b))
        cand_b = jnp.broadcast_to(cand, (rows, kb))
        cnt = count(lambda c, key: key >= cand_b)
        ok = cnt >= topk
        return b + 1, jnp.where(ok, cand, thr), jnp.where(ok, cnt, n_ge)

    thr0 = jnp.full((rows, 1), INT_MIN, jnp.int32)
    n0 = jnp.full((rows, 1), 0, jnp.int32) + nk * kb
    _, thr, n_ge = lax.while_loop(bit_cond, lambda carry: bit_step(bit_step(carry)), (jnp.int32(0), thr0, n0))
    thr_b = jnp.broadcast_to(thr, (rows, kb))

    cut_ref[...] = jnp.full((rows, LANE), INT_MAX, jnp.int32)

    @pl.when(jnp.max(n_ge) > topk)
    def _():
        n_eq = count(lambda c, key: key == thr_b)
        need = topk - (n_ge - n_eq)

        def idx_step(b, lo):
            step = jnp.int32(1) << (n_idx_bits - 1 - b)
            mid_b = jnp.broadcast_to(lo + step - 1, (rows, kb))
            f = count(lambda c, key: (key == thr_b) & (c * kb + lane <= mid_b))
            return jnp.where(f < need, lo + step, lo)

        lo = lax.fori_loop(0, n_idx_bits, idx_step, jnp.zeros((rows, 1), jnp.int32))
        cut = jnp.where(n_ge > topk, lo, INT_MAX)
        cut_ref[...] = jnp.broadcast_to(cut, (rows, LANE))

    return thr


def _topk_threshold_t(key_ref, cut_ref, nk, *, kb, cols, topk, n_idx_bits):
    acc_rows = 8 * SUBLANE
    assert kb % acc_rows == 0
    pos0 = lax.broadcasted_iota(jnp.int32, (kb, cols), 0)

    def count(pred_of_chunk):
        def body(c, cnt):
            p = pred_of_chunk(c, key_ref[c])
            for j in range(kb // acc_rows):
                cnt = cnt + jnp.where(p[j * acc_rows:(j + 1) * acc_rows, :], 1, 0)
            return cnt
        cnt = lax.fori_loop(0, nk, body, jnp.zeros((acc_rows, cols), jnp.int32))
        return jnp.sum(cnt, axis=0, keepdims=True)

    def bit_cond(carry):
        b, _, n_ge = carry
        return jnp.logical_and(b < 32, jnp.max(jnp.abs(n_ge - topk)) > 0)

    def bit_step(carry):
        b, thr, n_ge = carry
        cand = thr ^ (jnp.int32(1) << (31 - b))
        cand_b = jnp.broadcast_to(cand, (kb, cols))
        cnt = count(lambda c, key: key >= cand_b)
        ok = cnt >= topk
        return b + 1, jnp.where(ok, cand, thr), jnp.where(ok, cnt, n_ge)

    thr0 = jnp.full((1, cols), INT_MIN, jnp.int32)
    n0 = jnp.full((1, cols), 0, jnp.int32) + nk * kb
    _, thr, n_ge = lax.while_loop(bit_cond, lambda carry: bit_step(bit_step(carry)), (jnp.int32(0), thr0, n0))
    thr_b = jnp.broadcast_to(thr, (kb, cols))

    cut_ref[...] = jnp.full((SUBLANE, cols), INT_MAX, jnp.int32)

    @pl.when(jnp.max(n_ge) > topk)
    def _():
        n_eq = count(lambda c, key: key == thr_b)
        need = topk - (n_ge - n_eq)

        def idx_step(b, lo):
            step = jnp.int32(1) << (n_idx_bits - 1 - b)
            mid_b = jnp.broadcast_to(lo + step - 1, (kb, cols))
            f = count(lambda c, key: (key == thr_b) & (c * kb + pos0 <= mid_b))
            return jnp.where(f < need, lo + step, lo)

        lo = lax.fori_loop(0, n_idx_bits, idx_step, jnp.zeros((1, cols), jnp.int32))
        cut = jnp.where(n_ge > topk, lo, INT_MAX)
        cut_ref[...] = jnp.broadcast_to(cut, (SUBLANE, cols))

    return thr


UNDERFLOW_GUARD = 1e-30
LOG2_E = math.log2(math.e)


def _dsa_t_kernel(iq_ref, iw_ref, q_ref, mb_ref, ik_ref, k_ref, vt_ref, o_ref,
                  key_ref, cut_ref, m_ref, l_ref, acc_ref, *, qb, kb, topk, n_idx_bits):
    i = pl.program_id(0)
    nk = ((i + 1) * qb + kb - 1) // kb
    rep = N_HEADS // KV_HEADS
    kpos0 = lax.broadcasted_iota(jnp.int32, (kb, qb), 0)
    qpos = i * qb + lax.broadcasted_iota(jnp.int32, (kb, qb), 1)
    iw = iw_ref[0]

    def score_chunk(c, carry):
        ikc = ik_ref[c]
        acc = jnp.zeros((kb, qb), jnp.float32)
        for hp in range(IDX_HEADS // 2):
            s2 = jnp.dot(ikc, iq_ref[0, hp], preferred_element_type=jnp.float32)
            for hh in range(2):
                h = 2 * hp + hh
                acc = acc + iw[h:h + 1, :] * jnp.maximum(s2[:, hh * qb:(hh + 1) * qb], 0.0)
        acc = jnp.where(c * kb + kpos0 <= qpos, acc, -jnp.inf)
        key_ref[c] = _ordered_key(acc)
        return carry

    lax.fori_loop(0, nk, score_chunk, 0)

    thr = _topk_threshold_t(key_ref, cut_ref, nk, kb=kb, cols=qb, topk=topk, n_idx_bits=n_idx_bits)
    thr_b = jnp.broadcast_to(thr, (kb, qb))
    cut_b = jnp.broadcast_to(cut_ref[0:1, :], (kb, qb))

    def mask_bias(c):
        key = key_ref[c]
        kpos = c * kb + kpos0
        sel = (key > thr_b) | ((key == thr_b) & (kpos <= cut_b))
        sel = sel & (kpos <= qpos)
        return jnp.tile(jnp.where(sel, 0.0, MASK_BIAS), (1, rep))

    l_ref[...] = jnp.zeros(l_ref.shape, jnp.float32)
    acc_ref[...] = jnp.zeros(acc_ref.shape, jnp.float32)

    def attend_chunk(c, carry):
        bias = mask_bias(c)
        for g in range(KV_HEADS):
            s = jnp.dot(k_ref[c, g], q_ref[0, g], preferred_element_type=jnp.float32) + (bias - mb_ref[0, g])
            p = jnp.exp2(s)
            l_ref[g] = l_ref[g] + jnp.sum(p, axis=0, keepdims=True)
            acc_ref[g] = acc_ref[g] + jnp.dot(vt_ref[c, g], p.astype(jnp.bfloat16), preferred_element_type=jnp.float32)
        return carry

    lax.fori_loop(0, nk, attend_chunk, 0)

    @pl.when(jnp.min(l_ref[...]) < UNDERFLOW_GUARD)
    def _():
        m_ref[...] = jnp.full(m_ref.shape, MASK_BIAS, jnp.float32)
        l_ref[...] = jnp.zeros(l_ref.shape, jnp.float32)
        acc_ref[...] = jnp.zeros(acc_ref.shape, jnp.float32)

        def attend_chunk_running_max(c, carry):
            bias = mask_bias(c)
            for g in range(KV_HEADS):
                s = jnp.dot(k_ref[c, g], q_ref[0, g], preferred_element_type=jnp.float32) + bias
                m_prev = m_ref[g]
                m_new = jnp.maximum(m_prev, jnp.max(s, axis=0, keepdims=True))
                alpha = jnp.exp2(m_prev - m_new)
                p = jnp.exp2(s - m_new)
                l_ref[g] = alpha * l_ref[g] + jnp.sum(p, axis=0, keepdims=True)
                pv = jnp.dot(vt_ref[c, g], p.astype(jnp.bfloat16), preferred_element_type=jnp.float32)
                acc_ref[g] = acc_ref[g] * alpha + pv
                m_ref[g] = m_new
            return carry

        lax.fori_loop(0, nk, attend_chunk_running_max, 0)

    for g in range(KV_HEADS):
        o_ref[0, g] = acc_ref[g] / l_ref[g]


def _prompt_dsa_attention_t(q, k, v, iq, iw, ik, *, qb=128, kb=1024):
    s_len = q.shape[0]
    topk = min(IDX_TOPK_MAX, s_len // 4)
    nqb, nkc = s_len // qb, s_len // kb
    rep = N_HEADS // KV_HEADS
    bf = jnp.bfloat16
    q_s = (q * (HEAD_DIM ** -0.5 * LOG2_E)).astype(bf)
    k_bf = k.astype(bf)
    q_norm = jnp.sqrt(jnp.sum(jnp.square(q_s.astype(jnp.float32)), axis=-1))
    k_norm = jnp.sqrt(jnp.max(jnp.sum(jnp.square(k_bf.astype(jnp.float32)), axis=-1), axis=0))
    bound = q_norm.reshape(nqb, qb, KV_HEADS, rep) * k_norm[None, None, :, None]
    bound = bound.transpose(0, 2, 3, 1).reshape(nqb, KV_HEADS, 1, rep * qb)
    q_t = q_s.reshape(nqb, qb, KV_HEADS, rep, HEAD_DIM).transpose(0, 2, 4, 3, 1)
    q_t = q_t.reshape(nqb, KV_HEADS, HEAD_DIM, rep * qb)
    iq_t = iq.astype(bf).reshape(nqb, qb, IDX_HEADS // 2, 2, IDX_DIM).transpose(0, 2, 4, 3, 1)
    iq_t = iq_t.reshape(nqb, IDX_HEADS // 2, IDX_DIM, 2 * qb)
    iw_t = iw.reshape(nqb, qb, IDX_HEADS).transpose(0, 2, 1)
    ik_c = ik.astype(bf).reshape(nkc, kb, IDX_DIM)
    k_c = k_bf.reshape(nkc, kb, KV_HEADS, HEAD_DIM).transpose(0, 2, 1, 3)
    v_t = v.astype(bf).reshape(nkc, kb, KV_HEADS, HEAD_DIM).transpose(0, 2, 3, 1)
    n_idx_bits = max(1, (s_len - 1).bit_length())
    body = functools.partial(_dsa_t_kernel, qb=qb, kb=kb, topk=topk, n_idx_bits=n_idx_bits)
    whole = lambda shape: pl.BlockSpec(shape, lambda i: (0,) * len(shape), pipeline_mode=pl.Buffered(1))
    out = pl.pallas_call(
        body,
        grid=(nqb,),
        in_specs=[
            pl.BlockSpec((1, IDX_HEADS // 2, IDX_DIM, 2 * qb), lambda i: (i, 0, 0, 0)),
            pl.BlockSpec((1, IDX_HEADS, qb), lambda i: (i, 0, 0)),
            pl.BlockSpec((1, KV_HEADS, HEAD_DIM, rep * qb), lambda i: (i, 0, 0, 0)),
            pl.BlockSpec((1, KV_HEADS, 1, rep * qb), lambda i: (i, 0, 0, 0)),
            whole((nkc, kb, IDX_DIM)),
            whole((nkc, KV_HEADS, kb, HEAD_DIM)),
            whole((nkc, KV_HEADS, HEAD_DIM, kb)),
        ],
        out_specs=pl.BlockSpec((1, KV_HEADS, HEAD_DIM, rep * qb), lambda i: (i, 0, 0, 0)),
        out_shape=jax.ShapeDtypeStruct((nqb, KV_HEADS, HEAD_DIM, rep * qb), jnp.float32),
        scratch_shapes=[
            pltpu.VMEM((nkc, kb, qb), jnp.int32),
            pltpu.VMEM((SUBLANE, qb), jnp.int32),
            pltpu.VMEM((KV_HEADS, 1, rep * qb), jnp.float32),
            pltpu.VMEM((KV_HEADS, 1, rep * qb), jnp.float32),
            pltpu.VMEM((KV_HEADS, HEAD_DIM, rep * qb), jnp.float32),
        ],
        compiler_params=pltpu.CompilerParams(dimension_semantics=("arbitrary",), vmem_limit_bytes=VMEM_LIMIT_BYTES),
        name="prompt_dsa_attention",
    )(iq_t, iw_t, q_t, bound, ik_c, k_c, v_t)
    out = out.reshape(nqb, KV_HEADS, HEAD_DIM, rep, qb).transpose(0, 4, 1, 3, 2)
    return out.reshape(s_len, N_HEADS * HEAD_DIM)


def _sample_index_kernel(pt_ref, iq_ref, iw_ref, iknew_ref, *rest, pp, n_pages, topk, n_idx_bits):
    page_refs, o_ref, (key_ref, wide_ref, wb_ref, cut_ref) = rest[:pp], rest[pp], rest[pp + 1:]
    pg = pl.program_id(1)
    t_new = o_ref.shape[1]
    step_keys = pp * PAGE_SIZE
    n_keys = (n_pages + 1) * PAGE_SIZE

    @pl.when(pg == 0)
    def _():
        wb_ref[...] = jnp.broadcast_to(iw_ref[0], wb_ref.shape)

    iq = iq_ref[0]

    def scores(ik_t):
        s = jnp.dot(iq, ik_t.astype(jnp.bfloat16), preferred_element_type=jnp.float32)
        s = jnp.tile(wb_ref[...], (1, ik_t.shape[1] // LANE)) * jnp.maximum(s, 0.0)
        acc = jnp.zeros((t_new, ik_t.shape[1]), jnp.float32)
        for h in range(IDX_HEADS):
            acc = acc + s[h * t_new:(h + 1) * t_new, :]
        return acc

    key_ref[pg] = _ordered_key(scores(jnp.concatenate([r[0] for r in page_refs], axis=1)))

    @pl.when(pg == pl.num_programs(1) - 1)
    def _():
        for c in range(n_pages // pp):
            wide_ref[0, :, c * step_keys:(c + 1) * step_keys] = key_ref[c]
        row = lax.broadcasted_iota(jnp.int32, (t_new, PAGE_SIZE), 0)
        lane = lax.broadcasted_iota(jnp.int32, (t_new, PAGE_SIZE), 1)
        s_new = jnp.where(lane <= row, scores(iknew_ref[0]), -jnp.inf)
        wide_ref[0, :, n_pages * PAGE_SIZE:] = _ordered_key(s_new)
        thr = _topk_threshold(wide_ref, cut_ref, 1, rows=t_new, kb=n_keys, topk=topk, n_idx_bits=n_idx_bits)
        key = wide_ref[0]
        kpos = lax.broadcasted_iota(jnp.int32, (t_new, n_keys), 1)
        qpos = n_pages * PAGE_SIZE + lax.broadcasted_iota(jnp.int32, (t_new, n_keys), 0)
        thr_b = jnp.broadcast_to(thr, (t_new, n_keys))
        cut_b = jnp.tile(cut_ref[...], (1, n_pages + 1))
        sel = (key > thr_b) | ((key == thr_b) & (kpos <= cut_b))
        o_ref[0] = jnp.where(sel & (kpos <= qpos), 0.0, MASK_BIAS)


def _sample_attend_kernel(pt_ref, q_ref, bias_ref, biasnew_ref, knew_ref, vnew_ref, *rest, pp, n_pages):
    k_refs, v_refs, o_ref, (m_ref, l_ref, acc_ref) = rest[:pp], rest[pp:2 * pp], rest[2 * pp], rest[2 * pp + 1:]
    pg = pl.program_id(1)
    n_rows = q_ref.shape[1]
    t_new = bias_ref.shape[1]

    @pl.when(pg == 0)
    def _():
        m_ref[...] = jnp.full(m_ref.shape, MASK_BIAS, jnp.float32)
        l_ref[...] = jnp.zeros(l_ref.shape, jnp.float32)
        acc_ref[...] = jnp.zeros(acc_ref.shape, jnp.float32)

    q = q_ref[0]

    def step(bias, k_t, v_t):
        n_keys = k_t.shape[1]
        s = jnp.dot(q, k_t.astype(jnp.bfloat16), preferred_element_type=jnp.float32)
        s = s + jnp.tile(bias, (n_rows // t_new, 1))
        m_prev = m_ref[...]
        m_new = jnp.maximum(m_prev, jnp.max(s, axis=-1, keepdims=True))
        alpha = jnp.exp(m_prev - m_new)
        p = jnp.exp(s - jnp.tile(m_new, (1, n_keys // LANE)))
        l_ref[...] = alpha * l_ref[...] + jnp.sum(p, axis=-1, keepdims=True)
        pv = lax.dot_general(p.astype(jnp.bfloat16), v_t.astype(jnp.bfloat16), (((1,), (1,)), ((), ())),
                             preferred_element_type=jnp.float32)
        acc_ref[...] = acc_ref[...] * jnp.tile(alpha, (1, KV_W // LANE)) + pv
        m_ref[...] = m_new

    step(bias_ref[0], jnp.concatenate([r[0] for r in k_refs], axis=1), jnp.concatenate([r[0] for r in v_refs], axis=1))

    @pl.when(pg == pl.num_programs(1) - 1)
    def _():
        step(biasnew_ref[0], knew_ref[0], vnew_ref[0])
        acc = acc_ref[...]
        row_head = lax.broadcasted_iota(jnp.int32, acc.shape, 0) // (n_rows // KV_HEADS)
        col_head = lax.broadcasted_iota(jnp.int32, acc.shape, 1) // HEAD_DIM
        own = jnp.where(row_head == col_head, acc, 0.0)
        folded = own[:, :LANE] + own[:, LANE:]
        folded = folded + pltpu.roll(folded, HEAD_DIM, axis=1)
        o_ref[0] = folded / l_ref[...]


def _sample_dsa_attention(q, k, v, iq, iw, ik, cache_k, cache_v, cache_idx_k, page_table, *, pp_index=64, pp=32):
    b, t = q.shape[:2]
    n_pages = page_table.shape[1]
    n_pool = cache_k.shape[0]
    past = n_pages * PAGE_SIZE
    topk = min(IDX_TOPK_MAX, (past + t) // 4)
    n_idx_bits = max(1, (past + PAGE_SIZE - 1).bit_length())
    pp_index, pp = min(pp_index, n_pages), min(pp, n_pages)
    assert n_pages % pp == 0 and n_pages % pp_index == 0 and t == SUBLANE and KV_W == 2 * LANE
    npg = n_pages // pp
    n_keys = past + PAGE_SIZE
    bf = jnp.bfloat16
    rep = N_HEADS // KV_HEADS
    n_rows = N_HEADS * t

    iq_r = iq.astype(bf).transpose(0, 2, 1, 3).reshape(b, IDX_HEADS * t, IDX_DIM)
    iw_r = iw.transpose(0, 2, 1).reshape(b, IDX_HEADS * t, 1)
    new_page = lambda a: jnp.pad(jnp.swapaxes(a, 1, 2), ((0, 0), (0, 0), (0, PAGE_SIZE - t)))
    pages_t = lambda c: jnp.swapaxes(c.reshape(n_pool, PAGE_SIZE, -1), 1, 2)
    ik_new = new_page(ik)
    k_new = new_page(k.reshape(b, t, KV_W))
    v_new = new_page(v.reshape(b, t, KV_W))
    qs = (q * HEAD_DIM ** -0.5).astype(bf).reshape(b, t, KV_HEADS, rep, HEAD_DIM).transpose(0, 2, 3, 1, 4)
    q_bd = (qs[:, :, :, :, None, :] * jnp.eye(KV_HEADS, dtype=bf)[None, :, None, None, :, None])
    q_bd = q_bd.reshape(b, n_rows, KV_W)

    page_map = lambda per_step, j: (lambda bi, pg, pt: (pt[bi, pg * per_step + j], 0, 0))
    per_seq = lambda shape: pl.BlockSpec((1,) + shape, lambda bi, pg, pt: (bi,) + (0,) * len(shape))
    params = pltpu.CompilerParams(dimension_semantics=("arbitrary", "arbitrary"), vmem_limit_bytes=VMEM_LIMIT_BYTES)

    bias = pl.pallas_call(
        functools.partial(_sample_index_kernel, pp=pp_index, n_pages=n_pages, topk=topk, n_idx_bits=n_idx_bits),
        grid_spec=pltpu.PrefetchScalarGridSpec(
            num_scalar_prefetch=1, grid=(b, n_pages // pp_index),
            in_specs=[per_seq((IDX_HEADS * t, IDX_DIM)), per_seq((IDX_HEADS * t, 1)), per_seq((IDX_DIM, PAGE_SIZE))]
            + [pl.BlockSpec((1, IDX_DIM, PAGE_SIZE), page_map(pp_index, j)) for j in range(pp_index)],
            out_specs=per_seq((t, n_keys)),
            scratch_shapes=[pltpu.VMEM((n_pages // pp_index, t, pp_index * PAGE_SIZE), jnp.int32),
                            pltpu.VMEM((1, t, n_keys), jnp.int32),
                            pltpu.VMEM((IDX_HEADS * t, PAGE_SIZE), jnp.float32),
                            pltpu.VMEM((t, LANE), jnp.int32)]),
        out_shape=jax.ShapeDtypeStruct((b, t, n_keys), jnp.float32),
        compiler_params=params,
        name="sample_index",
    )(page_table, iq_r, iw_r, ik_new, *([pages_t(cache_idx_k)] * pp_index))

    ck = pages_t(cache_k)
    cv = pages_t(cache_v)
    out = pl.pallas_call(
        functools.partial(_sample_attend_kernel, pp=pp, n_pages=n_pages),
        grid_spec=pltpu.PrefetchScalarGridSpec(
            num_scalar_prefetch=1, grid=(b, npg),
            in_specs=[per_seq((n_rows, KV_W)),
                      pl.BlockSpec((1, t, pp * PAGE_SIZE), lambda bi, pg, pt: (bi, 0, pg)),
                      pl.BlockSpec((1, t, PAGE_SIZE), lambda bi, pg, pt: (bi, 0, n_pages)),
                      per_seq((KV_W, PAGE_SIZE)), per_seq((KV_W, PAGE_SIZE))]
            + [pl.BlockSpec((1, KV_W, PAGE_SIZE), page_map(pp, j)) for j in range(pp)] * 2,
            out_specs=per_seq((n_rows, LANE)),
            scratch_shapes=[pltpu.VMEM((n_rows, LANE), jnp.float32), pltpu.VMEM((n_rows, LANE), jnp.float32),
                            pltpu.VMEM((n_rows, KV_W), jnp.float32)]),
        out_shape=jax.ShapeDtypeStruct((b, n_rows, LANE), jnp.float32),
        compiler_params=params,
        name="sample_attend",
    )(page_table, q_bd, bias, bias, k_new, v_new, *([ck] * pp), *([cv] * pp))
    out = out[:, :, :HEAD_DIM].reshape(b, N_HEADS, t, HEAD_DIM).transpose(0, 2, 1, 3)
    return out.reshape(b, t, N_HEADS * HEAD_DIM)


GELU_C = math.sqrt(2.0 / math.pi)


def _rglru_kernel(x_ref, xg_ref, prev_ref, h0_ref, cw_ref, cb_ref, wa_ref, ba_ref, wi_ref, bi_ref, lam_ref,
                  o_ref, hl_ref, conv_ref, a_ref, u_ref, xprev_ref, h_ref, *, rows, per_group_state):
    i = pl.program_id(0)
    n_groups = rows // SUBLANE
    d_rnn = x_ref.shape[1]
    r8 = lax.broadcasted_iota(jnp.int32, (SUBLANE, d_rnn), 0)
    grp = lambda g: pl.ds(pl.multiple_of(g * SUBLANE, SUBLANE), SUBLANE)

    if not per_group_state:
        @pl.when(i == 0)
        def _():
            xprev_ref[...] = jnp.zeros_like(xprev_ref)
            h_ref[...] = jnp.zeros_like(h_ref)

    cw = cw_ref[...]
    cb = cb_ref[...]

    def conv_group(g, prev):
        x8 = x_ref[grp(g), :]
        if per_group_state:
            prev = prev_ref[grp(g), :]
        out = cb + x8 * cw[CONV_W - 1:CONV_W, :]
        for d in range(1, CONV_W):
            shifted = jnp.where(r8 < d, pltpu.roll(prev, d, axis=0), pltpu.roll(x8, d, axis=0))
            out = out + shifted * cw[CONV_W - 1 - d:CONV_W - d, :]
        conv_ref[grp(g), :] = out
        return x8

    zeros8 = jnp.zeros((SUBLANE, d_rnn), jnp.float32)
    xlast = lax.fori_loop(0, n_groups, conv_group, zeros8 if per_group_state else xprev_ref[...])
    if not per_group_state:
        xprev_ref[...] = xlast

    xc = conv_ref[...]
    xb = xc.astype(jnp.bfloat16)
    r_parts, i_parts = [], []
    for n in range(RG_BLOCKS):
        xn = xb[:, n * RG_BLOCK_W:(n + 1) * RG_BLOCK_W]
        r_parts.append(jnp.dot(xn, wa_ref[n], preferred_element_type=jnp.float32))
        i_parts.append(jnp.dot(xn, wi_ref[n], preferred_element_type=jnp.float32))
    r = jax.nn.sigmoid(jnp.concatenate(r_parts, axis=-1) + ba_ref[...])
    ig = jax.nn.sigmoid(jnp.concatenate(i_parts, axis=-1) + bi_ref[...])
    lam = lam_ref[...]
    log_sig_lam = -(jnp.maximum(-lam, 0.0) + jnp.log1p(jnp.exp(-jnp.abs(lam))))
    log_a = RG_C * r * log_sig_lam
    a = jnp.exp(log_a)
    a_ref[...] = a
    u_ref[...] = jnp.sqrt(-jnp.tanh(log_a) * (a * a + 1.0)) * (ig * xc)

    def scan_group(g, hprev):
        a = a_ref[grp(g), :]
        u = u_ref[grp(g), :]
        if per_group_state:
            hprev = jnp.broadcast_to(h0_ref[pl.ds(g, 1), :], (SUBLANE, d_rnn))
        for d in (1, 2, 4):
            u = jnp.where(r8 >= d, a * pltpu.roll(u, d, axis=0) + u, u)
            a = jnp.where(r8 >= d, a * pltpu.roll(a, d, axis=0), a)
        h = a * hprev + u
        u_ref[grp(g), :] = h
        hlast = jnp.broadcast_to(h[SUBLANE - 1:SUBLANE, :], (SUBLANE, d_rnn))
        if per_group_state:
            hl_ref[pl.ds(g, 1), :] = h[SUBLANE - 1:SUBLANE, :]
        return hlast

    hlast = lax.fori_loop(0, n_groups, scan_group, zeros8 if per_group_state else h_ref[...])
    if not per_group_state:
        h_ref[...] = hlast
        hl_ref[...] = hlast

    xg = xg_ref[...]
    gelu = 0.5 * xg * (1.0 + jnp.tanh(GELU_C * (xg + 0.044715 * (xg * xg * xg))))
    o_ref[...] = u_ref[...] * gelu


def _rglru(z, row0, n_rows, prev, h0, conv_w, conv_b, w_a, b_a, w_i, b_i, lam, *, per_group_state, tile_rows):
    d = D_RNN
    assert row0 % tile_rows == 0 and n_rows % tile_rows == 0
    row = lambda a: a.reshape(1, d)
    const = lambda shape: pl.BlockSpec(shape, lambda i: (0,) * len(shape))
    n_hl = n_rows // SUBLANE if per_group_state else SUBLANE
    body = functools.partial(_rglru_kernel, rows=tile_rows, per_group_state=per_group_state)
    tile = pl.BlockSpec((tile_rows, d), lambda i: (i, 0))
    z_col = lambda name: pl.BlockSpec((tile_rows, d), lambda i, c=Z_COLS[name] // d: (row0 // tile_rows + i, c))
    if per_group_state:
        assert tile_rows == n_rows
    return pl.pallas_call(
        body,
        grid=(n_rows // tile_rows,),
        in_specs=[z_col("xr"), z_col("xg"), const(prev.shape), const(h0.shape), const((CONV_W, d)), const((1, d)),
                  const(w_a.shape), const((1, d)), const(w_i.shape), const((1, d)), const((1, d))],
        out_specs=[tile, const((n_hl, d))],
        out_shape=[jax.ShapeDtypeStruct((n_rows, d), jnp.float32), jax.ShapeDtypeStruct((n_hl, d), jnp.float32)],
        scratch_shapes=[pltpu.VMEM((tile_rows, d), jnp.float32), pltpu.VMEM((tile_rows, d), jnp.float32),
                        pltpu.VMEM((tile_rows, d), jnp.float32), pltpu.VMEM((SUBLANE, d), jnp.float32),
                        pltpu.VMEM((SUBLANE, d), jnp.float32)],
        compiler_params=pltpu.CompilerParams(dimension_semantics=("arbitrary",), vmem_limit_bytes=VMEM_LIMIT_BYTES),
        name="rglru_sample" if per_group_state else "rglru_prompt",
    )(z, z, prev, h0, conv_w, row(conv_b), w_a.astype(jnp.bfloat16), row(b_a), w_i.astype(jnp.bfloat16), row(b_i), row(lam))


def _rglru_prompt(z, row0, n_rows, conv_w, conv_b, w_a, b_a, w_i, b_i, lam, tile_rows=512):
    dummy = jnp.zeros((SUBLANE, D_RNN), jnp.float32)
    rnn, hl = _rglru(z, row0, n_rows, dummy, dummy, conv_w, conv_b, w_a, b_a, w_i, b_i, lam,
                     per_group_state=False, tile_rows=tile_rows)
    return rnn, hl[0]


def _rglru_sample(z, row0, state_conv, state_h, conv_w, conv_b, w_a, b_a, w_i, b_i, lam):
    b, d = state_h.shape
    prev = jnp.concatenate([jnp.zeros((b, SUBLANE - (CONV_W - 1), d), jnp.float32), state_conv], axis=1)
    return _rglru(z, row0, b * SUBLANE, prev.reshape(b * SUBLANE, d), state_h,
                  conv_w, conv_b, w_a, b_a, w_i, b_i, lam, per_group_state=True, tile_rows=b * SUBLANE)


def _rope_tables(pos):
    half = HEAD_DIM // 2
    inv_freq = ROPE_THETA ** (-jnp.arange(half, dtype=jnp.float32) / half)
    ang = pos.astype(jnp.float32)[:, None] * inv_freq[None, :]
    cos, sin = jnp.cos(ang), jnp.sin(ang)
    cos2 = jnp.concatenate([cos, cos], axis=-1)
    sin2 = jnp.concatenate([-sin, sin], axis=-1)
    return jnp.tile(cos2, (1, LANE // HEAD_DIM)), jnp.tile(sin2, (1, LANE // HEAD_DIM))


def _rotate(x, cos2, sin2):
    width = x.shape[1]
    half = HEAD_DIM // 2
    lane = lax.broadcasted_iota(jnp.int32, x.shape, 1)
    partner = jnp.where((lane & half) == 0, pltpu.roll(x, width - half, axis=1), pltpu.roll(x, half, axis=1))
    reps = width // LANE
    return x * jnp.tile(cos2, (1, reps)) + partner * jnp.tile(sin2, (1, reps))


def _head_mean_square(x, seg_ref):
    sq = x * x
    hi = sq.astype(jnp.bfloat16)
    lo = (sq - hi.astype(jnp.float32)).astype(jnp.bfloat16)
    seg = seg_ref[...]
    return (jnp.dot(hi, seg, preferred_element_type=jnp.float32)
            + jnp.dot(lo, seg, preferred_element_type=jnp.float32))


def _qk_post_kernel(q_ref, k_ref, iq_ref, ikw_ref, cos_ref, sin_ref, qg_ref, kg_ref, segq_ref, segk_ref,
                    qo_ref, ko_ref, iqo_ref, iko_ref, iwo_ref):
    cos2, sin2 = cos_ref[...], sin_ref[...]
    q = q_ref[...]
    qn = q * lax.rsqrt(_head_mean_square(q, segq_ref) + EPS) * qg_ref[...]
    qo_ref[...] = _rotate(qn, cos2, sin2)
    k = k_ref[...]
    kn = k * lax.rsqrt(_head_mean_square(k, segk_ref) + EPS) * kg_ref[...]
    ko_ref[...] = _rotate(kn, cos2, sin2)
    iqo_ref[...] = _rotate(iq_ref[...], cos2, sin2)
    ikw = ikw_ref[...]
    iko_ref[...] = _rotate(ikw, cos2, sin2)
    iwo_ref[...] = ikw * (IDX_HEADS ** -0.5 * IDX_DIM ** -0.5)


def _qk_post(z, pos, q_norm_g, k_norm_g, tm=512):
    n = z.shape[0]
    cos2, sin2 = _rope_tables(pos)
    seg = lambda w: (jnp.kron(jnp.eye(w // HEAD_DIM), jnp.ones((HEAD_DIM, HEAD_DIM))) / HEAD_DIM).astype(jnp.bfloat16)
    col = lambda name, w: pl.BlockSpec((tm, w), lambda i, c=Z_COLS[name] // w: (i, c))
    const = lambda shape: pl.BlockSpec(shape, lambda i: (0,) * len(shape))
    rows = lambda w: pl.BlockSpec((tm, w), lambda i: (i, 0))
    f32 = jnp.float32
    return pl.pallas_call(
        _qk_post_kernel,
        grid=(n // tm,),
        in_specs=[col("q", Q_W), col("k", KV_W), col("iq", IQ_W), col("ikw", LANE), rows(LANE), rows(LANE),
                  const((1, Q_W)), const((1, KV_W)), const((Q_W, Q_W)), const((KV_W, KV_W))],
        out_specs=[rows(Q_W), rows(KV_W), rows(IQ_W), rows(LANE), rows(LANE)],
        out_shape=[jax.ShapeDtypeStruct((n, Q_W), f32), jax.ShapeDtypeStruct((n, KV_W), f32),
                   jax.ShapeDtypeStruct((n, IQ_W), f32), jax.ShapeDtypeStruct((n, LANE), f32),
                   jax.ShapeDtypeStruct((n, LANE), f32)],
        compiler_params=pltpu.CompilerParams(dimension_semantics=("arbitrary",), vmem_limit_bytes=VMEM_LIMIT_BYTES),
        name="qk_post",
    )(z, z, z, z, cos2, sin2, jnp.tile(q_norm_g, N_HEADS).reshape(1, Q_W), jnp.tile(k_norm_g, KV_HEADS).reshape(1, KV_W),
      seg(Q_W), seg(KV_W))


def _out_proj_kernel(x_ref, attn_ref, rnn_ref, ga_ref, gr_ref, wa_ref, wr_ref, wo_ref, o_ref):
    bf = jnp.bfloat16
    a = jnp.dot(attn_ref[...].astype(bf), wa_ref[...], preferred_element_type=jnp.float32)
    r = jnp.dot(rnn_ref[...].astype(bf), wr_ref[...], preferred_element_type=jnp.float32)
    merged = jax.nn.sigmoid(ga_ref[...]) * a + jax.nn.sigmoid(gr_ref[...]) * r
    o_ref[...] = x_ref[...] + jnp.dot(merged.astype(bf), wo_ref[...], preferred_element_type=jnp.float32)


def _out_proj(x, attn, rnn, z, w_attn_out, w_rnn_out, w_out, tm=512):
    n, d = x.shape
    bf = jnp.bfloat16
    rows = pl.BlockSpec((tm, d), lambda i: (i, 0))
    col = lambda name: pl.BlockSpec((tm, d), lambda i, c=Z_COLS[name] // d: (i, c))
    const = pl.BlockSpec((d, d), lambda i: (0, 0))
    return pl.pallas_call(
        _out_proj_kernel,
        grid=(n // tm,),
        in_specs=[rows, rows, rows, col("ga"), col("gr"), const, const, const],
        out_specs=rows,
        out_shape=jax.ShapeDtypeStruct((n, d), jnp.float32),
        compiler_params=pltpu.CompilerParams(dimension_semantics=("arbitrary",), vmem_limit_bytes=VMEM_LIMIT_BYTES),
        name="out_proj",
    )(x, attn, rnn, z, z, w_attn_out.astype(bf), w_rnn_out.astype(bf), w_out.astype(bf))


def _rms_norm_rows(x, g):
    return x * lax.rsqrt(jnp.mean(x * x, axis=-1, keepdims=True) + EPS) * g


def _router_kernel(x_ref, g_ref, wr_ref, br_ref, tri_ref, e_ref, gate_ref, rank_ref, cnt_ref, run_ref):
    i = pl.program_id(0)
    tm = x_ref.shape[0]

    @pl.when(i == 0)
    def _():
        run_ref[...] = jnp.zeros_like(run_ref)

    xn_bf = _rms_norm_rows(x_ref[...], g_ref[...]).astype(jnp.bfloat16)
    logits = lax.dot_general(wr_ref[...], xn_bf, (((1,), (1,)), ((), ())), preferred_element_type=jnp.float32)
    logits = logits + br_ref[...]

    expert = lax.broadcasted_iota(jnp.int32, (N_EXPERTS, tm), 0)
    member = jnp.zeros((N_EXPERTS, tm), jnp.float32)
    picked, values = [], []
    for k in range(TOP_K):
        mx = jnp.max(logits, axis=0, keepdims=True)
        idx = jnp.min(jnp.where(logits == mx, expert, N_EXPERTS), axis=0, keepdims=True)
        hit = expert == idx
        member = jnp.where(hit, 1.0, member)
        logits = jnp.where(hit, -jnp.inf, logits)
        picked.append(idx)
        values.append(mx)
        e_ref[k:k + 1, :] = idx

    ex = [jnp.exp(v - values[0]) for v in values]
    denom = ex[0] + ex[1] + ex[2] + ex[3]
    for k in range(TOP_K):
        gate_ref[k:k + 1, :] = ex[k] / denom

    before = jnp.dot(member.astype(jnp.bfloat16), tri_ref[...], preferred_element_type=jnp.float32)
    before = before + jnp.tile(run_ref[...], (1, tm // LANE))
    for k in range(TOP_K):
        r = jnp.sum(jnp.where(expert == picked[k], before, 0.0), axis=0, keepdims=True)
        rank_ref[k:k + 1, :] = r.astype(jnp.int32)
    run = run_ref[...] + jnp.sum(member, axis=1, keepdims=True)
    run_ref[...] = run
    cnt_ref[...] = run.astype(jnp.int32)


def _route(x2d, norm_g, w_router, b_router, tm=512):
    n, d = x2d.shape
    tri = (jnp.arange(tm)[:, None] < jnp.arange(tm)[None, :]).astype(jnp.bfloat16)
    const = lambda shape: pl.BlockSpec(shape, lambda i: (0,) * len(shape))
    rows4 = pl.BlockSpec((TOP_K, tm), lambda i: (0, i))
    top_e, gates, rank, cnt = pl.pallas_call(
        _router_kernel,
        grid=(n // tm,),
        in_specs=[pl.BlockSpec((tm, d), lambda i: (i, 0)), const((1, d)), const((N_EXPERTS, d)),
                  const((N_EXPERTS, 1)), const((tm, tm))],
        out_specs=[rows4, rows4, rows4, const((N_EXPERTS, LANE))],
        out_shape=[jax.ShapeDtypeStruct((TOP_K, n), jnp.int32),
                   jax.ShapeDtypeStruct((TOP_K, n), jnp.float32), jax.ShapeDtypeStruct((TOP_K, n), jnp.int32),
                   jax.ShapeDtypeStruct((N_EXPERTS, LANE), jnp.int32)],
        scratch_shapes=[pltpu.VMEM((N_EXPERTS, LANE), jnp.float32)],
        compiler_params=pltpu.CompilerParams(dimension_semantics=("arbitrary",)),
        name="moe_router",
    )(x2d, norm_g.reshape(1, d), w_router.T.astype(jnp.bfloat16), b_router.reshape(N_EXPERTS, 1), tri)
    return top_e, gates, rank, cnt[:, 0]


def _expert_kernel(be_ref, used_ref, xs_ref, g_ref, wgu_ref, bgu_ref, wd_ref, bd_ref, o_ref, *, blk):
    i = pl.program_id(0)

    @pl.when(i * blk < used_ref[0])
    def _():
        xn = _rms_norm_rows(xs_ref[...], g_ref[...]).astype(jnp.bfloat16)
        gu = jnp.dot(xn, wgu_ref[0].astype(jnp.bfloat16), preferred_element_type=jnp.float32) + bgu_ref[0]
        g = jnp.minimum(gu[:, :D_FF], SWIGLU_LIMIT)
        u = jnp.clip(gu[:, D_FF:], -SWIGLU_LIMIT, SWIGLU_LIMIT)
        act = (u + 1.0) * (g * jax.nn.sigmoid(SWIGLU_ALPHA * g))
        o_ref[...] = (jnp.dot(act.astype(jnp.bfloat16), wd_ref[0].astype(jnp.bfloat16),
                              preferred_element_type=jnp.float32) + bd_ref[0])

    @pl.when(i * blk >= used_ref[0])
    def _():
        o_ref[...] = jnp.zeros_like(o_ref)


def _experts(xs, norm_g, blk_exp, n_used, w_gate_up, b_gate_up, w_down, b_down, *, blk):
    n_slots, d = xs.shape
    n_blocks = n_slots // blk
    return pl.pallas_call(
        functools.partial(_expert_kernel, blk=blk),
        grid_spec=pltpu.PrefetchScalarGridSpec(
            num_scalar_prefetch=2, grid=(n_blocks,),
            in_specs=[pl.BlockSpec((blk, d), lambda i, be, nu: (i, 0)),
                      pl.BlockSpec((1, d), lambda i, be, nu: (0, 0)),
                      pl.BlockSpec((1, d, 2 * D_FF), lambda i, be, nu: (be[i], 0, 0)),
                      pl.BlockSpec((1, 1, 2 * D_FF), lambda i, be, nu: (be[i], 0, 0)),
                      pl.BlockSpec((1, D_FF, d), lambda i, be, nu: (be[i], 0, 0)),
                      pl.BlockSpec((1, 1, d), lambda i, be, nu: (be[i], 0, 0))],
            out_specs=pl.BlockSpec((blk, d), lambda i, be, nu: (i, 0))),
        out_shape=jax.ShapeDtypeStruct((n_slots, d), jnp.float32),
        compiler_params=pltpu.CompilerParams(dimension_semantics=("arbitrary",), vmem_limit_bytes=VMEM_LIMIT_BYTES),
        name="moe_experts",
    )(blk_exp, n_used, xs, norm_g.reshape(1, d), w_gate_up, b_gate_up.reshape(N_EXPERTS, 1, 2 * D_FF),
      w_down, b_down.reshape(N_EXPERTS, 1, d))


def _row_copy(src_hbm, src_row, dst_ref, dst_row, sem):
    return pltpu.make_async_copy(src_hbm.at[pl.ds(src_row, 1)], dst_ref.at[pl.ds(dst_row, 1)], sem)


def _dispatch_kernel(dest_ref, x_ref, xs_in_hbm, xs_hbm, sem, *, tm):
    del xs_in_hbm

    def issue(t, carry):
        for k in range(TOP_K):
            _row_copy(x_ref, t, xs_hbm, dest_ref[k, t], sem).start()
        return carry

    lax.fori_loop(0, tm, issue, 0)

    def drain(t, carry):
        for k in range(TOP_K):
            _row_copy(x_ref, 0, xs_hbm, 0, sem).wait()
        return carry

    lax.fori_loop(0, tm, drain, 0)


def _dispatch(x2d, dest, n_slots, tm=128):
    n, d = x2d.shape
    return pl.pallas_call(
        functools.partial(_dispatch_kernel, tm=tm),
        grid=(n // tm,),
        in_specs=[pl.BlockSpec((TOP_K, tm), lambda i: (0, i), memory_space=pltpu.SMEM),
                  pl.BlockSpec((tm, d), lambda i: (i, 0)), pl.BlockSpec(memory_space=pl.ANY)],
        out_specs=pl.BlockSpec(memory_space=pl.ANY),
        out_shape=jax.ShapeDtypeStruct((n_slots, d), x2d.dtype),
        scratch_shapes=[pltpu.SemaphoreType.DMA(())],
        input_output_aliases={2: 0},
        compiler_params=pltpu.CompilerParams(dimension_semantics=("arbitrary",)),
        name="moe_dispatch",
    )(dest, x2d, jnp.zeros((n_slots, d), x2d.dtype))


def _combine_kernel(dest_ref, x_ref, gate_ref, ys_hbm, o_ref, buf_ref, sem, *, tm):
    def issue(t, carry):
        for k in range(TOP_K):
            _row_copy(ys_hbm, dest_ref[k, t], buf_ref.at[k], t, sem).start()
        return carry

    lax.fori_loop(0, tm, issue, 0)

    def drain(t, carry):
        for k in range(TOP_K):
            _row_copy(ys_hbm, 0, buf_ref.at[k], 0, sem).wait()
        return carry

    lax.fori_loop(0, tm, drain, 0)
    gate = gate_ref[...]
    acc = x_ref[...]
    for k in range(TOP_K):
        acc = acc + gate[:, k:k + 1] * buf_ref[k]
    o_ref[...] = acc


def _combine(x2d, ys, dest, gates_t, tm=128):
    n, d = x2d.shape
    return pl.pallas_call(
        functools.partial(_combine_kernel, tm=tm),
        grid=(n // tm,),
        in_specs=[pl.BlockSpec((TOP_K, tm), lambda i: (0, i), memory_space=pltpu.SMEM),
                  pl.BlockSpec((tm, d), lambda i: (i, 0)), pl.BlockSpec((tm, TOP_K), lambda i: (i, 0)),
                  pl.BlockSpec(memory_space=pl.ANY)],
        out_specs=pl.BlockSpec((tm, d), lambda i: (i, 0)),
        out_shape=jax.ShapeDtypeStruct((n, d), jnp.float32),
        scratch_shapes=[pltpu.VMEM((TOP_K, tm, d), jnp.float32), pltpu.SemaphoreType.DMA(())],
        compiler_params=pltpu.CompilerParams(dimension_semantics=("arbitrary",)),
        name="moe_combine",
    )(dest, x2d, gates_t, ys)


MOE_ROWS = 512


def _moe(x2d, norm_g, w_router, b_router, w_gate_up, b_gate_up, w_down, b_down, *, blk=MOE_ROWS, tm=512):
    n, d = x2d.shape
    top_e, gates, rank, counts = _route(x2d, norm_g, w_router, b_router, tm=tm)
    padded = (counts + blk - 1) // blk * blk
    pad_end = jnp.cumsum(padded)
    pad_start = pad_end - padded
    experts = jnp.arange(N_EXPERTS, dtype=jnp.int32)
    start_of = jnp.sum(jnp.where(top_e[..., None] == experts, pad_start.astype(jnp.int32), 0), axis=-1)
    dest = start_of + rank
    n_blocks = -(-(n * TOP_K + N_EXPERTS * (blk - 1)) // blk)
    n_slots = n_blocks * blk
    block_start = jnp.arange(n_blocks, dtype=jnp.int32) * blk
    blk_exp = jnp.minimum(jnp.sum(pad_end[None, :] <= block_start[:, None], axis=1), N_EXPERTS - 1).astype(jnp.int32)
    n_used = pad_end[-1:].astype(jnp.int32)
    xs = _dispatch(x2d, dest, n_slots)
    ys = _experts(xs, norm_g, blk_exp, n_used, w_gate_up, b_gate_up, w_down, b_down, blk=blk)
    return _combine(x2d, ys, dest, gates.T)


def kernel(x_prompt, x_sample, cache_k, cache_v, cache_idx_k, state_conv, state_h, page_table,
           norm1_g, w_in, q_norm_g, k_norm_g, conv_w, conv_b, rg_w_a, rg_b_a, rg_w_i, rg_b_i, rg_lambda,
           w_attn_out, w_rnn_out, w_out, norm2_g, w_router, b_router, w_gate_up, b_gate_up, w_down, b_down):
    assert x_prompt.shape[0] == 1 and norm1_g.shape[0] == 1 and x_sample.shape[1] == SUBLANE
    s_len = x_prompt.shape[1]
    db, t_new = x_sample.shape[:2]
    n_s = db * t_new
    past = page_table.shape[1] * PAGE_SIZE
    pos = jnp.concatenate([jnp.arange(s_len, dtype=jnp.int32),
                           jnp.tile(past + jnp.arange(t_new, dtype=jnp.int32), db)])
    x_all = jnp.concatenate([x_prompt.reshape(s_len, D_MODEL), x_sample.reshape(n_s, D_MODEL)], axis=0)

    z = _norm_proj(x_all, norm1_g[0], _permuted_w_in(w_in[0]))
    q, k, iq, ik, iw = _qk_post(z, pos, q_norm_g[0], k_norm_g[0])
    v = z[:, Z_COLS["v"]:Z_COLS["v"] + KV_W]
    xr = z[:, Z_COLS["xr"]:Z_COLS["xr"] + D_RNN]
    ik = ik[:, :IDX_DIM]
    iw = iw[:, IDX_DIM:IDX_DIM + IDX_HEADS]
    heads = lambda a, lo, hi, n_heads: a[lo:hi].reshape(hi - lo, n_heads, -1)
    seqs = lambda a: a.reshape((db, t_new) + a.shape[1:])
    p_rows, s_rows = (0, s_len), (s_len, s_len + n_s)

    attn_p = _prompt_dsa_attention_t(heads(q, *p_rows, N_HEADS), heads(k, *p_rows, KV_HEADS), heads(v, *p_rows, KV_HEADS),
                                     heads(iq, *p_rows, IDX_HEADS), iw[:s_len], ik[:s_len])
    attn_s = _sample_dsa_attention(seqs(heads(q, *s_rows, N_HEADS)), seqs(heads(k, *s_rows, KV_HEADS)),
                                   seqs(heads(v, *s_rows, KV_HEADS)), seqs(heads(iq, *s_rows, IDX_HEADS)),
                                   seqs(iw[s_len:]), seqs(ik[s_len:]), cache_k[0], cache_v[0], cache_idx_k[0], page_table)
    rg = (conv_w[0], conv_b[0], rg_w_a[0], rg_b_a[0], rg_w_i[0], rg_b_i[0], rg_lambda[0])
    rnn_p, h_p = _rglru_prompt(z, 0, s_len, *rg)
    rnn_s, h_s = _rglru_sample(z, s_len, state_conv[0], state_h[0], *rg)

    attn = jnp.concatenate([attn_p, attn_s.reshape(n_s, Q_W)], axis=0)
    rnn = jnp.concatenate([rnn_p, rnn_s], axis=0)
    x_mid = _out_proj(x_all, attn, rnn, z, w_attn_out[0], w_rnn_out[0], w_out[0])
    y = _moe(x_mid, norm2_g[0], w_router[0], b_router[0], w_gate_up[0], b_gate_up[0], w_down[0], b_down[0])

    tail = CONV_W - 1
    xr_s = xr[s_len:].reshape(db, t_new, D_RNN)
    conv_s = jnp.concatenate([state_conv[0], xr_s], axis=1)[:, -tail:]
    kv_p = lambda a: a[:s_len].reshape(1, 1, s_len, KV_HEADS, HEAD_DIM)
    kv_s = lambda a: a[s_len:].reshape(1, db, t_new, KV_HEADS, HEAD_DIM)
    return (y[:s_len].reshape(x_prompt.shape), y[s_len:].reshape(x_sample.shape),
            kv_p(k), kv_p(v), ik[:s_len].reshape(1, 1, s_len, IDX_DIM),
            xr[s_len - tail:s_len].reshape(1, 1, tail, D_RNN), h_p.reshape(1, 1, D_RNN),
            kv_s(k), kv_s(v), ik[s_len:].reshape(1, db, t_new, IDX_DIM),
            conv_s[None], h_s[None])
```
